```python
import math
import jax, jax.numpy as jnp
from jax import lax
import numpy as np

D_MODEL = 1024
BATCH = 4
SEQ = 8192
DEPTH = 2

HEAD_DIM = 64
ROPE_THETA = 10000.0
NORM_EPS = 1e-6
NEG_INF = -1e30
Q_BLOCK = 128
N_BRANCH = 4
BRANCH_WIDTH = 4 * HEAD_DIM
MAX_POS_OFFSET = 4096

NSA_HEADS = 4
NSA_KV_HEADS = 1
NSA_GROUP = NSA_HEADS // NSA_KV_HEADS
NSA_CMP_LEN = 32
NSA_CMP_STRIDE = 16
NSA_SEL_BLOCK = 64
NSA_N_SEL = 16
NSA_WINDOW = 512
NSA_CMP_HIDDEN = 128
NSA_Q_CHUNK = 64
NSA_FORCED_SCORE = 1e4

SB_HEADS = 4

DIFF_HEADS = 4
DIFF_DIM = 32

SWA_HEADS = 4
SWA_KV_HEADS = 2
SWA_GROUP = SWA_HEADS // SWA_KV_HEADS
SWA_WINDOW = 128

N_EXPERTS = 32
TOP_K = 4
D_FF = D_MODEL
SWIGLU_ALPHA = 1.702
SWIGLU_LIMIT = 7.0
MOE_CHUNK = 256

IN_SPLITS = (
    NSA_HEADS * HEAD_DIM,
    NSA_KV_HEADS * HEAD_DIM, NSA_KV_HEADS * HEAD_DIM,
    NSA_KV_HEADS * HEAD_DIM, NSA_KV_HEADS * HEAD_DIM,
    NSA_KV_HEADS * HEAD_DIM, NSA_KV_HEADS * HEAD_DIM,
    3 * NSA_HEADS,
    SB_HEADS * HEAD_DIM, SB_HEADS * HEAD_DIM, SB_HEADS * HEAD_DIM,
    DIFF_HEADS * 2 * DIFF_DIM, DIFF_HEADS * 2 * DIFF_DIM, DIFF_HEADS * 2 * DIFF_DIM,
    SWA_HEADS * HEAD_DIM, SWA_KV_HEADS * HEAD_DIM, SWA_KV_HEADS * HEAD_DIM,
    N_BRANCH * D_MODEL,
)
IN_COLS = sum(IN_SPLITS)

kernel_name = 'hybrid_gated_nsa_stickbreak_diff_swa_moe'


def rms_norm(x, g):
    xf = x.astype(jnp.float32)
    y = xf * lax.rsqrt(jnp.mean(xf * xf, axis=-1, keepdims=True) + NORM_EPS)
    return (y * g.astype(jnp.float32)).astype(x.dtype)


def rope(x, pos):
    d = x.shape[-1]
    half = d // 2
    inv = ROPE_THETA ** (-jnp.arange(half, dtype=jnp.float32) * 2.0 / d)
    ang = pos.astype(jnp.float32)[..., None] * inv
    cos = jnp.cos(ang)[:, :, None, :]
    sin = jnp.sin(ang)[:, :, None, :]
    xf = x.astype(jnp.float32)
    x1, x2 = xf[..., :half], xf[..., half:]
    return jnp.concatenate([x1 * cos - x2 * sin, x2 * cos + x1 * sin], axis=-1).astype(x.dtype)


def masked_softmax(s, mask):
    s = jnp.where(mask, s.astype(jnp.float32), NEG_INF)
    m = jnp.max(s, axis=-1, keepdims=True)
    p = jnp.where(mask, jnp.exp(s - m), 0.0)
    return p / jnp.maximum(jnp.sum(p, axis=-1, keepdims=True), 1e-30)


def nsa_mixer(q, kc, vc, ks, vs, kw, vw, gates, pos, qn, kn, pe_k, w1_k, w2_k, pe_v, w1_v, w2_v):
    Bn, Sn = q.shape[0], q.shape[1]
    G, HG, dh = NSA_KV_HEADS, NSA_GROUP, HEAD_DIM
    scale = dh ** -0.5
    q = rope(rms_norm(q, qn), pos)
    kc = rope(rms_norm(kc, kn[0]), pos)
    ks = rope(rms_norm(ks, kn[1]), pos)
    kw = rope(rms_norm(kw, kn[2]), pos)

    n_cmp = (Sn - NSA_CMP_LEN) // NSA_CMP_STRIDE + 1
    cmp_start = jnp.arange(n_cmp) * NSA_CMP_STRIDE
    idx = cmp_start[:, None] + jnp.arange(NSA_CMP_LEN)[None, :]

    def compress(t, pe, w1, w2):
        blk = t[:, idx] + pe[:, None, :]
        blk = jnp.moveaxis(blk, 3, 2)
        flat = blk.reshape(Bn, n_cmp, G, NSA_CMP_LEN * dh)
        out = jax.nn.gelu(flat @ w1) @ w2
        return jnp.moveaxis(out, 2, 1)

    k_cmp = compress(kc, pe_k, w1_k, w2_k)
    v_cmp = compress(vc, pe_v, w1_v, w2_v)

    n_sel = Sn // NSA_SEL_BLOCK
    n_top = min(NSA_N_SEL, n_sel)
    sel_start = jnp.arange(n_sel) * NSA_SEL_BLOCK
    overlap = ((cmp_start[:, None] <= (sel_start + NSA_SEL_BLOCK - 1)[None, :]) &
               ((cmp_start + NSA_CMP_LEN - 1)[:, None] >= sel_start[None, :])).astype(jnp.float32)
    ks_blk = ks.transpose(0, 2, 1, 3).reshape(Bn, G, n_sel, NSA_SEL_BLOCK, dh)
    vs_blk = vs.transpose(0, 2, 1, 3).reshape(Bn, G, n_sel, NSA_SEL_BLOCK, dh)
    gather_blocks = jax.vmap(jax.vmap(lambda kb, ix: kb[ix]))

    pad = ((0, 0), (0, 0), (NSA_WINDOW, 0), (0, 0))
    kw_pad = jnp.pad(kw.transpose(0, 2, 1, 3), pad)
    vw_pad = jnp.pad(vw.transpose(0, 2, 1, 3), pad)

    qh = q.reshape(Bn, Sn, G, HG, dh).transpose(0, 2, 3, 1, 4)
    gh = jax.nn.sigmoid(gates.astype(jnp.float32)).astype(q.dtype).transpose(0, 2, 3, 1, 4)
    QC = NSA_Q_CHUNK
    blk_id = jnp.arange(n_sel)

    def chunk(ci):
        q0 = ci * QC
        qc = lax.dynamic_slice_in_dim(qh, q0, QC, axis=3)
        gc = lax.dynamic_slice_in_dim(gh, q0, QC, axis=3)
        t = q0 + jnp.arange(QC)
        s_c = jnp.einsum('bghqd,bgnd->bghqn', qc, k_cmp) * scale
        valid_c = (cmp_start + NSA_CMP_LEN - 1)[None, :] <= t[:, None]
        p_c = masked_softmax(s_c, valid_c)
        o_c = jnp.einsum('bghqn,bgnd->bghqd', p_c.astype(v_cmp.dtype), v_cmp)
        imp = jnp.einsum('bghqn,nm->bgqm', p_c, overlap)
        cur = (t // NSA_SEL_BLOCK)[:, None]
        valid_s = sel_start[None, :] <= t[:, None]
        forced = (blk_id[None, :] == 0) | (blk_id[None, :] == cur) | (blk_id[None, :] == cur - 1)
        score = jnp.where(forced, NSA_FORCED_SCORE, jnp.where(valid_s, imp, -1.0))
        _, sel = lax.top_k(score, n_top)
        k_sel = gather_blocks(ks_blk, sel)
        v_sel = gather_blocks(vs_blk, sel)
        kpos = sel[..., None] * NSA_SEL_BLOCK + jnp.arange(NSA_SEL_BLOCK)
        valid_sel = (kpos <= t[None, None, :, None, None]).reshape(Bn, G, 1, QC, n_top * NSA_SEL_BLOCK)
        s_s = jnp.einsum('bghqd,bgqnkd->bghqnk', qc, k_sel) * scale
        p_s = masked_softmax(s_s.reshape(Bn, G, HG, QC, n_top * NSA_SEL_BLOCK), valid_sel)
        o_s = jnp.einsum('bghqm,bgqmd->bghqd', p_s.astype(v_sel.dtype),
                         v_sel.reshape(Bn, G, QC, n_top * NSA_SEL_BLOCK, dh))
        k_win = lax.dynamic_slice_in_dim(kw_pad, q0, NSA_WINDOW + QC, axis=2)
        v_win = lax.dynamic_slice_in_dim(vw_pad, q0, NSA_WINDOW + QC, axis=2)
        kp = q0 - NSA_WINDOW + jnp.arange(NSA_WINDOW + QC)
        valid_w = ((kp[None, :] <= t[:, None]) & (t[:, None] - kp[None, :] < NSA_WINDOW) & (kp[None, :] >= 0))
        s_w = jnp.einsum('bghqd,bgkd->bghqk', qc, k_win) * scale
        p_w = masked_softmax(s_w, valid_w)
        o_w = jnp.einsum('bghqk,bgkd->bghqd', p_w.astype(v_win.dtype), v_win)
        return gc[..., 0:1] * o_c + gc[..., 1:2] * o_s + gc[..., 2:3] * o_w

    outs = lax.map(chunk, jnp.arange(Sn // QC))
    return outs.transpose(1, 0, 4, 2, 3, 5).reshape(Bn, Sn, NSA_HEADS * dh)


def stick_breaking_mixer(q, k, v):
    Bn, Sn = q.shape[0], q.shape[1]
    q, k, v = (a.transpose(0, 2, 1, 3) for a in (q, k, v))
    scale = HEAD_DIM ** -0.5
    s_idx = jnp.arange(Sn)

    def block(bi):
        q0 = bi * Q_BLOCK
        qb = lax.dynamic_slice_in_dim(q, q0, Q_BLOCK, axis=2)
        z = jnp.einsum('bhqd,bhkd->bhqk', qb, k).astype(jnp.float32) * scale
        t = q0 + jnp.arange(Q_BLOCK)
        strict = s_idx[None, :] < t[:, None]
        log_keep = jnp.where(strict, jax.nn.log_sigmoid(-z), 0.0)
        after = lax.cumsum(log_keep, axis=3, reverse=True) - log_keep
        a = jnp.where(strict, jnp.exp(jax.nn.log_sigmoid(z) + after), 0.0)
        return jnp.einsum('bhqk,bhkd->bhqd', a.astype(v.dtype), v)

    outs = lax.map(block, jnp.arange(Sn // Q_BLOCK))
    return outs.transpose(1, 0, 3, 2, 4).reshape(Bn, Sn, SB_HEADS * HEAD_DIM)


def diff_mixer(q, k, v, pos, qn, kn, lq1, lk1, lq2, lk2, subln, lam_init):
    Bn, Sn = q.shape[0], q.shape[1]
    H, dd = DIFF_HEADS, DIFF_DIM
    scale = dd ** -0.5
    q = rope(rms_norm(q.reshape(Bn, Sn, H * 2, dd), qn), pos).reshape(Bn, Sn, H, 2, dd)
    k = rope(rms_norm(k.reshape(Bn, Sn, H * 2, dd), kn), pos).reshape(Bn, Sn, H, 2, dd)
    q = q.transpose(0, 3, 2, 1, 4)
    k = k.transpose(0, 3, 2, 1, 4)
    vt = v.transpose(0, 2, 1, 3)
    f32 = jnp.float32
    lam = (jnp.exp(jnp.sum(lq1.astype(f32) * lk1.astype(f32))) -
           jnp.exp(jnp.sum(lq2.astype(f32) * lk2.astype(f32))) + lam_init)
    s_idx = jnp.arange(Sn)

    def block(bi):
        q0 = bi * Q_BLOCK
        qb = lax.dynamic_slice_in_dim(q, q0, Q_BLOCK, axis=3)
        s = jnp.einsum('bmhqd,bmhkd->bmhqk', qb, k) * scale
        t = q0 + jnp.arange(Q_BLOCK)
        p = masked_softmax(s, s_idx[None, :] <= t[:, None])
        a = p[:, 0] - lam * p[:, 1]
        return jnp.einsum('bhqk,bhkd->bhqd', a.astype(vt.dtype), vt)

    outs = lax.map(block, jnp.arange(Sn // Q_BLOCK))
    o = outs.transpose(1, 0, 3, 2, 4).reshape(Bn, Sn, H, 2 * dd)
    o = rms_norm(o, subln) * (1.0 - lam_init)
    return o.reshape(Bn, Sn, H * 2 * dd)


def swa_mixer(q, k, v, pos, qn, kn, sinks):
    Bn, Sn = q.shape[0], q.shape[1]
    G, HG, dh = SWA_KV_HEADS, SWA_GROUP, HEAD_DIM
    scale = dh ** -0.5
    q = rope(rms_norm(q, qn), pos)
    k = rope(rms_norm(k, kn), pos)
    nb = Sn // Q_BLOCK
    qb = q.reshape(Bn, nb, Q_BLOCK, G, HG, dh)

    def band(t):
        tb = t.reshape(Bn, nb, Q_BLOCK, G, dh)
        prev = jnp.pad(tb, ((0, 0), (1, 0), (0, 0), (0, 0), (0, 0)))[:, :-1]
        return jnp.concatenate([prev, tb], axis=2)

    kk, vv = band(k), band(v)
    s = jnp.einsum('bnqghd,bnkgd->bnghqk', qb, kk).astype(jnp.float32) * scale
    qpos = jnp.arange(nb)[:, None] * Q_BLOCK + jnp.arange(Q_BLOCK)[None, :]
    kpos = jnp.arange(nb)[:, None] * Q_BLOCK - Q_BLOCK + jnp.arange(2 * Q_BLOCK)[None, :]
    valid = ((kpos[:, None, :] <= qpos[:, :, None]) &
             (qpos[:, :, None] - kpos[:, None, :] < SWA_WINDOW) &
             (kpos[:, None, :] >= 0))[None, :, None, None]
    sink = sinks.astype(jnp.float32).reshape(G, HG)[None, None, :, :, None, None]
    s = jnp.where(valid, s, NEG_INF)
    m = jnp.maximum(jnp.max(s, axis=-1, keepdims=True), sink)
    p = jnp.where(valid, jnp.exp(s - m), 0.0)
    p = p / (jnp.sum(p, axis=-1, keepdims=True) + jnp.exp(sink - m))
    o = jnp.einsum('bnghqk,bnkgd->bnqghd', p.astype(vv.dtype), vv)
    return o.reshape(Bn, Sn, SWA_HEADS * dh)


def mixer_block(h, positions, w_in, nsa_qn, nsa_kn, nsa_pe_k, nsa_w1_k, nsa_w2_k, nsa_pe_v, nsa_w1_v,
                nsa_w2_v, dif_qn, dif_kn, dif_lq1, dif_lk1, dif_lq2, dif_lk2, dif_subln, swa_qn, swa_kn,
                swa_sinks, w_up, w_out, lam_init):
    Bn, Sn, _ = h.shape
    proj = h @ w_in
    split_points = np.cumsum(IN_SPLITS)[:-1].tolist()
    (a_q, a_kc, a_vc, a_ks, a_vs, a_kw, a_vw, a_g, b_q, b_k, b_v, c_q, c_k, c_v,
     d_q, d_k, d_v, br_g) = jnp.split(proj, split_points, axis=-1)

    def heads(t, n, d):
        return t.reshape(Bn, Sn, n, d)

    hd = HEAD_DIM
    o_a = nsa_mixer(heads(a_q, NSA_HEADS, hd), heads(a_kc, NSA_KV_HEADS, hd), heads(a_vc, NSA_KV_HEADS, hd),
                    heads(a_ks, NSA_KV_HEADS, hd), heads(a_vs, NSA_KV_HEADS, hd),
                    heads(a_kw, NSA_KV_HEADS, hd), heads(a_vw, NSA_KV_HEADS, hd),
                    a_g.reshape(Bn, Sn, NSA_KV_HEADS, NSA_GROUP, 3), positions,
                    nsa_qn, nsa_kn, nsa_pe_k, nsa_w1_k, nsa_w2_k, nsa_pe_v, nsa_w1_v, nsa_w2_v)
    o_b = stick_breaking_mixer(heads(b_q, SB_HEADS, hd), heads(b_k, SB_HEADS, hd), heads(b_v, SB_HEADS, hd))
    o_c = diff_mixer(c_q.reshape(Bn, Sn, DIFF_HEADS, 2, DIFF_DIM), c_k.reshape(Bn, Sn, DIFF_HEADS, 2, DIFF_DIM),
                     heads(c_v, DIFF_HEADS, 2 * DIFF_DIM), positions, dif_qn, dif_kn,
                     dif_lq1, dif_lk1, dif_lq2, dif_lk2, dif_subln, lam_init)
    o_d = swa_mixer(heads(d_q, SWA_HEADS, hd), heads(d_k, SWA_KV_HEADS, hd), heads(d_v, SWA_KV_HEADS, hd),
                    positions, swa_qn, swa_kn, swa_sinks)
    gates = jax.nn.sigmoid(br_g.astype(jnp.float32)).astype(h.dtype).reshape(Bn, Sn, N_BRANCH, D_MODEL)
    merged = gates[:, :, 0] * (o_a @ w_up[0])
    for i, o in ((1, o_b), (2, o_c), (3, o_d)):
        merged = merged + gates[:, :, i] * (o @ w_up[i])
    return merged @ w_out


def moe_ffn(h, w_router, b_router, w_gu, b_gu, w_dn, b_dn):
    Bn, Sn, Dm = h.shape
    n_tok = Bn * Sn
    n_asg = n_tok * TOP_K
    hf = h.reshape(n_tok, Dm)
    logits = (hf @ w_router + b_router).astype(jnp.float32)
    top_v, top_e = lax.top_k(logits, TOP_K)
    top_w = jax.nn.softmax(top_v, axis=-1)
    e_flat = top_e.reshape(-1)
    tok_flat = jnp.arange(n_asg, dtype=jnp.int32) // TOP_K
    w_flat = top_w.reshape(-1)
    order = jnp.argsort(e_flat)
    e_sorted = e_flat[order]
    counts = jnp.bincount(e_flat, length=N_EXPERTS)
    starts = jnp.cumsum(counts) - counts
    padded = (counts + MOE_CHUNK - 1) // MOE_CHUNK * MOE_CHUNK
    pad_ends = jnp.cumsum(padded)
    pad_starts = pad_ends - padded
    dest = pad_starts[e_sorted] + jnp.arange(n_asg, dtype=jnp.int32) - starts[e_sorted]
    n_chunks = -(-n_asg // MOE_CHUNK) + N_EXPERTS
    n_rows = n_chunks * MOE_CHUNK
    row_tok = jnp.full((n_rows,), n_tok, jnp.int32).at[dest].set(tok_flat[order])
    row_w = jnp.zeros((n_rows,), jnp.float32).at[dest].set(w_flat[order])
    hf_pad = jnp.concatenate([hf, jnp.zeros((1, Dm), hf.dtype)], axis=0)
    rows = hf_pad[row_tok].reshape(n_chunks, MOE_CHUNK, Dm)
    chunk_e = jnp.minimum(jnp.searchsorted(pad_ends, jnp.arange(n_chunks, dtype=jnp.int32) * MOE_CHUNK,
                                           side='right'), N_EXPERTS - 1)

    def expert_rows(args):
        xr, e = args
        gu = xr @ w_gu[e] + b_gu[e]
        g = jnp.minimum(gu[:, :D_FF], SWIGLU_LIMIT)
        u = jnp.clip(gu[:, D_FF:], -SWIGLU_LIMIT, SWIGLU_LIMIT)
        act = g * jax.nn.sigmoid(SWIGLU_ALPHA * g) * (u + 1.0)
        return act @ w_dn[e] + b_dn[e]

    y = lax.map(expert_rows, (rows, chunk_e)).reshape(n_rows, Dm)
    y = y * row_w[:, None].astype(y.dtype)
    out = jax.ops.segment_sum(y, row_tok, num_segments=n_tok + 1)[:n_tok]
    return out.reshape(Bn, Sn, Dm)


def setup_inputs(seed: int = 0) -> dict:
    key = jax.random.key(seed)
    ks = jax.random.split(key, 40)
    f32 = jnp.float32
    L, D = DEPTH, D_MODEL

    def nrm(k, shape, scale):
        return jax.random.normal(k, shape, f32) * scale

    def gain(k, shape):
        return 1.0 + 0.02 * jax.random.normal(k, shape, f32)

    x = nrm(ks[0], (BATCH, SEQ, D), 1.0)
    c = nrm(ks[1], (BATCH, D), 1.0)
    positions = (jnp.arange(SEQ, dtype=jnp.int32)[None, :] +
                 jax.random.randint(ks[2], (BATCH, 1), 0, MAX_POS_OFFSET, dtype=jnp.int32))
    cmp_in = NSA_CMP_LEN * HEAD_DIM
    return {
        'x': x,
        'c': c,
        'positions': positions,
        'w_ada': nrm(ks[3], (L, D, 6 * D), 0.5 * D ** -0.5),
        'b_ada': nrm(ks[4], (L, 6 * D), 0.01),
        'norm1': gain(ks[5], (L, D)),
        'norm2': gain(ks[6], (L, D)),
        'w_in': nrm(ks[7], (L, D, IN_COLS), D ** -0.5),
        'nsa_qn': gain(ks[8], (L, HEAD_DIM)),
        'nsa_kn': gain(ks[9], (L, 3, HEAD_DIM)),
        'nsa_pe_k': nrm(ks[10], (L, NSA_CMP_LEN, HEAD_DIM), 0.1),
        'nsa_w1_k': nrm(ks[11], (L, cmp_in, NSA_CMP_HIDDEN), cmp_in ** -0.5),
        'nsa_w2_k': nrm(ks[12], (L, NSA_CMP_HIDDEN, HEAD_DIM), NSA_CMP_HIDDEN ** -0.5),
        'nsa_pe_v': nrm(ks[13], (L, NSA_CMP_LEN, HEAD_DIM), 0.1),
        'nsa_w1_v': nrm(ks[14], (L, cmp_in, NSA_CMP_HIDDEN), cmp_in ** -0.5),
        'nsa_w2_v': nrm(ks[15], (L, NSA_CMP_HIDDEN, HEAD_DIM), NSA_CMP_HIDDEN ** -0.5),
        'dif_qn': gain(ks[16], (L, DIFF_DIM)),
        'dif_kn': gain(ks[17], (L, DIFF_DIM)),
        'dif_lq1': nrm(ks[18], (L, DIFF_DIM), 0.1),
        'dif_lk1': nrm(ks[19], (L, DIFF_DIM), 0.1),
        'dif_lq2': nrm(ks[20], (L, DIFF_DIM), 0.1),
        'dif_lk2': nrm(ks[21], (L, DIFF_DIM), 0.1),
        'dif_subln': gain(ks[22], (L, 2 * DIFF_DIM)),
        'swa_qn': gain(ks[23], (L, HEAD_DIM)),
        'swa_kn': gain(ks[24], (L, HEAD_DIM)),
        'swa_sinks': nrm(ks[25], (L, SWA_HEADS), 1.0),
        'w_up': nrm(ks[26], (L, N_BRANCH, BRANCH_WIDTH, D), BRANCH_WIDTH ** -0.5),
        'w_out': nrm(ks[27], (L, D, D), D ** -0.5),
        'w_router': nrm(ks[28], (L, D, N_EXPERTS), D ** -0.5),
        'b_router': nrm(ks[29], (L, N_EXPERTS), 0.01),
        'w_gu': nrm(ks[30], (L, N_EXPERTS, D, 2 * D_FF), D ** -0.5),
        'b_gu': nrm(ks[31], (L, N_EXPERTS, 2 * D_FF), 0.01),
        'w_dn': nrm(ks[32], (L, N_EXPERTS, D_FF, D), D_FF ** -0.5),
        'b_dn': nrm(ks[33], (L, N_EXPERTS, D), 0.01),
    }


def reference(x, c, positions, w_ada, b_ada, norm1, norm2, w_in, nsa_qn, nsa_kn, nsa_pe_k, nsa_w1_k,
              nsa_w2_k, nsa_pe_v, nsa_w1_v, nsa_w2_v, dif_qn, dif_kn, dif_lq1, dif_lk1, dif_lq2, dif_lk2,
              dif_subln, swa_qn, swa_kn, swa_sinks, w_up, w_out, w_router, b_router, w_gu, b_gu, w_dn, b_dn):
    for l in range(DEPTH):
        lam_init = 0.8 - 0.6 * math.exp(-0.3 * l)
        mod = (c @ w_ada[l] + b_ada[l])[:, None, :]
        sh1, sc1, g1, sh2, sc2, g2 = jnp.split(mod, 6, axis=-1)
        h = rms_norm(x, norm1[l]) * (1.0 + sc1) + sh1
        y = mixer_block(h, positions, w_in[l], nsa_qn[l], nsa_kn[l], nsa_pe_k[l], nsa_w1_k[l], nsa_w2_k[l],
                        nsa_pe_v[l], nsa_w1_v[l], nsa_w2_v[l], dif_qn[l], dif_kn[l], dif_lq1[l], dif_lk1[l],
                        dif_lq2[l], dif_lk2[l], dif_subln[l], swa_qn[l], swa_kn[l], swa_sinks[l],
                        w_up[l], w_out[l], lam_init)
        x = x + g1 * y
        h = rms_norm(x, norm2[l]) * (1.0 + sc2) + sh2
        x = x + g2 * moe_ffn(h, w_router[l], b_router[l], w_gu[l], b_gu[l], w_dn[l], b_dn[l])
    return x
```

```python
import functools
import math

import numpy as np
import jax
import jax.numpy as jnp
from jax import lax
from jax.experimental import pallas as pl
from jax.experimental.pallas import tpu as pltpu

F32 = jnp.float32
BF16 = jnp.bfloat16

HEAD_DIM = 64
ROPE_THETA = 10000.0
NORM_EPS = 1e-6
NEG_INF = -1e30
KNOCKOUT = -3e38
N_BRANCH = 4

NSA_HEADS = 4
NSA_CMP_LEN = 32
NSA_CMP_STRIDE = 16
NSA_SEL_BLOCK = 64
NSA_N_SEL = 16
NSA_WINDOW = 512
NSA_FORCED_SCORE = 1e4

SB_HEADS = 4
DIFF_HEADS = 4
DIFF_DIM = 32
SWA_HEADS = 4
SWA_KV_HEADS = 2
SWA_WINDOW = 128

N_EXPERTS = 32
TOP_K = 4
SWIGLU_ALPHA = 1.702
SWIGLU_LIMIT = 7.0

LANES = 128
MOE_ROWS = 512
VMEM_LIMIT = 56 * 1024 * 1024


def _cparams(*sem):
    return pltpu.CompilerParams(dimension_semantics=sem, vmem_limit_bytes=VMEM_LIMIT)


def _dot(a, b):
    return jnp.dot(a, b, preferred_element_type=F32)


def _dot_nt(a, b):
    return lax.dot_general(a, b, (((1,), (1,)), ((), ())), preferred_element_type=F32)


def _iota2(shape, dim):
    return lax.broadcasted_iota(jnp.int32, shape, dim)


def _linear_kernel(x_ref, w_ref, b_ref, o_ref, *, precision):
    o_ref[...] = jnp.dot(x_ref[...], w_ref[...], preferred_element_type=F32,
                         precision=precision) + b_ref[...]


def _linear(x, w, b, tn, precision=None):
    m, k = x.shape
    n = w.shape[1]
    return pl.pallas_call(
        functools.partial(_linear_kernel, precision=precision),
        grid=(n // tn,),
        in_specs=[pl.BlockSpec((m, k), lambda j: (0, 0)),
                  pl.BlockSpec((k, tn), lambda j: (0, j)),
                  pl.BlockSpec((1, tn), lambda j: (0, j))],
        out_specs=pl.BlockSpec((m, tn), lambda j: (0, j)),
        out_shape=jax.ShapeDtypeStruct((m, n), F32),
        compiler_params=_cparams("arbitrary"),
        name="linear",
    )(x, w, b.reshape(1, n))


def _cmp_mlp_kernel(a_ref, b_ref, pe_ref, w2_ref, o_ref):
    hid = jax.nn.gelu(a_ref[...] + b_ref[...] + pe_ref[...])
    o_ref[...] = _dot(hid.astype(BF16), w2_ref[...])


def _compress(t, pe, w1, w2):
    bn, sn, dh = t.shape
    st = NSA_CMP_STRIDE
    half = st * dh
    nb = sn // st
    t16 = t.reshape(bn * nb, half).astype(BF16)
    w1cat = jnp.concatenate([w1[:half], w1[half:]], axis=1).astype(BF16)
    hidden = w1.shape[1]
    ab = _linear(t16, w1cat, jnp.zeros((2 * hidden,), F32), tn=2 * hidden)
    ab = ab.reshape(bn, nb, 2 * hidden)
    a = ab[:, :, :hidden]
    b_next = jnp.concatenate([ab[:, 1:, hidden:], jnp.zeros((bn, 1, hidden), F32)], axis=1)
    pe_term = jnp.dot(pe.reshape(1, NSA_CMP_LEN * dh), w1, precision=lax.Precision.HIGHEST)
    rows = bn * nb
    tm = min(512, rows)
    out = pl.pallas_call(
        _cmp_mlp_kernel,
        grid=(rows // tm,),
        in_specs=[pl.BlockSpec((tm, hidden), lambda i: (i, 0)),
                  pl.BlockSpec((tm, hidden), lambda i: (i, 0)),
                  pl.BlockSpec((1, hidden), lambda i: (0, 0)),
                  pl.BlockSpec((hidden, dh), lambda i: (0, 0))],
        out_specs=pl.BlockSpec((tm, dh), lambda i: (i, 0)),
        out_shape=jax.ShapeDtypeStruct((rows, dh), F32),
        compiler_params=_cparams("parallel"),
        name="cmp_mlp",
    )(a.reshape(rows, hidden), b_next.reshape(rows, hidden), pe_term, w2.astype(BF16))
    return out.reshape(bn, nb, dh)


def _norm_mod(x, g, sc, sh):
    r = lax.rsqrt(jnp.mean(x * x, axis=-1, keepdims=True) + NORM_EPS)
    return (x * r * g) * (1.0 + sc) + sh


def _proj_kernel(x_ref, g_ref, sc_ref, sh_ref, w_ref, o_ref, h_scr):
    @pl.when(pl.program_id(2) == 0)
    def _():
        h_scr[...] = _norm_mod(x_ref[0], g_ref[...], sc_ref[0], sh_ref[0]).astype(BF16)

    o_ref[0] = _dot(h_scr[...], w_ref[...])


def _in_projection(x, g, sc, sh, w, tm=1024, tn=768):
    bn, sn, d = x.shape
    n = w.shape[1]
    return pl.pallas_call(
        _proj_kernel,
        grid=(bn, sn // tm, n // tn),
        in_specs=[pl.BlockSpec((1, tm, d), lambda b, i, j: (b, i, 0)),
                  pl.BlockSpec((1, d), lambda b, i, j: (0, 0)),
                  pl.BlockSpec((1, 1, d), lambda b, i, j: (b, 0, 0)),
                  pl.BlockSpec((1, 1, d), lambda b, i, j: (b, 0, 0)),
                  pl.BlockSpec((d, tn), lambda b, i, j: (0, j))],
        out_specs=pl.BlockSpec((1, tm, tn), lambda b, i, j: (b, i, j)),
        out_shape=jax.ShapeDtypeStruct((bn, sn, n), F32),
        scratch_shapes=[pltpu.VMEM((tm, d), BF16)],
        compiler_params=_cparams("parallel", "parallel", "arbitrary"),
        name="in_proj",
    )(x, g.reshape(1, d), sc, sh, w)


def _banded_kernel(*refs, tile, window, has_sink):
    if has_sink:
        sink_ref, q_ref, kp_ref, kc_ref, vp_ref, vc_ref, o_ref = refs
    else:
        q_ref, kp_ref, kc_ref, vp_ref, vc_ref, o_ref = refs
    i = pl.program_id(2)
    q = q_ref[0, 0]
    row = _iota2((tile, tile), 0)
    col = _iota2((tile, tile), 1)
    t = i * tile + row
    kpos_p = (i - 1) * tile + col
    valid_p = (t - kpos_p < window) & (kpos_p >= 0)
    valid_c = (col <= row) & (row - col < window)
    s_p = jnp.where(valid_p, _dot_nt(q, kp_ref[0, 0]), NEG_INF)
    s_c = jnp.where(valid_c, _dot_nt(q, kc_ref[0, 0]), NEG_INF)
    m = jnp.maximum(jnp.max(s_p, axis=-1, keepdims=True), jnp.max(s_c, axis=-1, keepdims=True))
    if has_sink:
        sink = sink_ref[pl.program_id(1)]
        m = jnp.maximum(m, sink)
    p_p = jnp.where(valid_p, jnp.exp(s_p - m), 0.0)
    p_c = jnp.where(valid_c, jnp.exp(s_c - m), 0.0)
    den = jnp.sum(p_p, axis=-1, keepdims=True) + jnp.sum(p_c, axis=-1, keepdims=True)
    if has_sink:
        den = den + jnp.exp(sink - m)
    else:
        den = jnp.maximum(den, 1e-30)
    o = _dot(p_p.astype(BF16), vp_ref[0, 0]) + _dot(p_c.astype(BF16), vc_ref[0, 0])
    o_ref[0, 0] = o / den


def _banded_attention(q, k, v, window, tile, sinks=None):
    bn, hq, sn, d = q.shape
    grp = hq // k.shape[1]
    assert tile >= window and sn % tile == 0
    has_sink = sinks is not None
    qspec = pl.BlockSpec((1, 1, tile, d), lambda b, h, i: (b, h, i, 0))
    prev = pl.BlockSpec((1, 1, tile, d), lambda b, h, i: (b, h // grp, jnp.maximum(i - 1, 0), 0))
    cur = pl.BlockSpec((1, 1, tile, d), lambda b, h, i: (b, h // grp, i, 0))
    in_specs = [qspec, prev, cur, prev, cur]
    args = [q, k, k, v, v]
    if has_sink:
        in_specs = [pl.BlockSpec(memory_space=pltpu.SMEM)] + in_specs
        args = [sinks.astype(F32)] + args
    return pl.pallas_call(
        functools.partial(_banded_kernel, tile=tile, window=window, has_sink=has_sink),
        grid=(bn, hq, sn // tile),
        in_specs=in_specs,
        out_specs=pl.BlockSpec((1, 1, tile, d), lambda b, h, i: (b, h, i, 0)),
        out_shape=jax.ShapeDtypeStruct((bn, hq, sn, d), F32),
        compiler_params=_cparams("parallel", "parallel", "parallel"),
        name="banded_attn",
    )(*args)


def _nsa_cmp_kernel(q_ref, kc_ref, vc_ref, ov_ref, oc_ref, sel_ref, *, tq, n_cmp, n_top, heads):
    i = pl.program_id(1)
    ncp = kc_ref.shape[1]
    nsel = ov_ref.shape[1]
    dh = q_ref.shape[3]
    t = i * tq + _iota2((tq, ncp), 0)
    n = _iota2((tq, ncp), 1)
    valid = (n * NSA_CMP_STRIDE + (NSA_CMP_LEN - 1) <= t) & (n < n_cmp)
    kc = kc_ref[0]
    vc = vc_ref[0]
    psum = jnp.zeros((tq, ncp), F32)
    for h in range(heads):
        s = jnp.where(valid, _dot_nt(q_ref[0, h], kc), NEG_INF)
        m = jnp.max(s, axis=-1, keepdims=True)
        p = jnp.where(valid, jnp.exp(s - m), 0.0)
        p = p / jnp.maximum(jnp.sum(p, axis=-1, keepdims=True), 1e-30)
        oc_ref[0, :, h * dh:(h + 1) * dh] = _dot(p.astype(BF16), vc)
        psum = psum + p
    hi = psum.astype(BF16)
    lo = (psum - hi.astype(F32)).astype(BF16)
    imp = _dot(hi, ov_ref[...]) + _dot(lo, ov_ref[...])

    tt = i * tq + _iota2((tq, nsel), 0)
    blk = _iota2((tq, nsel), 1)
    cur = tt >> (NSA_SEL_BLOCK.bit_length() - 1)
    forced = (blk == 0) | (blk == cur) | (blk == cur - 1)
    valid_s = blk * NSA_SEL_BLOCK <= tt
    score = jnp.where(forced, NSA_FORCED_SCORE, jnp.where(valid_s, imp, -1.0))
    blk_f = blk.astype(F32)

    def pick(_, carry):
        score, sel = carry
        m = jnp.max(score, axis=-1, keepdims=True)
        first = jnp.min(jnp.where(score == m, blk_f, float(nsel)), axis=-1, keepdims=True)
        hit = blk_f == first
        return jnp.where(hit, KNOCKOUT, score), jnp.where(hit, 1.0, sel)

    _, sel = lax.fori_loop(0, n_top, pick, (score, jnp.zeros((tq, nsel), F32)))
    sel_ref[0] = sel.astype(BF16)


def _nsa_cmp(q, kc, vc, overlap, n_cmp, n_top, tq=256):
    bn, heads, sn, dh = q.shape
    ncp = kc.shape[1]
    nsel = overlap.shape[1]
    return pl.pallas_call(
        functools.partial(_nsa_cmp_kernel, tq=tq, n_cmp=n_cmp, n_top=n_top, heads=heads),
        grid=(bn, sn // tq),
        in_specs=[pl.BlockSpec((1, heads, tq, dh), lambda b, i: (b, 0, i, 0)),
                  pl.BlockSpec((1, ncp, dh), lambda b, i: (b, 0, 0)),
                  pl.BlockSpec((1, ncp, dh), lambda b, i: (b, 0, 0)),
                  pl.BlockSpec((ncp, nsel), lambda b, i: (0, 0))],
        out_specs=[pl.BlockSpec((1, tq, heads * dh), lambda b, i: (b, i, 0)),
                   pl.BlockSpec((1, tq, nsel), lambda b, i: (b, i, 0))],
        out_shape=[jax.ShapeDtypeStruct((bn, sn, heads * dh), F32),
                   jax.ShapeDtypeStruct((bn, sn, nsel), BF16)],
        compiler_params=_cparams("parallel", "parallel"),
        name="nsa_cmp_topk",
    )(q, kc, vc, overlap)


def _nsa_sel_kernel(q_ref, k_ref, v_ref, sel_ref, e_ref, o_ref, m_scr, l_scr, acc_scr, *, tq, tk, heads):
    i = pl.program_id(1)
    j = pl.program_id(2)
    dh = q_ref.shape[3]

    @pl.when(j == 0)
    def _():
        m_scr[...] = jnp.full(m_scr.shape, NEG_INF, F32)
        l_scr[...] = jnp.zeros(l_scr.shape, F32)
        acc_scr[...] = jnp.zeros(acc_scr.shape, F32)

    @pl.when(j * tk <= i * tq + (tq - 1))
    def _():
        selm = _dot(sel_ref[0], e_ref[...])
        t = i * tq + _iota2((tq, tk), 0)
        kpos = j * tk + _iota2((tq, tk), 1)
        valid = (selm > 0.5) & (kpos <= t)
        k = k_ref[0]
        v = v_ref[0]
        for h in range(heads):
            s = jnp.where(valid, _dot_nt(q_ref[0, h], k), NEG_INF)
            m_old = m_scr[h]
            m_new = jnp.maximum(m_old, jnp.max(s, axis=-1, keepdims=True))
            alpha = jnp.exp(m_old - m_new)
            p = jnp.where(valid, jnp.exp(s - m_new), 0.0)
            l_scr[h] = alpha * l_scr[h] + jnp.sum(p, axis=-1, keepdims=True)
            acc_scr[h] = alpha * acc_scr[h] + _dot(p.astype(BF16), v)
            m_scr[h] = m_new

    @pl.when(j == pl.num_programs(2) - 1)
    def _():
        for h in range(heads):
            o_ref[0, :, h * dh:(h + 1) * dh] = acc_scr[h] / jnp.maximum(l_scr[h], 1e-30)


def _nsa_sel(q, k, v, sel, expand, tq=256, tk=512):
    bn, heads, sn, dh = q.shape
    nsel = sel.shape[2]

    def kv_map(b, i, j):
        return (b, jnp.minimum(j, (i * tq + tq - 1) // tk), 0)

    return pl.pallas_call(
        functools.partial(_nsa_sel_kernel, tq=tq, tk=tk, heads=heads),
        grid=(bn, sn // tq, sn // tk),
        in_specs=[pl.BlockSpec((1, heads, tq, dh), lambda b, i, j: (b, 0, i, 0)),
                  pl.BlockSpec((1, tk, dh), kv_map),
                  pl.BlockSpec((1, tk, dh), kv_map),
                  pl.BlockSpec((1, tq, nsel), lambda b, i, j: (b, i, 0)),
                  pl.BlockSpec((nsel, tk), lambda b, i, j: (0, jnp.minimum(j, (i * tq + tq - 1) // tk)))],
        out_specs=pl.BlockSpec((1, tq, heads * dh), lambda b, i, j: (b, i, 0)),
        out_shape=jax.ShapeDtypeStruct((bn, sn, heads * dh), F32),
        scratch_shapes=[pltpu.VMEM((heads, tq, 1), F32), pltpu.VMEM((heads, tq, 1), F32),
                        pltpu.VMEM((heads, tq, dh), F32)],
        compiler_params=_cparams("parallel", "parallel", "arbitrary"),
        name="nsa_selected_attn",
    )(q, k, v, sel, expand)


def _sb_kernel(q_ref, k_ref, v_ref, u_ref, o_ref, carry_scr, acc_scr, *, tq, tk):
    i = pl.program_id(2)
    j = pl.program_id(3)
    last = (i * tq + tq - 1) // tk

    @pl.when(j == 0)
    def _():
        carry_scr[...] = jnp.zeros(carry_scr.shape, F32)
        acc_scr[...] = jnp.zeros(acc_scr.shape, F32)

    @pl.when(j <= last)
    def _():
        jj = last - j
        z = _dot_nt(q_ref[0, 0], k_ref[0, 0])
        t = i * tq + _iota2((tq, tk), 0)
        kpos = jj * tk + _iota2((tq, tk), 1)
        strict = kpos < t
        tail = jnp.log1p(jnp.exp(-jnp.abs(z)))
        log_keep = jnp.where(strict, -(jnp.maximum(z, 0.0) + tail), 0.0)
        hi = log_keep.astype(BF16)
        lo = (log_keep - hi.astype(F32)).astype(BF16)
        after = _dot(hi, u_ref[...]) + _dot(lo, u_ref[...]) + carry_scr[...]
        log_p = -(jnp.maximum(-z, 0.0) + tail)
        a = jnp.where(strict, jnp.exp(log_p + after), 0.0)
        acc_scr[...] += _dot(a.astype(BF16), v_ref[0, 0])
        carry_scr[...] += jnp.sum(log_keep, axis=-1, keepdims=True)

    @pl.when(j == pl.num_programs(3) - 1)
    def _():
        o_ref[0, 0] = acc_scr[...]


def _stick_breaking(q, k, v, tq=256, tk=256):
    bn, heads, sn, dh = q.shape
    u = (np.arange(tk)[:, None] > np.arange(tk)[None, :]).astype(np.float32)
    u = jnp.asarray(u, BF16)

    def kv_map(b, h, i, j):
        return (b, h, jnp.maximum((i * tq + tq - 1) // tk - j, 0), 0)

    return pl.pallas_call(
        functools.partial(_sb_kernel, tq=tq, tk=tk),
        grid=(bn, heads, sn // tq, sn // tk),
        in_specs=[pl.BlockSpec((1, 1, tq, dh), lambda b, h, i, j: (b, h, i, 0)),
                  pl.BlockSpec((1, 1, tk, dh), kv_map),
                  pl.BlockSpec((1, 1, tk, dh), kv_map),
                  pl.BlockSpec((tk, tk), lambda b, h, i, j: (0, 0))],
        out_specs=pl.BlockSpec((1, 1, tq, dh), lambda b, h, i, j: (b, h, i, 0)),
        out_shape=jax.ShapeDtypeStruct((bn, heads, sn, dh), F32),
        scratch_shapes=[pltpu.VMEM((tq, 1), F32), pltpu.VMEM((tq, dh), F32)],
        compiler_params=_cparams("parallel", "parallel", "parallel", "arbitrary"),
        name="stick_breaking_attn",
    )(q, k, v, u)


def _diff_kernel(sc_ref, q_ref, k_ref, v_ref, g_ref, o_ref, m_scr, l_scr, acc_scr, *, tq, tk):
    i = pl.program_id(2)
    j = pl.program_id(3)

    @pl.when(j == 0)
    def _():
        m_scr[...] = jnp.full(m_scr.shape, NEG_INF, F32)
        l_scr[...] = jnp.zeros(l_scr.shape, F32)
        acc_scr[...] = jnp.zeros(acc_scr.shape, F32)

    @pl.when(j * tk <= i * tq + (tq - 1))
    def _():
        t = i * tq + _iota2((tq, tk), 0)
        kpos = j * tk + _iota2((tq, tk), 1)
        valid = kpos <= t
        v = v_ref[0, 0]
        for mi in range(2):
            s = jnp.where(valid, _dot_nt(q_ref[0, mi], k_ref[0, mi]), NEG_INF)
            m_old = m_scr[mi]
            m_new = jnp.maximum(m_old, jnp.max(s, axis=-1, keepdims=True))
            alpha = jnp.exp(m_old - m_new)
            p = jnp.where(valid, jnp.exp(s - m_new), 0.0)
            l_scr[mi] = alpha * l_scr[mi] + jnp.sum(p, axis=-1, keepdims=True)
            acc_scr[mi] = alpha * acc_scr[mi] + _dot(p.astype(BF16), v)
            m_scr[mi] = m_new

    @pl.when(j == pl.num_programs(3) - 1)
    def _():
        lam = sc_ref[0]
        post = sc_ref[1]
        o = (acc_scr[0] / jnp.maximum(l_scr[0], 1e-30)
             - lam * (acc_scr[1] / jnp.maximum(l_scr[1], 1e-30)))
        r = lax.rsqrt(jnp.mean(o * o, axis=-1, keepdims=True) + NORM_EPS)
        o_ref[0, 0] = (o * r * g_ref[...]) * post


def _diff_attention(q, k, v, subln, lam, post, tq=256, tk=512):
    bn, h2, sn, dd = q.shape
    heads = h2 // 2
    dv = v.shape[3]
    scal = jnp.stack([lam, post]).astype(F32)

    def kv_map(b, h, i, j):
        return (b, h, jnp.minimum(j, (i * tq + tq - 1) // tk), 0)

    return pl.pallas_call(
        functools.partial(_diff_kernel, tq=tq, tk=tk),
        grid=(bn, heads, sn // tq, sn // tk),
        in_specs=[pl.BlockSpec(memory_space=pltpu.SMEM),
                  pl.BlockSpec((1, 2, tq, dd), lambda b, h, i, j: (b, h, i, 0)),
                  pl.BlockSpec((1, 2, tk, dd), kv_map),
                  pl.BlockSpec((1, 1, tk, dv), kv_map),
                  pl.BlockSpec((1, dv), lambda b, h, i, j: (0, 0))],
        out_specs=pl.BlockSpec((1, 1, tq, dv), lambda b, h, i, j: (b, h, i, 0)),
        out_shape=jax.ShapeDtypeStruct((bn, heads, sn, dv), F32),
        scratch_shapes=[pltpu.VMEM((2, tq, 1), F32), pltpu.VMEM((2, tq, 1), F32),
                        pltpu.VMEM((2, tq, dv), F32)],
        compiler_params=_cparams("parallel", "parallel", "parallel", "arbitrary"),
        name="diff_attn",
    )(scal, q, k, v, subln.reshape(1, dv).astype(F32))


def _merge_kernel(brg_ref, ga0_ref, ga1_ref, ga2_ref, oc_ref, os_ref, ow_ref, ob_ref, ocd_ref, od_ref,
                  wup_ref, wout_ref, x_ref, g1_ref, o_ref, *, d):
    o_a = (jax.nn.sigmoid(ga0_ref[0]) * oc_ref[0] + jax.nn.sigmoid(ga1_ref[0]) * os_ref[0]
           + jax.nn.sigmoid(ga2_ref[0]) * ow_ref[0])
    branches = (o_a, ob_ref[0], ocd_ref[0], od_ref[0])
    merged = None
    for bi, o in enumerate(branches):
        gate = jax.nn.sigmoid(brg_ref[0, :, bi * d:(bi + 1) * d])
        term = gate * _dot(o.astype(BF16), wup_ref[bi])
        merged = term if merged is None else merged + term
    y = _dot(merged.astype(BF16), wout_ref[...])
    o_ref[0] = x_ref[0] + g1_ref[0] * y


def _merge(proj, o_c, o_s, o_w, o_b, o_cd, o_d, w_up, w_out, x, g1, ga_col, tm=256):
    bn, sn, d = x.shape
    bw = o_b.shape[2]
    assert ga_col % bw == 0
    gblk = ga_col // bw
    row = lambda b, i: (b, i, 0)
    bspec = pl.BlockSpec((1, tm, bw), row)
    return pl.pallas_call(
        functools.partial(_merge_kernel, d=d),
        grid=(bn, sn // tm),
        in_specs=[pl.BlockSpec((1, tm, N_BRANCH * d), row),
                  pl.BlockSpec((1, tm, bw), lambda b, i: (b, i, gblk)),
                  pl.BlockSpec((1, tm, bw), lambda b, i: (b, i, gblk + 1)),
                  pl.BlockSpec((1, tm, bw), lambda b, i: (b, i, gblk + 2)),
                  bspec, bspec, bspec, bspec, bspec, bspec,
                  pl.BlockSpec((N_BRANCH, bw, d), lambda b, i: (0, 0, 0)),
                  pl.BlockSpec((d, d), lambda b, i: (0, 0)),
                  pl.BlockSpec((1, tm, d), row),
                  pl.BlockSpec((1, 1, d), lambda b, i: (b, 0, 0))],
        out_specs=pl.BlockSpec((1, tm, d), row),
        out_shape=jax.ShapeDtypeStruct((bn, sn, d), F32),
        compiler_params=_cparams("parallel", "parallel"),
        name="branch_merge",
    )(proj, proj, proj, proj, o_c, o_s, o_w, o_b, o_cd, o_d, w_up, w_out, x, g1)


def _router_kernel(x_ref, g_ref, sc_ref, sh_ref, wr_ref, br_ref, h_ref, e_ref, w_ref):
    h = _norm_mod(x_ref[0], g_ref[...], sc_ref[0], sh_ref[0])
    h_ref[0] = h.astype(BF16)
    logits = jnp.dot(h, wr_ref[...], preferred_element_type=F32,
                     precision=lax.Precision.HIGHEST) + br_ref[...]
    lane = _iota2(logits.shape, 1)
    lane_f = lane.astype(F32)
    cur = logits
    vals, idxs = [], []
    for _ in range(TOP_K):
        m = jnp.max(cur, axis=-1, keepdims=True)
        first = jnp.min(jnp.where(cur == m, lane_f, float(LANES)), axis=-1, keepdims=True)
        vals.append(m)
        idxs.append(first)
        cur = jnp.where(lane_f == first, KNOCKOUT, cur)
    exps = [jnp.exp(v - vals[0]) for v in vals]
    den = exps[0]
    for e in exps[1:]:
        den = den + e
    e_out = jnp.zeros(logits.shape, F32)
    w_out = jnp.zeros(logits.shape, F32)
    for k in range(TOP_K):
        e_out = jnp.where(lane == k, idxs[k], e_out)
        w_out = jnp.where(lane == k, exps[k] / den, w_out)
    e_ref[0] = e_out.astype(jnp.int32)
    w_ref[0] = w_out


def _router(x, g, sc, sh, w_router, b_router, tm=512):
    bn, sn, d = x.shape
    ne = w_router.shape[1]
    wr = jnp.zeros((d, LANES), F32).at[:, :ne].set(w_router)
    br = jnp.full((1, LANES), NEG_INF, F32).at[0, :ne].set(b_router)
    row = lambda b, i: (b, i, 0)
    return pl.pallas_call(
        _router_kernel,
        grid=(bn, sn // tm),
        in_specs=[pl.BlockSpec((1, tm, d), row),
                  pl.BlockSpec((1, d), lambda b, i: (0, 0)),
                  pl.BlockSpec((1, 1, d), lambda b, i: (b, 0, 0)),
                  pl.BlockSpec((1, 1, d), lambda b, i: (b, 0, 0)),
                  pl.BlockSpec((d, LANES), lambda b, i: (0, 0)),
                  pl.BlockSpec((1, LANES), lambda b, i: (0, 0))],
        out_specs=[pl.BlockSpec((1, tm, d), row),
                   pl.BlockSpec((1, tm, LANES), row),
                   pl.BlockSpec((1, tm, LANES), row)],
        out_shape=[jax.ShapeDtypeStruct((bn, sn, d), BF16),
                   jax.ShapeDtypeStruct((bn, sn, LANES), jnp.int32),
                   jax.ShapeDtypeStruct((bn, sn, LANES), F32)],
        compiler_params=_cparams("parallel", "parallel"),
        name="moe_router",
    )(x, g.reshape(1, d), sc, sh, wr, br)


def _expert_kernel(ce_ref, x_ref, wgu_ref, bgu_ref, wdn_ref, bdn_ref, rw_ref, o_ref, *, ff, fc):
    del ce_ref
    x = x_ref[...]
    y = None
    for c in range(ff // fc):
        g = _dot(x, wgu_ref[0, :, c * fc:(c + 1) * fc]) + bgu_ref[0, :, c * fc:(c + 1) * fc]
        u = _dot(x, wgu_ref[0, :, ff + c * fc:ff + (c + 1) * fc]) + bgu_ref[0, :, ff + c * fc:ff + (c + 1) * fc]
        g = jnp.minimum(g, SWIGLU_LIMIT)
        u = jnp.clip(u, -SWIGLU_LIMIT, SWIGLU_LIMIT)
        act = g * jax.nn.sigmoid(SWIGLU_ALPHA * g) * (u + 1.0)
        part = _dot(act.astype(BF16), wdn_ref[0, c * fc:(c + 1) * fc, :])
        y = part if y is None else y + part
    o_ref[...] = (y + bdn_ref[0]) * rw_ref[...]


def _expert_ffn(rows, chunk_e, w_gu, b_gu, w_dn, b_dn, row_w, tm=MOE_ROWS, fc=512):
    n_rows, d = rows.shape
    ne, _, ff2 = w_gu.shape
    ff = ff2 // 2
    grid_spec = pltpu.PrefetchScalarGridSpec(
        num_scalar_prefetch=1,
        grid=(n_rows // tm,),
        in_specs=[pl.BlockSpec((tm, d), lambda c, ce: (c, 0)),
                  pl.BlockSpec((1, d, ff2), lambda c, ce: (ce[c], 0, 0)),
                  pl.BlockSpec((1, 1, ff2), lambda c, ce: (ce[c], 0, 0)),
                  pl.BlockSpec((1, ff, d), lambda c, ce: (ce[c], 0, 0)),
                  pl.BlockSpec((1, 1, d), lambda c, ce: (ce[c], 0, 0)),
                  pl.BlockSpec((tm, 1), lambda c, ce: (c, 0))],
        out_specs=pl.BlockSpec((tm, d), lambda c, ce: (c, 0)),
    )
    return pl.pallas_call(
        functools.partial(_expert_kernel, ff=ff, fc=fc),
        grid_spec=grid_spec,
        out_shape=jax.ShapeDtypeStruct((n_rows, d), F32),
        compiler_params=_cparams("arbitrary"),
        name="moe_expert_ffn",
    )(chunk_e, rows, w_gu, b_gu.reshape(ne, 1, ff2), w_dn, b_dn.reshape(ne, 1, d), row_w.reshape(n_rows, 1))


def _combine_kernel(y_ref, x_ref, g2_ref, o_ref, *, d):
    y = y_ref[0]
    tot = y[:, 0:d]
    for k in range(1, TOP_K):
        tot = tot + y[:, k * d:(k + 1) * d]
    o_ref[0] = x_ref[0] + g2_ref[0] * tot


def _combine(y4, x, g2, tm=512):
    bn, sn, d = x.shape
    row = lambda b, i: (b, i, 0)
    return pl.pallas_call(
        functools.partial(_combine_kernel, d=d),
        grid=(bn, sn // tm),
        in_specs=[pl.BlockSpec((1, tm, TOP_K * d), row),
                  pl.BlockSpec((1, tm, d), row),
                  pl.BlockSpec((1, 1, d), lambda b, i: (b, 0, 0))],
        out_specs=pl.BlockSpec((1, tm, d), row),
        out_shape=jax.ShapeDtypeStruct((bn, sn, d), F32),
        compiler_params=_cparams("parallel", "parallel"),
        name="moe_combine",
    )(y4, x, g2)


def _moe(x, g, sc, sh, g2, w_router, b_router, w_gu, b_gu, w_dn, b_dn):
    bn, sn, d = x.shape
    n_tok = bn * sn
    n_asg = n_tok * TOP_K
    tm = MOE_ROWS
    h, e_out, w_out = _router(x, g, sc, sh, w_router, b_router)
    e_flat = e_out[:, :, :TOP_K].reshape(-1)
    w_flat = w_out[:, :, :TOP_K].reshape(-1)
    tok_flat = jnp.arange(n_asg, dtype=jnp.int32) // TOP_K
    order = jnp.argsort(e_flat)
    e_sorted = e_flat[order]
    counts = jnp.bincount(e_flat, length=N_EXPERTS)
    starts = jnp.cumsum(counts) - counts
    padded = (counts + tm - 1) // tm * tm
    pad_ends = jnp.cumsum(padded)
    pad_starts = pad_ends - padded
    dest = (pad_starts[e_sorted] + jnp.arange(n_asg, dtype=jnp.int32) - starts[e_sorted]).astype(jnp.int32)
    n_chunks = n_asg // tm + N_EXPERTS
    n_rows = n_chunks * tm
    row_tok = jnp.full((n_rows,), n_tok, jnp.int32).at[dest].set(tok_flat[order])
    row_w = jnp.zeros((n_rows,), F32).at[dest].set(w_flat[order])
    pos = jnp.zeros((n_asg,), jnp.int32).at[order].set(dest)
    h_pad = jnp.concatenate([h.reshape(n_tok, d), jnp.zeros((1, d), BF16)], axis=0)
    rows = h_pad[row_tok]
    chunk_e = jnp.minimum(jnp.searchsorted(pad_ends, jnp.arange(n_chunks, dtype=jnp.int32) * tm, side='right'),
                          N_EXPERTS - 1).astype(jnp.int32)
    y = _expert_ffn(rows, chunk_e, w_gu, b_gu, w_dn, b_dn, row_w)
    y4 = y[pos].reshape(bn, sn, TOP_K * d)
    return _combine(y4, x, g2)


def _rms(x, g):
    return x * lax.rsqrt(jnp.mean(x * x, axis=-1, keepdims=True) + NORM_EPS) * g


def _rope(x, pos):
    d = x.shape[-1]
    half = d // 2
    inv = ROPE_THETA ** (-jnp.arange(half, dtype=F32) * 2.0 / d)
    ang = pos.astype(F32)[..., None] * inv
    cos = jnp.cos(ang)[:, :, None, :]
    sin = jnp.sin(ang)[:, :, None, :]
    x1, x2 = x[..., :half], x[..., half:]
    return jnp.concatenate([x1 * cos - x2 * sin, x2 * cos + x1 * sin], axis=-1)


def _head_major(t):
    return t.transpose(0, 2, 1, 3)


def _layer_columns(d):
    cols = {}
    off = 0
    for name, width in (("br_g", N_BRANCH * d), ("ga", 3 * NSA_HEADS * HEAD_DIM), ("a_q", 256),
                        ("a_kc", 64), ("a_vc", 64), ("a_ks", 64), ("a_vs", 64), ("a_kw", 64), ("a_vw", 64),
                        ("b_q", 256), ("b_k", 256), ("b_v", 256), ("c_q", 256), ("c_k", 256), ("c_v", 256),
                        ("d_q", 256), ("d_k", 128), ("d_v", 128)):
        cols[name] = (off, width)
        off += width
    return cols, off


def _reorder_w_in(w_in, d, n_pad):
    ref_splits = (256, 64, 64, 64, 64, 64, 64, 12, 256, 256, 256, 256, 256, 256, 256, 128, 128, N_BRANCH * d)
    names = ("a_q", "a_kc", "a_vc", "a_ks", "a_vs", "a_kw", "a_vw", "a_g", "b_q", "b_k", "b_v",
             "c_q", "c_k", "c_v", "d_q", "d_k", "d_v", "br_g")
    starts = np.cumsum((0,) + ref_splits)
    src = {n: (int(starts[i]), ref_splits[i]) for i, n in enumerate(names)}
    cols, total = _layer_columns(d)
    idx = np.zeros((n_pad,), np.int32)
    keep = np.zeros((n_pad,), np.float32)
    for name, (off, width) in cols.items():
        if name == "ga":
            g0 = src["a_g"][0]
            for j in range(3):
                for h in range(NSA_HEADS):
                    base = off + j * NSA_HEADS * HEAD_DIM + h * HEAD_DIM
                    idx[base:base + HEAD_DIM] = g0 + h * 3 + j
        else:
            idx[off:off + width] = src[name][0] + np.arange(width)
        keep[off:off + width] = 1.0
    w = w_in[:, idx] * keep[None, :]
    return w.astype(BF16)


def _mixer_layer(x, positions, mod, p, lam_init):
    bn, sn, d = x.shape
    sh1, sc1, g1 = mod[0], mod[1], mod[2]
    cols, total = _layer_columns(d)
    n_pad = -(-total // 768) * 768
    w_in = _reorder_w_in(p["w_in"], d, n_pad)
    proj = _in_projection(x, p["norm1"], sc1, sh1, w_in)

    def col(name):
        off, width = cols[name]
        return proj[:, :, off:off + width]

    hd = HEAD_DIM
    scale = hd ** -0.5
    a_q = _rope(_rms(col("a_q").reshape(bn, sn, NSA_HEADS, hd), p["nsa_qn"]), positions)
    a_q = _head_major(a_q * scale).astype(BF16)
    kc = _rope(_rms(col("a_kc").reshape(bn, sn, 1, hd), p["nsa_kn"][0]), positions)[:, :, 0]
    ks = _rope(_rms(col("a_ks").reshape(bn, sn, 1, hd), p["nsa_kn"][1]), positions)[:, :, 0]
    kw = _rope(_rms(col("a_kw").reshape(bn, sn, 1, hd), p["nsa_kn"][2]), positions)[:, :, 0]
    k_cmp = _compress(kc, p["nsa_pe_k"], p["nsa_w1_k"], p["nsa_w2_k"]).astype(BF16)
    v_cmp = _compress(col("a_vc"), p["nsa_pe_v"], p["nsa_w1_v"], p["nsa_w2_v"]).astype(BF16)
    n_cmp = (sn - NSA_CMP_LEN) // NSA_CMP_STRIDE + 1
    n_sel = sn // NSA_SEL_BLOCK
    n_top = min(NSA_N_SEL, n_sel)
    cmp_start = np.arange(sn // NSA_CMP_STRIDE) * NSA_CMP_STRIDE
    sel_start = np.arange(n_sel) * NSA_SEL_BLOCK
    overlap = ((cmp_start[:, None] <= (sel_start + NSA_SEL_BLOCK - 1)[None, :]) &
               ((cmp_start + NSA_CMP_LEN - 1)[:, None] >= sel_start[None, :]) &
               (np.arange(sn // NSA_CMP_STRIDE) < n_cmp)[:, None]).astype(np.float32)
    o_c, sel = _nsa_cmp(a_q, k_cmp, v_cmp, jnp.asarray(overlap, BF16), n_cmp, n_top)
    expand = (np.arange(n_sel)[:, None] == (np.arange(sn) // NSA_SEL_BLOCK)[None, :]).astype(np.float32)
    o_s = _nsa_sel(a_q, ks.astype(BF16), col("a_vs").astype(BF16), sel, jnp.asarray(expand, BF16))
    o_w = _banded_attention(a_q, kw.astype(BF16)[:, None], col("a_vw").astype(BF16)[:, None],
                            NSA_WINDOW, NSA_WINDOW)
    o_w = _head_major(o_w).reshape(bn, sn, NSA_HEADS * hd)
    b_q = _head_major(col("b_q").reshape(bn, sn, SB_HEADS, hd) * scale).astype(BF16)
    b_k = _head_major(col("b_k").reshape(bn, sn, SB_HEADS, hd)).astype(BF16)
    b_v = _head_major(col("b_v").reshape(bn, sn, SB_HEADS, hd)).astype(BF16)
    o_b = _head_major(_stick_breaking(b_q, b_k, b_v)).reshape(bn, sn, SB_HEADS * hd)
    dd = DIFF_DIM
    c_q = _rope(_rms(col("c_q").reshape(bn, sn, DIFF_HEADS * 2, dd), p["dif_qn"]), positions)
    c_k = _rope(_rms(col("c_k").reshape(bn, sn, DIFF_HEADS * 2, dd), p["dif_kn"]), positions)
    c_q = _head_major(c_q * dd ** -0.5).astype(BF16)
    c_k = _head_major(c_k).astype(BF16)
    c_v = _head_major(col("c_v").reshape(bn, sn, DIFF_HEADS, 2 * dd)).astype(BF16)
    lam = (jnp.exp(jnp.sum(p["dif_lq1"] * p["dif_lk1"])) - jnp.exp(jnp.sum(p["dif_lq2"] * p["dif_lk2"]))
           + lam_init)
    o_cd = _diff_attention(c_q, c_k, c_v, p["dif_subln"], lam, jnp.asarray(1.0 - lam_init, F32))
    o_cd = _head_major(o_cd).reshape(bn, sn, DIFF_HEADS * 2 * dd)
    d_q = _rope(_rms(col("d_q").reshape(bn, sn, SWA_HEADS, hd), p["swa_qn"]), positions)
    d_k = _rope(_rms(col("d_k").reshape(bn, sn, SWA_KV_HEADS, hd), p["swa_kn"]), positions)
    d_q = _head_major(d_q * scale).astype(BF16)
    d_k = _head_major(d_k).astype(BF16)
    d_v = _head_major(col("d_v").reshape(bn, sn, SWA_KV_HEADS, hd)).astype(BF16)
    o_d = _banded_attention(d_q, d_k, d_v, SWA_WINDOW, 2 * SWA_WINDOW, sinks=p["swa_sinks"])
    o_d = _head_major(o_d).reshape(bn, sn, SWA_HEADS * hd)
    return _merge(proj, o_c, o_s, o_w, o_b, o_cd, o_d, p["w_up"].astype(BF16), p["w_out"].astype(BF16),
                  x, g1, cols["ga"][0])


def kernel(x, c, positions, w_ada, b_ada, norm1, norm2, w_in, nsa_qn, nsa_kn, nsa_pe_k, nsa_w1_k, nsa_w2_k,
           nsa_pe_v, nsa_w1_v, nsa_w2_v, dif_qn, dif_kn, dif_lq1, dif_lk1, dif_lq2, dif_lk2, dif_subln,
           swa_qn, swa_kn, swa_sinks, w_up, w_out, w_router, b_router, w_gu, b_gu, w_dn, b_dn):
    bn, sn, d = x.shape
    depth = w_ada.shape[0]
    c_pad = jnp.zeros((8, d), F32).at[:bn].set(c)
    for l in range(depth):
        lam_init = 0.8 - 0.6 * math.exp(-0.3 * l)
        mod = _linear(c_pad, w_ada[l], b_ada[l], tn=512, precision=lax.Precision.HIGHEST)[:bn]
        mod = mod.reshape(bn, 6, 1, d).transpose(1, 0, 2, 3)
        p = dict(norm1=norm1[l], w_in=w_in[l], nsa_qn=nsa_qn[l], nsa_kn=nsa_kn[l], nsa_pe_k=nsa_pe_k[l],
                 nsa_w1_k=nsa_w1_k[l], nsa_w2_k=nsa_w2_k[l], nsa_pe_v=nsa_pe_v[l], nsa_w1_v=nsa_w1_v[l],
                 nsa_w2_v=nsa_w2_v[l], dif_qn=dif_qn[l], dif_kn=dif_kn[l], dif_lq1=dif_lq1[l],
                 dif_lk1=dif_lk1[l], dif_lq2=dif_lq2[l], dif_lk2=dif_lk2[l], dif_subln=dif_subln[l],
                 swa_qn=swa_qn[l], swa_kn=swa_kn[l], swa_sinks=swa_sinks[l], w_up=w_up[l], w_out=w_out[l])
        x = _mixer_layer(x, positions, mod, p, lam_init)
        x = _moe(x, norm2[l], mod[4], mod[3], mod[5], w_router[l], b_router[l],
                 w_gu[l].astype(BF16), b_gu[l], w_dn[l].astype(BF16), b_dn[l])
    return x
```

```python
import functools
import math

import numpy as np
import jax
import jax.numpy as jnp
from jax import lax
from jax.experimental import pallas as pl
from jax.experimental.pallas import tpu as pltpu

F32 = jnp.float32
BF16 = jnp.bfloat16

HEAD_DIM = 64
ROPE_THETA = 10000.0
NORM_EPS = 1e-6
NEG_INF = -1e30
KNOCKOUT = -3e38
N_BRANCH = 4

NSA_HEADS = 4
NSA_CMP_LEN = 32
NSA_CMP_STRIDE = 16
NSA_SEL_BLOCK = 64
NSA_N_SEL = 16
NSA_WINDOW = 512
NSA_FORCED_SCORE = 1e4

SB_HEADS = 4
DIFF_HEADS = 4
DIFF_DIM = 32
SWA_HEADS = 4
SWA_KV_HEADS = 2
SWA_WINDOW = 128

N_EXPERTS = 32
TOP_K = 4
SWIGLU_ALPHA = 1.702
SWIGLU_LIMIT = 7.0

LANES = 128
LOG2E = 1.4426950408889634
MOE_ROWS = 512
VMEM_LIMIT = 56 * 1024 * 1024


def _cparams(*sem):
    return pltpu.CompilerParams(dimension_semantics=sem, vmem_limit_bytes=VMEM_LIMIT)


def _dot(a, b):
    return jnp.dot(a, b, preferred_element_type=F32)


def _dot_nt(a, b):
    return lax.dot_general(a, b, (((1,), (1,)), ((), ())), preferred_element_type=F32)


def _iota2(shape, dim):
    return lax.broadcasted_iota(jnp.int32, shape, dim)


def _linear_kernel(x_ref, w_ref, b_ref, o_ref, *, precision):
    o_ref[...] = jnp.dot(x_ref[...], w_ref[...], preferred_element_type=F32,
                         precision=precision) + b_ref[...]


def _linear(x, w, b, tn, precision=None):
    m, k = x.shape
    n = w.shape[1]
    return pl.pallas_call(
        functools.partial(_linear_kernel, precision=precision),
        grid=(n // tn,),
        in_specs=[pl.BlockSpec((m, k), lambda j: (0, 0)),
                  pl.BlockSpec((k, tn), lambda j: (0, j)),
                  pl.BlockSpec((1, tn), lambda j: (0, j))],
        out_specs=pl.BlockSpec((m, tn), lambda j: (0, j)),
        out_shape=jax.ShapeDtypeStruct((m, n), F32),
        compiler_params=_cparams("arbitrary"),
        name="linear",
    )(x, w, b.reshape(1, n))


def _cmp_mlp_kernel(a_ref, b_ref, pe_ref, w2_ref, o_ref):
    hid = jax.nn.gelu(a_ref[...] + b_ref[...] + pe_ref[...])
    o_ref[...] = _dot(hid.astype(BF16), w2_ref[...])


def _compress(t, pe, w1, w2):
    bn, sn, dh = t.shape
    st = NSA_CMP_STRIDE
    half = st * dh
    nb = sn // st
    t16 = t.reshape(bn * nb, half).astype(BF16)
    w1cat = jnp.concatenate([w1[:half], w1[half:]], axis=1).astype(BF16)
    hidden = w1.shape[1]
    ab = _linear(t16, w1cat, jnp.zeros((2 * hidden,), F32), tn=2 * hidden)
    ab = ab.reshape(bn, nb, 2 * hidden)
    a = ab[:, :, :hidden]
    b_next = jnp.concatenate([ab[:, 1:, hidden:], jnp.zeros((bn, 1, hidden), F32)], axis=1)
    pe_term = jnp.dot(pe.reshape(1, NSA_CMP_LEN * dh), w1, precision=lax.Precision.HIGHEST)
    rows = bn * nb
    tm = min(512, rows)
    out = pl.pallas_call(
        _cmp_mlp_kernel,
        grid=(rows // tm,),
        in_specs=[pl.BlockSpec((tm, hidden), lambda i: (i, 0)),
                  pl.BlockSpec((tm, hidden), lambda i: (i, 0)),
                  pl.BlockSpec((1, hidden), lambda i: (0, 0)),
                  pl.BlockSpec((hidden, dh), lambda i: (0, 0))],
        out_specs=pl.BlockSpec((tm, dh), lambda i: (i, 0)),
        out_shape=jax.ShapeDtypeStruct((rows, dh), F32),
        compiler_params=_cparams("parallel"),
        name="cmp_mlp",
    )(a.reshape(rows, hidden), b_next.reshape(rows, hidden), pe_term, w2.astype(BF16))
    return out.reshape(bn, nb, dh)


def _norm_mod(x, g, sc, sh):
    r = lax.rsqrt(jnp.mean(x * x, axis=-1, keepdims=True) + NORM_EPS)
    return (x * r * g) * (1.0 + sc) + sh


def _proj_kernel(x_ref, g_ref, sc_ref, sh_ref, w_ref, o_ref, h_scr):
    @pl.when(pl.program_id(2) == 0)
    def _():
        h_scr[...] = _norm_mod(x_ref[0], g_ref[...], sc_ref[0], sh_ref[0]).astype(BF16)

    o_ref[0] = _dot(h_scr[...], w_ref[...])


def _in_projection(x, g, sc, sh, w, tm=1024, tn=768):
    bn, sn, d = x.shape
    n = w.shape[1]
    return pl.pallas_call(
        _proj_kernel,
        grid=(bn, sn // tm, n // tn),
        in_specs=[pl.BlockSpec((1, tm, d), lambda b, i, j: (b, i, 0)),
                  pl.BlockSpec((1, d), lambda b, i, j: (0, 0)),
                  pl.BlockSpec((1, 1, d), lambda b, i, j: (b, 0, 0)),
                  pl.BlockSpec((1, 1, d), lambda b, i, j: (b, 0, 0)),
                  pl.BlockSpec((d, tn), lambda b, i, j: (0, j))],
        out_specs=pl.BlockSpec((1, tm, tn), lambda b, i, j: (b, i, j)),
        out_shape=jax.ShapeDtypeStruct((bn, sn, n), F32),
        scratch_shapes=[pltpu.VMEM((tm, d), BF16)],
        compiler_params=_cparams("parallel", "parallel", "arbitrary"),
        name="in_proj",
    )(x, g.reshape(1, d), sc, sh, w)


def _banded_kernel(*refs, tile, window, has_sink):
    if has_sink:
        sink_ref, q_ref, kp_ref, kc_ref, vp_ref, vc_ref, o_ref = refs
    else:
        q_ref, kp_ref, kc_ref, vp_ref, vc_ref, o_ref = refs
    i = pl.program_id(2)
    q = q_ref[0, 0]
    row = _iota2((tile, tile), 0)
    col = _iota2((tile, tile), 1)
    t = i * tile + row
    kpos_p = (i - 1) * tile + col
    valid_p = (t - kpos_p < window) & (kpos_p >= 0)
    valid_c = (col <= row) & (row - col < window)
    s_p = jnp.where(valid_p, _dot_nt(q, kp_ref[0, 0]), NEG_INF)
    s_c = jnp.where(valid_c, _dot_nt(q, kc_ref[0, 0]), NEG_INF)
    m = jnp.maximum(jnp.max(s_p, axis=-1, keepdims=True), jnp.max(s_c, axis=-1, keepdims=True))
    if has_sink:
        sink = sink_ref[pl.program_id(1)]
        m = jnp.maximum(m, sink)
    p_p = jnp.where(valid_p, jnp.exp(s_p - m), 0.0)
    p_c = jnp.where(valid_c, jnp.exp(s_c - m), 0.0)
    den = jnp.sum(p_p, axis=-1, keepdims=True) + jnp.sum(p_c, axis=-1, keepdims=True)
    if has_sink:
        den = den + jnp.exp(sink - m)
    else:
        den = jnp.maximum(den, 1e-30)
    o = _dot(p_p.astype(BF16), vp_ref[0, 0]) + _dot(p_c.astype(BF16), vc_ref[0, 0])
    o_ref[0, 0] = o / den


def _banded_attention(q, k, v, window, tile, sinks=None):
    bn, hq, sn, d = q.shape
    grp = hq // k.shape[1]
    assert tile >= window and sn % tile == 0
    has_sink = sinks is not None
    qspec = pl.BlockSpec((1, 1, tile, d), lambda b, h, i: (b, h, i, 0))
    prev = pl.BlockSpec((1, 1, tile, d), lambda b, h, i: (b, h // grp, jnp.maximum(i - 1, 0), 0))
    cur = pl.BlockSpec((1, 1, tile, d), lambda b, h, i: (b, h // grp, i, 0))
    in_specs = [qspec, prev, cur, prev, cur]
    args = [q, k, k, v, v]
    if has_sink:
        in_specs = [pl.BlockSpec(memory_space=pltpu.SMEM)] + in_specs
        args = [sinks.astype(F32)] + args
    return pl.pallas_call(
        functools.partial(_banded_kernel, tile=tile, window=window, has_sink=has_sink),
        grid=(bn, hq, sn // tile),
        in_specs=in_specs,
        out_specs=pl.BlockSpec((1, 1, tile, d), lambda b, h, i: (b, h, i, 0)),
        out_shape=jax.ShapeDtypeStruct((bn, hq, sn, d), F32),
        compiler_params=_cparams("parallel", "parallel", "parallel"),
        name="banded_attn",
    )(*args)


def _nsa_cmp_kernel(q_ref, kc_ref, vc_ref, ov_ref, oc_ref, sel_ref, *, tq, n_cmp, n_top, heads):
    i = pl.program_id(1)
    ncp = kc_ref.shape[1]
    nsel = ov_ref.shape[1]
    dh = q_ref.shape[3]
    t = i * tq + _iota2((tq, ncp), 0)
    n = _iota2((tq, ncp), 1)
    valid = (n * NSA_CMP_STRIDE + (NSA_CMP_LEN - 1) <= t) & (n < n_cmp)
    kc = kc_ref[0]
    vc = vc_ref[0]
    psum = jnp.zeros((tq, ncp), F32)
    for h in range(heads):
        s = jnp.where(valid, _dot_nt(q_ref[0, h], kc), NEG_INF)
        m = jnp.max(s, axis=-1, keepdims=True)
        p = jnp.where(valid, jnp.exp(s - m), 0.0)
        p = p / jnp.maximum(jnp.sum(p, axis=-1, keepdims=True), 1e-30)
        oc_ref[0, :, h * dh:(h + 1) * dh] = _dot(p.astype(BF16), vc)
        psum = psum + p
    hi = psum.astype(BF16)
    lo = (psum - hi.astype(F32)).astype(BF16)
    imp = _dot(hi, ov_ref[...]) + _dot(lo, ov_ref[...])

    tt = i * tq + _iota2((tq, nsel), 0)
    blk = _iota2((tq, nsel), 1)
    cur = tt >> (NSA_SEL_BLOCK.bit_length() - 1)
    forced = (blk == 0) | (blk == cur) | (blk == cur - 1)
    valid_s = blk * NSA_SEL_BLOCK <= tt
    score = jnp.where(forced, NSA_FORCED_SCORE, jnp.where(valid_s, imp, -1.0))
    blk_f = blk.astype(F32)

    def pick(_, carry):
        score, sel = carry
        m = jnp.max(score, axis=-1, keepdims=True)
        first = jnp.min(jnp.where(score == m, blk_f, float(nsel)), axis=-1, keepdims=True)
        hit = blk_f == first
        return jnp.where(hit, KNOCKOUT, score), jnp.where(hit, 0.0, sel)

    _, bias = lax.fori_loop(0, n_top, pick, (score, jnp.full((tq, nsel), NEG_INF, F32)))
    sel_ref[0] = bias.astype(BF16)


def _nsa_cmp(q, kc, vc, overlap, n_cmp, n_top, tq=256):
    bn, heads, sn, dh = q.shape
    ncp = kc.shape[1]
    nsel = overlap.shape[1]
    return pl.pallas_call(
        functools.partial(_nsa_cmp_kernel, tq=tq, n_cmp=n_cmp, n_top=n_top, heads=heads),
        grid=(bn, sn // tq),
        in_specs=[pl.BlockSpec((1, heads, tq, dh), lambda b, i: (b, 0, i, 0)),
                  pl.BlockSpec((1, ncp, dh), lambda b, i: (b, 0, 0)),
                  pl.BlockSpec((1, ncp, dh), lambda b, i: (b, 0, 0)),
                  pl.BlockSpec((ncp, nsel), lambda b, i: (0, 0))],
        out_specs=[pl.BlockSpec((1, tq, heads * dh), lambda b, i: (b, i, 0)),
                   pl.BlockSpec((1, tq, nsel), lambda b, i: (b, i, 0))],
        out_shape=[jax.ShapeDtypeStruct((bn, sn, heads * dh), F32),
                   jax.ShapeDtypeStruct((bn, sn, nsel), BF16)],
        compiler_params=_cparams("parallel", "parallel"),
        name="nsa_cmp_topk",
    )(q, kc, vc, overlap)


def _nsa_sel_kernel(q_ref, kx_ref, v_ref, bias_ref, o_ref, qx_scr, *, tq, tk, heads):
    i = pl.program_id(1)
    dh = v_ref.shape[2]
    bias = bias_ref[0]
    for h in range(heads):
        qx_scr[h] = jnp.concatenate([bias, q_ref[0, h]], axis=1)

    def tile(j, state, diagonal):
        start = pl.multiple_of(j * tk, tk)
        kx = kx_ref[0, pl.ds(start, tk), :]
        v = v_ref[0, pl.ds(start, tk), :]
        if diagonal:
            causal = (start + _iota2((tq, tk), 1)) <= (i * tq + _iota2((tq, tk), 0))
        new_state = []
        for h in range(heads):
            m_old, l_old, acc_old = state[h]
            s = _dot_nt(qx_scr[h], kx)
            if diagonal:
                s = jnp.where(causal, s, NEG_INF)
            m_new = jnp.maximum(m_old, jnp.max(s, axis=-1, keepdims=True))
            alpha = jnp.exp(m_old - m_new)
            p = jnp.exp(s - m_new)
            l_new = alpha * l_old + jnp.sum(p, axis=-1, keepdims=True)
            acc_new = alpha * acc_old + _dot(p.astype(BF16), v)
            new_state.append((m_new, l_new, acc_new))
        return tuple(new_state)

    init = tuple((jnp.full((tq, 1), NEG_INF, F32), jnp.zeros((tq, 1), F32), jnp.zeros((tq, dh), F32))
                 for _ in range(heads))
    n_full = (i * tq) // tk
    state = lax.fori_loop(0, n_full, lambda j, st: tile(j, st, False), init)
    state = tile(n_full, state, True)
    for h in range(heads):
        o_ref[0, :, h * dh:(h + 1) * dh] = state[h][2] / jnp.maximum(state[h][1], 1e-30)


def _nsa_sel(q, kx, v, bias, tq=512, tk=512):
    bn, heads, sn, dh = q.shape
    nsel = bias.shape[2]
    tk = min(tk, sn)
    assert tk % tq == 0 and sn % tk == 0
    return pl.pallas_call(
        functools.partial(_nsa_sel_kernel, tq=tq, tk=tk, heads=heads),
        grid=(bn, sn // tq),
        in_specs=[pl.BlockSpec((1, heads, tq, dh), lambda b, i: (b, 0, i, 0)),
                  pl.BlockSpec((1, sn, nsel + dh), lambda b, i: (b, 0, 0)),
                  pl.BlockSpec((1, sn, dh), lambda b, i: (b, 0, 0)),
                  pl.BlockSpec((1, tq, nsel), lambda b, i: (b, i, 0))],
        out_specs=pl.BlockSpec((1, tq, heads * dh), lambda b, i: (b, i, 0)),
        out_shape=jax.ShapeDtypeStruct((bn, sn, heads * dh), F32),
        scratch_shapes=[pltpu.VMEM((heads, tq, nsel + dh), BF16)],
        compiler_params=_cparams("parallel", "arbitrary"),
        name="nsa_selected_attn",
    )(q, kx, v, bias)


def _sb_kernel(q_ref, k_ref, v_ref, u_ref, o_ref, *, tq, tk):
    i = pl.program_id(2)
    q = q_ref[0, 0]
    per_q = tq // tk

    def tile(jj, carry, diagonal):
        start = pl.multiple_of(jj * tk, tk)
        nz = _dot_nt(q, k_ref[0, 0, pl.ds(start, tk), :])
        neg_abs = lax.bitcast_convert_type(lax.bitcast_convert_type(nz, jnp.uint32) | jnp.uint32(0x80000000), F32)
        log_keep = jnp.minimum(nz, 0.0) - jnp.log2(1.0 + jnp.exp2(neg_abs))
        if diagonal:
            strict = (start + _iota2((tq, tk), 1)) < (i * tq + _iota2((tq, tk), 0))
            log_keep = jnp.where(strict, log_keep, 0.0)
        hi = log_keep.astype(BF16)
        lo = (log_keep - hi.astype(F32)).astype(BF16)
        cum = _dot(jnp.concatenate([hi, lo], axis=1), u_ref[...])
        a = jnp.exp2(cum + jnp.concatenate([carry] * (tk // LANES), axis=1) - nz)
        if diagonal:
            a = jnp.where(strict, a, 0.0)
        out = _dot(a.astype(BF16), v_ref[0, 0, pl.ds(start, tk), :])
        return out, carry + jnp.broadcast_to(cum[:, 0:1], carry.shape)

    def group(first, carry, diagonal):
        total = None
        for r in range(per_q):
            out, carry = tile(first - r, carry, diagonal)
            total = out if total is None else total + out
        return total, carry

    acc, carry = group(i * per_q + per_q - 1, jnp.zeros((tq, LANES), F32), True)

    def body(p, state):
        acc, carry = state
        out, carry = group((i - p) * per_q - 1, carry, False)
        return acc + out, carry

    acc, _ = lax.fori_loop(0, i, body, (acc, carry))
    o_ref[0, 0] = acc


def _stick_breaking(q, k, v, tq=512, tk=256):
    bn, heads, sn, dh = q.shape
    incl = (np.arange(tk)[:, None] >= np.arange(tk)[None, :]).astype(np.float32)
    u = jnp.asarray(np.concatenate([incl, incl], axis=0), BF16)
    kv = pl.BlockSpec((1, 1, sn, dh), lambda b, h, i: (b, h, 0, 0))
    return pl.pallas_call(
        functools.partial(_sb_kernel, tq=tq, tk=tk),
        grid=(bn, heads, sn // tq),
        in_specs=[pl.BlockSpec((1, 1, tq, dh), lambda b, h, i: (b, h, i, 0)), kv, kv,
                  pl.BlockSpec((2 * tk, tk), lambda b, h, i: (0, 0))],
        out_specs=pl.BlockSpec((1, 1, tq, dh), lambda b, h, i: (b, h, i, 0)),
        out_shape=jax.ShapeDtypeStruct((bn, heads, sn, dh), F32),
        compiler_params=_cparams("parallel", "parallel", "arbitrary"),
        name="stick_breaking_attn",
    )(q, k, v, u)


def _diff_kernel(sc_ref, q_ref, k_ref, v_ref, g_ref, o_ref, *, tq, tk):
    i = pl.program_id(2)
    dv = v_ref.shape[3]

    def tile(j, state, diagonal):
        start = pl.multiple_of(j * tk, tk)
        v = v_ref[0, 0, pl.ds(start, tk), :]
        if diagonal:
            causal = (start + _iota2((tq, tk), 1)) <= (i * tq + _iota2((tq, tk), 0))
        new_state = []
        for mi in range(2):
            m_old, l_old, acc_old = state[mi]
            s = _dot_nt(q_ref[0, mi], k_ref[0, mi, pl.ds(start, tk), :])
            if diagonal:
                s = jnp.where(causal, s, NEG_INF)
            m_new = jnp.maximum(m_old, jnp.max(s, axis=-1, keepdims=True))
            alpha = jnp.exp(m_old - m_new)
            p = jnp.exp(s - m_new)
            l_new = alpha * l_old + jnp.sum(p, axis=-1, keepdims=True)
            acc_new = alpha * acc_old + _dot(p.astype(BF16), v)
            new_state.append((m_new, l_new, acc_new))
        return tuple(new_state)

    init = tuple((jnp.full((tq, 1), NEG_INF, F32), jnp.zeros((tq, 1), F32), jnp.zeros((tq, dv), F32))
                 for _ in range(2))
    n_full = (i * tq) // tk
    state = lax.fori_loop(0, n_full, lambda j, st: tile(j, st, False), init)
    state = tile(n_full, state, True)
    lam = sc_ref[0]
    post = sc_ref[1]
    o = (state[0][2] / jnp.maximum(state[0][1], 1e-30)
         - lam * (state[1][2] / jnp.maximum(state[1][1], 1e-30)))
    r = lax.rsqrt(jnp.mean(o * o, axis=-1, keepdims=True) + NORM_EPS)
    o_ref[0, 0] = (o * r * g_ref[...]) * post


def _diff_attention(q, k, v, subln, lam, post, tq=512, tk=512):
    bn, h2, sn, dd = q.shape
    heads = h2 // 2
    dv = v.shape[3]
    tk = min(tk, sn)
    assert tk % tq == 0 and sn % tk == 0
    scal = jnp.stack([lam, post]).astype(F32)
    return pl.pallas_call(
        functools.partial(_diff_kernel, tq=tq, tk=tk),
        grid=(bn, heads, sn // tq),
        in_specs=[pl.BlockSpec(memory_space=pltpu.SMEM),
                  pl.BlockSpec((1, 2, tq, dd), lambda b, h, i: (b, h, i, 0)),
                  pl.BlockSpec((1, 2, sn, dd), lambda b, h, i: (b, h, 0, 0)),
                  pl.BlockSpec((1, 1, sn, dv), lambda b, h, i: (b, h, 0, 0)),
                  pl.BlockSpec((1, dv), lambda b, h, i: (0, 0))],
        out_specs=pl.BlockSpec((1, 1, tq, dv), lambda b, h, i: (b, h, i, 0)),
        out_shape=jax.ShapeDtypeStruct((bn, heads, sn, dv), F32),
        compiler_params=_cparams("parallel", "parallel", "arbitrary"),
        name="diff_attn",
    )(scal, q, k, v, subln.reshape(1, dv).astype(F32))


def _merge_kernel(brg_ref, ga0_ref, ga1_ref, ga2_ref, oc_ref, os_ref, ow_ref, ob_ref, ocd_ref, od_ref,
                  wup_ref, wout_ref, x_ref, g1_ref, o_ref, *, d):
    o_a = (jax.nn.sigmoid(ga0_ref[0]) * oc_ref[0] + jax.nn.sigmoid(ga1_ref[0]) * os_ref[0]
           + jax.nn.sigmoid(ga2_ref[0]) * ow_ref[0])
    branches = (o_a, ob_ref[0], ocd_ref[0], od_ref[0])
    merged = None
    for bi, o in enumerate(branches):
        gate = jax.nn.sigmoid(brg_ref[0, :, bi * d:(bi + 1) * d])
        term = gate * _dot(o.astype(BF16), wup_ref[bi])
        merged = term if merged is None else merged + term
    y = _dot(merged.astype(BF16), wout_ref[...])
    o_ref[0] = x_ref[0] + g1_ref[0] * y


def _merge(proj, o_c, o_s, o_w, o_b, o_cd, o_d, w_up, w_out, x, g1, ga_col, tm=256):
    bn, sn, d = x.shape
    bw = o_b.shape[2]
    assert ga_col % bw == 0
    gblk = ga_col // bw
    row = lambda b, i: (b, i, 0)
    bspec = pl.BlockSpec((1, tm, bw), row)
    return pl.pallas_call(
        functools.partial(_merge_kernel, d=d),
        grid=(bn, sn // tm),
        in_specs=[pl.BlockSpec((1, tm, N_BRANCH * d), row),
                  pl.BlockSpec((1, tm, bw), lambda b, i: (b, i, gblk)),
                  pl.BlockSpec((1, tm, bw), lambda b, i: (b, i, gblk + 1)),
                  pl.BlockSpec((1, tm, bw), lambda b, i: (b, i, gblk + 2)),
                  bspec, bspec, bspec, bspec, bspec, bspec,
                  pl.BlockSpec((N_BRANCH, bw, d), lambda b, i: (0, 0, 0)),
                  pl.BlockSpec((d, d), lambda b, i: (0, 0)),
                  pl.BlockSpec((1, tm, d), row),
                  pl.BlockSpec((1, 1, d), lambda b, i: (b, 0, 0))],
        out_specs=pl.BlockSpec((1, tm, d), row),
        out_shape=jax.ShapeDtypeStruct((bn, sn, d), F32),
        compiler_params=_cparams("parallel", "parallel"),
        name="branch_merge",
    )(proj, proj, proj, proj, o_c, o_s, o_w, o_b, o_cd, o_d, w_up, w_out, x, g1)


def _router_kernel(x_ref, g_ref, sc_ref, sh_ref, wr_ref, br_ref, h_ref, e_ref, w_ref):
    h = _norm_mod(x_ref[0], g_ref[...], sc_ref[0], sh_ref[0])
    h_ref[0] = h.astype(BF16)
    logits = jnp.dot(h, wr_ref[...], preferred_element_type=F32,
                     precision=lax.Precision.HIGHEST) + br_ref[...]
    lane = _iota2(logits.shape, 1)
    lane_f = lane.astype(F32)
    cur = logits
    vals, idxs = [], []
    for _ in range(TOP_K):
        m = jnp.max(cur, axis=-1, keepdims=True)
        first = jnp.min(jnp.where(cur == m, lane_f, float(LANES)), axis=-1, keepdims=True)
        vals.append(m)
        idxs.append(first)
        cur = jnp.where(lane_f == first, KNOCKOUT, cur)
    exps = [jnp.exp(v - vals[0]) for v in vals]
    den = exps[0]
    for e in exps[1:]:
        den = den + e
    e_out = jnp.zeros(logits.shape, F32)
    w_out = jnp.zeros(logits.shape, F32)
    for k in range(TOP_K):
        e_out = jnp.where(lane == k, idxs[k], e_out)
        w_out = jnp.where(lane == k, exps[k] / den, w_out)
    e_ref[0] = e_out.astype(jnp.int32)
    w_ref[0] = w_out


def _router(x, g, sc, sh, w_router, b_router, tm=512):
    bn, sn, d = x.shape
    ne = w_router.shape[1]
    wr = jnp.zeros((d, LANES), F32).at[:, :ne].set(w_router)
    br = jnp.full((1, LANES), NEG_INF, F32).at[0, :ne].set(b_router)
    row = lambda b, i: (b, i, 0)
    return pl.pallas_call(
        _router_kernel,
        grid=(bn, sn // tm),
        in_specs=[pl.BlockSpec((1, tm, d), row),
                  pl.BlockSpec((1, d), lambda b, i: (0, 0)),
                  pl.BlockSpec((1, 1, d), lambda b, i: (b, 0, 0)),
                  pl.BlockSpec((1, 1, d), lambda b, i: (b, 0, 0)),
                  pl.BlockSpec((d, LANES), lambda b, i: (0, 0)),
                  pl.BlockSpec((1, LANES), lambda b, i: (0, 0))],
        out_specs=[pl.BlockSpec((1, tm, d), row),
                   pl.BlockSpec((1, tm, LANES), row),
                   pl.BlockSpec((1, tm, LANES), row)],
        out_shape=[jax.ShapeDtypeStruct((bn, sn, d), BF16),
                   jax.ShapeDtypeStruct((bn, sn, LANES), jnp.int32),
                   jax.ShapeDtypeStruct((bn, sn, LANES), F32)],
        compiler_params=_cparams("parallel", "parallel"),
        name="moe_router",
    )(x, g.reshape(1, d), sc, sh, wr, br)


def _expert_kernel(ce_ref, x_ref, wgu_ref, bgu_ref, wdn_ref, bdn_ref, rw_ref, o_ref, *, ff, fc):
    del ce_ref
    x = x_ref[...]
    y = None
    for c in range(ff // fc):
        g = _dot(x, wgu_ref[0, :, c * fc:(c + 1) * fc]) + bgu_ref[0, :, c * fc:(c + 1) * fc]
        u = _dot(x, wgu_ref[0, :, ff + c * fc:ff + (c + 1) * fc]) + bgu_ref[0, :, ff + c * fc:ff + (c + 1) * fc]
        g = jnp.minimum(g, SWIGLU_LIMIT)
        u = jnp.clip(u, -SWIGLU_LIMIT, SWIGLU_LIMIT)
        act = g * jax.nn.sigmoid(SWIGLU_ALPHA * g) * (u + 1.0)
        part = _dot(act.astype(BF16), wdn_ref[0, c * fc:(c + 1) * fc, :])
        y = part if y is None else y + part
    o_ref[...] = (y + bdn_ref[0]) * rw_ref[...]


def _expert_ffn(rows, chunk_e, w_gu, b_gu, w_dn, b_dn, row_w, tm=MOE_ROWS, fc=512):
    n_rows, d = rows.shape
    ne, _, ff2 = w_gu.shape
    ff = ff2 // 2
    grid_spec = pltpu.PrefetchScalarGridSpec(
        num_scalar_prefetch=1,
        grid=(n_rows // tm,),
        in_specs=[pl.BlockSpec((tm, d), lambda c, ce: (c, 0)),
                  pl.BlockSpec((1, d, ff2), lambda c, ce: (ce[c], 0, 0)),
                  pl.BlockSpec((1, 1, ff2), lambda c, ce: (ce[c], 0, 0)),
                  pl.BlockSpec((1, ff, d), lambda c, ce: (ce[c], 0, 0)),
                  pl.BlockSpec((1, 1, d), lambda c, ce: (ce[c], 0, 0)),
                  pl.BlockSpec((tm, 1), lambda c, ce: (c, 0))],
        out_specs=pl.BlockSpec((tm, d), lambda c, ce: (c, 0)),
    )
    return pl.pallas_call(
        functools.partial(_expert_kernel, ff=ff, fc=fc),
        grid_spec=grid_spec,
        out_shape=jax.ShapeDtypeStruct((n_rows, d), F32),
        compiler_params=_cparams("arbitrary"),
        name="moe_expert_ffn",
    )(chunk_e, rows, w_gu, b_gu.reshape(ne, 1, ff2), w_dn, b_dn.reshape(ne, 1, d), row_w.reshape(n_rows, 1))


def _combine_kernel(y_ref, x_ref, g2_ref, o_ref, *, d):
    y = y_ref[0]
    tot = y[:, 0:d]
    for k in range(1, TOP_K):
        tot = tot + y[:, k * d:(k + 1) * d]
    o_ref[0] = x_ref[0] + g2_ref[0] * tot


def _combine(y4, x, g2, tm=512):
    bn, sn, d = x.shape
    row = lambda b, i: (b, i, 0)
    return pl.pallas_call(
        functools.partial(_combine_kernel, d=d),
        grid=(bn, sn // tm),
        in_specs=[pl.BlockSpec((1, tm, TOP_K * d), row),
                  pl.BlockSpec((1, tm, d), row),
                  pl.BlockSpec((1, 1, d), lambda b, i: (b, 0, 0))],
        out_specs=pl.BlockSpec((1, tm, d), row),
        out_shape=jax.ShapeDtypeStruct((bn, sn, d), F32),
        compiler_params=_cparams("parallel", "parallel"),
        name="moe_combine",
    )(y4, x, g2)


def _moe(x, g, sc, sh, g2, w_router, b_router, w_gu, b_gu, w_dn, b_dn):
    bn, sn, d = x.shape
    n_tok = bn * sn
    n_asg = n_tok * TOP_K
    tm = MOE_ROWS
    h, e_out, w_out = _router(x, g, sc, sh, w_router, b_router)
    e_flat = e_out[:, :, :TOP_K].reshape(-1)
    w_flat = w_out[:, :, :TOP_K].reshape(-1)
    order = jnp.argsort(e_flat)
    e_sorted = e_flat[order]
    counts = jnp.bincount(e_flat, length=N_EXPERTS)
    starts = jnp.cumsum(counts) - counts
    padded = (counts + tm - 1) // tm * tm
    pad_ends = jnp.cumsum(padded)
    pad_starts = pad_ends - padded
    dest = (pad_starts[e_sorted] + jnp.arange(n_asg, dtype=jnp.int32) - starts[e_sorted]).astype(jnp.int32)
    n_chunks = n_asg // tm + N_EXPERTS
    n_rows = n_chunks * tm
    chunk_e = jnp.minimum(jnp.searchsorted(pad_ends, jnp.arange(n_chunks, dtype=jnp.int32) * tm, side='right'),
                          N_EXPERTS - 1).astype(jnp.int32)
    r = jnp.arange(n_rows, dtype=jnp.int32)
    row_e = jnp.repeat(chunk_e, tm)
    idx_in_e = r - pad_starts[row_e].astype(jnp.int32)
    row_valid = idx_in_e < counts[row_e]
    src = jnp.clip(starts[row_e].astype(jnp.int32) + idx_in_e, 0, n_asg - 1)
    asg = order[src].astype(jnp.int32)
    row_tok = jnp.where(row_valid, asg // TOP_K, n_tok)
    row_w = jnp.where(row_valid, w_flat[asg], 0.0)
    pos = dest[jnp.argsort(order)]
    h_pad = jnp.concatenate([h.reshape(n_tok, d), jnp.zeros((1, d), BF16)], axis=0)
    rows = h_pad[row_tok]
    y = _expert_ffn(rows, chunk_e, w_gu, b_gu, w_dn, b_dn, row_w)
    y4 = y[pos].reshape(bn, sn, TOP_K * d)
    return _combine(y4, x, g2)


def _rms(x, g):
    return x * lax.rsqrt(jnp.mean(x * x, axis=-1, keepdims=True) + NORM_EPS) * g


def _rope(x, pos):
    d = x.shape[-1]
    half = d // 2
    inv = ROPE_THETA ** (-jnp.arange(half, dtype=F32) * 2.0 / d)
    ang = pos.astype(F32)[..., None] * inv
    cos = jnp.cos(ang)[:, :, None, :]
    sin = jnp.sin(ang)[:, :, None, :]
    x1, x2 = x[..., :half], x[..., half:]
    return jnp.concatenate([x1 * cos - x2 * sin, x2 * cos + x1 * sin], axis=-1)


def _head_major(t):
    return t.transpose(0, 2, 1, 3)


def _layer_columns(d):
    cols = {}
    off = 0
    for name, width in (("br_g", N_BRANCH * d), ("ga", 3 * NSA_HEADS * HEAD_DIM), ("a_q", 256),
                        ("a_kc", 64), ("a_vc", 64), ("a_ks", 64), ("a_vs", 64), ("a_kw", 64), ("a_vw", 64),
                        ("b_q", 256), ("b_k", 256), ("b_v", 256), ("c_q", 256), ("c_k", 256), ("c_v", 256),
                        ("d_q", 256), ("d_k", 128), ("d_v", 128)):
        cols[name] = (off, width)
        off += width
    return cols, off


def _reorder_w_in(w_in, d, n_pad):
    ref_splits = (256, 64, 64, 64, 64, 64, 64, 12, 256, 256, 256, 256, 256, 256, 256, 128, 128, N_BRANCH * d)
    names = ("a_q", "a_kc", "a_vc", "a_ks", "a_vs", "a_kw", "a_vw", "a_g", "b_q", "b_k", "b_v",
             "c_q", "c_k", "c_v", "d_q", "d_k", "d_v", "br_g")
    starts = np.cumsum((0,) + ref_splits)
    src = {n: (int(starts[i]), ref_splits[i]) for i, n in enumerate(names)}
    cols, total = _layer_columns(d)
    idx = np.zeros((n_pad,), np.int32)
    keep = np.zeros((n_pad,), np.float32)
    for name, (off, width) in cols.items():
        if name == "ga":
            g0 = src["a_g"][0]
            for j in range(3):
                for h in range(NSA_HEADS):
                    base = off + j * NSA_HEADS * HEAD_DIM + h * HEAD_DIM
                    idx[base:base + HEAD_DIM] = g0 + h * 3 + j
        else:
            idx[off:off + width] = src[name][0] + np.arange(width)
        keep[off:off + width] = 1.0
    w = w_in[:, idx] * keep[None, :]
    return w.astype(BF16)


def _mixer_layer(x, positions, mod, p, lam_init):
    bn, sn, d = x.shape
    sh1, sc1, g1 = mod[0], mod[1], mod[2]
    cols, total = _layer_columns(d)
    n_pad = -(-total // 768) * 768
    w_in = _reorder_w_in(p["w_in"], d, n_pad)
    proj = _in_projection(x, p["norm1"], sc1, sh1, w_in)

    def col(name):
        off, width = cols[name]
        return proj[:, :, off:off + width]

    hd = HEAD_DIM
    scale = hd ** -0.5
    a_q = _rope(_rms(col("a_q").reshape(bn, sn, NSA_HEADS, hd), p["nsa_qn"]), positions)
    a_q = _head_major(a_q * scale).astype(BF16)
    kc = _rope(_rms(col("a_kc").reshape(bn, sn, 1, hd), p["nsa_kn"][0]), positions)[:, :, 0]
    ks = _rope(_rms(col("a_ks").reshape(bn, sn, 1, hd), p["nsa_kn"][1]), positions)[:, :, 0]
    kw = _rope(_rms(col("a_kw").reshape(bn, sn, 1, hd), p["nsa_kn"][2]), positions)[:, :, 0]
    k_cmp = _compress(kc, p["nsa_pe_k"], p["nsa_w1_k"], p["nsa_w2_k"]).astype(BF16)
    v_cmp = _compress(col("a_vc"), p["nsa_pe_v"], p["nsa_w1_v"], p["nsa_w2_v"]).astype(BF16)
    n_cmp = (sn - NSA_CMP_LEN) // NSA_CMP_STRIDE + 1
    n_sel = sn // NSA_SEL_BLOCK
    n_top = min(NSA_N_SEL, n_sel)
    cmp_start = np.arange(sn // NSA_CMP_STRIDE) * NSA_CMP_STRIDE
    sel_start = np.arange(n_sel) * NSA_SEL_BLOCK
    overlap = ((cmp_start[:, None] <= (sel_start + NSA_SEL_BLOCK - 1)[None, :]) &
               ((cmp_start + NSA_CMP_LEN - 1)[:, None] >= sel_start[None, :]) &
               (np.arange(sn // NSA_CMP_STRIDE) < n_cmp)[:, None]).astype(np.float32)
    o_c, sel = _nsa_cmp(a_q, k_cmp, v_cmp, jnp.asarray(overlap, BF16), n_cmp, n_top)
    block_onehot = ((np.arange(sn) // NSA_SEL_BLOCK)[:, None] == np.arange(n_sel)[None, :]).astype(np.float32)
    kx = jnp.concatenate([jnp.broadcast_to(jnp.asarray(block_onehot, BF16), (bn, sn, n_sel)), ks.astype(BF16)],
                         axis=-1)
    o_s = _nsa_sel(a_q, kx, col("a_vs").astype(BF16), sel)
    o_w = _banded_attention(a_q, kw.astype(BF16)[:, None], col("a_vw").astype(BF16)[:, None],
                            NSA_WINDOW, NSA_WINDOW)
    o_w = _head_major(o_w).reshape(bn, sn, NSA_HEADS * hd)
    b_q = _head_major(col("b_q").reshape(bn, sn, SB_HEADS, hd) * (-scale * LOG2E)).astype(BF16)
    b_k = _head_major(col("b_k").reshape(bn, sn, SB_HEADS, hd)).astype(BF16)
    b_v = _head_major(col("b_v").reshape(bn, sn, SB_HEADS, hd)).astype(BF16)
    o_b = _head_major(_stick_breaking(b_q, b_k, b_v)).reshape(bn, sn, SB_HEADS * hd)
    dd = DIFF_DIM
    c_q = _rope(_rms(col("c_q").reshape(bn, sn, DIFF_HEADS * 2, dd), p["dif_qn"]), positions)
    c_k = _rope(_rms(col("c_k").reshape(bn, sn, DIFF_HEADS * 2, dd), p["dif_kn"]), positions)
    c_q = _head_major(c_q * dd ** -0.5).astype(BF16)
    c_k = _head_major(c_k).astype(BF16)
    c_v = _head_major(col("c_v").reshape(bn, sn, DIFF_HEADS, 2 * dd)).astype(BF16)
    lam = (jnp.exp(jnp.sum(p["dif_lq1"] * p["dif_lk1"])) - jnp.exp(jnp.sum(p["dif_lq2"] * p["dif_lk2"]))
           + lam_init)
    o_cd = _diff_attention(c_q, c_k, c_v, p["dif_subln"], lam, jnp.asarray(1.0 - lam_init, F32))
    o_cd = _head_major(o_cd).reshape(bn, sn, DIFF_HEADS * 2 * dd)
    d_q = _rope(_rms(col("d_q").reshape(bn, sn, SWA_HEADS, hd), p["swa_qn"]), positions)
    d_k = _rope(_rms(col("d_k").reshape(bn, sn, SWA_KV_HEADS, hd), p["swa_kn"]), positions)
    d_q = _head_major(d_q * scale).astype(BF16)
    d_k = _head_major(d_k).astype(BF16)
    d_v = _head_major(col("d_v").reshape(bn, sn, SWA_KV_HEADS, hd)).astype(BF16)
    o_d = _banded_attention(d_q, d_k, d_v, SWA_WINDOW, 2 * SWA_WINDOW, sinks=p["swa_sinks"])
    o_d = _head_major(o_d).reshape(bn, sn, SWA_HEADS * hd)
    return _merge(proj, o_c, o_s, o_w, o_b, o_cd, o_d, p["w_up"].astype(BF16), p["w_out"].astype(BF16),
                  x, g1, cols["ga"][0])


def kernel(x, c, positions, w_ada, b_ada, norm1, norm2, w_in, nsa_qn, nsa_kn, nsa_pe_k, nsa_w1_k, nsa_w2_k,
           nsa_pe_v, nsa_w1_v, nsa_w2_v, dif_qn, dif_kn, dif_lq1, dif_lk1, dif_lq2, dif_lk2, dif_subln,
           swa_qn, swa_kn, swa_sinks, w_up, w_out, w_router, b_router, w_gu, b_gu, w_dn, b_dn):
    bn, sn, d = x.shape
    depth = w_ada.shape[0]
    c_pad = jnp.zeros((8, d), F32).at[:bn].set(c)
    for l in range(depth):
        lam_init = 0.8 - 0.6 * math.exp(-0.3 * l)
        mod = _linear(c_pad, w_ada[l], b_ada[l], tn=512, precision=lax.Precision.HIGHEST)[:bn]
        mod = mod.reshape(bn, 6, 1, d).transpose(1, 0, 2, 3)
        p = dict(norm1=norm1[l], w_in=w_in[l], nsa_qn=nsa_qn[l], nsa_kn=nsa_kn[l], nsa_pe_k=nsa_pe_k[l],
                 nsa_w1_k=nsa_w1_k[l], nsa_w2_k=nsa_w2_k[l], nsa_pe_v=nsa_pe_v[l], nsa_w1_v=nsa_w1_v[l],
                 nsa_w2_v=nsa_w2_v[l], dif_qn=dif_qn[l], dif_kn=dif_kn[l], dif_lq1=dif_lq1[l],
                 dif_lk1=dif_lk1[l], dif_lq2=dif_lq2[l], dif_lk2=dif_lk2[l], dif_subln=dif_subln[l],
                 swa_qn=swa_qn[l], swa_kn=swa_kn[l], swa_sinks=swa_sinks[l], w_up=w_up[l], w_out=w_out[l])
        x = _mixer_layer(x, positions, mod, p, lam_init)
        x = _moe(x, norm2[l], mod[4], mod[3], mod[5], w_router[l], b_router[l],
                 w_gu[l].astype(BF16), b_gu[l], w_dn[l].astype(BF16), b_dn[l])
    return x
```

```python
import functools
import math

import numpy as np
import jax
import jax.numpy as jnp
from jax import lax
from jax.experimental import pallas as pl
from jax.experimental.pallas import tpu as pltpu

F32 = jnp.float32
BF16 = jnp.bfloat16

HEAD_DIM = 64
ROPE_THETA = 10000.0
NORM_EPS = 1e-6
NEG_INF = -1e30
KNOCKOUT = -3e38
N_BRANCH = 4

NSA_HEADS = 4
NSA_CMP_LEN = 32
NSA_CMP_STRIDE = 16
NSA_SEL_BLOCK = 64
NSA_N_SEL = 16
NSA_WINDOW = 512
NSA_FORCED_SCORE = 1e4

SB_HEADS = 4
DIFF_HEADS = 4
DIFF_DIM = 32
SWA_HEADS = 4
SWA_KV_HEADS = 2
SWA_WINDOW = 128

N_EXPERTS = 32
TOP_K = 4
SWIGLU_ALPHA = 1.702
SWIGLU_LIMIT = 7.0

LANES = 128
LOG2E = 1.4426950408889634
MOE_ROWS = 512
VMEM_LIMIT = 56 * 1024 * 1024


def _cparams(*sem):
    return pltpu.CompilerParams(dimension_semantics=sem, vmem_limit_bytes=VMEM_LIMIT)


def _dot(a, b):
    return jnp.dot(a, b, preferred_element_type=F32)


def _dot_nt(a, b):
    return lax.dot_general(a, b, (((1,), (1,)), ((), ())), preferred_element_type=F32)


def _iota2(shape, dim):
    return lax.broadcasted_iota(jnp.int32, shape, dim)


def _linear_kernel(x_ref, w_ref, b_ref, o_ref, *, precision):
    o_ref[...] = jnp.dot(x_ref[...], w_ref[...], preferred_element_type=F32,
                         precision=precision) + b_ref[...]


def _linear(x, w, b, tn, precision=None):
    m, k = x.shape
    n = w.shape[1]
    return pl.pallas_call(
        functools.partial(_linear_kernel, precision=precision),
        grid=(n // tn,),
        in_specs=[pl.BlockSpec((m, k), lambda j: (0, 0)),
                  pl.BlockSpec((k, tn), lambda j: (0, j)),
                  pl.BlockSpec((1, tn), lambda j: (0, j))],
        out_specs=pl.BlockSpec((m, tn), lambda j: (0, j)),
        out_shape=jax.ShapeDtypeStruct((m, n), F32),
        compiler_params=_cparams("arbitrary"),
        name="linear",
    )(x, w, b.reshape(1, n))


def _cmp_mlp_kernel(a_ref, b_ref, pe_ref, w2_ref, o_ref):
    hid = jax.nn.gelu(a_ref[...] + b_ref[...] + pe_ref[...])
    o_ref[...] = _dot(hid.astype(BF16), w2_ref[...])


def _compress(t, pe, w1, w2):
    bn, sn, dh = t.shape
    st = NSA_CMP_STRIDE
    half = st * dh
    nb = sn // st
    t16 = t.reshape(bn * nb, half).astype(BF16)
    w1cat = jnp.concatenate([w1[:half], w1[half:]], axis=1).astype(BF16)
    hidden = w1.shape[1]
    ab = _linear(t16, w1cat, jnp.zeros((2 * hidden,), F32), tn=2 * hidden)
    ab = ab.reshape(bn, nb, 2 * hidden)
    a = ab[:, :, :hidden]
    b_next = jnp.concatenate([ab[:, 1:, hidden:], jnp.zeros((bn, 1, hidden), F32)], axis=1)
    pe_term = jnp.dot(pe.reshape(1, NSA_CMP_LEN * dh), w1, precision=lax.Precision.HIGHEST)
    rows = bn * nb
    tm = min(512, rows)
    out = pl.pallas_call(
        _cmp_mlp_kernel,
        grid=(rows // tm,),
        in_specs=[pl.BlockSpec((tm, hidden), lambda i: (i, 0)),
                  pl.BlockSpec((tm, hidden), lambda i: (i, 0)),
                  pl.BlockSpec((1, hidden), lambda i: (0, 0)),
                  pl.BlockSpec((hidden, dh), lambda i: (0, 0))],
        out_specs=pl.BlockSpec((tm, dh), lambda i: (i, 0)),
        out_shape=jax.ShapeDtypeStruct((rows, dh), F32),
        compiler_params=_cparams("parallel"),
        name="cmp_mlp",
    )(a.reshape(rows, hidden), b_next.reshape(rows, hidden), pe_term, w2.astype(BF16))
    return out.reshape(bn, nb, dh)


def _norm_mod(x, g, sc, sh):
    r = lax.rsqrt(jnp.mean(x * x, axis=-1, keepdims=True) + NORM_EPS)
    return (x * r * g) * (1.0 + sc) + sh


def _proj_kernel(x_ref, g_ref, sc_ref, sh_ref, w_ref, o_ref, h_scr):
    @pl.when(pl.program_id(2) == 0)
    def _():
        h_scr[...] = _norm_mod(x_ref[0], g_ref[...], sc_ref[0], sh_ref[0]).astype(BF16)

    o_ref[0] = _dot(h_scr[...], w_ref[...])


def _in_projection(x, g, sc, sh, w, tm=1024, tn=768):
    bn, sn, d = x.shape
    n = w.shape[1]
    return pl.pallas_call(
        _proj_kernel,
        grid=(bn, sn // tm, n // tn),
        in_specs=[pl.BlockSpec((1, tm, d), lambda b, i, j: (b, i, 0)),
                  pl.BlockSpec((1, d), lambda b, i, j: (0, 0)),
                  pl.BlockSpec((1, 1, d), lambda b, i, j: (b, 0, 0)),
                  pl.BlockSpec((1, 1, d), lambda b, i, j: (b, 0, 0)),
                  pl.BlockSpec((d, tn), lambda b, i, j: (0, j))],
        out_specs=pl.BlockSpec((1, tm, tn), lambda b, i, j: (b, i, j)),
        out_shape=jax.ShapeDtypeStruct((bn, sn, n), F32),
        scratch_shapes=[pltpu.VMEM((tm, d), BF16)],
        compiler_params=_cparams("parallel", "parallel", "arbitrary"),
        name="in_proj",
    )(x, g.reshape(1, d), sc, sh, w)


def _banded_kernel(*refs, tile, window, has_sink):
    if has_sink:
        sink_ref, q_ref, kp_ref, kc_ref, vp_ref, vc_ref, o_ref = refs
    else:
        q_ref, kp_ref, kc_ref, vp_ref, vc_ref, o_ref = refs
    i = pl.program_id(2)
    q = q_ref[0, 0]
    row = _iota2((tile, tile), 0)
    col = _iota2((tile, tile), 1)
    t = i * tile + row
    kpos_p = (i - 1) * tile + col
    valid_p = (t - kpos_p < window) & (kpos_p >= 0)
    valid_c = (col <= row) & (row - col < window)
    s_p = jnp.where(valid_p, _dot_nt(q, kp_ref[0, 0]), NEG_INF)
    s_c = jnp.where(valid_c, _dot_nt(q, kc_ref[0, 0]), NEG_INF)
    m = jnp.maximum(jnp.max(s_p, axis=-1, keepdims=True), jnp.max(s_c, axis=-1, keepdims=True))
    if has_sink:
        sink = sink_ref[pl.program_id(1)]
        m = jnp.maximum(m, sink)
    p_p = jnp.where(valid_p, jnp.exp(s_p - m), 0.0)
    p_c = jnp.where(valid_c, jnp.exp(s_c - m), 0.0)
    den = jnp.sum(p_p, axis=-1, keepdims=True) + jnp.sum(p_c, axis=-1, keepdims=True)
    if has_sink:
        den = den + jnp.exp(sink - m)
    else:
        den = jnp.maximum(den, 1e-30)
    o = _dot(p_p.astype(BF16), vp_ref[0, 0]) + _dot(p_c.astype(BF16), vc_ref[0, 0])
    o_ref[0, 0] = o / den


def _banded_attention(q, k, v, window, tile, sinks=None):
    bn, hq, sn, d = q.shape
    grp = hq // k.shape[1]
    assert tile >= window and sn % tile == 0
    has_sink = sinks is not None
    qspec = pl.BlockSpec((1, 1, tile, d), lambda b, h, i: (b, h, i, 0))
    prev = pl.BlockSpec((1, 1, tile, d), lambda b, h, i: (b, h // grp, jnp.maximum(i - 1, 0), 0))
    cur = pl.BlockSpec((1, 1, tile, d), lambda b, h, i: (b, h // grp, i, 0))
    in_specs = [qspec, prev, cur, prev, cur]
    args = [q, k, k, v, v]
    if has_sink:
        in_specs = [pl.BlockSpec(memory_space=pltpu.SMEM)] + in_specs
        args = [sinks.astype(F32)] + args
    return pl.pallas_call(
        functools.partial(_banded_kernel, tile=tile, window=window, has_sink=has_sink),
        grid=(bn, hq, sn // tile),
        in_specs=in_specs,
        out_specs=pl.BlockSpec((1, 1, tile, d), lambda b, h, i: (b, h, i, 0)),
        out_shape=jax.ShapeDtypeStruct((bn, hq, sn, d), F32),
        compiler_params=_cparams("parallel", "parallel", "parallel"),
        name="banded_attn",
    )(*args)


def _nsa_cmp_kernel(q_ref, kc_ref, vc_ref, ov_ref, oc_ref, sel_ref, *, tq, n_cmp, n_top, heads):
    i = pl.program_id(1)
    ncp = kc_ref.shape[1]
    nsel = ov_ref.shape[1]
    dh = q_ref.shape[3]
    t = i * tq + _iota2((tq, ncp), 0)
    n = _iota2((tq, ncp), 1)
    valid = (n * NSA_CMP_STRIDE + (NSA_CMP_LEN - 1) <= t) & (n < n_cmp)
    kc = kc_ref[0]
    vc = vc_ref[0]
    psum = jnp.zeros((tq, ncp), F32)
    for h in range(heads):
        s = jnp.where(valid, _dot_nt(q_ref[0, h], kc), NEG_INF)
        m = jnp.max(s, axis=-1, keepdims=True)
        p = jnp.where(valid, jnp.exp(s - m), 0.0)
        p = p / jnp.maximum(jnp.sum(p, axis=-1, keepdims=True), 1e-30)
        oc_ref[0, :, h * dh:(h + 1) * dh] = _dot(p.astype(BF16), vc)
        psum = psum + p
    hi = psum.astype(BF16)
    lo = (psum - hi.astype(F32)).astype(BF16)
    imp = _dot(hi, ov_ref[...]) + _dot(lo, ov_ref[...])

    tt = i * tq + _iota2((tq, nsel), 0)
    blk = _iota2((tq, nsel), 1)
    cur = tt >> (NSA_SEL_BLOCK.bit_length() - 1)
    forced = (blk == 0) | (blk == cur) | (blk == cur - 1)
    valid_s = blk * NSA_SEL_BLOCK <= tt
    score = jnp.where(forced, NSA_FORCED_SCORE, jnp.where(valid_s, imp, -1.0))
    blk_f = blk.astype(F32)

    def pick(_, carry):
        score, sel = carry
        m = jnp.max(score, axis=-1, keepdims=True)
        first = jnp.min(jnp.where(score == m, blk_f, float(nsel)), axis=-1, keepdims=True)
        hit = blk_f == first
        return jnp.where(hit, KNOCKOUT, score), jnp.where(hit, 0.0, sel)

    _, bias = lax.fori_loop(0, n_top, pick, (score, jnp.full((tq, nsel), NEG_INF, F32)))
    sel_ref[0] = bias.astype(BF16)


def _nsa_cmp(q, kc, vc, overlap, n_cmp, n_top, tq=256):
    bn, heads, sn, dh = q.shape
    ncp = kc.shape[1]
    nsel = overlap.shape[1]
    return pl.pallas_call(
        functools.partial(_nsa_cmp_kernel, tq=tq, n_cmp=n_cmp, n_top=n_top, heads=heads),
        grid=(bn, sn // tq),
        in_specs=[pl.BlockSpec((1, heads, tq, dh), lambda b, i: (b, 0, i, 0)),
                  pl.BlockSpec((1, ncp, dh), lambda b, i: (b, 0, 0)),
                  pl.BlockSpec((1, ncp, dh), lambda b, i: (b, 0, 0)),
                  pl.BlockSpec((ncp, nsel), lambda b, i: (0, 0))],
        out_specs=[pl.BlockSpec((1, tq, heads * dh), lambda b, i: (b, i, 0)),
                   pl.BlockSpec((1, tq, nsel), lambda b, i: (b, i, 0))],
        out_shape=[jax.ShapeDtypeStruct((bn, sn, heads * dh), F32),
                   jax.ShapeDtypeStruct((bn, sn, nsel), BF16)],
        compiler_params=_cparams("parallel", "parallel"),
        name="nsa_cmp_topk",
    )(q, kc, vc, overlap)


def _nsa_sel_kernel(q_ref, kx_ref, v_ref, bias_ref, o_ref, qx_scr, *, tq, tk, heads):
    i = pl.program_id(1)
    dh = v_ref.shape[2]
    bias = bias_ref[0]
    for h in range(heads):
        qx_scr[h] = jnp.concatenate([bias, q_ref[0, h]], axis=1)

    def tile(j, state, diagonal):
        start = pl.multiple_of(j * tk, tk)
        kx = kx_ref[0, pl.ds(start, tk), :]
        v = v_ref[0, pl.ds(start, tk), :]
        if diagonal:
            causal = (start + _iota2((tq, tk), 1)) <= (i * tq + _iota2((tq, tk), 0))
        new_state = []
        for h in range(heads):
            m_old, l_old, acc_old = state[h]
            s = _dot_nt(qx_scr[h], kx)
            if diagonal:
                s = jnp.where(causal, s, NEG_INF)
            m_new = jnp.maximum(m_old, jnp.max(s, axis=-1, keepdims=True))
            alpha = jnp.exp(m_old - m_new)
            p = jnp.exp(s - m_new)
            l_new = alpha * l_old + jnp.sum(p, axis=-1, keepdims=True)
            acc_new = alpha * acc_old + _dot(p.astype(BF16), v)
            new_state.append((m_new, l_new, acc_new))
        return tuple(new_state)

    init = tuple((jnp.full((tq, 1), NEG_INF, F32), jnp.zeros((tq, 1), F32), jnp.zeros((tq, dh), F32))
                 for _ in range(heads))
    n_full = (i * tq) // tk
    state = lax.fori_loop(0, n_full, lambda j, st: tile(j, st, False), init)
    state = tile(n_full, state, True)
    for h in range(heads):
        o_ref[0, :, h * dh:(h + 1) * dh] = state[h][2] / jnp.maximum(state[h][1], 1e-30)


def _nsa_sel(q, kx, v, bias, tq=512, tk=512):
    bn, heads, sn, dh = q.shape
    nsel = bias.shape[2]
    tk = min(tk, sn)
    assert tk % tq == 0 and sn % tk == 0
    return pl.pallas_call(
        functools.partial(_nsa_sel_kernel, tq=tq, tk=tk, heads=heads),
        grid=(bn, sn // tq),
        in_specs=[pl.BlockSpec((1, heads, tq, dh), lambda b, i: (b, 0, i, 0)),
                  pl.BlockSpec((1, sn, nsel + dh), lambda b, i: (b, 0, 0)),
                  pl.BlockSpec((1, sn, dh), lambda b, i: (b, 0, 0)),
                  pl.BlockSpec((1, tq, nsel), lambda b, i: (b, i, 0))],
        out_specs=pl.BlockSpec((1, tq, heads * dh), lambda b, i: (b, i, 0)),
        out_shape=jax.ShapeDtypeStruct((bn, sn, heads * dh), F32),
        scratch_shapes=[pltpu.VMEM((heads, tq, nsel + dh), BF16)],
        compiler_params=_cparams("parallel", "arbitrary"),
        name="nsa_selected_attn",
    )(q, kx, v, bias)


def _sb_kernel(q_ref, k_ref, v_ref, u_ref, o_ref, *, tq, tk):
    i = pl.program_id(2)
    q = q_ref[0, 0]
    per_q = tq // tk

    def tile(jj, carry, diagonal):
        start = pl.multiple_of(jj * tk, tk)
        nz = _dot_nt(q, k_ref[0, 0, pl.ds(start, tk), :])
        neg_abs = lax.bitcast_convert_type(lax.bitcast_convert_type(nz, jnp.uint32) | jnp.uint32(0x80000000), F32)
        log_keep = jnp.minimum(nz, 0.0) - jnp.log2(1.0 + jnp.exp2(neg_abs))
        if diagonal:
            strict = (start + _iota2((tq, tk), 1)) < (i * tq + _iota2((tq, tk), 0))
            log_keep = jnp.where(strict, log_keep, 0.0)
        hi = log_keep.astype(BF16)
        lo = (log_keep - hi.astype(F32)).astype(BF16)
        cum = _dot(jnp.concatenate([hi, lo], axis=1), u_ref[...])
        a = jnp.exp2(cum + jnp.concatenate([carry] * (tk // LANES), axis=1) - nz)
        if diagonal:
            a = jnp.where(strict, a, 0.0)
        out = _dot(a.astype(BF16), v_ref[0, 0, pl.ds(start, tk), :])
        return out, carry + jnp.broadcast_to(cum[:, 0:1], carry.shape)

    def group(first, carry, diagonal):
        total = None
        for r in range(per_q):
            out, carry = tile(first - r, carry, diagonal)
            total = out if total is None else total + out
        return total, carry

    acc, carry = group(i * per_q + per_q - 1, jnp.zeros((tq, LANES), F32), True)

    def body(p, state):
        acc, carry = state
        out, carry = group((i - p) * per_q - 1, carry, False)
        return acc + out, carry

    acc, _ = lax.fori_loop(0, i, body, (acc, carry))
    o_ref[0, 0] = acc


def _stick_breaking(q, k, v, tq=512, tk=256):
    bn, heads, sn, dh = q.shape
    incl = (np.arange(tk)[:, None] >= np.arange(tk)[None, :]).astype(np.float32)
    u = jnp.asarray(np.concatenate([incl, incl], axis=0), BF16)
    kv = pl.BlockSpec((1, 1, sn, dh), lambda b, h, i: (b, h, 0, 0))
    return pl.pallas_call(
        functools.partial(_sb_kernel, tq=tq, tk=tk),
        grid=(bn, heads, sn // tq),
        in_specs=[pl.BlockSpec((1, 1, tq, dh), lambda b, h, i: (b, h, i, 0)), kv, kv,
                  pl.BlockSpec((2 * tk, tk), lambda b, h, i: (0, 0))],
        out_specs=pl.BlockSpec((1, 1, tq, dh), lambda b, h, i: (b, h, i, 0)),
        out_shape=jax.ShapeDtypeStruct((bn, heads, sn, dh), F32),
        compiler_params=_cparams("parallel", "parallel", "arbitrary"),
        name="stick_breaking_attn",
    )(q, k, v, u)


def _diff_kernel(sc_ref, q_ref, k_ref, v_ref, g_ref, o_ref, *, tq, tk):
    i = pl.program_id(2)
    dv = v_ref.shape[3]

    def tile(j, state, diagonal):
        start = pl.multiple_of(j * tk, tk)
        v = v_ref[0, 0, pl.ds(start, tk), :]
        if diagonal:
            causal = (start + _iota2((tq, tk), 1)) <= (i * tq + _iota2((tq, tk), 0))
        new_state = []
        for mi in range(2):
            m_old, l_old, acc_old = state[mi]
            s = _dot_nt(q_ref[0, mi], k_ref[0, mi, pl.ds(start, tk), :])
            if diagonal:
                s = jnp.where(causal, s, NEG_INF)
            m_new = jnp.maximum(m_old, jnp.max(s, axis=-1, keepdims=True))
            alpha = jnp.exp(m_old - m_new)
            p = jnp.exp(s - m_new)
            l_new = alpha * l_old + jnp.sum(p, axis=-1, keepdims=True)
            acc_new = alpha * acc_old + _dot(p.astype(BF16), v)
            new_state.append((m_new, l_new, acc_new))
        return tuple(new_state)

    init = tuple((jnp.full((tq, 1), NEG_INF, F32), jnp.zeros((tq, 1), F32), jnp.zeros((tq, dv), F32))
                 for _ in range(2))
    n_full = (i * tq) // tk
    state = lax.fori_loop(0, n_full, lambda j, st: tile(j, st, False), init)
    state = tile(n_full, state, True)
    lam = sc_ref[0]
    post = sc_ref[1]
    o = (state[0][2] / jnp.maximum(state[0][1], 1e-30)
         - lam * (state[1][2] / jnp.maximum(state[1][1], 1e-30)))
    r = lax.rsqrt(jnp.mean(o * o, axis=-1, keepdims=True) + NORM_EPS)
    o_ref[0, 0] = (o * r * g_ref[...]) * post


def _diff_attention(q, k, v, subln, lam, post, tq=512, tk=512):
    bn, h2, sn, dd = q.shape
    heads = h2 // 2
    dv = v.shape[3]
    tk = min(tk, sn)
    assert tk % tq == 0 and sn % tk == 0
    scal = jnp.stack([lam, post]).astype(F32)
    return pl.pallas_call(
        functools.partial(_diff_kernel, tq=tq, tk=tk),
        grid=(bn, heads, sn // tq),
        in_specs=[pl.BlockSpec(memory_space=pltpu.SMEM),
                  pl.BlockSpec((1, 2, tq, dd), lambda b, h, i: (b, h, i, 0)),
                  pl.BlockSpec((1, 2, sn, dd), lambda b, h, i: (b, h, 0, 0)),
                  pl.BlockSpec((1, 1, sn, dv), lambda b, h, i: (b, h, 0, 0)),
                  pl.BlockSpec((1, dv), lambda b, h, i: (0, 0))],
        out_specs=pl.BlockSpec((1, 1, tq, dv), lambda b, h, i: (b, h, i, 0)),
        out_shape=jax.ShapeDtypeStruct((bn, heads, sn, dv), F32),
        compiler_params=_cparams("parallel", "parallel", "arbitrary"),
        name="diff_attn",
    )(scal, q, k, v, subln.reshape(1, dv).astype(F32))


def _merge_kernel(brg_ref, ga0_ref, ga1_ref, ga2_ref, oc_ref, os_ref, ow_ref, ob_ref, ocd_ref, od_ref,
                  wup_ref, wout_ref, x_ref, g1_ref, o_ref, *, d):
    def heads(ref):
        return jnp.concatenate([ref[0, h] for h in range(ref.shape[1])], axis=1)

    o_a = (jax.nn.sigmoid(ga0_ref[0]) * oc_ref[0] + jax.nn.sigmoid(ga1_ref[0]) * os_ref[0]
           + jax.nn.sigmoid(ga2_ref[0]) * heads(ow_ref))
    branches = (o_a, heads(ob_ref), heads(ocd_ref), heads(od_ref))
    merged = None
    for bi, o in enumerate(branches):
        gate = jax.nn.sigmoid(brg_ref[0, :, bi * d:(bi + 1) * d])
        term = gate * _dot(o.astype(BF16), wup_ref[bi])
        merged = term if merged is None else merged + term
    y = _dot(merged.astype(BF16), wout_ref[...])
    o_ref[0] = x_ref[0] + g1_ref[0] * y


def _merge(proj, o_c, o_s, o_w, o_b, o_cd, o_d, w_up, w_out, x, g1, ga_col, tm=256):
    bn, sn, d = x.shape
    bw = o_c.shape[2]
    nh, dh = o_b.shape[1], o_b.shape[3]
    assert ga_col % bw == 0
    gblk = ga_col // bw
    row = lambda b, i: (b, i, 0)
    bspec = pl.BlockSpec((1, tm, bw), row)
    hspec = pl.BlockSpec((1, nh, tm, dh), lambda b, i: (b, 0, i, 0))
    return pl.pallas_call(
        functools.partial(_merge_kernel, d=d),
        grid=(bn, sn // tm),
        in_specs=[pl.BlockSpec((1, tm, N_BRANCH * d), row),
                  pl.BlockSpec((1, tm, bw), lambda b, i: (b, i, gblk)),
                  pl.BlockSpec((1, tm, bw), lambda b, i: (b, i, gblk + 1)),
                  pl.BlockSpec((1, tm, bw), lambda b, i: (b, i, gblk + 2)),
                  bspec, bspec, hspec, hspec, hspec, hspec,
                  pl.BlockSpec((N_BRANCH, bw, d), lambda b, i: (0, 0, 0)),
                  pl.BlockSpec((d, d), lambda b, i: (0, 0)),
                  pl.BlockSpec((1, tm, d), row),
                  pl.BlockSpec((1, 1, d), lambda b, i: (b, 0, 0))],
        out_specs=pl.BlockSpec((1, tm, d), row),
        out_shape=jax.ShapeDtypeStruct((bn, sn, d), F32),
        compiler_params=_cparams("parallel", "parallel"),
        name="branch_merge",
    )(proj, proj, proj, proj, o_c, o_s, o_w, o_b, o_cd, o_d, w_up, w_out, x, g1)


def _router_kernel(x_ref, g_ref, sc_ref, sh_ref, wr_ref, br_ref, tri_ref, h_ref, e_ref, w_ref, rank_ref, cnt_ref,
                   run_scr):
    @pl.when((pl.program_id(0) == 0) & (pl.program_id(1) == 0))
    def _():
        run_scr[...] = jnp.zeros(run_scr.shape, F32)

    h = _norm_mod(x_ref[0], g_ref[...], sc_ref[0], sh_ref[0])
    h_ref[0] = h.astype(BF16)
    logits = jnp.dot(h, wr_ref[...], preferred_element_type=F32,
                     precision=lax.Precision.HIGHEST) + br_ref[...]
    lane = _iota2(logits.shape, 1)
    lane_f = lane.astype(F32)
    cur = logits
    vals, idxs = [], []
    chosen = jnp.zeros(logits.shape, F32)
    for _ in range(TOP_K):
        m = jnp.max(cur, axis=-1, keepdims=True)
        first = jnp.min(jnp.where(cur == m, lane_f, float(LANES)), axis=-1, keepdims=True)
        vals.append(m)
        idxs.append(first)
        hit = lane_f == first
        cur = jnp.where(hit, KNOCKOUT, cur)
        chosen = jnp.where(hit, 1.0, chosen)
    exps = [jnp.exp(v - vals[0]) for v in vals]
    den = exps[0]
    for e in exps[1:]:
        den = den + e
    earlier = _dot(tri_ref[...], chosen.astype(BF16)) + run_scr[0:1, :]
    e_out = jnp.zeros(logits.shape, F32)
    w_out = jnp.zeros(logits.shape, F32)
    r_out = jnp.zeros(logits.shape, F32)
    for k in range(TOP_K):
        rank_k = jnp.sum(jnp.where(lane_f == idxs[k], earlier, 0.0), axis=-1, keepdims=True)
        e_out = jnp.where(lane == k, idxs[k], e_out)
        w_out = jnp.where(lane == k, exps[k] / den, w_out)
        r_out = jnp.where(lane == k, rank_k, r_out)
    e_ref[0] = e_out.astype(jnp.int32)
    w_ref[0] = w_out
    rank_ref[0] = r_out.astype(jnp.int32)
    total = run_scr[...] + jnp.sum(chosen, axis=0, keepdims=True)
    run_scr[...] = total
    cnt_ref[...] = total


def _router(x, g, sc, sh, w_router, b_router, tm=512):
    bn, sn, d = x.shape
    ne = w_router.shape[1]
    wr = jnp.zeros((d, LANES), F32).at[:, :ne].set(w_router)
    br = jnp.full((1, LANES), NEG_INF, F32).at[0, :ne].set(b_router)
    tri = jnp.asarray((np.arange(tm)[:, None] > np.arange(tm)[None, :]).astype(np.float32), BF16)
    row = lambda b, i: (b, i, 0)
    return pl.pallas_call(
        _router_kernel,
        grid=(bn, sn // tm),
        in_specs=[pl.BlockSpec((1, tm, d), row),
                  pl.BlockSpec((1, d), lambda b, i: (0, 0)),
                  pl.BlockSpec((1, 1, d), lambda b, i: (b, 0, 0)),
                  pl.BlockSpec((1, 1, d), lambda b, i: (b, 0, 0)),
                  pl.BlockSpec((d, LANES), lambda b, i: (0, 0)),
                  pl.BlockSpec((1, LANES), lambda b, i: (0, 0)),
                  pl.BlockSpec((tm, tm), lambda b, i: (0, 0))],
        out_specs=[pl.BlockSpec((1, tm, d), row),
                   pl.BlockSpec((1, tm, LANES), row),
                   pl.BlockSpec((1, tm, LANES), row),
                   pl.BlockSpec((1, tm, LANES), row),
                   pl.BlockSpec((8, LANES), lambda b, i: (0, 0))],
        out_shape=[jax.ShapeDtypeStruct((bn, sn, d), BF16),
                   jax.ShapeDtypeStruct((bn, sn, LANES), jnp.int32),
                   jax.ShapeDtypeStruct((bn, sn, LANES), F32),
                   jax.ShapeDtypeStruct((bn, sn, LANES), jnp.int32),
                   jax.ShapeDtypeStruct((8, LANES), F32)],
        scratch_shapes=[pltpu.VMEM((8, LANES), F32)],
        compiler_params=_cparams("arbitrary", "arbitrary"),
        name="moe_router",
    )(x, g.reshape(1, d), sc, sh, wr, br, tri)


def _expert_kernel(ce_ref, x_ref, wgu_ref, bgu_ref, wdn_ref, bdn_ref, rw_ref, o_ref, *, ff, fc):
    del ce_ref
    x = x_ref[...]
    y = None
    for c in range(ff // fc):
        g = _dot(x, wgu_ref[0, :, c * fc:(c + 1) * fc]) + bgu_ref[0, :, c * fc:(c + 1) * fc]
        u = _dot(x, wgu_ref[0, :, ff + c * fc:ff + (c + 1) * fc]) + bgu_ref[0, :, ff + c * fc:ff + (c + 1) * fc]
        g = jnp.minimum(g, SWIGLU_LIMIT)
        u = jnp.clip(u, -SWIGLU_LIMIT, SWIGLU_LIMIT)
        act = g * jax.nn.sigmoid(SWIGLU_ALPHA * g) * (u + 1.0)
        part = _dot(act.astype(BF16), wdn_ref[0, c * fc:(c + 1) * fc, :])
        y = part if y is None else y + part
    o_ref[...] = (y + bdn_ref[0]) * rw_ref[...]


def _expert_ffn(rows, chunk_e, w_gu, b_gu, w_dn, b_dn, row_w, tm=MOE_ROWS, fc=512):
    n_rows, d = rows.shape
    ne, _, ff2 = w_gu.shape
    ff = ff2 // 2
    grid_spec = pltpu.PrefetchScalarGridSpec(
        num_scalar_prefetch=1,
        grid=(n_rows // tm,),
        in_specs=[pl.BlockSpec((tm, d), lambda c, ce: (c, 0)),
                  pl.BlockSpec((1, d, ff2), lambda c, ce: (ce[c], 0, 0)),
                  pl.BlockSpec((1, 1, ff2), lambda c, ce: (ce[c], 0, 0)),
                  pl.BlockSpec((1, ff, d), lambda c, ce: (ce[c], 0, 0)),
                  pl.BlockSpec((1, 1, d), lambda c, ce: (ce[c], 0, 0)),
                  pl.BlockSpec((tm, 1), lambda c, ce: (c, 0))],
        out_specs=pl.BlockSpec((tm, d), lambda c, ce: (c, 0)),
    )
    return pl.pallas_call(
        functools.partial(_expert_kernel, ff=ff, fc=fc),
        grid_spec=grid_spec,
        out_shape=jax.ShapeDtypeStruct((n_rows, d), F32),
        compiler_params=_cparams("arbitrary"),
        name="moe_expert_ffn",
    )(chunk_e, rows, w_gu, b_gu.reshape(ne, 1, ff2), w_dn, b_dn.reshape(ne, 1, d), row_w.reshape(n_rows, 1))


def _combine_kernel(y_ref, x_ref, g2_ref, o_ref):
    tot = y_ref[0]
    for k in range(1, TOP_K):
        tot = tot + y_ref[k]
    o_ref[0] = x_ref[0] + g2_ref[0] * tot


def _combine(y4, x, g2, tm=512):
    bn, sn, d = x.shape
    row = lambda b, i: (b, i, 0)
    nt = sn // tm
    return pl.pallas_call(
        _combine_kernel,
        grid=(bn, nt),
        in_specs=[pl.BlockSpec((TOP_K, tm, d), lambda b, i: (0, b * nt + i, 0)),
                  pl.BlockSpec((1, tm, d), row),
                  pl.BlockSpec((1, 1, d), lambda b, i: (b, 0, 0))],
        out_specs=pl.BlockSpec((1, tm, d), row),
        out_shape=jax.ShapeDtypeStruct((bn, sn, d), F32),
        compiler_params=_cparams("parallel", "parallel"),
        name="moe_combine",
    )(y4, x, g2)


def _moe(x, g, sc, sh, g2, w_router, b_router, w_gu, b_gu, w_dn, b_dn):
    bn, sn, d = x.shape
    n_tok = bn * sn
    n_asg = n_tok * TOP_K
    tm = MOE_ROWS
    h, e_out, w_out, rank_out, totals = _router(x, g, sc, sh, w_router, b_router)
    e_tok = e_out[:, :, :TOP_K].reshape(n_tok, TOP_K)
    w_flat = w_out[:, :, :TOP_K].reshape(-1)
    rank = rank_out[:, :, :TOP_K].reshape(n_tok, TOP_K)
    counts = totals[0, :N_EXPERTS].astype(jnp.int32)
    starts = jnp.cumsum(counts) - counts
    padded = (counts + tm - 1) // tm * tm
    pad_ends = jnp.cumsum(padded)
    pad_starts = pad_ends - padded
    pos = pad_starts[e_tok] + rank
    n_chunks = n_asg // tm + N_EXPERTS
    n_rows = n_chunks * tm
    chunk_e = jnp.minimum(jnp.searchsorted(pad_ends, jnp.arange(n_chunks, dtype=jnp.int32) * tm, side='right'),
                          N_EXPERTS - 1).astype(jnp.int32)
    order = jnp.argsort(e_tok.reshape(-1))
    r = jnp.arange(n_rows, dtype=jnp.int32)
    row_e = jnp.repeat(chunk_e, tm)
    idx_in_e = r - pad_starts[row_e]
    row_valid = idx_in_e < counts[row_e]
    asg = order[jnp.clip(starts[row_e] + idx_in_e, 0, n_asg - 1)].astype(jnp.int32)
    row_tok = jnp.where(row_valid, asg // TOP_K, n_tok)
    row_w = jnp.where(row_valid, w_flat[asg], 0.0)
    h_pad = jnp.concatenate([h.reshape(n_tok, d), jnp.zeros((1, d), BF16)], axis=0)
    rows = h_pad[row_tok]
    y = _expert_ffn(rows, chunk_e, w_gu, b_gu, w_dn, b_dn, row_w)
    return _combine(y[pos.T], x, g2)


def _norm_rope(x, g, cos, sin, bd, hd):
    sq = x * x
    hi = sq.astype(BF16)
    lo = (sq - hi.astype(F32)).astype(BF16)
    ss = _dot(hi, bd) + _dot(lo, bd)
    y = x * lax.rsqrt(ss * (1.0 / hd) + NORM_EPS) * g
    half = hd // 2
    first = (_iota2(x.shape, 1) & (hd - 1)) < half
    partner = jnp.where(first, pltpu.roll(y, LANES - half, 1), pltpu.roll(y, half, 1))
    return y * cos + partner * sin


def _prep_kernel(aq_ref, bq_ref, bk_ref, bv_ref, cq_ref, ck_ref, cv_ref, dq_ref, akc_ref, aks_ref, akw_ref,
                 dk_ref, dv_ref, c64_ref, s64_ref, c32_ref, s32_ref, gaq_ref, gak_ref, gcq_ref, gck_ref,
                 gdq_ref, gdk_ref, bd64_ref, bd32_ref,
                 oaq_ref, okc_ref, ovc_ref, okx_ref, ovs_ref, okw_ref, ovw_ref, obq_ref, obk_ref, obv_ref,
                 ocq_ref, ock_ref, ocv_ref, odq_ref, odk_ref, odv_ref, *, ts, sb_scale):
    hd, dd = HEAD_DIM, DIFF_DIM
    c64, s64, c32, s32 = c64_ref[0], s64_ref[0], c32_ref[0], s32_ref[0]
    bd64, bd32 = bd64_ref[...], bd32_ref[...]

    def slabs(ref):
        x = ref[0]
        return [x[:, c * LANES:(c + 1) * LANES] for c in range(x.shape[1] // LANES)]

    def put_heads(o_ref, c, y, width):
        per = LANES // width
        for u in range(per):
            o_ref[0, c * per + u] = y[:, u * width:(u + 1) * width].astype(o_ref.dtype)

    for src, gain, dst in ((aq_ref, gaq_ref, oaq_ref), (dq_ref, gdq_ref, odq_ref), (dk_ref, gdk_ref, odk_ref)):
        for c, x in enumerate(slabs(src)):
            put_heads(dst, c, _norm_rope(x, gain[:, c * LANES:(c + 1) * LANES], c64, s64, bd64, hd), hd)
    kc = _norm_rope(akc_ref[0], gak_ref[0:1, :], c64, s64, bd64, hd)
    ks = _norm_rope(aks_ref[0], gak_ref[1:2, :], c64, s64, bd64, hd)
    kw = _norm_rope(akw_ref[0], gak_ref[2:3, :], c64, s64, bd64, hd)
    okc_ref[0] = kc[:, :hd].astype(BF16)
    ovc_ref[0] = akc_ref[0][:, hd:].astype(BF16)
    nsel = okx_ref.shape[2] - hd
    blk = (pl.program_id(1) * ts + _iota2((ts, nsel), 0)) >> (NSA_SEL_BLOCK.bit_length() - 1)
    okx_ref[0, :, :nsel] = jnp.where(blk == _iota2((ts, nsel), 1), 1.0, 0.0).astype(BF16)
    okx_ref[0, :, nsel:] = ks[:, :hd].astype(BF16)
    ovs_ref[0] = aks_ref[0][:, hd:].astype(BF16)
    okw_ref[0, 0] = kw[:, :hd].astype(BF16)
    ovw_ref[0, 0] = akw_ref[0][:, hd:].astype(BF16)
    for c, x in enumerate(slabs(bq_ref)):
        put_heads(obq_ref, c, x * sb_scale, hd)
    for src, dst in ((bk_ref, obk_ref), (bv_ref, obv_ref), (cv_ref, ocv_ref)):
        for c, x in enumerate(slabs(src)):
            put_heads(dst, c, x, hd)
    put_heads(odv_ref, 0, dv_ref[0], hd)
    for src, gain, dst in ((cq_ref, gcq_ref, ocq_ref), (ck_ref, gck_ref, ock_ref)):
        for c, x in enumerate(slabs(src)):
            put_heads(dst, c, _norm_rope(x, gain[:, c * LANES:(c + 1) * LANES], c32, s32, bd32, dd), dd)


def _rope_tables(positions, hd):
    half = hd // 2
    inv = ROPE_THETA ** (-jnp.arange(half, dtype=F32) * 2.0 / hd)
    ang = positions.astype(F32)[..., None] * inv
    cos, sin = jnp.cos(ang), jnp.sin(ang)
    reps = LANES // hd
    return (jnp.tile(jnp.concatenate([cos, cos], axis=-1), (1, 1, reps)),
            jnp.tile(jnp.concatenate([-sin, sin], axis=-1), (1, 1, reps)))


def _prep(proj, cols, tables, p, n_sel, ts=512):
    bn, sn, _ = proj.shape
    hd, dd = HEAD_DIM, DIFF_DIM
    scale = hd ** -0.5
    c64, s64, c32, s32 = tables

    def cspec(name):
        off, width = cols[name]
        assert off % width == 0
        return pl.BlockSpec((1, ts, width), lambda b, i, blk=off // width: (b, i, blk))

    def tile_gain(g, reps, mult=1.0):
        return (jnp.tile(g.astype(F32), reps) * mult).reshape(1, -1)

    ones = jnp.ones((hd,), F32)
    gak = jnp.stack([jnp.concatenate([p["nsa_kn"][j].astype(F32), ones]) for j in range(3)])
    gains = [tile_gain(p["nsa_qn"], NSA_HEADS, scale), gak,
             tile_gain(p["dif_qn"], 2 * DIFF_HEADS, dd ** -0.5), tile_gain(p["dif_kn"], 2 * DIFF_HEADS),
             tile_gain(p["swa_qn"], SWA_HEADS, scale), tile_gain(p["swa_kn"], SWA_KV_HEADS)]
    lane = np.arange(LANES)
    bd64 = jnp.asarray((lane[:, None] // hd == lane[None, :] // hd).astype(np.float32), BF16)
    bd32 = jnp.asarray((lane[:, None] // dd == lane[None, :] // dd).astype(np.float32), BF16)
    names = ("a_q", "b_q", "b_k", "b_v", "c_q", "c_k", "c_v", "d_q", "a_kcvc", "a_ksvs", "a_kwvw", "d_k", "d_v")
    tab = pl.BlockSpec((1, ts, LANES), lambda b, i: (b, i, 0))
    full = lambda a: pl.BlockSpec(a.shape, lambda b, i: (0,) * a.ndim)

    def hm(nh, w):
        return (jax.ShapeDtypeStruct((bn, nh, sn, w), BF16), pl.BlockSpec((1, nh, ts, w), lambda b, i: (b, 0, i, 0)))

    def tm_(w):
        return (jax.ShapeDtypeStruct((bn, sn, w), BF16), pl.BlockSpec((1, ts, w), lambda b, i: (b, i, 0)))

    outs = [hm(NSA_HEADS, hd), tm_(hd), tm_(hd), tm_(n_sel + hd), tm_(hd), hm(1, hd), hm(1, hd),
            hm(SB_HEADS, hd), hm(SB_HEADS, hd), hm(SB_HEADS, hd),
            hm(2 * DIFF_HEADS, dd), hm(2 * DIFF_HEADS, dd), hm(DIFF_HEADS, 2 * dd),
            hm(SWA_HEADS, hd), hm(SWA_KV_HEADS, hd), hm(SWA_KV_HEADS, hd)]
    consts = gains + [bd64, bd32]
    res = pl.pallas_call(
        functools.partial(_prep_kernel, ts=ts, sb_scale=-scale * LOG2E),
        grid=(bn, sn // ts),
        in_specs=[cspec(n) for n in names] + [tab] * 4 + [full(a) for a in consts],
        out_specs=[o[1] for o in outs],
        out_shape=[o[0] for o in outs],
        compiler_params=_cparams("parallel", "parallel"),
        name="mixer_prep",
    )(*([proj] * len(names)), c64, s64, c32, s32, *consts)
    keys = ("a_q", "kc", "vc", "kx", "vs", "kw", "vw", "b_q", "b_k", "b_v", "c_q", "c_k", "c_v", "d_q", "d_k", "d_v")
    return dict(zip(keys, res))


def _layer_columns(d):
    cols = {}
    off = 0
    for name, width in (("br_g", N_BRANCH * d), ("ga", 3 * NSA_HEADS * HEAD_DIM), ("a_q", 256),
                        ("b_q", 256), ("b_k", 256), ("b_v", 256), ("c_q", 256), ("c_k", 256), ("c_v", 256),
                        ("d_q", 256), ("a_kcvc", 128), ("a_ksvs", 128), ("a_kwvw", 128), ("d_k", 128), ("d_v", 128)):
        cols[name] = (off, width)
        off += width
    return cols, off


def _reorder_w_in(w_in, d, n_pad):
    ref_splits = (256, 128, 128, 128, 12, 256, 256, 256, 256, 256, 256, 256, 128, 128, N_BRANCH * d)
    names = ("a_q", "a_kcvc", "a_ksvs", "a_kwvw", "a_g", "b_q", "b_k", "b_v",
             "c_q", "c_k", "c_v", "d_q", "d_k", "d_v", "br_g")
    starts = np.cumsum((0,) + ref_splits)
    src = {n: (int(starts[i]), ref_splits[i]) for i, n in enumerate(names)}
    cols, total = _layer_columns(d)
    idx = np.zeros((n_pad,), np.int32)
    keep = np.zeros((n_pad,), np.float32)
    for name, (off, width) in cols.items():
        if name == "ga":
            g0 = src["a_g"][0]
            for j in range(3):
                for h in range(NSA_HEADS):
                    base = off + j * NSA_HEADS * HEAD_DIM + h * HEAD_DIM
                    idx[base:base + HEAD_DIM] = g0 + h * 3 + j
        else:
            idx[off:off + width] = src[name][0] + np.arange(width)
        keep[off:off + width] = 1.0
    w = w_in[:, idx] * keep[None, :]
    return w.astype(BF16)


def _mixer_layer(x, positions, tables, mod, p, lam_init):
    bn, sn, d = x.shape
    sh1, sc1, g1 = mod[0], mod[1], mod[2]
    cols, total = _layer_columns(d)
    n_pad = -(-total // 768) * 768
    w_in = _reorder_w_in(p["w_in"], d, n_pad)
    proj = _in_projection(x, p["norm1"], sc1, sh1, w_in)
    n_cmp = (sn - NSA_CMP_LEN) // NSA_CMP_STRIDE + 1
    n_sel = sn // NSA_SEL_BLOCK
    n_top = min(NSA_N_SEL, n_sel)
    t = _prep(proj, cols, tables, p, n_sel)
    k_cmp = _compress(t["kc"], p["nsa_pe_k"], p["nsa_w1_k"], p["nsa_w2_k"]).astype(BF16)
    v_cmp = _compress(t["vc"], p["nsa_pe_v"], p["nsa_w1_v"], p["nsa_w2_v"]).astype(BF16)
    cmp_start = np.arange(sn // NSA_CMP_STRIDE) * NSA_CMP_STRIDE
    sel_start = np.arange(n_sel) * NSA_SEL_BLOCK
    overlap = ((cmp_start[:, None] <= (sel_start + NSA_SEL_BLOCK - 1)[None, :]) &
               ((cmp_start + NSA_CMP_LEN - 1)[:, None] >= sel_start[None, :]) &
               (np.arange(sn // NSA_CMP_STRIDE) < n_cmp)[:, None]).astype(np.float32)
    o_c, bias = _nsa_cmp(t["a_q"], k_cmp, v_cmp, jnp.asarray(overlap, BF16), n_cmp, n_top)
    o_s = _nsa_sel(t["a_q"], t["kx"], t["vs"], bias)
    o_w = _banded_attention(t["a_q"], t["kw"], t["vw"], NSA_WINDOW, NSA_WINDOW)
    o_b = _stick_breaking(t["b_q"], t["b_k"], t["b_v"])
    lam = (jnp.exp(jnp.sum(p["dif_lq1"] * p["dif_lk1"])) - jnp.exp(jnp.sum(p["dif_lq2"] * p["dif_lk2"]))
           + lam_init)
    o_cd = _diff_attention(t["c_q"], t["c_k"], t["c_v"], p["dif_subln"], lam, jnp.asarray(1.0 - lam_init, F32))
    o_d = _banded_attention(t["d_q"], t["d_k"], t["d_v"], SWA_WINDOW, 2 * SWA_WINDOW, sinks=p["swa_sinks"])
    return _merge(proj, o_c, o_s, o_w, o_b, o_cd, o_d, p["w_up"].astype(BF16), p["w_out"].astype(BF16),
                  x, g1, cols["ga"][0])


def kernel(x, c, positions, w_ada, b_ada, norm1, norm2, w_in, nsa_qn, nsa_kn, nsa_pe_k, nsa_w1_k, nsa_w2_k,
           nsa_pe_v, nsa_w1_v, nsa_w2_v, dif_qn, dif_kn, dif_lq1, dif_lk1, dif_lq2, dif_lk2, dif_subln,
           swa_qn, swa_kn, swa_sinks, w_up, w_out, w_router, b_router, w_gu, b_gu, w_dn, b_dn):
    bn, sn, d = x.shape
    depth = w_ada.shape[0]
    c_pad = jnp.zeros((8, d), F32).at[:bn].set(c)
    tables = _rope_tables(positions, HEAD_DIM) + _rope_tables(positions, DIFF_DIM)
    for l in range(depth):
        lam_init = 0.8 - 0.6 * math.exp(-0.3 * l)
        mod = _linear(c_pad, w_ada[l], b_ada[l], tn=512, precision=lax.Precision.HIGHEST)[:bn]
        mod = mod.reshape(bn, 6, 1, d).transpose(1, 0, 2, 3)
        p = dict(norm1=norm1[l], w_in=w_in[l], nsa_qn=nsa_qn[l], nsa_kn=nsa_kn[l], nsa_pe_k=nsa_pe_k[l],
                 nsa_w1_k=nsa_w1_k[l], nsa_w2_k=nsa_w2_k[l], nsa_pe_v=nsa_pe_v[l], nsa_w1_v=nsa_w1_v[l],
                 nsa_w2_v=nsa_w2_v[l], dif_qn=dif_qn[l], dif_kn=dif_kn[l], dif_lq1=dif_lq1[l],
                 dif_lk1=dif_lk1[l], dif_lq2=dif_lq2[l], dif_lk2=dif_lk2[l], dif_subln=dif_subln[l],
                 swa_qn=swa_qn[l], swa_kn=swa_kn[l], swa_sinks=swa_sinks[l], w_up=w_up[l], w_out=w_out[l])
        x = _mixer_layer(x, positions, tables, mod, p, lam_init)
        x = _moe(x, norm2[l], mod[4], mod[3], mod[5], w_router[l], b_router[l],
                 w_gu[l].astype(BF16), b_gu[l], w_dn[l].astype(BF16), b_dn[l])
    return x
```

```python
import functools
import math

import numpy as np
import jax
import jax.numpy as jnp
from jax import lax
from jax.experimental import pallas as pl
from jax.experimental.pallas import tpu as pltpu

F32 = jnp.float32
BF16 = jnp.bfloat16

HEAD_DIM = 64
ROPE_THETA = 10000.0
NORM_EPS = 1e-6
NEG_INF = -1e30
KNOCKOUT = -3e38
N_BRANCH = 4

NSA_HEADS = 4
NSA_CMP_LEN = 32
NSA_CMP_STRIDE = 16
NSA_SEL_BLOCK = 64
NSA_N_SEL = 16
NSA_WINDOW = 512
NSA_FORCED_SCORE = 1e4

SB_HEADS = 4
DIFF_HEADS = 4
DIFF_DIM = 32
SWA_HEADS = 4
SWA_KV_HEADS = 2
SWA_WINDOW = 128

N_EXPERTS = 32
TOP_K = 4
SWIGLU_ALPHA = 1.702
SWIGLU_LIMIT = 7.0

LANES = 128
LOG2E = 1.4426950408889634
MOE_ROWS = 512
VMEM_LIMIT = 56 * 1024 * 1024


def _cparams(*sem):
    return pltpu.CompilerParams(dimension_semantics=sem, vmem_limit_bytes=VMEM_LIMIT)


def _dot(a, b):
    return jnp.dot(a, b, preferred_element_type=F32)


def _dot_nt(a, b):
    return lax.dot_general(a, b, (((1,), (1,)), ((), ())), preferred_element_type=F32)


def _iota2(shape, dim):
    return lax.broadcasted_iota(jnp.int32, shape, dim)


def _linear_kernel(x_ref, w_ref, b_ref, o_ref, *, precision):
    o_ref[...] = jnp.dot(x_ref[...], w_ref[...], preferred_element_type=F32,
                         precision=precision) + b_ref[...]


def _linear(x, w, b, tn, precision=None):
    m, k = x.shape
    n = w.shape[1]
    return pl.pallas_call(
        functools.partial(_linear_kernel, precision=precision),
        grid=(n // tn,),
        in_specs=[pl.BlockSpec((m, k), lambda j: (0, 0)),
                  pl.BlockSpec((k, tn), lambda j: (0, j)),
                  pl.BlockSpec((1, tn), lambda j: (0, j))],
        out_specs=pl.BlockSpec((m, tn), lambda j: (0, j)),
        out_shape=jax.ShapeDtypeStruct((m, n), F32),
        compiler_params=_cparams("arbitrary"),
        name="linear",
    )(x, w, b.reshape(1, n))


def _cmp_mlp_kernel(a_ref, b_ref, pe_ref, w2_ref, o_ref):
    hid = jax.nn.gelu(a_ref[...] + b_ref[...] + pe_ref[...])
    o_ref[...] = _dot(hid.astype(BF16), w2_ref[...])


def _compress(t, pe, w1, w2):
    bn, sn, dh = t.shape
    st = NSA_CMP_STRIDE
    half = st * dh
    nb = sn // st
    t16 = t.reshape(bn * nb, half).astype(BF16)
    w1cat = jnp.concatenate([w1[:half], w1[half:]], axis=1).astype(BF16)
    hidden = w1.shape[1]
    ab = _linear(t16, w1cat, jnp.zeros((2 * hidden,), F32), tn=2 * hidden)
    ab = ab.reshape(bn, nb, 2 * hidden)
    a = ab[:, :, :hidden]
    b_next = jnp.concatenate([ab[:, 1:, hidden:], jnp.zeros((bn, 1, hidden), F32)], axis=1)
    pe_term = jnp.dot(pe.reshape(1, NSA_CMP_LEN * dh), w1, precision=lax.Precision.HIGHEST)
    rows = bn * nb
    tm = min(512, rows)
    out = pl.pallas_call(
        _cmp_mlp_kernel,
        grid=(rows // tm,),
        in_specs=[pl.BlockSpec((tm, hidden), lambda i: (i, 0)),
                  pl.BlockSpec((tm, hidden), lambda i: (i, 0)),
                  pl.BlockSpec((1, hidden), lambda i: (0, 0)),
                  pl.BlockSpec((hidden, dh), lambda i: (0, 0))],
        out_specs=pl.BlockSpec((tm, dh), lambda i: (i, 0)),
        out_shape=jax.ShapeDtypeStruct((rows, dh), F32),
        compiler_params=_cparams("parallel"),
        name="cmp_mlp",
    )(a.reshape(rows, hidden), b_next.reshape(rows, hidden), pe_term, w2.astype(BF16))
    return out.reshape(bn, nb, dh)


def _norm_mod(x, g, sc, sh):
    r = lax.rsqrt(jnp.mean(x * x, axis=-1, keepdims=True) + NORM_EPS)
    return (x * r * g) * (1.0 + sc) + sh


def _proj_kernel(x_ref, g_ref, sc_ref, sh_ref, w_ref, o_ref, h_scr):
    @pl.when(pl.program_id(2) == 0)
    def _():
        h_scr[...] = _norm_mod(x_ref[0], g_ref[...], sc_ref[0], sh_ref[0]).astype(BF16)

    o_ref[0] = _dot(h_scr[...], w_ref[...])


def _in_projection(x, g, sc, sh, w, tm=1024, tn=768):
    bn, sn, d = x.shape
    n = w.shape[1]
    return pl.pallas_call(
        _proj_kernel,
        grid=(bn, sn // tm, n // tn),
        in_specs=[pl.BlockSpec((1, tm, d), lambda b, i, j: (b, i, 0)),
                  pl.BlockSpec((1, d), lambda b, i, j: (0, 0)),
                  pl.BlockSpec((1, 1, d), lambda b, i, j: (b, 0, 0)),
                  pl.BlockSpec((1, 1, d), lambda b, i, j: (b, 0, 0)),
                  pl.BlockSpec((d, tn), lambda b, i, j: (0, j))],
        out_specs=pl.BlockSpec((1, tm, tn), lambda b, i, j: (b, i, j)),
        out_shape=jax.ShapeDtypeStruct((bn, sn, n), F32),
        scratch_shapes=[pltpu.VMEM((tm, d), BF16)],
        compiler_params=_cparams("parallel", "parallel", "arbitrary"),
        name="in_proj",
    )(x, g.reshape(1, d), sc, sh, w)


def _banded_kernel(*refs, tile, window, has_sink):
    if has_sink:
        sink_ref, q_ref, kp_ref, kc_ref, vp_ref, vc_ref, o_ref = refs
    else:
        q_ref, kp_ref, kc_ref, vp_ref, vc_ref, o_ref = refs
    i = pl.program_id(2)
    q = q_ref[0, 0]
    row = _iota2((tile, tile), 0)
    col = _iota2((tile, tile), 1)
    t = i * tile + row
    kpos_p = (i - 1) * tile + col
    valid_p = (t - kpos_p < window) & (kpos_p >= 0)
    valid_c = (col <= row) & (row - col < window)
    s_p = jnp.where(valid_p, _dot_nt(q, kp_ref[0, 0]), NEG_INF)
    s_c = jnp.where(valid_c, _dot_nt(q, kc_ref[0, 0]), NEG_INF)
    m = jnp.maximum(jnp.max(s_p, axis=-1, keepdims=True), jnp.max(s_c, axis=-1, keepdims=True))
    if has_sink:
        sink = sink_ref[pl.program_id(1)]
        m = jnp.maximum(m, sink)
    p_p = jnp.where(valid_p, jnp.exp(s_p - m), 0.0)
    p_c = jnp.where(valid_c, jnp.exp(s_c - m), 0.0)
    den = jnp.sum(p_p, axis=-1, keepdims=True) + jnp.sum(p_c, axis=-1, keepdims=True)
    if has_sink:
        den = den + jnp.exp(sink - m)
    else:
        den = jnp.maximum(den, 1e-30)
    o = _dot(p_p.astype(BF16), vp_ref[0, 0]) + _dot(p_c.astype(BF16), vc_ref[0, 0])
    o_ref[0, 0] = o / den


def _banded_attention(q, k, v, window, tile, sinks=None):
    bn, hq, sn, d = q.shape
    grp = hq // k.shape[1]
    assert tile >= window and sn % tile == 0
    has_sink = sinks is not None
    qspec = pl.BlockSpec((1, 1, tile, d), lambda b, h, i: (b, h, i, 0))
    prev = pl.BlockSpec((1, 1, tile, d), lambda b, h, i: (b, h // grp, jnp.maximum(i - 1, 0), 0))
    cur = pl.BlockSpec((1, 1, tile, d), lambda b, h, i: (b, h // grp, i, 0))
    in_specs = [qspec, prev, cur, prev, cur]
    args = [q, k, k, v, v]
    if has_sink:
        in_specs = [pl.BlockSpec(memory_space=pltpu.SMEM)] + in_specs
        args = [sinks.astype(F32)] + args
    return pl.pallas_call(
        functools.partial(_banded_kernel, tile=tile, window=window, has_sink=has_sink),
        grid=(bn, hq, sn // tile),
        in_specs=in_specs,
        out_specs=pl.BlockSpec((1, 1, tile, d), lambda b, h, i: (b, h, i, 0)),
        out_shape=jax.ShapeDtypeStruct((bn, hq, sn, d), F32),
        compiler_params=_cparams("parallel", "parallel", "parallel"),
        name="banded_attn",
    )(*args)


def _nsa_cmp_kernel(q_ref, kc_ref, vc_ref, ov_ref, oc_ref, sel_ref, *, tq, n_cmp, n_top, heads):
    i = pl.program_id(1)
    ncp = kc_ref.shape[1]
    nsel = ov_ref.shape[1]
    dh = q_ref.shape[3]
    t = i * tq + _iota2((tq, ncp), 0)
    n = _iota2((tq, ncp), 1)
    valid = (n * NSA_CMP_STRIDE + (NSA_CMP_LEN - 1) <= t) & (n < n_cmp)
    kc = kc_ref[0]
    vc = vc_ref[0]
    psum = jnp.zeros((tq, ncp), F32)
    for h in range(heads):
        s = jnp.where(valid, _dot_nt(q_ref[0, h], kc), NEG_INF)
        m = jnp.max(s, axis=-1, keepdims=True)
        p = jnp.where(valid, jnp.exp(s - m), 0.0)
        p = p / jnp.maximum(jnp.sum(p, axis=-1, keepdims=True), 1e-30)
        oc_ref[0, :, h * dh:(h + 1) * dh] = _dot(p.astype(BF16), vc)
        psum = psum + p
    hi = psum.astype(BF16)
    lo = (psum - hi.astype(F32)).astype(BF16)
    imp = _dot(hi, ov_ref[...]) + _dot(lo, ov_ref[...])

    tt = i * tq + _iota2((tq, nsel), 0)
    blk = _iota2((tq, nsel), 1)
    cur = tt >> (NSA_SEL_BLOCK.bit_length() - 1)
    forced = (blk == 0) | (blk == cur) | (blk == cur - 1)
    valid_s = blk * NSA_SEL_BLOCK <= tt
    score = jnp.where(forced, NSA_FORCED_SCORE, jnp.where(valid_s, imp, -1.0))
    blk_f = blk.astype(F32)

    def pick(_, carry):
        score, sel = carry
        m = jnp.max(score, axis=-1, keepdims=True)
        first = jnp.min(jnp.where(score == m, blk_f, float(nsel)), axis=-1, keepdims=True)
        hit = blk_f == first
        return jnp.where(hit, KNOCKOUT, score), jnp.where(hit, 0.0, sel)

    _, bias = lax.fori_loop(0, n_top, pick, (score, jnp.full((tq, nsel), NEG_INF, F32)))
    sel_ref[0] = bias.astype(BF16)


def _nsa_cmp(q, kc, vc, overlap, n_cmp, n_top, tq=256):
    bn, heads, sn, dh = q.shape
    ncp = kc.shape[1]
    nsel = overlap.shape[1]
    return pl.pallas_call(
        functools.partial(_nsa_cmp_kernel, tq=tq, n_cmp=n_cmp, n_top=n_top, heads=heads),
        grid=(bn, sn // tq),
        in_specs=[pl.BlockSpec((1, heads, tq, dh), lambda b, i: (b, 0, i, 0)),
                  pl.BlockSpec((1, ncp, dh), lambda b, i: (b, 0, 0)),
                  pl.BlockSpec((1, ncp, dh), lambda b, i: (b, 0, 0)),
                  pl.BlockSpec((ncp, nsel), lambda b, i: (0, 0))],
        out_specs=[pl.BlockSpec((1, tq, heads * dh), lambda b, i: (b, i, 0)),
                   pl.BlockSpec((1, tq, nsel), lambda b, i: (b, i, 0))],
        out_shape=[jax.ShapeDtypeStruct((bn, sn, heads * dh), F32),
                   jax.ShapeDtypeStruct((bn, sn, nsel), BF16)],
        compiler_params=_cparams("parallel", "parallel"),
        name="nsa_cmp_topk",
    )(q, kc, vc, overlap)


def _nsa_sel_kernel(q_ref, kx_ref, v_ref, bias_ref, o_ref, qx_scr, *, tq, tk, heads):
    i = pl.program_id(1)
    dh = v_ref.shape[2]
    bias = bias_ref[0]
    for h in range(heads):
        qx_scr[h] = jnp.concatenate([bias, q_ref[0, h]], axis=1)

    def tile(j, state, diagonal):
        start = pl.multiple_of(j * tk, tk)
        kx = kx_ref[0, pl.ds(start, tk), :]
        v = v_ref[0, pl.ds(start, tk), :]
        if diagonal:
            causal = (start + _iota2((tq, tk), 1)) <= (i * tq + _iota2((tq, tk), 0))
        new_state = []
        for h in range(heads):
            m_old, l_old, acc_old = state[h]
            s = _dot_nt(qx_scr[h], kx)
            if diagonal:
                s = jnp.where(causal, s, NEG_INF)
            m_new = jnp.maximum(m_old, jnp.max(s, axis=-1, keepdims=True))
            alpha = jnp.exp(m_old - m_new)
            p = jnp.exp(s - m_new)
            l_new = alpha * l_old + jnp.sum(p, axis=-1, keepdims=True)
            acc_new = alpha * acc_old + _dot(p.astype(BF16), v)
            new_state.append((m_new, l_new, acc_new))
        return tuple(new_state)

    init = tuple((jnp.full((tq, 1), NEG_INF, F32), jnp.zeros((tq, 1), F32), jnp.zeros((tq, dh), F32))
                 for _ in range(heads))
    n_full = (i * tq) // tk
    state = lax.fori_loop(0, n_full, lambda j, st: tile(j, st, False), init)
    state = tile(n_full, state, True)
    for h in range(heads):
        o_ref[0, :, h * dh:(h + 1) * dh] = state[h][2] / jnp.maximum(state[h][1], 1e-30)


def _nsa_sel(q, kx, v, bias, tq=1024, tk=1024):
    bn, heads, sn, dh = q.shape
    nsel = bias.shape[2]
    tq, tk = min(tq, sn), min(tk, sn)
    assert tk % tq == 0 and sn % tk == 0
    return pl.pallas_call(
        functools.partial(_nsa_sel_kernel, tq=tq, tk=tk, heads=heads),
        grid=(bn, sn // tq),
        in_specs=[pl.BlockSpec((1, heads, tq, dh), lambda b, i: (b, 0, i, 0)),
                  pl.BlockSpec((1, sn, nsel + dh), lambda b, i: (b, 0, 0)),
                  pl.BlockSpec((1, sn, dh), lambda b, i: (b, 0, 0)),
                  pl.BlockSpec((1, tq, nsel), lambda b, i: (b, i, 0))],
        out_specs=pl.BlockSpec((1, tq, heads * dh), lambda b, i: (b, i, 0)),
        out_shape=jax.ShapeDtypeStruct((bn, sn, heads * dh), F32),
        scratch_shapes=[pltpu.VMEM((heads, tq, nsel + dh), BF16)],
        compiler_params=_cparams("parallel", "arbitrary"),
        name="nsa_selected_attn",
    )(q, kx, v, bias)


def _sb_kernel(q_ref, k_ref, v_ref, u_ref, o_ref, *, tq, tk):
    i = pl.program_id(2)
    q = q_ref[0, 0]
    per_q = tq // tk

    def tile(jj, carry, diagonal):
        start = pl.multiple_of(jj * tk, tk)
        nz = _dot_nt(q, k_ref[0, 0, pl.ds(start, tk), :])
        neg_abs = lax.bitcast_convert_type(lax.bitcast_convert_type(nz, jnp.uint32) | jnp.uint32(0x80000000), F32)
        log_keep = jnp.minimum(nz, 0.0) - jnp.log2(1.0 + jnp.exp2(neg_abs))
        if diagonal:
            strict = (start + _iota2((tq, tk), 1)) < (i * tq + _iota2((tq, tk), 0))
            log_keep = jnp.where(strict, log_keep, 0.0)
        cum = _dot(log_keep.astype(BF16), u_ref[...])
        a = jnp.exp2(cum + jnp.concatenate([carry] * (tk // LANES), axis=1) - nz)
        if diagonal:
            a = jnp.where(strict, a, 0.0)
        out = _dot(a.astype(BF16), v_ref[0, 0, pl.ds(start, tk), :])
        return out, carry + jnp.broadcast_to(cum[:, 0:1], carry.shape)

    def group(first, carry, diagonal):
        total = None
        for r in range(per_q):
            out, carry = tile(first - r, carry, diagonal)
            total = out if total is None else total + out
        return total, carry

    acc, carry = group(i * per_q + per_q - 1, jnp.zeros((tq, LANES), F32), True)

    def body(p, state):
        acc, carry = state
        out, carry = group((i - p) * per_q - 1, carry, False)
        return acc + out, carry

    acc, _ = lax.fori_loop(0, i, body, (acc, carry))
    o_ref[0, 0] = acc


def _stick_breaking(q, k, v, tq=1024, tk=256):
    bn, heads, sn, dh = q.shape
    tq = min(tq, sn)
    incl = (np.arange(tk)[:, None] >= np.arange(tk)[None, :]).astype(np.float32)
    u = jnp.asarray(incl, BF16)
    kv = pl.BlockSpec((1, 1, sn, dh), lambda b, h, i: (b, h, 0, 0))
    return pl.pallas_call(
        functools.partial(_sb_kernel, tq=tq, tk=tk),
        grid=(bn, heads, sn // tq),
        in_specs=[pl.BlockSpec((1, 1, tq, dh), lambda b, h, i: (b, h, i, 0)), kv, kv,
                  pl.BlockSpec((tk, tk), lambda b, h, i: (0, 0))],
        out_specs=pl.BlockSpec((1, 1, tq, dh), lambda b, h, i: (b, h, i, 0)),
        out_shape=jax.ShapeDtypeStruct((bn, heads, sn, dh), F32),
        compiler_params=_cparams("parallel", "parallel", "arbitrary"),
        name="stick_breaking_attn",
    )(q, k, v, u)


def _diff_kernel(sc_ref, q_ref, k_ref, v_ref, g_ref, o_ref, *, tq, tk):
    i = pl.program_id(2)
    dv = v_ref.shape[3]

    def tile(j, state, diagonal):
        start = pl.multiple_of(j * tk, tk)
        v = v_ref[0, 0, pl.ds(start, tk), :]
        if diagonal:
            causal = (start + _iota2((tq, tk), 1)) <= (i * tq + _iota2((tq, tk), 0))
        new_state = []
        for mi in range(2):
            m_old, l_old, acc_old = state[mi]
            s = _dot_nt(q_ref[0, mi], k_ref[0, mi, pl.ds(start, tk), :])
            if diagonal:
                s = jnp.where(causal, s, NEG_INF)
            m_new = jnp.maximum(m_old, jnp.max(s, axis=-1, keepdims=True))
            alpha = jnp.exp(m_old - m_new)
            p = jnp.exp(s - m_new)
            l_new = alpha * l_old + jnp.sum(p, axis=-1, keepdims=True)
            acc_new = alpha * acc_old + _dot(p.astype(BF16), v)
            new_state.append((m_new, l_new, acc_new))
        return tuple(new_state)

    init = tuple((jnp.full((tq, 1), NEG_INF, F32), jnp.zeros((tq, 1), F32), jnp.zeros((tq, dv), F32))
                 for _ in range(2))
    n_full = (i * tq) // tk
    state = lax.fori_loop(0, n_full, lambda j, st: tile(j, st, False), init)
    state = tile(n_full, state, True)
    lam = sc_ref[0]
    post = sc_ref[1]
    o = (state[0][2] / jnp.maximum(state[0][1], 1e-30)
         - lam * (state[1][2] / jnp.maximum(state[1][1], 1e-30)))
    r = lax.rsqrt(jnp.mean(o * o, axis=-1, keepdims=True) + NORM_EPS)
    o_ref[0, 0] = (o * r * g_ref[...]) * post


def _diff_attention(q, k, v, subln, lam, post, tq=1024, tk=1024):
    bn, h2, sn, dd = q.shape
    heads = h2 // 2
    dv = v.shape[3]
    tq, tk = min(tq, sn), min(tk, sn)
    assert tk % tq == 0 and sn % tk == 0
    scal = jnp.stack([lam, post]).astype(F32)
    return pl.pallas_call(
        functools.partial(_diff_kernel, tq=tq, tk=tk),
        grid=(bn, heads, sn // tq),
        in_specs=[pl.BlockSpec(memory_space=pltpu.SMEM),
                  pl.BlockSpec((1, 2, tq, dd), lambda b, h, i: (b, h, i, 0)),
                  pl.BlockSpec((1, 2, sn, dd), lambda b, h, i: (b, h, 0, 0)),
                  pl.BlockSpec((1, 1, sn, dv), lambda b, h, i: (b, h, 0, 0)),
                  pl.BlockSpec((1, dv), lambda b, h, i: (0, 0))],
        out_specs=pl.BlockSpec((1, 1, tq, dv), lambda b, h, i: (b, h, i, 0)),
        out_shape=jax.ShapeDtypeStruct((bn, heads, sn, dv), F32),
        compiler_params=_cparams("parallel", "parallel", "arbitrary"),
        name="diff_attn",
    )(scal, q, k, v, subln.reshape(1, dv).astype(F32))


def _merge_kernel(brg_ref, ga0_ref, ga1_ref, ga2_ref, oc_ref, os_ref, ow_ref, ob_ref, ocd_ref, od_ref,
                  wup_ref, wout_ref, x_ref, g1_ref, o_ref, *, d):
    def heads(ref):
        return jnp.concatenate([ref[0, h] for h in range(ref.shape[1])], axis=1)

    o_a = (jax.nn.sigmoid(ga0_ref[0]) * oc_ref[0] + jax.nn.sigmoid(ga1_ref[0]) * os_ref[0]
           + jax.nn.sigmoid(ga2_ref[0]) * heads(ow_ref))
    branches = (o_a, heads(ob_ref), heads(ocd_ref), heads(od_ref))
    merged = None
    for bi, o in enumerate(branches):
        gate = jax.nn.sigmoid(brg_ref[0, :, bi * d:(bi + 1) * d])
        term = gate * _dot(o.astype(BF16), wup_ref[bi])
        merged = term if merged is None else merged + term
    y = _dot(merged.astype(BF16), wout_ref[...])
    o_ref[0] = x_ref[0] + g1_ref[0] * y


def _merge(proj, o_c, o_s, o_w, o_b, o_cd, o_d, w_up, w_out, x, g1, ga_col, tm=256):
    bn, sn, d = x.shape
    bw = o_c.shape[2]
    nh, dh = o_b.shape[1], o_b.shape[3]
    assert ga_col % bw == 0
    gblk = ga_col // bw
    row = lambda b, i: (b, i, 0)
    bspec = pl.BlockSpec((1, tm, bw), row)
    hspec = pl.BlockSpec((1, nh, tm, dh), lambda b, i: (b, 0, i, 0))
    return pl.pallas_call(
        functools.partial(_merge_kernel, d=d),
        grid=(bn, sn // tm),
        in_specs=[pl.BlockSpec((1, tm, N_BRANCH * d), row),
                  pl.BlockSpec((1, tm, bw), lambda b, i: (b, i, gblk)),
                  pl.BlockSpec((1, tm, bw), lambda b, i: (b, i, gblk + 1)),
                  pl.BlockSpec((1, tm, bw), lambda b, i: (b, i, gblk + 2)),
                  bspec, bspec, hspec, hspec, hspec, hspec,
                  pl.BlockSpec((N_BRANCH, bw, d), lambda b, i: (0, 0, 0)),
                  pl.BlockSpec((d, d), lambda b, i: (0, 0)),
                  pl.BlockSpec((1, tm, d), row),
                  pl.BlockSpec((1, 1, d), lambda b, i: (b, 0, 0))],
        out_specs=pl.BlockSpec((1, tm, d), row),
        out_shape=jax.ShapeDtypeStruct((bn, sn, d), F32),
        compiler_params=_cparams("parallel", "parallel"),
        name="branch_merge",
    )(proj, proj, proj, proj, o_c, o_s, o_w, o_b, o_cd, o_d, w_up, w_out, x, g1)


def _router_kernel(x_ref, g_ref, sc_ref, sh_ref, wr_ref, br_ref, tri_ref, h_ref, e_ref, w_ref, rank_ref, cnt_ref,
                   run_scr):
    @pl.when((pl.program_id(0) == 0) & (pl.program_id(1) == 0))
    def _():
        run_scr[...] = jnp.zeros(run_scr.shape, F32)

    h = _norm_mod(x_ref[0], g_ref[...], sc_ref[0], sh_ref[0])
    h_ref[0] = h.astype(BF16)
    logits = jnp.dot(h, wr_ref[...], preferred_element_type=F32,
                     precision=lax.Precision.HIGHEST) + br_ref[...]
    lane = _iota2(logits.shape, 1)
    lane_f = lane.astype(F32)
    cur = logits
    vals, idxs = [], []
    chosen = jnp.zeros(logits.shape, F32)
    for _ in range(TOP_K):
        m = jnp.max(cur, axis=-1, keepdims=True)
        first = jnp.min(jnp.where(cur == m, lane_f, float(LANES)), axis=-1, keepdims=True)
        vals.append(m)
        idxs.append(first)
        hit = lane_f == first
        cur = jnp.where(hit, KNOCKOUT, cur)
        chosen = jnp.where(hit, 1.0, chosen)
    exps = [jnp.exp(v - vals[0]) for v in vals]
    den = exps[0]
    for e in exps[1:]:
        den = den + e
    earlier = _dot(tri_ref[...], chosen.astype(BF16)) + run_scr[0:1, :]
    e_out = jnp.zeros(logits.shape, F32)
    w_out = jnp.zeros(logits.shape, F32)
    r_out = jnp.zeros(logits.shape, F32)
    for k in range(TOP_K):
        rank_k = jnp.sum(jnp.where(lane_f == idxs[k], earlier, 0.0), axis=-1, keepdims=True)
        e_out = jnp.where(lane == k, idxs[k], e_out)
        w_out = jnp.where(lane == k, exps[k] / den, w_out)
        r_out = jnp.where(lane == k, rank_k, r_out)
    e_ref[0] = e_out.astype(jnp.int32)
    w_ref[0] = w_out
    rank_ref[0] = r_out.astype(jnp.int32)
    total = run_scr[...] + jnp.sum(chosen, axis=0, keepdims=True)
    run_scr[...] = total
    cnt_ref[...] = total


def _router(x, g, sc, sh, w_router, b_router, tm=512):
    bn, sn, d = x.shape
    ne = w_router.shape[1]
    wr = jnp.zeros((d, LANES), F32).at[:, :ne].set(w_router)
    br = jnp.full((1, LANES), NEG_INF, F32).at[0, :ne].set(b_router)
    tri = jnp.asarray((np.arange(tm)[:, None] > np.arange(tm)[None, :]).astype(np.float32), BF16)
    row = lambda b, i: (b, i, 0)
    return pl.pallas_call(
        _router_kernel,
        grid=(bn, sn // tm),
        in_specs=[pl.BlockSpec((1, tm, d), row),
                  pl.BlockSpec((1, d), lambda b, i: (0, 0)),
                  pl.BlockSpec((1, 1, d), lambda b, i: (b, 0, 0)),
                  pl.BlockSpec((1, 1, d), lambda b, i: (b, 0, 0)),
                  pl.BlockSpec((d, LANES), lambda b, i: (0, 0)),
                  pl.BlockSpec((1, LANES), lambda b, i: (0, 0)),
                  pl.BlockSpec((tm, tm), lambda b, i: (0, 0))],
        out_specs=[pl.BlockSpec((1, tm, d), row),
                   pl.BlockSpec((1, tm, LANES), row),
                   pl.BlockSpec((1, tm, LANES), row),
                   pl.BlockSpec((1, tm, LANES), row),
                   pl.BlockSpec((8, LANES), lambda b, i: (0, 0))],
        out_shape=[jax.ShapeDtypeStruct((bn, sn, d), BF16),
                   jax.ShapeDtypeStruct((bn, sn, LANES), jnp.int32),
                   jax.ShapeDtypeStruct((bn, sn, LANES), F32),
                   jax.ShapeDtypeStruct((bn, sn, LANES), jnp.int32),
                   jax.ShapeDtypeStruct((8, LANES), F32)],
        scratch_shapes=[pltpu.VMEM((8, LANES), F32)],
        compiler_params=_cparams("arbitrary", "arbitrary"),
        name="moe_router",
    )(x, g.reshape(1, d), sc, sh, wr, br, tri)


def _expert_kernel(ce_ref, x_ref, wgu_ref, bgu_ref, wdn_ref, bdn_ref, rw_ref, o_ref, *, ff, fc):
    del ce_ref
    x = x_ref[...]
    y = None
    for c in range(ff // fc):
        g = _dot(x, wgu_ref[0, :, c * fc:(c + 1) * fc]) + bgu_ref[0, :, c * fc:(c + 1) * fc]
        u = _dot(x, wgu_ref[0, :, ff + c * fc:ff + (c + 1) * fc]) + bgu_ref[0, :, ff + c * fc:ff + (c + 1) * fc]
        g = jnp.minimum(g, SWIGLU_LIMIT)
        u = jnp.clip(u, -SWIGLU_LIMIT, SWIGLU_LIMIT)
        act = g * jax.nn.sigmoid(SWIGLU_ALPHA * g) * (u + 1.0)
        part = _dot(act.astype(BF16), wdn_ref[0, c * fc:(c + 1) * fc, :])
        y = part if y is None else y + part
    o_ref[...] = (y + bdn_ref[0]) * rw_ref[...]


def _expert_ffn(rows, chunk_e, w_gu, b_gu, w_dn, b_dn, row_w, tm=MOE_ROWS, fc=512):
    n_rows, d = rows.shape
    ne, _, ff2 = w_gu.shape
    ff = ff2 // 2
    grid_spec = pltpu.PrefetchScalarGridSpec(
        num_scalar_prefetch=1,
        grid=(n_rows // tm,),
        in_specs=[pl.BlockSpec((tm, d), lambda c, ce: (c, 0)),
                  pl.BlockSpec((1, d, ff2), lambda c, ce: (ce[c], 0, 0)),
                  pl.BlockSpec((1, 1, ff2), lambda c, ce: (ce[c], 0, 0)),
                  pl.BlockSpec((1, ff, d), lambda c, ce: (ce[c], 0, 0)),
                  pl.BlockSpec((1, 1, d), lambda c, ce: (ce[c], 0, 0)),
                  pl.BlockSpec((tm, 1), lambda c, ce: (c, 0))],
        out_specs=pl.BlockSpec((tm, d), lambda c, ce: (c, 0)),
    )
    return pl.pallas_call(
        functools.partial(_expert_kernel, ff=ff, fc=fc),
        grid_spec=grid_spec,
        out_shape=jax.ShapeDtypeStruct((n_rows, d), F32),
        compiler_params=_cparams("arbitrary"),
        name="moe_expert_ffn",
    )(chunk_e, rows, w_gu, b_gu.reshape(ne, 1, ff2), w_dn, b_dn.reshape(ne, 1, d), row_w.reshape(n_rows, 1))


def _combine_kernel(y_ref, x_ref, g2_ref, o_ref):
    tot = y_ref[0]
    for k in range(1, TOP_K):
        tot = tot + y_ref[k]
    o_ref[0] = x_ref[0] + g2_ref[0] * tot


def _combine(y4, x, g2, tm=512):
    bn, sn, d = x.shape
    row = lambda b, i: (b, i, 0)
    nt = sn // tm
    return pl.pallas_call(
        _combine_kernel,
        grid=(bn, nt),
        in_specs=[pl.BlockSpec((TOP_K, tm, d), lambda b, i: (0, b * nt + i, 0)),
                  pl.BlockSpec((1, tm, d), row),
                  pl.BlockSpec((1, 1, d), lambda b, i: (b, 0, 0))],
        out_specs=pl.BlockSpec((1, tm, d), row),
        out_shape=jax.ShapeDtypeStruct((bn, sn, d), F32),
        compiler_params=_cparams("parallel", "parallel"),
        name="moe_combine",
    )(y4, x, g2)


def _moe(x, g, sc, sh, g2, w_router, b_router, w_gu, b_gu, w_dn, b_dn):
    bn, sn, d = x.shape
    n_tok = bn * sn
    n_asg = n_tok * TOP_K
    tm = MOE_ROWS
    h, e_out, w_out, rank_out, totals = _router(x, g, sc, sh, w_router, b_router)
    e_tok = e_out[:, :, :TOP_K].reshape(n_tok, TOP_K)
    w_flat = w_out[:, :, :TOP_K].reshape(-1)
    rank = rank_out[:, :, :TOP_K].reshape(n_tok, TOP_K)
    counts = totals[0, :N_EXPERTS].astype(jnp.int32)
    starts = jnp.cumsum(counts) - counts
    padded = (counts + tm - 1) // tm * tm
    pad_ends = jnp.cumsum(padded)
    pad_starts = pad_ends - padded
    pos = pad_starts[e_tok] + rank
    n_chunks = n_asg // tm + N_EXPERTS
    n_rows = n_chunks * tm
    chunk_e = jnp.minimum(jnp.searchsorted(pad_ends, jnp.arange(n_chunks, dtype=jnp.int32) * tm, side='right'),
                          N_EXPERTS - 1).astype(jnp.int32)
    order = jnp.argsort(e_tok.reshape(-1))
    in_chunk = jnp.arange(tm, dtype=jnp.int32)[None, :]
    idx_in_e = (jnp.arange(n_chunks, dtype=jnp.int32) * tm - pad_starts[chunk_e])[:, None] + in_chunk
    row_valid = (idx_in_e < counts[chunk_e][:, None]).reshape(-1)
    src = jnp.clip(starts[chunk_e][:, None] + idx_in_e, 0, n_asg - 1).reshape(-1)
    asg = order[src].astype(jnp.int32)
    row_tok = jnp.where(row_valid, asg // TOP_K, n_tok)
    row_w = jnp.where(row_valid, w_flat[asg], 0.0)
    h_pad = jnp.concatenate([h.reshape(n_tok, d), jnp.zeros((1, d), BF16)], axis=0)
    rows = h_pad[row_tok]
    y = _expert_ffn(rows, chunk_e, w_gu, b_gu, w_dn, b_dn, row_w)
    return _combine(y[pos.T], x, g2)


def _norm_rope(x, g, cos, sin, bd, hd):
    sq = x * x
    hi = sq.astype(BF16)
    lo = (sq - hi.astype(F32)).astype(BF16)
    ss = _dot(hi, bd) + _dot(lo, bd)
    y = x * lax.rsqrt(ss * (1.0 / hd) + NORM_EPS) * g
    half = hd // 2
    first = (_iota2(x.shape, 1) & (hd - 1)) < half
    partner = jnp.where(first, pltpu.roll(y, LANES - half, 1), pltpu.roll(y, half, 1))
    return y * cos + partner * sin


def _prep_kernel(aq_ref, bq_ref, bk_ref, bv_ref, cq_ref, ck_ref, cv_ref, dq_ref, akc_ref, aks_ref, akw_ref,
                 dk_ref, dv_ref, c64_ref, s64_ref, c32_ref, s32_ref, gaq_ref, gak_ref, gcq_ref, gck_ref,
                 gdq_ref, gdk_ref, bd64_ref, bd32_ref,
                 oaq_ref, okc_ref, ovc_ref, okx_ref, ovs_ref, okw_ref, ovw_ref, obq_ref, obk_ref, obv_ref,
                 ocq_ref, ock_ref, ocv_ref, odq_ref, odk_ref, odv_ref, *, ts, sb_scale):
    hd, dd = HEAD_DIM, DIFF_DIM
    c64, s64, c32, s32 = c64_ref[0], s64_ref[0], c32_ref[0], s32_ref[0]
    bd64, bd32 = bd64_ref[...], bd32_ref[...]

    def slabs(ref):
        x = ref[0]
        return [x[:, c * LANES:(c + 1) * LANES] for c in range(x.shape[1] // LANES)]

    def put_heads(o_ref, c, y, width):
        per = LANES // width
        for u in range(per):
            o_ref[0, c * per + u] = y[:, u * width:(u + 1) * width].astype(o_ref.dtype)

    for src, gain, dst in ((aq_ref, gaq_ref, oaq_ref), (dq_ref, gdq_ref, odq_ref), (dk_ref, gdk_ref, odk_ref)):
        for c, x in enumerate(slabs(src)):
            put_heads(dst, c, _norm_rope(x, gain[:, c * LANES:(c + 1) * LANES], c64, s64, bd64, hd), hd)
    kc = _norm_rope(akc_ref[0], gak_ref[0:1, :], c64, s64, bd64, hd)
    ks = _norm_rope(aks_ref[0], gak_ref[1:2, :], c64, s64, bd64, hd)
    kw = _norm_rope(akw_ref[0], gak_ref[2:3, :], c64, s64, bd64, hd)
    okc_ref[0] = kc[:, :hd].astype(BF16)
    ovc_ref[0] = akc_ref[0][:, hd:].astype(BF16)
    nsel = okx_ref.shape[2] - hd
    blk = (pl.program_id(1) * ts + _iota2((ts, nsel), 0)) >> (NSA_SEL_BLOCK.bit_length() - 1)
    okx_ref[0, :, :nsel] = jnp.where(blk == _iota2((ts, nsel), 1), 1.0, 0.0).astype(BF16)
    okx_ref[0, :, nsel:] = ks[:, :hd].astype(BF16)
    ovs_ref[0] = aks_ref[0][:, hd:].astype(BF16)
    okw_ref[0, 0] = kw[:, :hd].astype(BF16)
    ovw_ref[0, 0] = akw_ref[0][:, hd:].astype(BF16)
    for c, x in enumerate(slabs(bq_ref)):
        put_heads(obq_ref, c, x * sb_scale, hd)
    for src, dst in ((bk_ref, obk_ref), (bv_ref, obv_ref), (cv_ref, ocv_ref)):
        for c, x in enumerate(slabs(src)):
            put_heads(dst, c, x, hd)
    put_heads(odv_ref, 0, dv_ref[0], hd)
    for src, gain, dst in ((cq_ref, gcq_ref, ocq_ref), (ck_ref, gck_ref, ock_ref)):
        for c, x in enumerate(slabs(src)):
            put_heads(dst, c, _norm_rope(x, gain[:, c * LANES:(c + 1) * LANES], c32, s32, bd32, dd), dd)


def _rope_tables(positions, hd):
    half = hd // 2
    inv = ROPE_THETA ** (-jnp.arange(half, dtype=F32) * 2.0 / hd)
    ang = positions.astype(F32)[..., None] * inv
    cos, sin = jnp.cos(ang), jnp.sin(ang)
    reps = LANES // hd
    return (jnp.tile(jnp.concatenate([cos, cos], axis=-1), (1, 1, reps)),
            jnp.tile(jnp.concatenate([-sin, sin], axis=-1), (1, 1, reps)))


def _prep(proj, cols, tables, p, n_sel, ts=512):
    bn, sn, _ = proj.shape
    hd, dd = HEAD_DIM, DIFF_DIM
    scale = hd ** -0.5
    c64, s64, c32, s32 = tables

    def cspec(name):
        off, width = cols[name]
        assert off % width == 0
        return pl.BlockSpec((1, ts, width), lambda b, i, blk=off // width: (b, i, blk))

    def tile_gain(g, reps, mult=1.0):
        return (jnp.tile(g.astype(F32), reps) * mult).reshape(1, -1)

    ones = jnp.ones((hd,), F32)
    gak = jnp.stack([jnp.concatenate([p["nsa_kn"][j].astype(F32), ones]) for j in range(3)])
    gains = [tile_gain(p["nsa_qn"], NSA_HEADS, scale), gak,
             tile_gain(p["dif_qn"], 2 * DIFF_HEADS, dd ** -0.5), tile_gain(p["dif_kn"], 2 * DIFF_HEADS),
             tile_gain(p["swa_qn"], SWA_HEADS, scale), tile_gain(p["swa_kn"], SWA_KV_HEADS)]
    lane = np.arange(LANES)
    bd64 = jnp.asarray((lane[:, None] // hd == lane[None, :] // hd).astype(np.float32), BF16)
    bd32 = jnp.asarray((lane[:, None] // dd == lane[None, :] // dd).astype(np.float32), BF16)
    names = ("a_q", "b_q", "b_k", "b_v", "c_q", "c_k", "c_v", "d_q", "a_kcvc", "a_ksvs", "a_kwvw", "d_k", "d_v")
    tab = pl.BlockSpec((1, ts, LANES), lambda b, i: (b, i, 0))
    full = lambda a: pl.BlockSpec(a.shape, lambda b, i: (0,) * a.ndim)

    def hm(nh, w):
        return (jax.ShapeDtypeStruct((bn, nh, sn, w), BF16), pl.BlockSpec((1, nh, ts, w), lambda b, i: (b, 0, i, 0)))

    def tm_(w):
        return (jax.ShapeDtypeStruct((bn, sn, w), BF16), pl.BlockSpec((1, ts, w), lambda b, i: (b, i, 0)))

    outs = [hm(NSA_HEADS, hd), tm_(hd), tm_(hd), tm_(n_sel + hd), tm_(hd), hm(1, hd), hm(1, hd),
            hm(SB_HEADS, hd), hm(SB_HEADS, hd), hm(SB_HEADS, hd),
            hm(2 * DIFF_HEADS, dd), hm(2 * DIFF_HEADS, dd), hm(DIFF_HEADS, 2 * dd),
            hm(SWA_HEADS, hd), hm(SWA_KV_HEADS, hd), hm(SWA_KV_HEADS, hd)]
    consts = gains + [bd64, bd32]
    res = pl.pallas_call(
        functools.partial(_prep_kernel, ts=ts, sb_scale=-scale * LOG2E),
        grid=(bn, sn // ts),
        in_specs=[cspec(n) for n in names] + [tab] * 4 + [full(a) for a in consts],
        out_specs=[o[1] for o in outs],
        out_shape=[o[0] for o in outs],
        compiler_params=_cparams("parallel", "parallel"),
        name="mixer_prep",
    )(*([proj] * len(names)), c64, s64, c32, s32, *consts)
    keys = ("a_q", "kc", "vc", "kx", "vs", "kw", "vw", "b_q", "b_k", "b_v", "c_q", "c_k", "c_v", "d_q", "d_k", "d_v")
    return dict(zip(keys, res))


def _layer_columns(d):
    cols = {}
    off = 0
    for name, width in (("br_g", N_BRANCH * d), ("ga", 3 * NSA_HEADS * HEAD_DIM), ("a_q", 256),
                        ("b_q", 256), ("b_k", 256), ("b_v", 256), ("c_q", 256), ("c_k", 256), ("c_v", 256),
                        ("d_q", 256), ("a_kcvc", 128), ("a_ksvs", 128), ("a_kwvw", 128), ("d_k", 128), ("d_v", 128)):
        cols[name] = (off, width)
        off += width
    return cols, off


def _reorder_w_in(w_in, d, n_pad):
    ref_splits = (256, 128, 128, 128, 12, 256, 256, 256, 256, 256, 256, 256, 128, 128, N_BRANCH * d)
    names = ("a_q", "a_kcvc", "a_ksvs", "a_kwvw", "a_g", "b_q", "b_k", "b_v",
             "c_q", "c_k", "c_v", "d_q", "d_k", "d_v", "br_g")
    starts = np.cumsum((0,) + ref_splits)
    src = {n: (int(starts[i]), ref_splits[i]) for i, n in enumerate(names)}
    cols, total = _layer_columns(d)
    idx = np.zeros((n_pad,), np.int32)
    keep = np.zeros((n_pad,), np.float32)
    for name, (off, width) in cols.items():
        if name == "ga":
            g0 = src["a_g"][0]
            for j in range(3):
                for h in range(NSA_HEADS):
                    base = off + j * NSA_HEADS * HEAD_DIM + h * HEAD_DIM
                    idx[base:base + HEAD_DIM] = g0 + h * 3 + j
        else:
            idx[off:off + width] = src[name][0] + np.arange(width)
        keep[off:off + width] = 1.0
    w = w_in[:, idx] * keep[None, :]
    return w.astype(BF16)


def _mixer_layer(x, positions, tables, mod, p, lam_init):
    bn, sn, d = x.shape
    sh1, sc1, g1 = mod[0], mod[1], mod[2]
    cols, total = _layer_columns(d)
    n_pad = -(-total // 768) * 768
    w_in = _reorder_w_in(p["w_in"], d, n_pad)
    proj = _in_projection(x, p["norm1"], sc1, sh1, w_in)
    n_cmp = (sn - NSA_CMP_LEN) // NSA_CMP_STRIDE + 1
    n_sel = sn // NSA_SEL_BLOCK
    n_top = min(NSA_N_SEL, n_sel)
    t = _prep(proj, cols, tables, p, n_sel)
    k_cmp = _compress(t["kc"], p["nsa_pe_k"], p["nsa_w1_k"], p["nsa_w2_k"]).astype(BF16)
    v_cmp = _compress(t["vc"], p["nsa_pe_v"], p["nsa_w1_v"], p["nsa_w2_v"]).astype(BF16)
    cmp_start = np.arange(sn // NSA_CMP_STRIDE) * NSA_CMP_STRIDE
    sel_start = np.arange(n_sel) * NSA_SEL_BLOCK
    overlap = ((cmp_start[:, None] <= (sel_start + NSA_SEL_BLOCK - 1)[None, :]) &
               ((cmp_start + NSA_CMP_LEN - 1)[:, None] >= sel_start[None, :]) &
               (np.arange(sn // NSA_CMP_STRIDE) < n_cmp)[:, None]).astype(np.float32)
    o_c, bias = _nsa_cmp(t["a_q"], k_cmp, v_cmp, jnp.asarray(overlap, BF16), n_cmp, n_top)
    o_s = _nsa_sel(t["a_q"], t["kx"], t["vs"], bias)
    o_w = _banded_attention(t["a_q"], t["kw"], t["vw"], NSA_WINDOW, NSA_WINDOW)
    o_b = _stick_breaking(t["b_q"], t["b_k"], t["b_v"])
    lam = (jnp.exp(jnp.sum(p["dif_lq1"] * p["dif_lk1"])) - jnp.exp(jnp.sum(p["dif_lq2"] * p["dif_lk2"]))
           + lam_init)
    o_cd = _diff_attention(t["c_q"], t["c_k"], t["c_v"], p["dif_subln"], lam, jnp.asarray(1.0 - lam_init, F32))
    o_d = _banded_attention(t["d_q"], t["d_k"], t["d_v"], SWA_WINDOW, 2 * SWA_WINDOW, sinks=p["swa_sinks"])
    return _merge(proj, o_c, o_s, o_w, o_b, o_cd, o_d, p["w_up"].astype(BF16), p["w_out"].astype(BF16),
                  x, g1, cols["ga"][0])


def kernel(x, c, positions, w_ada, b_ada, norm1, norm2, w_in, nsa_qn, nsa_kn, nsa_pe_k, nsa_w1_k, nsa_w2_k,
           nsa_pe_v, nsa_w1_v, nsa_w2_v, dif_qn, dif_kn, dif_lq1, dif_lk1, dif_lq2, dif_lk2, dif_subln,
           swa_qn, swa_kn, swa_sinks, w_up, w_out, w_router, b_router, w_gu, b_gu, w_dn, b_dn):
    bn, sn, d = x.shape
    depth = w_ada.shape[0]
    c_pad = jnp.zeros((8, d), F32).at[:bn].set(c)
    tables = _rope_tables(positions, HEAD_DIM) + _rope_tables(positions, DIFF_DIM)
    for l in range(depth):
        lam_init = 0.8 - 0.6 * math.exp(-0.3 * l)
        mod = _linear(c_pad, w_ada[l], b_ada[l], tn=512, precision=lax.Precision.HIGHEST)[:bn]
        mod = mod.reshape(bn, 6, 1, d).transpose(1, 0, 2, 3)
        p = dict(norm1=norm1[l], w_in=w_in[l], nsa_qn=nsa_qn[l], nsa_kn=nsa_kn[l], nsa_pe_k=nsa_pe_k[l],
                 nsa_w1_k=nsa_w1_k[l], nsa_w2_k=nsa_w2_k[l], nsa_pe_v=nsa_pe_v[l], nsa_w1_v=nsa_w1_v[l],
                 nsa_w2_v=nsa_w2_v[l], dif_qn=dif_qn[l], dif_kn=dif_kn[l], dif_lq1=dif_lq1[l],
                 dif_lk1=dif_lk1[l], dif_lq2=dif_lq2[l], dif_lk2=dif_lk2[l], dif_subln=dif_subln[l],
                 swa_qn=swa_qn[l], swa_kn=swa_kn[l], swa_sinks=swa_sinks[l], w_up=w_up[l], w_out=w_out[l])
        x = _mixer_layer(x, positions, tables, mod, p, lam_init)
        x = _moe(x, norm2[l], mod[4], mod[3], mod[5], w_router[l], b_router[l],
                 w_gu[l].astype(BF16), b_gu[l], w_dn[l].astype(BF16), b_dn[l])
    return x
```

```python
import functools
import math

import numpy as np
import jax
import jax.numpy as jnp
from jax import lax
from jax.experimental import pallas as pl
from jax.experimental.pallas import tpu as pltpu

F32 = jnp.float32
BF16 = jnp.bfloat16

HEAD_DIM = 64
ROPE_THETA = 10000.0
NORM_EPS = 1e-6
NEG_INF = -1e30
KNOCKOUT = -3e38
N_BRANCH = 4

NSA_HEADS = 4
NSA_CMP_LEN = 32
NSA_CMP_STRIDE = 16
NSA_SEL_BLOCK = 64
NSA_N_SEL = 16
NSA_WINDOW = 512
NSA_FORCED_SCORE = 1e4

SB_HEADS = 4
DIFF_HEADS = 4
DIFF_DIM = 32
SWA_HEADS = 4
SWA_KV_HEADS = 2
SWA_WINDOW = 128

N_EXPERTS = 32
TOP_K = 4
SWIGLU_ALPHA = 1.702
SWIGLU_LIMIT = 7.0

LANES = 128
LOG2E = 1.4426950408889634
MOE_ROWS = 512
VMEM_LIMIT = 56 * 1024 * 1024


def _cparams(*sem):
    return pltpu.CompilerParams(dimension_semantics=sem, vmem_limit_bytes=VMEM_LIMIT)


def _dot(a, b):
    return jnp.dot(a, b, preferred_element_type=F32)


def _dot_nt(a, b):
    return lax.dot_general(a, b, (((1,), (1,)), ((), ())), preferred_element_type=F32)


def _iota2(shape, dim):
    return lax.broadcasted_iota(jnp.int32, shape, dim)


def _linear_kernel(x_ref, w_ref, b_ref, o_ref, *, precision):
    o_ref[...] = jnp.dot(x_ref[...], w_ref[...], preferred_element_type=F32,
                         precision=precision) + b_ref[...]


def _linear(x, w, b, tn, precision=None):
    m, k = x.shape
    n = w.shape[1]
    return pl.pallas_call(
        functools.partial(_linear_kernel, precision=precision),
        grid=(n // tn,),
        in_specs=[pl.BlockSpec((m, k), lambda j: (0, 0)),
                  pl.BlockSpec((k, tn), lambda j: (0, j)),
                  pl.BlockSpec((1, tn), lambda j: (0, j))],
        out_specs=pl.BlockSpec((m, tn), lambda j: (0, j)),
        out_shape=jax.ShapeDtypeStruct((m, n), F32),
        compiler_params=_cparams("arbitrary"),
        name="linear",
    )(x, w, b.reshape(1, n))


def _cmp_mlp_kernel(a_ref, b_ref, pe_ref, w2_ref, o_ref):
    hid = jax.nn.gelu(a_ref[...] + b_ref[...] + pe_ref[...])
    o_ref[...] = _dot(hid.astype(BF16), w2_ref[...])


def _compress(t, pe, w1, w2):
    bn, sn, dh = t.shape
    st = NSA_CMP_STRIDE
    half = st * dh
    nb = sn // st
    t16 = t.reshape(bn * nb, half).astype(BF16)
    w1cat = jnp.concatenate([w1[:half], w1[half:]], axis=1).astype(BF16)
    hidden = w1.shape[1]
    ab = _linear(t16, w1cat, jnp.zeros((2 * hidden,), F32), tn=2 * hidden)
    ab = ab.reshape(bn, nb, 2 * hidden)
    a = ab[:, :, :hidden]
    b_next = jnp.concatenate([ab[:, 1:, hidden:], jnp.zeros((bn, 1, hidden), F32)], axis=1)
    pe_term = jnp.dot(pe.reshape(1, NSA_CMP_LEN * dh), w1, precision=lax.Precision.HIGHEST)
    rows = bn * nb
    tm = min(512, rows)
    out = pl.pallas_call(
        _cmp_mlp_kernel,
        grid=(rows // tm,),
        in_specs=[pl.BlockSpec((tm, hidden), lambda i: (i, 0)),
                  pl.BlockSpec((tm, hidden), lambda i: (i, 0)),
                  pl.BlockSpec((1, hidden), lambda i: (0, 0)),
                  pl.BlockSpec((hidden, dh), lambda i: (0, 0))],
        out_specs=pl.BlockSpec((tm, dh), lambda i: (i, 0)),
        out_shape=jax.ShapeDtypeStruct((rows, dh), F32),
        compiler_params=_cparams("parallel"),
        name="cmp_mlp",
    )(a.reshape(rows, hidden), b_next.reshape(rows, hidden), pe_term, w2.astype(BF16))
    return out.reshape(bn, nb, dh)


def _norm_mod(x, g, sc, sh):
    r = lax.rsqrt(jnp.mean(x * x, axis=-1, keepdims=True) + NORM_EPS)
    return (x * r * g) * (1.0 + sc) + sh


def _proj_kernel(x_ref, g_ref, sc_ref, sh_ref, w_ref, o_ref, h_scr):
    @pl.when(pl.program_id(2) == 0)
    def _():
        h_scr[...] = _norm_mod(x_ref[0], g_ref[...], sc_ref[0], sh_ref[0]).astype(BF16)

    o_ref[0] = _dot(h_scr[...], w_ref[...])


def _in_projection(x, g, sc, sh, w, tm=1024, tn=768):
    bn, sn, d = x.shape
    n = w.shape[1]
    return pl.pallas_call(
        _proj_kernel,
        grid=(bn, sn // tm, n // tn),
        in_specs=[pl.BlockSpec((1, tm, d), lambda b, i, j: (b, i, 0)),
                  pl.BlockSpec((1, d), lambda b, i, j: (0, 0)),
                  pl.BlockSpec((1, 1, d), lambda b, i, j: (b, 0, 0)),
                  pl.BlockSpec((1, 1, d), lambda b, i, j: (b, 0, 0)),
                  pl.BlockSpec((d, tn), lambda b, i, j: (0, j))],
        out_specs=pl.BlockSpec((1, tm, tn), lambda b, i, j: (b, i, j)),
        out_shape=jax.ShapeDtypeStruct((bn, sn, n), F32),
        scratch_shapes=[pltpu.VMEM((tm, d), BF16)],
        compiler_params=_cparams("parallel", "parallel", "arbitrary"),
        name="in_proj",
    )(x, g.reshape(1, d), sc, sh, w)


def _banded_kernel(*refs, tile, window, has_sink):
    if has_sink:
        sink_ref, q_ref, kp_ref, kc_ref, vp_ref, vc_ref, o_ref = refs
    else:
        q_ref, kp_ref, kc_ref, vp_ref, vc_ref, o_ref = refs
    i = pl.program_id(2)
    w = window
    upper = _iota2((w, w), 1) > _iota2((w, w), 0)
    first_bias = jnp.where(i > 0, 0.0, NEG_INF)
    for u in range(tile // w):
        q = q_ref[0, 0, u * w:(u + 1) * w, :]
        if u == 0:
            k_prev, v_prev = kp_ref[0, 0], vp_ref[0, 0]
        else:
            k_prev, v_prev = kc_ref[0, 0, (u - 1) * w:u * w, :], vc_ref[0, 0, (u - 1) * w:u * w, :]
        k_cur, v_cur = kc_ref[0, 0, u * w:(u + 1) * w, :], vc_ref[0, 0, u * w:(u + 1) * w, :]
        s_prev = _dot_nt(q, k_prev)
        if u == 0:
            s_prev = s_prev + first_bias
        s = jnp.where(upper, s_prev, _dot_nt(q, k_cur))
        m = jnp.max(s, axis=-1, keepdims=True)
        if has_sink:
            sink = sink_ref[pl.program_id(1)]
            m = jnp.maximum(m, sink)
        p = jnp.exp(s - m)
        den = jnp.sum(p, axis=-1, keepdims=True)
        if has_sink:
            den = den + jnp.exp(sink - m)
        else:
            den = jnp.maximum(den, 1e-30)
        o = (_dot(jnp.where(upper, p, 0.0).astype(BF16), v_prev)
             + _dot(jnp.where(upper, 0.0, p).astype(BF16), v_cur))
        o_ref[0, 0, u * w:(u + 1) * w, :] = o / den


def _banded_attention(q, k, v, window, sinks=None, tile=1024):
    bn, hq, sn, d = q.shape
    grp = hq // k.shape[1]
    tile = min(tile, sn)
    assert tile % window == 0 and sn % tile == 0
    per = tile // window
    has_sink = sinks is not None
    qspec = pl.BlockSpec((1, 1, tile, d), lambda b, h, i: (b, h, i, 0))
    prev = pl.BlockSpec((1, 1, window, d), lambda b, h, i: (b, h // grp, jnp.maximum(i * per - 1, 0), 0))
    cur = pl.BlockSpec((1, 1, tile, d), lambda b, h, i: (b, h // grp, i, 0))
    in_specs = [qspec, prev, cur, prev, cur]
    args = [q, k, k, v, v]
    if has_sink:
        in_specs = [pl.BlockSpec(memory_space=pltpu.SMEM)] + in_specs
        args = [sinks.astype(F32)] + args
    return pl.pallas_call(
        functools.partial(_banded_kernel, tile=tile, window=window, has_sink=has_sink),
        grid=(bn, hq, sn // tile),
        in_specs=in_specs,
        out_specs=pl.BlockSpec((1, 1, tile, d), lambda b, h, i: (b, h, i, 0)),
        out_shape=jax.ShapeDtypeStruct((bn, hq, sn, d), F32),
        compiler_params=_cparams("parallel", "parallel", "parallel"),
        name="banded_attn",
    )(*args)


def _nsa_cmp_kernel(q_ref, kc_ref, vct_ref, ovt_ref, oct_ref, bias_ref, *, tq, n_cmp, n_top, heads):
    i = pl.program_id(1)
    ncp = kc_ref.shape[1]
    nsel = ovt_ref.shape[0]
    dh = q_ref.shape[3]
    t = i * tq + _iota2((ncp, tq), 1)
    n = _iota2((ncp, tq), 0)
    valid = (n * NSA_CMP_STRIDE + (NSA_CMP_LEN - 1) <= t) & (n < n_cmp)
    kc = kc_ref[0]
    vct = vct_ref[0]
    psum = jnp.zeros((ncp, tq), F32)
    for h in range(heads):
        s = jnp.where(valid, _dot_nt(kc, q_ref[0, h]), NEG_INF)
        m = jnp.max(s, axis=0, keepdims=True)
        p = jnp.where(valid, jnp.exp(s - m), 0.0)
        p = p / jnp.maximum(jnp.sum(p, axis=0, keepdims=True), 1e-30)
        oct_ref[0, h * dh:(h + 1) * dh, :] = _dot(vct, p.astype(BF16))
        psum = psum + p
    hi = psum.astype(BF16)
    lo = (psum - hi.astype(F32)).astype(BF16)
    imp = _dot(ovt_ref[...], hi) + _dot(ovt_ref[...], lo)

    tt = i * tq + _iota2((nsel, tq), 1)
    blk = _iota2((nsel, tq), 0)
    cur = tt >> (NSA_SEL_BLOCK.bit_length() - 1)
    forced = (blk == 0) | (blk == cur) | (blk == cur - 1)
    valid_s = blk * NSA_SEL_BLOCK <= tt
    score = jnp.where(forced, NSA_FORCED_SCORE, jnp.where(valid_s, imp, -1.0))
    blk_f = blk.astype(F32)

    def pick(_, carry):
        score, sel = carry
        m = jnp.max(score, axis=0, keepdims=True)
        first = jnp.min(jnp.where(score == m, blk_f, float(nsel)), axis=0, keepdims=True)
        hit = blk_f == first
        return jnp.where(hit, KNOCKOUT, score), jnp.where(hit, 0.0, sel)

    _, bias = lax.fori_loop(0, n_top, pick, (score, jnp.full((nsel, tq), NEG_INF, F32)))
    bias_ref[0] = bias.astype(BF16)


def _nsa_cmp(q, kc, vc, overlap, n_cmp, n_top, tq=256):
    bn, heads, sn, dh = q.shape
    ncp = kc.shape[1]
    nsel = overlap.shape[1]
    oct, bias_t = pl.pallas_call(
        functools.partial(_nsa_cmp_kernel, tq=tq, n_cmp=n_cmp, n_top=n_top, heads=heads),
        grid=(bn, sn // tq),
        in_specs=[pl.BlockSpec((1, heads, tq, dh), lambda b, i: (b, 0, i, 0)),
                  pl.BlockSpec((1, ncp, dh), lambda b, i: (b, 0, 0)),
                  pl.BlockSpec((1, dh, ncp), lambda b, i: (b, 0, 0)),
                  pl.BlockSpec((nsel, ncp), lambda b, i: (0, 0))],
        out_specs=[pl.BlockSpec((1, heads * dh, tq), lambda b, i: (b, 0, i)),
                   pl.BlockSpec((1, nsel, tq), lambda b, i: (b, 0, i))],
        out_shape=[jax.ShapeDtypeStruct((bn, heads * dh, sn), F32),
                   jax.ShapeDtypeStruct((bn, nsel, sn), BF16)],
        compiler_params=_cparams("parallel", "parallel"),
        name="nsa_cmp_topk",
    )(q, kc, vc.transpose(0, 2, 1), overlap.T)
    return oct.transpose(0, 2, 1), bias_t.transpose(0, 2, 1)


def _nsa_sel_kernel(q_ref, kx_ref, v_ref, bias_ref, o_ref, qx_scr, *, tq, tk, heads):
    i = pl.program_id(1)
    dh = v_ref.shape[2]
    bias = bias_ref[0]
    for h in range(heads):
        qx_scr[h] = jnp.concatenate([bias, q_ref[0, h]], axis=1)

    def tile(j, state, diagonal):
        start = pl.multiple_of(j * tk, tk)
        kx = kx_ref[0, pl.ds(start, tk), :]
        v = v_ref[0, pl.ds(start, tk), :]
        if diagonal:
            causal = (start + _iota2((tq, tk), 1)) <= (i * tq + _iota2((tq, tk), 0))
        new_state = []
        for h in range(heads):
            m_old, l_old, acc_old = state[h]
            s = _dot_nt(qx_scr[h], kx)
            if diagonal:
                s = jnp.where(causal, s, NEG_INF)
            m_new = jnp.maximum(m_old, jnp.max(s, axis=-1, keepdims=True))
            alpha = jnp.exp(m_old - m_new)
            p = jnp.exp(s - m_new)
            l_new = alpha * l_old + jnp.sum(p, axis=-1, keepdims=True)
            acc_new = alpha * acc_old + _dot(p.astype(BF16), v)
            new_state.append((m_new, l_new, acc_new))
        return tuple(new_state)

    init = tuple((jnp.full((tq, 1), NEG_INF, F32), jnp.zeros((tq, 1), F32), jnp.zeros((tq, dh), F32))
                 for _ in range(heads))
    n_full = (i * tq) // tk
    state = lax.fori_loop(0, n_full, lambda j, st: tile(j, st, False), init)
    state = tile(n_full, state, True)
    for h in range(heads):
        o_ref[0, :, h * dh:(h + 1) * dh] = state[h][2] / jnp.maximum(state[h][1], 1e-30)


def _nsa_sel(q, kx, v, bias, tq=1024, tk=1024):
    bn, heads, sn, dh = q.shape
    nsel = bias.shape[2]
    tq, tk = min(tq, sn), min(tk, sn)
    assert tk % tq == 0 and sn % tk == 0
    return pl.pallas_call(
        functools.partial(_nsa_sel_kernel, tq=tq, tk=tk, heads=heads),
        grid=(bn, sn // tq),
        in_specs=[pl.BlockSpec((1, heads, tq, dh), lambda b, i: (b, 0, i, 0)),
                  pl.BlockSpec((1, sn, nsel + dh), lambda b, i: (b, 0, 0)),
                  pl.BlockSpec((1, sn, dh), lambda b, i: (b, 0, 0)),
                  pl.BlockSpec((1, tq, nsel), lambda b, i: (b, i, 0))],
        out_specs=pl.BlockSpec((1, tq, heads * dh), lambda b, i: (b, i, 0)),
        out_shape=jax.ShapeDtypeStruct((bn, sn, heads * dh), F32),
        scratch_shapes=[pltpu.VMEM((heads, tq, nsel + dh), BF16)],
        compiler_params=_cparams("parallel", "arbitrary"),
        name="nsa_selected_attn",
    )(q, kx, v, bias)


def _sb_kernel(q_ref, k_ref, v_ref, u_ref, o_ref, *, tq, tk):
    i = pl.program_id(2)
    q = q_ref[0, 0]
    per_q = tq // tk

    def tile(jj, carry, diagonal):
        start = pl.multiple_of(jj * tk, tk)
        nz = _dot_nt(q, k_ref[0, 0, pl.ds(start, tk), :])
        neg_abs = lax.bitcast_convert_type(lax.bitcast_convert_type(nz, jnp.uint32) | jnp.uint32(0x80000000), F32)
        log_keep = jnp.minimum(nz, 0.0) - jnp.log2(1.0 + jnp.exp2(neg_abs))
        if diagonal:
            strict = (start + _iota2((tq, tk), 1)) < (i * tq + _iota2((tq, tk), 0))
            log_keep = jnp.where(strict, log_keep, 0.0)
        cum = _dot(log_keep.astype(BF16), u_ref[...])
        a = jnp.exp2(cum + jnp.concatenate([carry] * (tk // LANES), axis=1) - nz)
        if diagonal:
            a = jnp.where(strict, a, 0.0)
        out = _dot(a.astype(BF16), v_ref[0, 0, pl.ds(start, tk), :])
        return out, carry + jnp.broadcast_to(cum[:, 0:1], carry.shape)

    def group(first, carry, diagonal):
        total = None
        for r in range(per_q):
            out, carry = tile(first - r, carry, diagonal)
            total = out if total is None else total + out
        return total, carry

    acc, carry = group(i * per_q + per_q - 1, jnp.zeros((tq, LANES), F32), True)

    def body(p, state):
        acc, carry = state
        out, carry = group((i - p) * per_q - 1, carry, False)
        return acc + out, carry

    acc, _ = lax.fori_loop(0, i, body, (acc, carry))
    o_ref[0, 0] = acc


def _stick_breaking(q, k, v, tq=1024, tk=256):
    bn, heads, sn, dh = q.shape
    tq = min(tq, sn)
    incl = (np.arange(tk)[:, None] >= np.arange(tk)[None, :]).astype(np.float32)
    u = jnp.asarray(incl, BF16)
    kv = pl.BlockSpec((1, 1, sn, dh), lambda b, h, i: (b, h, 0, 0))
    return pl.pallas_call(
        functools.partial(_sb_kernel, tq=tq, tk=tk),
        grid=(bn, heads, sn // tq),
        in_specs=[pl.BlockSpec((1, 1, tq, dh), lambda b, h, i: (b, h, i, 0)), kv, kv,
                  pl.BlockSpec((tk, tk), lambda b, h, i: (0, 0))],
        out_specs=pl.BlockSpec((1, 1, tq, dh), lambda b, h, i: (b, h, i, 0)),
        out_shape=jax.ShapeDtypeStruct((bn, heads, sn, dh), F32),
        compiler_params=_cparams("parallel", "parallel", "arbitrary"),
        name="stick_breaking_attn",
    )(q, k, v, u)


def _diff_kernel(sc_ref, q_ref, k_ref, v_ref, g_ref, o_ref, *, tq, tk):
    i = pl.program_id(2)
    dv = v_ref.shape[3]

    def tile(j, state, diagonal):
        start = pl.multiple_of(j * tk, tk)
        v = v_ref[0, 0, pl.ds(start, tk), :]
        if diagonal:
            causal = (start + _iota2((tq, tk), 1)) <= (i * tq + _iota2((tq, tk), 0))
        new_state = []
        for mi in range(2):
            m_old, l_old, acc_old = state[mi]
            s = _dot_nt(q_ref[0, mi], k_ref[0, mi, pl.ds(start, tk), :])
            if diagonal:
                s = jnp.where(causal, s, NEG_INF)
            m_new = jnp.maximum(m_old, jnp.max(s, axis=-1, keepdims=True))
            alpha = jnp.exp(m_old - m_new)
            p = jnp.exp(s - m_new)
            l_new = alpha * l_old + jnp.sum(p, axis=-1, keepdims=True)
            acc_new = alpha * acc_old + _dot(p.astype(BF16), v)
            new_state.append((m_new, l_new, acc_new))
        return tuple(new_state)

    init = tuple((jnp.full((tq, 1), NEG_INF, F32), jnp.zeros((tq, 1), F32), jnp.zeros((tq, dv), F32))
                 for _ in range(2))
    n_full = (i * tq) // tk
    state = lax.fori_loop(0, n_full, lambda j, st: tile(j, st, False), init)
    state = tile(n_full, state, True)
    lam = sc_ref[0]
    post = sc_ref[1]
    o = (state[0][2] / jnp.maximum(state[0][1], 1e-30)
         - lam * (state[1][2] / jnp.maximum(state[1][1], 1e-30)))
    r = lax.rsqrt(jnp.mean(o * o, axis=-1, keepdims=True) + NORM_EPS)
    o_ref[0, 0] = (o * r * g_ref[...]) * post


def _diff_attention(q, k, v, subln, lam, post, tq=1024, tk=1024):
    bn, h2, sn, dd = q.shape
    heads = h2 // 2
    dv = v.shape[3]
    tq, tk = min(tq, sn), min(tk, sn)
    assert tk % tq == 0 and sn % tk == 0
    scal = jnp.stack([lam, post]).astype(F32)
    return pl.pallas_call(
        functools.partial(_diff_kernel, tq=tq, tk=tk),
        grid=(bn, heads, sn // tq),
        in_specs=[pl.BlockSpec(memory_space=pltpu.SMEM),
                  pl.BlockSpec((1, 2, tq, dd), lambda b, h, i: (b, h, i, 0)),
                  pl.BlockSpec((1, 2, sn, dd), lambda b, h, i: (b, h, 0, 0)),
                  pl.BlockSpec((1, 1, sn, dv), lambda b, h, i: (b, h, 0, 0)),
                  pl.BlockSpec((1, dv), lambda b, h, i: (0, 0))],
        out_specs=pl.BlockSpec((1, 1, tq, dv), lambda b, h, i: (b, h, i, 0)),
        out_shape=jax.ShapeDtypeStruct((bn, heads, sn, dv), F32),
        compiler_params=_cparams("parallel", "parallel", "arbitrary"),
        name="diff_attn",
    )(scal, q, k, v, subln.reshape(1, dv).astype(F32))


def _merge_kernel(brg_ref, ga0_ref, ga1_ref, ga2_ref, oc_ref, os_ref, ow_ref, ob_ref, ocd_ref, od_ref,
                  wup_ref, wout_ref, x_ref, g1_ref, o_ref, *, d):
    def heads(ref):
        return jnp.concatenate([ref[0, h] for h in range(ref.shape[1])], axis=1)

    o_a = (jax.nn.sigmoid(ga0_ref[0]) * oc_ref[0] + jax.nn.sigmoid(ga1_ref[0]) * os_ref[0]
           + jax.nn.sigmoid(ga2_ref[0]) * heads(ow_ref))
    branches = (o_a, heads(ob_ref), heads(ocd_ref), heads(od_ref))
    merged = None
    for bi, o in enumerate(branches):
        gate = jax.nn.sigmoid(brg_ref[0, :, bi * d:(bi + 1) * d])
        term = gate * _dot(o.astype(BF16), wup_ref[bi])
        merged = term if merged is None else merged + term
    y = _dot(merged.astype(BF16), wout_ref[...])
    o_ref[0] = x_ref[0] + g1_ref[0] * y


def _merge(proj, o_c, o_s, o_w, o_b, o_cd, o_d, w_up, w_out, x, g1, ga_col, tm=256):
    bn, sn, d = x.shape
    bw = o_c.shape[2]
    nh, dh = o_b.shape[1], o_b.shape[3]
    assert ga_col % bw == 0
    gblk = ga_col // bw
    row = lambda b, i: (b, i, 0)
    bspec = pl.BlockSpec((1, tm, bw), row)
    hspec = pl.BlockSpec((1, nh, tm, dh), lambda b, i: (b, 0, i, 0))
    return pl.pallas_call(
        functools.partial(_merge_kernel, d=d),
        grid=(bn, sn // tm),
        in_specs=[pl.BlockSpec((1, tm, N_BRANCH * d), row),
                  pl.BlockSpec((1, tm, bw), lambda b, i: (b, i, gblk)),
                  pl.BlockSpec((1, tm, bw), lambda b, i: (b, i, gblk + 1)),
                  pl.BlockSpec((1, tm, bw), lambda b, i: (b, i, gblk + 2)),
                  bspec, bspec, hspec, hspec, hspec, hspec,
                  pl.BlockSpec((N_BRANCH, bw, d), lambda b, i: (0, 0, 0)),
                  pl.BlockSpec((d, d), lambda b, i: (0, 0)),
                  pl.BlockSpec((1, tm, d), row),
                  pl.BlockSpec((1, 1, d), lambda b, i: (b, 0, 0))],
        out_specs=pl.BlockSpec((1, tm, d), row),
        out_shape=jax.ShapeDtypeStruct((bn, sn, d), F32),
        compiler_params=_cparams("parallel", "parallel"),
        name="branch_merge",
    )(proj, proj, proj, proj, o_c, o_s, o_w, o_b, o_cd, o_d, w_up, w_out, x, g1)


def _router_kernel(x_ref, g_ref, sc_ref, sh_ref, wr_ref, br_ref, tri_ref, h_ref, e_ref, w_ref, rank_ref, cnt_ref,
                   run_scr):
    @pl.when((pl.program_id(0) == 0) & (pl.program_id(1) == 0))
    def _():
        run_scr[...] = jnp.zeros(run_scr.shape, F32)

    h = _norm_mod(x_ref[0], g_ref[...], sc_ref[0], sh_ref[0])
    h_ref[0] = h.astype(BF16)
    logits = jnp.dot(h, wr_ref[...], preferred_element_type=F32,
                     precision=lax.Precision.HIGHEST) + br_ref[...]
    lane = _iota2(logits.shape, 1)
    lane_f = lane.astype(F32)
    cur = logits
    vals, idxs = [], []
    chosen = jnp.zeros(logits.shape, F32)
    for _ in range(TOP_K):
        m = jnp.max(cur, axis=-1, keepdims=True)
        first = jnp.min(jnp.where(cur == m, lane_f, float(LANES)), axis=-1, keepdims=True)
        vals.append(m)
        idxs.append(first)
        hit = lane_f == first
        cur = jnp.where(hit, KNOCKOUT, cur)
        chosen = jnp.where(hit, 1.0, chosen)
    exps = [jnp.exp(v - vals[0]) for v in vals]
    den = exps[0]
    for e in exps[1:]:
        den = den + e
    earlier = _dot(tri_ref[...], chosen.astype(BF16)) + run_scr[0:1, :]
    e_out = jnp.zeros(logits.shape, F32)
    w_out = jnp.zeros(logits.shape, F32)
    r_out = jnp.zeros(logits.shape, F32)
    for k in range(TOP_K):
        rank_k = jnp.sum(jnp.where(lane_f == idxs[k], earlier, 0.0), axis=-1, keepdims=True)
        e_out = jnp.where(lane == k, idxs[k], e_out)
        w_out = jnp.where(lane == k, exps[k] / den, w_out)
        r_out = jnp.where(lane == k, rank_k, r_out)
    e_ref[0] = e_out.astype(jnp.int32)
    w_ref[0] = w_out
    rank_ref[0] = r_out.astype(jnp.int32)
    total = run_scr[...] + jnp.sum(chosen, axis=0, keepdims=True)
    run_scr[...] = total
    cnt_ref[...] = total


def _router(x, g, sc, sh, w_router, b_router, tm=512):
    bn, sn, d = x.shape
    ne = w_router.shape[1]
    wr = jnp.zeros((d, LANES), F32).at[:, :ne].set(w_router)
    br = jnp.full((1, LANES), NEG_INF, F32).at[0, :ne].set(b_router)
    tri = jnp.asarray((np.arange(tm)[:, None] > np.arange(tm)[None, :]).astype(np.float32), BF16)
    row = lambda b, i: (b, i, 0)
    return pl.pallas_call(
        _router_kernel,
        grid=(bn, sn // tm),
        in_specs=[pl.BlockSpec((1, tm, d), row),
                  pl.BlockSpec((1, d), lambda b, i: (0, 0)),
                  pl.BlockSpec((1, 1, d), lambda b, i: (b, 0, 0)),
                  pl.BlockSpec((1, 1, d), lambda b, i: (b, 0, 0)),
                  pl.BlockSpec((d, LANES), lambda b, i: (0, 0)),
                  pl.BlockSpec((1, LANES), lambda b, i: (0, 0)),
                  pl.BlockSpec((tm, tm), lambda b, i: (0, 0))],
        out_specs=[pl.BlockSpec((1, tm, d), row),
                   pl.BlockSpec((1, tm, LANES), row),
                   pl.BlockSpec((1, tm, LANES), row),
                   pl.BlockSpec((1, tm, LANES), row),
                   pl.BlockSpec((8, LANES), lambda b, i: (0, 0))],
        out_shape=[jax.ShapeDtypeStruct((bn, sn, d), BF16),
                   jax.ShapeDtypeStruct((bn, sn, LANES), jnp.int32),
                   jax.ShapeDtypeStruct((bn, sn, LANES), F32),
                   jax.ShapeDtypeStruct((bn, sn, LANES), jnp.int32),
                   jax.ShapeDtypeStruct((8, LANES), F32)],
        scratch_shapes=[pltpu.VMEM((8, LANES), F32)],
        compiler_params=_cparams("arbitrary", "arbitrary"),
        name="moe_router",
    )(x, g.reshape(1, d), sc, sh, wr, br, tri)


def _expert_kernel(ce_ref, x_ref, wgu_ref, bgu_ref, wdn_ref, bdn_ref, rw_ref, *rest, ff, fc):
    o_ref = rest[-1]
    x = x_ref[...]
    y = None
    for c in range(ff // fc):
        g = _dot(x, wgu_ref[0, :, c * fc:(c + 1) * fc]) + bgu_ref[0, :, c * fc:(c + 1) * fc]
        u = _dot(x, wgu_ref[0, :, ff + c * fc:ff + (c + 1) * fc]) + bgu_ref[0, :, ff + c * fc:ff + (c + 1) * fc]
        g = jnp.minimum(g, SWIGLU_LIMIT)
        u = jnp.clip(u, -SWIGLU_LIMIT, SWIGLU_LIMIT)
        act = g * jax.nn.sigmoid(SWIGLU_ALPHA * g) * (u + 1.0)
        part = _dot(act.astype(BF16), wdn_ref[0, c * fc:(c + 1) * fc, :])
        y = part if y is None else y + part
    o_ref[...] = (y + bdn_ref[0]) * rw_ref[...]


def _expert_ffn(h_pad, row_tok, chunk_e, w_gu, b_gu, w_dn, b_dn, row_w, tm=MOE_ROWS, fc=512, groups=4):
    n_rows = row_tok.shape[0]
    d = h_pad.shape[1]
    ne, _, ff2 = w_gu.shape
    ff = ff2 // 2
    n_chunks = n_rows // tm
    assert n_chunks % groups == 0
    per = n_chunks // groups
    b_gu3, b_dn3, row_w2 = b_gu.reshape(ne, 1, ff2), b_dn.reshape(ne, 1, d), row_w.reshape(n_rows, 1)
    y = None
    for gi in range(groups):
        off = gi * per
        rows = h_pad[row_tok[off * tm:(off + per) * tm]]
        in_specs = [pl.BlockSpec((tm, d), lambda c, ce: (c, 0)),
                    pl.BlockSpec((1, d, ff2), lambda c, ce, off=off: (ce[c + off], 0, 0)),
                    pl.BlockSpec((1, 1, ff2), lambda c, ce, off=off: (ce[c + off], 0, 0)),
                    pl.BlockSpec((1, ff, d), lambda c, ce, off=off: (ce[c + off], 0, 0)),
                    pl.BlockSpec((1, 1, d), lambda c, ce, off=off: (ce[c + off], 0, 0)),
                    pl.BlockSpec((tm, 1), lambda c, ce, off=off: (c + off, 0))]
        args = [chunk_e, rows, w_gu, b_gu3, w_dn, b_dn3, row_w2]
        aliases = {}
        if y is not None:
            in_specs.append(pl.BlockSpec(memory_space=pl.ANY))
            args.append(y)
            aliases = {len(args) - 1: 0}
        y = pl.pallas_call(
            functools.partial(_expert_kernel, ff=ff, fc=fc),
            grid_spec=pltpu.PrefetchScalarGridSpec(
                num_scalar_prefetch=1, grid=(per,), in_specs=in_specs,
                out_specs=pl.BlockSpec((tm, d), lambda c, ce, off=off: (c + off, 0))),
            out_shape=jax.ShapeDtypeStruct((n_rows, d), F32),
            input_output_aliases=aliases,
            compiler_params=_cparams("arbitrary"),
            name="moe_expert_ffn",
        )(*args)
    return y


def _combine_kernel(y_ref, x_ref, g2_ref, o_ref):
    tot = y_ref[0]
    for k in range(1, TOP_K):
        tot = tot + y_ref[k]
    o_ref[0] = x_ref[0] + g2_ref[0] * tot


def _combine(y4, x, g2, tm=512):
    bn, sn, d = x.shape
    row = lambda b, i: (b, i, 0)
    nt = sn // tm
    return pl.pallas_call(
        _combine_kernel,
        grid=(bn, nt),
        in_specs=[pl.BlockSpec((TOP_K, tm, d), lambda b, i: (0, b * nt + i, 0)),
                  pl.BlockSpec((1, tm, d), row),
                  pl.BlockSpec((1, 1, d), lambda b, i: (b, 0, 0))],
        out_specs=pl.BlockSpec((1, tm, d), row),
        out_shape=jax.ShapeDtypeStruct((bn, sn, d), F32),
        compiler_params=_cparams("parallel", "parallel"),
        name="moe_combine",
    )(y4, x, g2)


def _moe(x, g, sc, sh, g2, w_router, b_router, w_gu, b_gu, w_dn, b_dn):
    bn, sn, d = x.shape
    n_tok = bn * sn
    n_asg = n_tok * TOP_K
    tm = MOE_ROWS
    h, e_out, w_out, rank_out, totals = _router(x, g, sc, sh, w_router, b_router)
    e_tok = e_out[:, :, :TOP_K].reshape(n_tok, TOP_K)
    w_flat = w_out[:, :, :TOP_K].reshape(-1)
    rank = rank_out[:, :, :TOP_K].reshape(n_tok, TOP_K)
    counts = totals[0, :N_EXPERTS].astype(jnp.int32)
    starts = jnp.cumsum(counts) - counts
    padded = (counts + tm - 1) // tm * tm
    pad_ends = jnp.cumsum(padded)
    pad_starts = pad_ends - padded
    pos = pad_starts[e_tok] + rank
    n_chunks = n_asg // tm + N_EXPERTS
    n_rows = n_chunks * tm
    chunk_e = jnp.minimum(jnp.searchsorted(pad_ends, jnp.arange(n_chunks, dtype=jnp.int32) * tm, side='right'),
                          N_EXPERTS - 1).astype(jnp.int32)
    order = jnp.argsort(e_tok.reshape(-1))
    in_chunk = jnp.arange(tm, dtype=jnp.int32)[None, :]
    idx_in_e = (jnp.arange(n_chunks, dtype=jnp.int32) * tm - pad_starts[chunk_e])[:, None] + in_chunk
    row_valid = (idx_in_e < counts[chunk_e][:, None]).reshape(-1)
    src = jnp.clip(starts[chunk_e][:, None] + idx_in_e, 0, n_asg - 1).reshape(-1)
    asg = order[src].astype(jnp.int32)
    row_tok = jnp.where(row_valid, asg // TOP_K, n_tok)
    row_w = jnp.where(row_valid, w_flat[asg], 0.0)
    h_pad = jnp.concatenate([h.reshape(n_tok, d), jnp.zeros((1, d), BF16)], axis=0)
    y = _expert_ffn(h_pad, row_tok, chunk_e, w_gu, b_gu, w_dn, b_dn, row_w)
    return _combine(y[pos.T], x, g2)


def _norm_rope(x, g, cos, sin, bd, hd):
    sq = x * x
    hi = sq.astype(BF16)
    lo = (sq - hi.astype(F32)).astype(BF16)
    ss = _dot(hi, bd) + _dot(lo, bd)
    y = x * lax.rsqrt(ss * (1.0 / hd) + NORM_EPS) * g
    half = hd // 2
    first = (_iota2(x.shape, 1) & (hd - 1)) < half
    partner = jnp.where(first, pltpu.roll(y, LANES - half, 1), pltpu.roll(y, half, 1))
    return y * cos + partner * sin


def _prep_kernel(aq_ref, bq_ref, bk_ref, bv_ref, cq_ref, ck_ref, cv_ref, dq_ref, akc_ref, aks_ref, akw_ref,
                 dk_ref, dv_ref, c64_ref, s64_ref, c32_ref, s32_ref, gaq_ref, gak_ref, gcq_ref, gck_ref,
                 gdq_ref, gdk_ref, bd64_ref, bd32_ref,
                 oaq_ref, okc_ref, ovc_ref, okx_ref, ovs_ref, okw_ref, ovw_ref, obq_ref, obk_ref, obv_ref,
                 ocq_ref, ock_ref, ocv_ref, odq_ref, odk_ref, odv_ref, *, ts, sb_scale):
    hd, dd = HEAD_DIM, DIFF_DIM
    c64, s64, c32, s32 = c64_ref[0], s64_ref[0], c32_ref[0], s32_ref[0]
    bd64, bd32 = bd64_ref[...], bd32_ref[...]

    def slabs(ref):
        x = ref[0]
        return [x[:, c * LANES:(c + 1) * LANES] for c in range(x.shape[1] // LANES)]

    def put_heads(o_ref, c, y, width):
        per = LANES // width
        for u in range(per):
            o_ref[0, c * per + u] = y[:, u * width:(u + 1) * width].astype(o_ref.dtype)

    for src, gain, dst in ((aq_ref, gaq_ref, oaq_ref), (dq_ref, gdq_ref, odq_ref), (dk_ref, gdk_ref, odk_ref)):
        for c, x in enumerate(slabs(src)):
            put_heads(dst, c, _norm_rope(x, gain[:, c * LANES:(c + 1) * LANES], c64, s64, bd64, hd), hd)
    kc = _norm_rope(akc_ref[0], gak_ref[0:1, :], c64, s64, bd64, hd)
    ks = _norm_rope(aks_ref[0], gak_ref[1:2, :], c64, s64, bd64, hd)
    kw = _norm_rope(akw_ref[0], gak_ref[2:3, :], c64, s64, bd64, hd)
    okc_ref[0] = kc[:, :hd].astype(BF16)
    ovc_ref[0] = akc_ref[0][:, hd:].astype(BF16)
    nsel = okx_ref.shape[2] - hd
    blk = (pl.program_id(1) * ts + _iota2((ts, nsel), 0)) >> (NSA_SEL_BLOCK.bit_length() - 1)
    okx_ref[0, :, :nsel] = jnp.where(blk == _iota2((ts, nsel), 1), 1.0, 0.0).astype(BF16)
    okx_ref[0, :, nsel:] = ks[:, :hd].astype(BF16)
    ovs_ref[0] = aks_ref[0][:, hd:].astype(BF16)
    okw_ref[0, 0] = kw[:, :hd].astype(BF16)
    ovw_ref[0, 0] = akw_ref[0][:, hd:].astype(BF16)
    for c, x in enumerate(slabs(bq_ref)):
        put_heads(obq_ref, c, x * sb_scale, hd)
    for src, dst in ((bk_ref, obk_ref), (bv_ref, obv_ref), (cv_ref, ocv_ref)):
        for c, x in enumerate(slabs(src)):
            put_heads(dst, c, x, hd)
    put_heads(odv_ref, 0, dv_ref[0], hd)
    for src, gain, dst in ((cq_ref, gcq_ref, ocq_ref), (ck_ref, gck_ref, ock_ref)):
        for c, x in enumerate(slabs(src)):
            put_heads(dst, c, _norm_rope(x, gain[:, c * LANES:(c + 1) * LANES], c32, s32, bd32, dd), dd)


def _rope_tables(positions, hd):
    half = hd // 2
    inv = ROPE_THETA ** (-jnp.arange(half, dtype=F32) * 2.0 / hd)
    ang = positions.astype(F32)[..., None] * inv
    cos, sin = jnp.cos(ang), jnp.sin(ang)
    reps = LANES // hd
    return (jnp.tile(jnp.concatenate([cos, cos], axis=-1), (1, 1, reps)),
            jnp.tile(jnp.concatenate([-sin, sin], axis=-1), (1, 1, reps)))


def _prep(proj, cols, tables, p, n_sel, ts=512):
    bn, sn, _ = proj.shape
    hd, dd = HEAD_DIM, DIFF_DIM
    scale = hd ** -0.5
    c64, s64, c32, s32 = tables

    def cspec(name):
        off, width = cols[name]
        assert off % width == 0
        return pl.BlockSpec((1, ts, width), lambda b, i, blk=off // width: (b, i, blk))

    def tile_gain(g, reps, mult=1.0):
        return (jnp.tile(g.astype(F32), reps) * mult).reshape(1, -1)

    ones = jnp.ones((hd,), F32)
    gak = jnp.stack([jnp.concatenate([p["nsa_kn"][j].astype(F32), ones]) for j in range(3)])
    gains = [tile_gain(p["nsa_qn"], NSA_HEADS, scale), gak,
             tile_gain(p["dif_qn"], 2 * DIFF_HEADS, dd ** -0.5), tile_gain(p["dif_kn"], 2 * DIFF_HEADS),
             tile_gain(p["swa_qn"], SWA_HEADS, scale), tile_gain(p["swa_kn"], SWA_KV_HEADS)]
    lane = np.arange(LANES)
    bd64 = jnp.asarray((lane[:, None] // hd == lane[None, :] // hd).astype(np.float32), BF16)
    bd32 = jnp.asarray((lane[:, None] // dd == lane[None, :] // dd).astype(np.float32), BF16)
    names = ("a_q", "b_q", "b_k", "b_v", "c_q", "c_k", "c_v", "d_q", "a_kcvc", "a_ksvs", "a_kwvw", "d_k", "d_v")
    tab = pl.BlockSpec((1, ts, LANES), lambda b, i: (b, i, 0))
    full = lambda a: pl.BlockSpec(a.shape, lambda b, i: (0,) * a.ndim)

    def hm(nh, w):
        return (jax.ShapeDtypeStruct((bn, nh, sn, w), BF16), pl.BlockSpec((1, nh, ts, w), lambda b, i: (b, 0, i, 0)))

    def tm_(w):
        return (jax.ShapeDtypeStruct((bn, sn, w), BF16), pl.BlockSpec((1, ts, w), lambda b, i: (b, i, 0)))

    outs = [hm(NSA_HEADS, hd), tm_(hd), tm_(hd), tm_(n_sel + hd), tm_(hd), hm(1, hd), hm(1, hd),
            hm(SB_HEADS, hd), hm(SB_HEADS, hd), hm(SB_HEADS, hd),
            hm(2 * DIFF_HEADS, dd), hm(2 * DIFF_HEADS, dd), hm(DIFF_HEADS, 2 * dd),
            hm(SWA_HEADS, hd), hm(SWA_KV_HEADS, hd), hm(SWA_KV_HEADS, hd)]
    consts = gains + [bd64, bd32]
    res = pl.pallas_call(
        functools.partial(_prep_kernel, ts=ts, sb_scale=-scale * LOG2E),
        grid=(bn, sn // ts),
        in_specs=[cspec(n) for n in names] + [tab] * 4 + [full(a) for a in consts],
        out_specs=[o[1] for o in outs],
        out_shape=[o[0] for o in outs],
        compiler_params=_cparams("parallel", "parallel"),
        name="mixer_prep",
    )(*([proj] * len(names)), c64, s64, c32, s32, *consts)
    keys = ("a_q", "kc", "vc", "kx", "vs", "kw", "vw", "b_q", "b_k", "b_v", "c_q", "c_k", "c_v", "d_q", "d_k", "d_v")
    return dict(zip(keys, res))


def _layer_columns(d):
    cols = {}
    off = 0
    for name, width in (("br_g", N_BRANCH * d), ("ga", 3 * NSA_HEADS * HEAD_DIM), ("a_q", 256),
                        ("b_q", 256), ("b_k", 256), ("b_v", 256), ("c_q", 256), ("c_k", 256), ("c_v", 256),
                        ("d_q", 256), ("a_kcvc", 128), ("a_ksvs", 128), ("a_kwvw", 128), ("d_k", 128), ("d_v", 128)):
        cols[name] = (off, width)
        off += width
    return cols, off


def _reorder_w_in(w_in, d, n_pad):
    ref_splits = (256, 128, 128, 128, 12, 256, 256, 256, 256, 256, 256, 256, 128, 128, N_BRANCH * d)
    names = ("a_q", "a_kcvc", "a_ksvs", "a_kwvw", "a_g", "b_q", "b_k", "b_v",
             "c_q", "c_k", "c_v", "d_q", "d_k", "d_v", "br_g")
    starts = np.cumsum((0,) + ref_splits)
    src = {n: (int(starts[i]), ref_splits[i]) for i, n in enumerate(names)}
    cols, total = _layer_columns(d)
    idx = np.zeros((n_pad,), np.int32)
    keep = np.zeros((n_pad,), np.float32)
    for name, (off, width) in cols.items():
        if name == "ga":
            g0 = src["a_g"][0]
            for j in range(3):
                for h in range(NSA_HEADS):
                    base = off + j * NSA_HEADS * HEAD_DIM + h * HEAD_DIM
                    idx[base:base + HEAD_DIM] = g0 + h * 3 + j
        else:
            idx[off:off + width] = src[name][0] + np.arange(width)
        keep[off:off + width] = 1.0
    w = w_in[:, idx] * keep[None, :]
    return w.astype(BF16)


def _mixer_layer(x, positions, tables, mod, p, lam_init):
    bn, sn, d = x.shape
    sh1, sc1, g1 = mod[0], mod[1], mod[2]
    cols, total = _layer_columns(d)
    n_pad = -(-total // 768) * 768
    w_in = _reorder_w_in(p["w_in"], d, n_pad)
    proj = _in_projection(x, p["norm1"], sc1, sh1, w_in)
    n_cmp = (sn - NSA_CMP_LEN) // NSA_CMP_STRIDE + 1
    n_sel = sn // NSA_SEL_BLOCK
    n_top = min(NSA_N_SEL, n_sel)
    t = _prep(proj, cols, tables, p, n_sel)
    k_cmp = _compress(t["kc"], p["nsa_pe_k"], p["nsa_w1_k"], p["nsa_w2_k"]).astype(BF16)
    v_cmp = _compress(t["vc"], p["nsa_pe_v"], p["nsa_w1_v"], p["nsa_w2_v"]).astype(BF16)
    cmp_start = np.arange(sn // NSA_CMP_STRIDE) * NSA_CMP_STRIDE
    sel_start = np.arange(n_sel) * NSA_SEL_BLOCK
    overlap = ((cmp_start[:, None] <= (sel_start + NSA_SEL_BLOCK - 1)[None, :]) &
               ((cmp_start + NSA_CMP_LEN - 1)[:, None] >= sel_start[None, :]) &
               (np.arange(sn // NSA_CMP_STRIDE) < n_cmp)[:, None]).astype(np.float32)
    o_c, bias = _nsa_cmp(t["a_q"], k_cmp, v_cmp, jnp.asarray(overlap, BF16), n_cmp, n_top)
    o_s = _nsa_sel(t["a_q"], t["kx"], t["vs"], bias)
    o_w = _banded_attention(t["a_q"], t["kw"], t["vw"], NSA_WINDOW)
    o_b = _stick_breaking(t["b_q"], t["b_k"], t["b_v"])
    lam = (jnp.exp(jnp.sum(p["dif_lq1"] * p["dif_lk1"])) - jnp.exp(jnp.sum(p["dif_lq2"] * p["dif_lk2"]))
           + lam_init)
    o_cd = _diff_attention(t["c_q"], t["c_k"], t["c_v"], p["dif_subln"], lam, jnp.asarray(1.0 - lam_init, F32))
    o_d = _banded_attention(t["d_q"], t["d_k"], t["d_v"], SWA_WINDOW, sinks=p["swa_sinks"])
    return _merge(proj, o_c, o_s, o_w, o_b, o_cd, o_d, p["w_up"].astype(BF16), p["w_out"].astype(BF16),
                  x, g1, cols["ga"][0])


def kernel(x, c, positions, w_ada, b_ada, norm1, norm2, w_in, nsa_qn, nsa_kn, nsa_pe_k, nsa_w1_k, nsa_w2_k,
           nsa_pe_v, nsa_w1_v, nsa_w2_v, dif_qn, dif_kn, dif_lq1, dif_lk1, dif_lq2, dif_lk2, dif_subln,
           swa_qn, swa_kn, swa_sinks, w_up, w_out, w_router, b_router, w_gu, b_gu, w_dn, b_dn):
    bn, sn, d = x.shape
    depth = w_ada.shape[0]
    c_pad = jnp.zeros((8, d), F32).at[:bn].set(c)
    tables = _rope_tables(positions, HEAD_DIM) + _rope_tables(positions, DIFF_DIM)
    for l in range(depth):
        lam_init = 0.8 - 0.6 * math.exp(-0.3 * l)
        mod = _linear(c_pad, w_ada[l], b_ada[l], tn=512, precision=lax.Precision.HIGHEST)[:bn]
        mod = mod.reshape(bn, 6, 1, d).transpose(1, 0, 2, 3)
        p = dict(norm1=norm1[l], w_in=w_in[l], nsa_qn=nsa_qn[l], nsa_kn=nsa_kn[l], nsa_pe_k=nsa_pe_k[l],
                 nsa_w1_k=nsa_w1_k[l], nsa_w2_k=nsa_w2_k[l], nsa_pe_v=nsa_pe_v[l], nsa_w1_v=nsa_w1_v[l],
                 nsa_w2_v=nsa_w2_v[l], dif_qn=dif_qn[l], dif_kn=dif_kn[l], dif_lq1=dif_lq1[l],
                 dif_lk1=dif_lk1[l], dif_lq2=dif_lq2[l], dif_lk2=dif_lk2[l], dif_subln=dif_subln[l],
                 swa_qn=swa_qn[l], swa_kn=swa_kn[l], swa_sinks=swa_sinks[l], w_up=w_up[l], w_out=w_out[l])
        x = _mixer_layer(x, positions, tables, mod, p, lam_init)
        x = _moe(x, norm2[l], mod[4], mod[3], mod[5], w_router[l], b_router[l],
                 w_gu[l].astype(BF16), b_gu[l], w_dn[l].astype(BF16), b_dn[l])
    return x
```

```python
import functools
import math

import numpy as np
import jax
import jax.numpy as jnp
from jax import lax
from jax.experimental import pallas as pl
from jax.experimental.pallas import tpu as pltpu

F32 = jnp.float32
BF16 = jnp.bfloat16

HEAD_DIM = 64
ROPE_THETA = 10000.0
NORM_EPS = 1e-6
NEG_INF = -1e30
KNOCKOUT = -3e38
N_BRANCH = 4

NSA_HEADS = 4
NSA_CMP_LEN = 32
NSA_CMP_STRIDE = 16
NSA_SEL_BLOCK = 64
NSA_N_SEL = 16
NSA_WINDOW = 512
NSA_FORCED_SCORE = 1e4

SB_HEADS = 4
DIFF_HEADS = 4
DIFF_DIM = 32
SWA_HEADS = 4
SWA_KV_HEADS = 2
SWA_WINDOW = 128

N_EXPERTS = 32
TOP_K = 4
SWIGLU_ALPHA = 1.702
SWIGLU_LIMIT = 7.0

LANES = 128
LOG2E = 1.4426950408889634
MOE_ROWS = 512
VMEM_LIMIT = 56 * 1024 * 1024


def _cparams(*sem):
    return pltpu.CompilerParams(dimension_semantics=sem, vmem_limit_bytes=VMEM_LIMIT)


def _dot(a, b):
    return jnp.dot(a, b, preferred_element_type=F32)


def _dot_nt(a, b):
    return lax.dot_general(a, b, (((1,), (1,)), ((), ())), preferred_element_type=F32)


def _iota2(shape, dim):
    return lax.broadcasted_iota(jnp.int32, shape, dim)


def _linear_kernel(x_ref, w_ref, b_ref, o_ref, *, precision):
    o_ref[...] = jnp.dot(x_ref[...], w_ref[...], preferred_element_type=F32,
                         precision=precision) + b_ref[...]


def _linear(x, w, b, tn, precision=None):
    m, k = x.shape
    n = w.shape[1]
    return pl.pallas_call(
        functools.partial(_linear_kernel, precision=precision),
        grid=(n // tn,),
        in_specs=[pl.BlockSpec((m, k), lambda j: (0, 0)),
                  pl.BlockSpec((k, tn), lambda j: (0, j)),
                  pl.BlockSpec((1, tn), lambda j: (0, j))],
        out_specs=pl.BlockSpec((m, tn), lambda j: (0, j)),
        out_shape=jax.ShapeDtypeStruct((m, n), F32),
        compiler_params=_cparams("arbitrary"),
        name="linear",
    )(x, w, b.reshape(1, n))


def _cmp_mlp_kernel(a_ref, b_ref, pe_ref, w2_ref, o_ref):
    hid = jax.nn.gelu(a_ref[...] + b_ref[...] + pe_ref[...])
    o_ref[...] = _dot(hid.astype(BF16), w2_ref[...])


def _compress(t, pe, w1, w2):
    bn, sn, dh = t.shape
    st = NSA_CMP_STRIDE
    half = st * dh
    nb = sn // st
    t16 = t.reshape(bn * nb, half).astype(BF16)
    w1cat = jnp.concatenate([w1[:half], w1[half:]], axis=1).astype(BF16)
    hidden = w1.shape[1]
    ab = _linear(t16, w1cat, jnp.zeros((2 * hidden,), F32), tn=2 * hidden)
    ab = ab.reshape(bn, nb, 2 * hidden)
    a = ab[:, :, :hidden]
    b_next = jnp.concatenate([ab[:, 1:, hidden:], jnp.zeros((bn, 1, hidden), F32)], axis=1)
    pe_term = jnp.dot(pe.reshape(1, NSA_CMP_LEN * dh), w1, precision=lax.Precision.HIGHEST)
    rows = bn * nb
    tm = min(512, rows)
    out = pl.pallas_call(
        _cmp_mlp_kernel,
        grid=(rows // tm,),
        in_specs=[pl.BlockSpec((tm, hidden), lambda i: (i, 0)),
                  pl.BlockSpec((tm, hidden), lambda i: (i, 0)),
                  pl.BlockSpec((1, hidden), lambda i: (0, 0)),
                  pl.BlockSpec((hidden, dh), lambda i: (0, 0))],
        out_specs=pl.BlockSpec((tm, dh), lambda i: (i, 0)),
        out_shape=jax.ShapeDtypeStruct((rows, dh), F32),
        compiler_params=_cparams("parallel"),
        name="cmp_mlp",
    )(a.reshape(rows, hidden), b_next.reshape(rows, hidden), pe_term, w2.astype(BF16))
    return out.reshape(bn, nb, dh)


def _norm_mod(x, g, sc, sh):
    r = lax.rsqrt(jnp.mean(x * x, axis=-1, keepdims=True) + NORM_EPS)
    return (x * r * g) * (1.0 + sc) + sh


def _proj_kernel(x_ref, g_ref, sc_ref, sh_ref, w_ref, o_ref, h_scr):
    @pl.when(pl.program_id(2) == 0)
    def _():
        h_scr[...] = _norm_mod(x_ref[0], g_ref[...], sc_ref[0], sh_ref[0]).astype(BF16)

    o_ref[0] = _dot(h_scr[...], w_ref[...])


def _in_projection(x, g, sc, sh, w, tm=1024, tn=768):
    bn, sn, d = x.shape
    n = w.shape[1]
    return pl.pallas_call(
        _proj_kernel,
        grid=(bn, sn // tm, n // tn),
        in_specs=[pl.BlockSpec((1, tm, d), lambda b, i, j: (b, i, 0)),
                  pl.BlockSpec((1, d), lambda b, i, j: (0, 0)),
                  pl.BlockSpec((1, 1, d), lambda b, i, j: (b, 0, 0)),
                  pl.BlockSpec((1, 1, d), lambda b, i, j: (b, 0, 0)),
                  pl.BlockSpec((d, tn), lambda b, i, j: (0, j))],
        out_specs=pl.BlockSpec((1, tm, tn), lambda b, i, j: (b, i, j)),
        out_shape=jax.ShapeDtypeStruct((bn, sn, n), F32),
        scratch_shapes=[pltpu.VMEM((tm, d), BF16)],
        compiler_params=_cparams("parallel", "parallel", "arbitrary"),
        name="in_proj",
    )(x, g.reshape(1, d), sc, sh, w)


def _banded_kernel(*refs, tile, window, has_sink):
    if has_sink:
        sink_ref, q_ref, kp_ref, kc_ref, vp_ref, vc_ref, o_ref = refs
    else:
        q_ref, kp_ref, kc_ref, vp_ref, vc_ref, o_ref = refs
    i = pl.program_id(2)
    w = window
    upper = _iota2((w, w), 1) > _iota2((w, w), 0)
    first_bias = jnp.where(i > 0, 0.0, NEG_INF)
    for u in range(tile // w):
        q = q_ref[0, 0, u * w:(u + 1) * w, :]
        if u == 0:
            k_prev, v_prev = kp_ref[0, 0], vp_ref[0, 0]
        else:
            k_prev, v_prev = kc_ref[0, 0, (u - 1) * w:u * w, :], vc_ref[0, 0, (u - 1) * w:u * w, :]
        k_cur, v_cur = kc_ref[0, 0, u * w:(u + 1) * w, :], vc_ref[0, 0, u * w:(u + 1) * w, :]
        s_prev = _dot_nt(q, k_prev)
        if u == 0:
            s_prev = s_prev + first_bias
        s = jnp.where(upper, s_prev, _dot_nt(q, k_cur))
        m = jnp.max(s, axis=-1, keepdims=True)
        if has_sink:
            sink = sink_ref[pl.program_id(1)]
            m = jnp.maximum(m, sink)
        p = jnp.exp(s - m)
        den = jnp.sum(p, axis=-1, keepdims=True)
        if has_sink:
            den = den + jnp.exp(sink - m)
        else:
            den = jnp.maximum(den, 1e-30)
        o = (_dot(jnp.where(upper, p, 0.0).astype(BF16), v_prev)
             + _dot(jnp.where(upper, 0.0, p).astype(BF16), v_cur))
        o_ref[0, 0, u * w:(u + 1) * w, :] = o / den


def _banded_attention(q, k, v, window, sinks=None, tile=1024):
    bn, hq, sn, d = q.shape
    grp = hq // k.shape[1]
    tile = min(tile, sn)
    assert tile % window == 0 and sn % tile == 0
    per = tile // window
    has_sink = sinks is not None
    qspec = pl.BlockSpec((1, 1, tile, d), lambda b, h, i: (b, h, i, 0))
    prev = pl.BlockSpec((1, 1, window, d), lambda b, h, i: (b, h // grp, jnp.maximum(i * per - 1, 0), 0))
    cur = pl.BlockSpec((1, 1, tile, d), lambda b, h, i: (b, h // grp, i, 0))
    in_specs = [qspec, prev, cur, prev, cur]
    args = [q, k, k, v, v]
    if has_sink:
        in_specs = [pl.BlockSpec(memory_space=pltpu.SMEM)] + in_specs
        args = [sinks.astype(F32)] + args
    return pl.pallas_call(
        functools.partial(_banded_kernel, tile=tile, window=window, has_sink=has_sink),
        grid=(bn, hq, sn // tile),
        in_specs=in_specs,
        out_specs=pl.BlockSpec((1, 1, tile, d), lambda b, h, i: (b, h, i, 0)),
        out_shape=jax.ShapeDtypeStruct((bn, hq, sn, d), F32),
        compiler_params=_cparams("parallel", "parallel", "parallel"),
        name="banded_attn",
    )(*args)


def _nsa_cmp_kernel(q_ref, kc_ref, vct_ref, ovt_ref, oct_ref, bias_ref, *, tq, n_cmp, n_top, heads):
    i = pl.program_id(1)
    ncp = kc_ref.shape[1]
    nsel = ovt_ref.shape[0]
    dh = q_ref.shape[3]
    t = i * tq + _iota2((ncp, tq), 1)
    n = _iota2((ncp, tq), 0)
    valid = (n * NSA_CMP_STRIDE + (NSA_CMP_LEN - 1) <= t) & (n < n_cmp)
    kc = kc_ref[0]
    vct = vct_ref[0]
    psum = jnp.zeros((ncp, tq), F32)
    for h in range(heads):
        s = jnp.where(valid, _dot_nt(kc, q_ref[0, h]), NEG_INF)
        m = jnp.max(s, axis=0, keepdims=True)
        p = jnp.where(valid, jnp.exp(s - m), 0.0)
        p = p / jnp.maximum(jnp.sum(p, axis=0, keepdims=True), 1e-30)
        oct_ref[0, h * dh:(h + 1) * dh, :] = _dot(vct, p.astype(BF16))
        psum = psum + p
    hi = psum.astype(BF16)
    lo = (psum - hi.astype(F32)).astype(BF16)
    imp = _dot(ovt_ref[...], hi) + _dot(ovt_ref[...], lo)

    tt = i * tq + _iota2((nsel, tq), 1)
    blk = _iota2((nsel, tq), 0)
    cur = tt >> (NSA_SEL_BLOCK.bit_length() - 1)
    forced = (blk == 0) | (blk == cur) | (blk == cur - 1)
    valid_s = blk * NSA_SEL_BLOCK <= tt
    score = jnp.where(forced, NSA_FORCED_SCORE, jnp.where(valid_s, imp, -1.0))
    blk_f = blk.astype(F32)

    def pick(_, carry):
        score, sel = carry
        m = jnp.max(score, axis=0, keepdims=True)
        first = jnp.min(jnp.where(score == m, blk_f, float(nsel)), axis=0, keepdims=True)
        hit = blk_f == first
        return jnp.where(hit, KNOCKOUT, score), jnp.where(hit, 0.0, sel)

    _, bias = lax.fori_loop(0, n_top, pick, (score, jnp.full((nsel, tq), NEG_INF, F32)))
    bias_ref[0] = bias.astype(BF16)


def _nsa_cmp(q, kc, vc, overlap, n_cmp, n_top, tq=256):
    bn, heads, sn, dh = q.shape
    ncp = kc.shape[1]
    nsel = overlap.shape[1]
    oct, bias_t = pl.pallas_call(
        functools.partial(_nsa_cmp_kernel, tq=tq, n_cmp=n_cmp, n_top=n_top, heads=heads),
        grid=(bn, sn // tq),
        in_specs=[pl.BlockSpec((1, heads, tq, dh), lambda b, i: (b, 0, i, 0)),
                  pl.BlockSpec((1, ncp, dh), lambda b, i: (b, 0, 0)),
                  pl.BlockSpec((1, dh, ncp), lambda b, i: (b, 0, 0)),
                  pl.BlockSpec((nsel, ncp), lambda b, i: (0, 0))],
        out_specs=[pl.BlockSpec((1, heads * dh, tq), lambda b, i: (b, 0, i)),
                   pl.BlockSpec((1, nsel, tq), lambda b, i: (b, 0, i))],
        out_shape=[jax.ShapeDtypeStruct((bn, heads * dh, sn), F32),
                   jax.ShapeDtypeStruct((bn, nsel, sn), BF16)],
        compiler_params=_cparams("parallel", "parallel"),
        name="nsa_cmp_topk",
    )(q, kc, vc.transpose(0, 2, 1), overlap.T)
    return oct.transpose(0, 2, 1), bias_t.transpose(0, 2, 1)


def _nsa_sel_kernel(q_ref, kx_ref, v_ref, bias_ref, o_ref, qx_scr, *, tq, tk, heads):
    i = pl.program_id(1)
    dh = v_ref.shape[2]
    bias = bias_ref[0]
    for h in range(heads):
        qx_scr[h] = jnp.concatenate([bias, q_ref[0, h]], axis=1)

    def tile(j, state, diagonal):
        start = pl.multiple_of(j * tk, tk)
        kx = kx_ref[0, pl.ds(start, tk), :]
        v = v_ref[0, pl.ds(start, tk), :]
        if diagonal:
            causal = (start + _iota2((tq, tk), 1)) <= (i * tq + _iota2((tq, tk), 0))
        new_state = []
        for h in range(heads):
            m_old, l_old, acc_old = state[h]
            s = _dot_nt(qx_scr[h], kx)
            if diagonal:
                s = jnp.where(causal, s, NEG_INF)
            m_new = jnp.maximum(m_old, jnp.max(s, axis=-1, keepdims=True))
            alpha = jnp.exp(m_old - m_new)
            p = jnp.exp(s - m_new)
            l_new = alpha * l_old + jnp.sum(p, axis=-1, keepdims=True)
            acc_new = alpha * acc_old + _dot(p.astype(BF16), v)
            new_state.append((m_new, l_new, acc_new))
        return tuple(new_state)

    init = tuple((jnp.full((tq, 1), NEG_INF, F32), jnp.zeros((tq, 1), F32), jnp.zeros((tq, dh), F32))
                 for _ in range(heads))
    n_full = (i * tq) // tk
    state = lax.fori_loop(0, n_full, lambda j, st: tile(j, st, False), init)
    state = tile(n_full, state, True)
    for h in range(heads):
        o_ref[0, :, h * dh:(h + 1) * dh] = state[h][2] / jnp.maximum(state[h][1], 1e-30)


def _nsa_sel(q, kx, v, bias, tq=1024, tk=1024):
    bn, heads, sn, dh = q.shape
    nsel = bias.shape[2]
    tq, tk = min(tq, sn), min(tk, sn)
    assert tk % tq == 0 and sn % tk == 0
    return pl.pallas_call(
        functools.partial(_nsa_sel_kernel, tq=tq, tk=tk, heads=heads),
        grid=(bn, sn // tq),
        in_specs=[pl.BlockSpec((1, heads, tq, dh), lambda b, i: (b, 0, i, 0)),
                  pl.BlockSpec((1, sn, nsel + dh), lambda b, i: (b, 0, 0)),
                  pl.BlockSpec((1, sn, dh), lambda b, i: (b, 0, 0)),
                  pl.BlockSpec((1, tq, nsel), lambda b, i: (b, i, 0))],
        out_specs=pl.BlockSpec((1, tq, heads * dh), lambda b, i: (b, i, 0)),
        out_shape=jax.ShapeDtypeStruct((bn, sn, heads * dh), F32),
        scratch_shapes=[pltpu.VMEM((heads, tq, nsel + dh), BF16)],
        compiler_params=_cparams("parallel", "arbitrary"),
        name="nsa_selected_attn",
    )(q, kx, v, bias)


def _sb_kernel(q_ref, k_ref, v_ref, u_ref, o_ref, *, tq, tk):
    i = pl.program_id(2)
    q = q_ref[0, 0]
    per_q = tq // tk

    def tile(jj, carry, first_row):
        diagonal = first_row is not None
        r0 = first_row if diagonal else 0
        start = pl.multiple_of(jj * tk, tk)
        nz = _dot_nt(q[r0:], k_ref[0, 0, pl.ds(start, tk), :])
        neg_abs = lax.bitcast_convert_type(lax.bitcast_convert_type(nz, jnp.uint32) | jnp.uint32(0x80000000), F32)
        log_keep = jnp.minimum(nz, 0.0) - jnp.log2(1.0 + jnp.exp2(neg_abs))
        if diagonal:
            strict = _iota2(nz.shape, 1) < _iota2(nz.shape, 0)
            log_keep = jnp.where(strict, log_keep, 0.0)
        cum = _dot(log_keep.astype(BF16), u_ref[...])
        a = jnp.exp2(cum + jnp.concatenate([carry[r0:]] * (tk // LANES), axis=1) - nz)
        if diagonal:
            a = jnp.where(strict, a, 0.0)
        out = _dot(a.astype(BF16), v_ref[0, 0, pl.ds(start, tk), :])
        new_carry = carry[r0:] + jnp.broadcast_to(cum[:, 0:1], (tq - r0, LANES))
        if r0:
            out = jnp.concatenate([jnp.zeros((r0, out.shape[1]), F32), out], axis=0)
            new_carry = jnp.concatenate([carry[:r0], new_carry], axis=0)
        return out, new_carry

    def group(first, carry, diagonal):
        total = None
        for r in range(per_q):
            out, carry = tile(first - r, carry, (per_q - 1 - r) * tk if diagonal else None)
            total = out if total is None else total + out
        return total, carry

    acc, carry = group(i * per_q + per_q - 1, jnp.zeros((tq, LANES), F32), True)

    def body(p, state):
        acc, carry = state
        out, carry = group((i - p) * per_q - 1, carry, False)
        return acc + out, carry

    acc, _ = lax.fori_loop(0, i, body, (acc, carry))
    o_ref[0, 0] = acc


def _stick_breaking(q, k, v, tq=1024, tk=256):
    bn, heads, sn, dh = q.shape
    tq = min(tq, sn)
    incl = (np.arange(tk)[:, None] >= np.arange(tk)[None, :]).astype(np.float32)
    u = jnp.asarray(incl, BF16)
    kv = pl.BlockSpec((1, 1, sn, dh), lambda b, h, i: (b, h, 0, 0))
    return pl.pallas_call(
        functools.partial(_sb_kernel, tq=tq, tk=tk),
        grid=(bn, heads, sn // tq),
        in_specs=[pl.BlockSpec((1, 1, tq, dh), lambda b, h, i: (b, h, i, 0)), kv, kv,
                  pl.BlockSpec((tk, tk), lambda b, h, i: (0, 0))],
        out_specs=pl.BlockSpec((1, 1, tq, dh), lambda b, h, i: (b, h, i, 0)),
        out_shape=jax.ShapeDtypeStruct((bn, heads, sn, dh), F32),
        compiler_params=_cparams("parallel", "parallel", "arbitrary"),
        name="stick_breaking_attn",
    )(q, k, v, u)


def _diff_kernel(sc_ref, q_ref, k_ref, v_ref, g_ref, o_ref, *, tq, tk):
    i = pl.program_id(2)
    dv = v_ref.shape[3]

    def tile(j, state, diagonal):
        start = pl.multiple_of(j * tk, tk)
        v = v_ref[0, 0, pl.ds(start, tk), :]
        if diagonal:
            causal = (start + _iota2((tq, tk), 1)) <= (i * tq + _iota2((tq, tk), 0))
        new_state = []
        for mi in range(2):
            m_old, l_old, acc_old = state[mi]
            s = _dot_nt(q_ref[0, mi], k_ref[0, mi, pl.ds(start, tk), :])
            if diagonal:
                s = jnp.where(causal, s, NEG_INF)
            m_new = jnp.maximum(m_old, jnp.max(s, axis=-1, keepdims=True))
            alpha = jnp.exp(m_old - m_new)
            p = jnp.exp(s - m_new)
            l_new = alpha * l_old + jnp.sum(p, axis=-1, keepdims=True)
            acc_new = alpha * acc_old + _dot(p.astype(BF16), v)
            new_state.append((m_new, l_new, acc_new))
        return tuple(new_state)

    init = tuple((jnp.full((tq, 1), NEG_INF, F32), jnp.zeros((tq, 1), F32), jnp.zeros((tq, dv), F32))
                 for _ in range(2))
    n_full = (i * tq) // tk
    state = lax.fori_loop(0, n_full, lambda j, st: tile(j, st, False), init)
    state = tile(n_full, state, True)
    lam = sc_ref[0]
    post = sc_ref[1]
    o = (state[0][2] / jnp.maximum(state[0][1], 1e-30)
         - lam * (state[1][2] / jnp.maximum(state[1][1], 1e-30)))
    r = lax.rsqrt(jnp.mean(o * o, axis=-1, keepdims=True) + NORM_EPS)
    o_ref[0, 0] = (o * r * g_ref[...]) * post


def _diff_attention(q, k, v, subln, lam, post, tq=1024, tk=1024):
    bn, h2, sn, dd = q.shape
    heads = h2 // 2
    dv = v.shape[3]
    tq, tk = min(tq, sn), min(tk, sn)
    assert tk % tq == 0 and sn % tk == 0
    scal = jnp.stack([lam, post]).astype(F32)
    return pl.pallas_call(
        functools.partial(_diff_kernel, tq=tq, tk=tk),
        grid=(bn, heads, sn // tq),
        in_specs=[pl.BlockSpec(memory_space=pltpu.SMEM),
                  pl.BlockSpec((1, 2, tq, dd), lambda b, h, i: (b, h, i, 0)),
                  pl.BlockSpec((1, 2, sn, dd), lambda b, h, i: (b, h, 0, 0)),
                  pl.BlockSpec((1, 1, sn, dv), lambda b, h, i: (b, h, 0, 0)),
                  pl.BlockSpec((1, dv), lambda b, h, i: (0, 0))],
        out_specs=pl.BlockSpec((1, 1, tq, dv), lambda b, h, i: (b, h, i, 0)),
        out_shape=jax.ShapeDtypeStruct((bn, heads, sn, dv), F32),
        compiler_params=_cparams("parallel", "parallel", "arbitrary"),
        name="diff_attn",
    )(scal, q, k, v, subln.reshape(1, dv).astype(F32))


def _merge_kernel(brg_ref, ga0_ref, ga1_ref, ga2_ref, oc_ref, os_ref, ow_ref, ob_ref, ocd_ref, od_ref,
                  wup_ref, wout_ref, x_ref, g1_ref, o_ref, *, d):
    def heads(ref):
        return jnp.concatenate([ref[0, h] for h in range(ref.shape[1])], axis=1)

    o_a = (jax.nn.sigmoid(ga0_ref[0]) * oc_ref[0] + jax.nn.sigmoid(ga1_ref[0]) * os_ref[0]
           + jax.nn.sigmoid(ga2_ref[0]) * heads(ow_ref))
    branches = (o_a, heads(ob_ref), heads(ocd_ref), heads(od_ref))
    merged = None
    for bi, o in enumerate(branches):
        gate = jax.nn.sigmoid(brg_ref[0, :, bi * d:(bi + 1) * d])
        term = gate * _dot(o.astype(BF16), wup_ref[bi])
        merged = term if merged is None else merged + term
    y = _dot(merged.astype(BF16), wout_ref[...])
    o_ref[0] = x_ref[0] + g1_ref[0] * y


def _merge(proj, o_c, o_s, o_w, o_b, o_cd, o_d, w_up, w_out, x, g1, ga_col, tm=256):
    bn, sn, d = x.shape
    bw = o_c.shape[2]
    nh, dh = o_b.shape[1], o_b.shape[3]
    assert ga_col % bw == 0
    gblk = ga_col // bw
    row = lambda b, i: (b, i, 0)
    bspec = pl.BlockSpec((1, tm, bw), row)
    hspec = pl.BlockSpec((1, nh, tm, dh), lambda b, i: (b, 0, i, 0))
    return pl.pallas_call(
        functools.partial(_merge_kernel, d=d),
        grid=(bn, sn // tm),
        in_specs=[pl.BlockSpec((1, tm, N_BRANCH * d), row),
                  pl.BlockSpec((1, tm, bw), lambda b, i: (b, i, gblk)),
                  pl.BlockSpec((1, tm, bw), lambda b, i: (b, i, gblk + 1)),
                  pl.BlockSpec((1, tm, bw), lambda b, i: (b, i, gblk + 2)),
                  bspec, bspec, hspec, hspec, hspec, hspec,
                  pl.BlockSpec((N_BRANCH, bw, d), lambda b, i: (0, 0, 0)),
                  pl.BlockSpec((d, d), lambda b, i: (0, 0)),
                  pl.BlockSpec((1, tm, d), row),
                  pl.BlockSpec((1, 1, d), lambda b, i: (b, 0, 0))],
        out_specs=pl.BlockSpec((1, tm, d), row),
        out_shape=jax.ShapeDtypeStruct((bn, sn, d), F32),
        compiler_params=_cparams("parallel", "parallel"),
        name="branch_merge",
    )(proj, proj, proj, proj, o_c, o_s, o_w, o_b, o_cd, o_d, w_up, w_out, x, g1)


def _router_kernel(x_ref, g_ref, sc_ref, sh_ref, wr_ref, br_ref, tri_ref, h_ref, e_ref, w_ref, rank_ref, cnt_ref,
                   run_scr):
    @pl.when((pl.program_id(0) == 0) & (pl.program_id(1) == 0))
    def _():
        run_scr[...] = jnp.zeros(run_scr.shape, F32)

    h = _norm_mod(x_ref[0], g_ref[...], sc_ref[0], sh_ref[0])
    h_ref[0] = h.astype(BF16)
    logits = jnp.dot(h, wr_ref[...], preferred_element_type=F32,
                     precision=lax.Precision.HIGHEST) + br_ref[...]
    lane = _iota2(logits.shape, 1)
    lane_f = lane.astype(F32)
    cur = logits
    vals, idxs = [], []
    chosen = jnp.zeros(logits.shape, F32)
    for _ in range(TOP_K):
        m = jnp.max(cur, axis=-1, keepdims=True)
        first = jnp.min(jnp.where(cur == m, lane_f, float(LANES)), axis=-1, keepdims=True)
        vals.append(m)
        idxs.append(first)
        hit = lane_f == first
        cur = jnp.where(hit, KNOCKOUT, cur)
        chosen = jnp.where(hit, 1.0, chosen)
    exps = [jnp.exp(v - vals[0]) for v in vals]
    den = exps[0]
    for e in exps[1:]:
        den = den + e
    earlier = _dot(tri_ref[...], chosen.astype(BF16)) + run_scr[0:1, :]
    e_out = jnp.zeros(logits.shape, F32)
    w_out = jnp.zeros(logits.shape, F32)
    r_out = jnp.zeros(logits.shape, F32)
    for k in range(TOP_K):
        rank_k = jnp.sum(jnp.where(lane_f == idxs[k], earlier, 0.0), axis=-1, keepdims=True)
        e_out = jnp.where(lane == k, idxs[k], e_out)
        w_out = jnp.where(lane == k, exps[k] / den, w_out)
        r_out = jnp.where(lane == k, rank_k, r_out)
    e_ref[0] = e_out.astype(jnp.int32)
    w_ref[0] = w_out
    rank_ref[0] = r_out.astype(jnp.int32)
    total = run_scr[...] + jnp.sum(chosen, axis=0, keepdims=True)
    run_scr[...] = total
    cnt_ref[...] = total


def _router(x, g, sc, sh, w_router, b_router, tm=512):
    bn, sn, d = x.shape
    ne = w_router.shape[1]
    wr = jnp.zeros((d, LANES), F32).at[:, :ne].set(w_router)
    br = jnp.full((1, LANES), NEG_INF, F32).at[0, :ne].set(b_router)
    tri = jnp.asarray((np.arange(tm)[:, None] > np.arange(tm)[None, :]).astype(np.float32), BF16)
    row = lambda b, i: (b, i, 0)
    return pl.pallas_call(
        _router_kernel,
        grid=(bn, sn // tm),
        in_specs=[pl.BlockSpec((1, tm, d), row),
                  pl.BlockSpec((1, d), lambda b, i: (0, 0)),
                  pl.BlockSpec((1, 1, d), lambda b, i: (b, 0, 0)),
                  pl.BlockSpec((1, 1, d), lambda b, i: (b, 0, 0)),
                  pl.BlockSpec((d, LANES), lambda b, i: (0, 0)),
                  pl.BlockSpec((1, LANES), lambda b, i: (0, 0)),
                  pl.BlockSpec((tm, tm), lambda b, i: (0, 0))],
        out_specs=[pl.BlockSpec((1, tm, d), row),
                   pl.BlockSpec((1, tm, LANES), row),
                   pl.BlockSpec((1, tm, LANES), row),
                   pl.BlockSpec((1, tm, LANES), row),
                   pl.BlockSpec((8, LANES), lambda b, i: (0, 0))],
        out_shape=[jax.ShapeDtypeStruct((bn, sn, d), BF16),
                   jax.ShapeDtypeStruct((bn, sn, LANES), jnp.int32),
                   jax.ShapeDtypeStruct((bn, sn, LANES), F32),
                   jax.ShapeDtypeStruct((bn, sn, LANES), jnp.int32),
                   jax.ShapeDtypeStruct((8, LANES), F32)],
        scratch_shapes=[pltpu.VMEM((8, LANES), F32)],
        compiler_params=_cparams("arbitrary", "arbitrary"),
        name="moe_router",
    )(x, g.reshape(1, d), sc, sh, wr, br, tri)


def _expert_kernel(ce_ref, x_ref, wgu_ref, bgu_ref, wdn_ref, bdn_ref, rw_ref, *rest, ff, fc, off):
    o_ref, wgu_scr, wdn_scr = rest[-3:]
    c = pl.program_id(0)

    @pl.when((c == 0) | (ce_ref[c + off] != ce_ref[jnp.maximum(c + off - 1, 0)]))
    def _():
        wgu_scr[...] = wgu_ref[0].astype(BF16)
        wdn_scr[...] = wdn_ref[0].astype(BF16)

    x = x_ref[...]
    y = None
    for j in range(ff // fc):
        g = _dot(x, wgu_scr[:, j * fc:(j + 1) * fc]) + bgu_ref[0, :, j * fc:(j + 1) * fc]
        u = _dot(x, wgu_scr[:, ff + j * fc:ff + (j + 1) * fc]) + bgu_ref[0, :, ff + j * fc:ff + (j + 1) * fc]
        g = jnp.minimum(g, SWIGLU_LIMIT)
        u = jnp.clip(u, -SWIGLU_LIMIT, SWIGLU_LIMIT)
        act = g * jax.nn.sigmoid(SWIGLU_ALPHA * g) * (u + 1.0)
        part = _dot(act.astype(BF16), wdn_scr[j * fc:(j + 1) * fc, :])
        y = part if y is None else y + part
    o_ref[...] = (y + bdn_ref[0]) * rw_ref[...]


def _expert_ffn(h_pad, row_tok, chunk_e, w_gu, b_gu, w_dn, b_dn, row_w, tm=MOE_ROWS, fc=512, groups=4):
    n_rows = row_tok.shape[0]
    d = h_pad.shape[1]
    ne, _, ff2 = w_gu.shape
    ff = ff2 // 2
    n_chunks = n_rows // tm
    assert n_chunks % groups == 0
    per = n_chunks // groups
    b_gu3, b_dn3, row_w2 = b_gu.reshape(ne, 1, ff2), b_dn.reshape(ne, 1, d), row_w.reshape(n_rows, 1)
    y = None
    for gi in range(groups):
        off = gi * per
        rows = h_pad[row_tok[off * tm:(off + per) * tm]]
        in_specs = [pl.BlockSpec((tm, d), lambda c, ce: (c, 0)),
                    pl.BlockSpec((1, d, ff2), lambda c, ce, off=off: (ce[c + off], 0, 0)),
                    pl.BlockSpec((1, 1, ff2), lambda c, ce, off=off: (ce[c + off], 0, 0)),
                    pl.BlockSpec((1, ff, d), lambda c, ce, off=off: (ce[c + off], 0, 0)),
                    pl.BlockSpec((1, 1, d), lambda c, ce, off=off: (ce[c + off], 0, 0)),
                    pl.BlockSpec((tm, 1), lambda c, ce, off=off: (c + off, 0))]
        args = [chunk_e, rows, w_gu, b_gu3, w_dn, b_dn3, row_w2]
        aliases = {}
        if y is not None:
            in_specs.append(pl.BlockSpec(memory_space=pl.ANY))
            args.append(y)
            aliases = {len(args) - 1: 0}
        y = pl.pallas_call(
            functools.partial(_expert_kernel, ff=ff, fc=fc, off=off),
            grid_spec=pltpu.PrefetchScalarGridSpec(
                num_scalar_prefetch=1, grid=(per,), in_specs=in_specs,
                out_specs=pl.BlockSpec((tm, d), lambda c, ce, off=off: (c + off, 0)),
                scratch_shapes=[pltpu.VMEM((d, ff2), BF16), pltpu.VMEM((ff, d), BF16)]),
            out_shape=jax.ShapeDtypeStruct((n_rows, d), F32),
            input_output_aliases=aliases,
            compiler_params=_cparams("arbitrary"),
            name="moe_expert_ffn",
        )(*args)
    return y


def _combine_kernel(y_ref, x_ref, g2_ref, o_ref):
    tot = y_ref[0]
    for k in range(1, TOP_K):
        tot = tot + y_ref[k]
    o_ref[0] = x_ref[0] + g2_ref[0] * tot


def _combine(y4, x, g2, tm=512):
    bn, sn, d = x.shape
    row = lambda b, i: (b, i, 0)
    nt = sn // tm
    return pl.pallas_call(
        _combine_kernel,
        grid=(bn, nt),
        in_specs=[pl.BlockSpec((TOP_K, tm, d), lambda b, i: (0, b * nt + i, 0)),
                  pl.BlockSpec((1, tm, d), row),
                  pl.BlockSpec((1, 1, d), lambda b, i: (b, 0, 0))],
        out_specs=pl.BlockSpec((1, tm, d), row),
        out_shape=jax.ShapeDtypeStruct((bn, sn, d), F32),
        compiler_params=_cparams("parallel", "parallel"),
        name="moe_combine",
    )(y4, x, g2)


def _moe(x, g, sc, sh, g2, w_router, b_router, w_gu, b_gu, w_dn, b_dn):
    bn, sn, d = x.shape
    n_tok = bn * sn
    n_asg = n_tok * TOP_K
    tm = MOE_ROWS
    h, e_out, w_out, rank_out, totals = _router(x, g, sc, sh, w_router, b_router)
    e_tok = e_out[:, :, :TOP_K].reshape(n_tok, TOP_K)
    w_flat = w_out[:, :, :TOP_K].reshape(-1)
    rank = rank_out[:, :, :TOP_K].reshape(n_tok, TOP_K)
    counts = totals[0, :N_EXPERTS].astype(jnp.int32)
    starts = jnp.cumsum(counts) - counts
    padded = (counts + tm - 1) // tm * tm
    pad_ends = jnp.cumsum(padded)
    pad_starts = pad_ends - padded
    pos = pad_starts[e_tok] + rank
    n_chunks = n_asg // tm + N_EXPERTS
    n_rows = n_chunks * tm
    chunk_e = jnp.minimum(jnp.searchsorted(pad_ends, jnp.arange(n_chunks, dtype=jnp.int32) * tm, side='right'),
                          N_EXPERTS - 1).astype(jnp.int32)
    order = jnp.argsort(e_tok.reshape(-1))
    in_chunk = jnp.arange(tm, dtype=jnp.int32)[None, :]
    idx_in_e = (jnp.arange(n_chunks, dtype=jnp.int32) * tm - pad_starts[chunk_e])[:, None] + in_chunk
    row_valid = (idx_in_e < counts[chunk_e][:, None]).reshape(-1)
    src = jnp.clip(starts[chunk_e][:, None] + idx_in_e, 0, n_asg - 1).reshape(-1)
    asg = order[src].astype(jnp.int32)
    row_tok = jnp.where(row_valid, asg // TOP_K, n_tok)
    row_w = jnp.where(row_valid, w_flat[asg], 0.0)
    h_pad = jnp.concatenate([h.reshape(n_tok, d), jnp.zeros((1, d), BF16)], axis=0)
    y = _expert_ffn(h_pad, row_tok, chunk_e, w_gu, b_gu, w_dn, b_dn, row_w)
    return _combine(y[pos.T], x, g2)


def _norm_rope(x, g, cos, sin, bd, hd):
    sq = x * x
    hi = sq.astype(BF16)
    lo = (sq - hi.astype(F32)).astype(BF16)
    ss = _dot(hi, bd) + _dot(lo, bd)
    y = x * lax.rsqrt(ss * (1.0 / hd) + NORM_EPS) * g
    half = hd // 2
    first = (_iota2(x.shape, 1) & (hd - 1)) < half
    partner = jnp.where(first, pltpu.roll(y, LANES - half, 1), pltpu.roll(y, half, 1))
    return y * cos + partner * sin


def _prep_kernel(aq_ref, bq_ref, bk_ref, bv_ref, cq_ref, ck_ref, cv_ref, dq_ref, akc_ref, aks_ref, akw_ref,
                 dk_ref, dv_ref, c64_ref, s64_ref, c32_ref, s32_ref, gaq_ref, gak_ref, gcq_ref, gck_ref,
                 gdq_ref, gdk_ref, bd64_ref, bd32_ref,
                 oaq_ref, okc_ref, ovc_ref, okx_ref, ovs_ref, okw_ref, ovw_ref, obq_ref, obk_ref, obv_ref,
                 ocq_ref, ock_ref, ocv_ref, odq_ref, odk_ref, odv_ref, *, ts, sb_scale):
    hd, dd = HEAD_DIM, DIFF_DIM
    c64, s64, c32, s32 = c64_ref[0], s64_ref[0], c32_ref[0], s32_ref[0]
    bd64, bd32 = bd64_ref[...], bd32_ref[...]

    def slabs(ref):
        x = ref[0]
        return [x[:, c * LANES:(c + 1) * LANES] for c in range(x.shape[1] // LANES)]

    def put_heads(o_ref, c, y, width):
        per = LANES // width
        for u in range(per):
            o_ref[0, c * per + u] = y[:, u * width:(u + 1) * width].astype(o_ref.dtype)

    for src, gain, dst in ((aq_ref, gaq_ref, oaq_ref), (dq_ref, gdq_ref, odq_ref), (dk_ref, gdk_ref, odk_ref)):
        for c, x in enumerate(slabs(src)):
            put_heads(dst, c, _norm_rope(x, gain[:, c * LANES:(c + 1) * LANES], c64, s64, bd64, hd), hd)
    kc = _norm_rope(akc_ref[0], gak_ref[0:1, :], c64, s64, bd64, hd)
    ks = _norm_rope(aks_ref[0], gak_ref[1:2, :], c64, s64, bd64, hd)
    kw = _norm_rope(akw_ref[0], gak_ref[2:3, :], c64, s64, bd64, hd)
    okc_ref[0] = kc[:, :hd].astype(BF16)
    ovc_ref[0] = akc_ref[0][:, hd:].astype(BF16)
    nsel = okx_ref.shape[2] - hd
    blk = (pl.program_id(1) * ts + _iota2((ts, nsel), 0)) >> (NSA_SEL_BLOCK.bit_length() - 1)
    okx_ref[0, :, :nsel] = jnp.where(blk == _iota2((ts, nsel), 1), 1.0, 0.0).astype(BF16)
    okx_ref[0, :, nsel:] = ks[:, :hd].astype(BF16)
    ovs_ref[0] = aks_ref[0][:, hd:].astype(BF16)
    okw_ref[0, 0] = kw[:, :hd].astype(BF16)
    ovw_ref[0, 0] = akw_ref[0][:, hd:].astype(BF16)
    for c, x in enumerate(slabs(bq_ref)):
        put_heads(obq_ref, c, x * sb_scale, hd)
    for src, dst in ((bk_ref, obk_ref), (bv_ref, obv_ref), (cv_ref, ocv_ref)):
        for c, x in enumerate(slabs(src)):
            put_heads(dst, c, x, hd)
    put_heads(odv_ref, 0, dv_ref[0], hd)
    for src, gain, dst in ((cq_ref, gcq_ref, ocq_ref), (ck_ref, gck_ref, ock_ref)):
        for c, x in enumerate(slabs(src)):
            put_heads(dst, c, _norm_rope(x, gain[:, c * LANES:(c + 1) * LANES], c32, s32, bd32, dd), dd)


def _rope_tables(positions, hd):
    half = hd // 2
    inv = ROPE_THETA ** (-jnp.arange(half, dtype=F32) * 2.0 / hd)
    ang = positions.astype(F32)[..., None] * inv
    cos, sin = jnp.cos(ang), jnp.sin(ang)
    reps = LANES // hd
    return (jnp.tile(jnp.concatenate([cos, cos], axis=-1), (1, 1, reps)),
            jnp.tile(jnp.concatenate([-sin, sin], axis=-1), (1, 1, reps)))


def _prep(proj, cols, tables, p, n_sel, ts=512):
    bn, sn, _ = proj.shape
    hd, dd = HEAD_DIM, DIFF_DIM
    scale = hd ** -0.5
    c64, s64, c32, s32 = tables

    def cspec(name):
        off, width = cols[name]
        assert off % width == 0
        return pl.BlockSpec((1, ts, width), lambda b, i, blk=off // width: (b, i, blk))

    def tile_gain(g, reps, mult=1.0):
        return (jnp.tile(g.astype(F32), reps) * mult).reshape(1, -1)

    ones = jnp.ones((hd,), F32)
    gak = jnp.stack([jnp.concatenate([p["nsa_kn"][j].astype(F32), ones]) for j in range(3)])
    gains = [tile_gain(p["nsa_qn"], NSA_HEADS, scale), gak,
             tile_gain(p["dif_qn"], 2 * DIFF_HEADS, dd ** -0.5), tile_gain(p["dif_kn"], 2 * DIFF_HEADS),
             tile_gain(p["swa_qn"], SWA_HEADS, scale), tile_gain(p["swa_kn"], SWA_KV_HEADS)]
    lane = np.arange(LANES)
    bd64 = jnp.asarray((lane[:, None] // hd == lane[None, :] // hd).astype(np.float32), BF16)
    bd32 = jnp.asarray((lane[:, None] // dd == lane[None, :] // dd).astype(np.float32), BF16)
    names = ("a_q", "b_q", "b_k", "b_v", "c_q", "c_k", "c_v", "d_q", "a_kcvc", "a_ksvs", "a_kwvw", "d_k", "d_v")
    tab = pl.BlockSpec((1, ts, LANES), lambda b, i: (b, i, 0))
    full = lambda a: pl.BlockSpec(a.shape, lambda b, i: (0,) * a.ndim)

    def hm(nh, w):
        return (jax.ShapeDtypeStruct((bn, nh, sn, w), BF16), pl.BlockSpec((1, nh, ts, w), lambda b, i: (b, 0, i, 0)))

    def tm_(w):
        return (jax.ShapeDtypeStruct((bn, sn, w), BF16), pl.BlockSpec((1, ts, w), lambda b, i: (b, i, 0)))

    outs = [hm(NSA_HEADS, hd), tm_(hd), tm_(hd), tm_(n_sel + hd), tm_(hd), hm(1, hd), hm(1, hd),
            hm(SB_HEADS, hd), hm(SB_HEADS, hd), hm(SB_HEADS, hd),
            hm(2 * DIFF_HEADS, dd), hm(2 * DIFF_HEADS, dd), hm(DIFF_HEADS, 2 * dd),
            hm(SWA_HEADS, hd), hm(SWA_KV_HEADS, hd), hm(SWA_KV_HEADS, hd)]
    consts = gains + [bd64, bd32]
    res = pl.pallas_call(
        functools.partial(_prep_kernel, ts=ts, sb_scale=-scale * LOG2E),
        grid=(bn, sn // ts),
        in_specs=[cspec(n) for n in names] + [tab] * 4 + [full(a) for a in consts],
        out_specs=[o[1] for o in outs],
        out_shape=[o[0] for o in outs],
        compiler_params=_cparams("parallel", "parallel"),
        name="mixer_prep",
    )(*([proj] * len(names)), c64, s64, c32, s32, *consts)
    keys = ("a_q", "kc", "vc", "kx", "vs", "kw", "vw", "b_q", "b_k", "b_v", "c_q", "c_k", "c_v", "d_q", "d_k", "d_v")
    return dict(zip(keys, res))


def _layer_columns(d):
    cols = {}
    off = 0
    for name, width in (("br_g", N_BRANCH * d), ("ga", 3 * NSA_HEADS * HEAD_DIM), ("a_q", 256),
                        ("b_q", 256), ("b_k", 256), ("b_v", 256), ("c_q", 256), ("c_k", 256), ("c_v", 256),
                        ("d_q", 256), ("a_kcvc", 128), ("a_ksvs", 128), ("a_kwvw", 128), ("d_k", 128), ("d_v", 128)):
        cols[name] = (off, width)
        off += width
    return cols, off


def _reorder_w_in(w_in, d, n_pad):
    ref_splits = (256, 128, 128, 128, 12, 256, 256, 256, 256, 256, 256, 256, 128, 128, N_BRANCH * d)
    names = ("a_q", "a_kcvc", "a_ksvs", "a_kwvw", "a_g", "b_q", "b_k", "b_v",
             "c_q", "c_k", "c_v", "d_q", "d_k", "d_v", "br_g")
    starts = np.cumsum((0,) + ref_splits)
    src = {n: (int(starts[i]), ref_splits[i]) for i, n in enumerate(names)}
    cols, total = _layer_columns(d)
    pieces = []
    for name, (off, width) in cols.items():
        if name == "ga":
            g0 = src["a_g"][0]
            gates = w_in[:, g0:g0 + 3 * NSA_HEADS].reshape(-1, NSA_HEADS, 3)
            gates = jnp.broadcast_to(gates.transpose(0, 2, 1)[..., None], (w_in.shape[0], 3, NSA_HEADS, HEAD_DIM))
            pieces.append(gates.reshape(w_in.shape[0], width))
        else:
            pieces.append(w_in[:, src[name][0]:src[name][0] + width])
    pieces.append(jnp.zeros((w_in.shape[0], n_pad - total), w_in.dtype))
    return jnp.concatenate(pieces, axis=1).astype(BF16)


def _mixer_layer(x, positions, tables, mod, p, lam_init):
    bn, sn, d = x.shape
    sh1, sc1, g1 = mod[0], mod[1], mod[2]
    cols, total = _layer_columns(d)
    n_pad = -(-total // 768) * 768
    w_in = _reorder_w_in(p["w_in"], d, n_pad)
    proj = _in_projection(x, p["norm1"], sc1, sh1, w_in)
    n_cmp = (sn - NSA_CMP_LEN) // NSA_CMP_STRIDE + 1
    n_sel = sn // NSA_SEL_BLOCK
    n_top = min(NSA_N_SEL, n_sel)
    t = _prep(proj, cols, tables, p, n_sel)
    k_cmp = _compress(t["kc"], p["nsa_pe_k"], p["nsa_w1_k"], p["nsa_w2_k"]).astype(BF16)
    v_cmp = _compress(t["vc"], p["nsa_pe_v"], p["nsa_w1_v"], p["nsa_w2_v"]).astype(BF16)
    cmp_start = np.arange(sn // NSA_CMP_STRIDE) * NSA_CMP_STRIDE
    sel_start = np.arange(n_sel) * NSA_SEL_BLOCK
    overlap = ((cmp_start[:, None] <= (sel_start + NSA_SEL_BLOCK - 1)[None, :]) &
               ((cmp_start + NSA_CMP_LEN - 1)[:, None] >= sel_start[None, :]) &
               (np.arange(sn // NSA_CMP_STRIDE) < n_cmp)[:, None]).astype(np.float32)
    o_c, bias = _nsa_cmp(t["a_q"], k_cmp, v_cmp, jnp.asarray(overlap, BF16), n_cmp, n_top)
    o_s = _nsa_sel(t["a_q"], t["kx"], t["vs"], bias)
    o_w = _banded_attention(t["a_q"], t["kw"], t["vw"], NSA_WINDOW)
    o_b = _stick_breaking(t["b_q"], t["b_k"], t["b_v"])
    lam = (jnp.exp(jnp.sum(p["dif_lq1"] * p["dif_lk1"])) - jnp.exp(jnp.sum(p["dif_lq2"] * p["dif_lk2"]))
           + lam_init)
    o_cd = _diff_attention(t["c_q"], t["c_k"], t["c_v"], p["dif_subln"], lam, jnp.asarray(1.0 - lam_init, F32))
    o_d = _banded_attention(t["d_q"], t["d_k"], t["d_v"], SWA_WINDOW, sinks=p["swa_sinks"])
    return _merge(proj, o_c, o_s, o_w, o_b, o_cd, o_d, p["w_up"].astype(BF16), p["w_out"].astype(BF16),
                  x, g1, cols["ga"][0])


def kernel(x, c, positions, w_ada, b_ada, norm1, norm2, w_in, nsa_qn, nsa_kn, nsa_pe_k, nsa_w1_k, nsa_w2_k,
           nsa_pe_v, nsa_w1_v, nsa_w2_v, dif_qn, dif_kn, dif_lq1, dif_lk1, dif_lq2, dif_lk2, dif_subln,
           swa_qn, swa_kn, swa_sinks, w_up, w_out, w_router, b_router, w_gu, b_gu, w_dn, b_dn):
    bn, sn, d = x.shape
    depth = w_ada.shape[0]
    c_pad = jnp.zeros((8, d), F32).at[:bn].set(c)
    tables = _rope_tables(positions, HEAD_DIM) + _rope_tables(positions, DIFF_DIM)
    for l in range(depth):
        lam_init = 0.8 - 0.6 * math.exp(-0.3 * l)
        mod = _linear(c_pad, w_ada[l], b_ada[l], tn=512, precision=lax.Precision.HIGHEST)[:bn]
        mod = mod.reshape(bn, 6, 1, d).transpose(1, 0, 2, 3)
        p = dict(norm1=norm1[l], w_in=w_in[l], nsa_qn=nsa_qn[l], nsa_kn=nsa_kn[l], nsa_pe_k=nsa_pe_k[l],
                 nsa_w1_k=nsa_w1_k[l], nsa_w2_k=nsa_w2_k[l], nsa_pe_v=nsa_pe_v[l], nsa_w1_v=nsa_w1_v[l],
                 nsa_w2_v=nsa_w2_v[l], dif_qn=dif_qn[l], dif_kn=dif_kn[l], dif_lq1=dif_lq1[l],
                 dif_lk1=dif_lk1[l], dif_lq2=dif_lq2[l], dif_lk2=dif_lk2[l], dif_subln=dif_subln[l],
                 swa_qn=swa_qn[l], swa_kn=swa_kn[l], swa_sinks=swa_sinks[l], w_up=w_up[l], w_out=w_out[l])
        x = _mixer_layer(x, positions, tables, mod, p, lam_init)
        x = _moe(x, norm2[l], mod[4], mod[3], mod[5], w_router[l], b_router[l],
                 w_gu[l], b_gu[l], w_dn[l], b_dn[l])
    return x
```

```python
import functools
import math

import numpy as np
import jax
import jax.numpy as jnp
from jax import lax
from jax.experimental import pallas as pl
from jax.experimental.pallas import tpu as pltpu

F32 = jnp.float32
BF16 = jnp.bfloat16

HEAD_DIM = 64
ROPE_THETA = 10000.0
NORM_EPS = 1e-6
NEG_INF = -1e30
KNOCKOUT = -3e38
N_BRANCH = 4

NSA_HEADS = 4
NSA_CMP_LEN = 32
NSA_CMP_STRIDE = 16
NSA_SEL_BLOCK = 64
NSA_N_SEL = 16
NSA_WINDOW = 512
NSA_FORCED_SCORE = 1e4

SB_HEADS = 4
DIFF_HEADS = 4
DIFF_DIM = 32
SWA_HEADS = 4
SWA_KV_HEADS = 2
SWA_WINDOW = 128

N_EXPERTS = 32
TOP_K = 4
SWIGLU_ALPHA = 1.702
SWIGLU_LIMIT = 7.0

LANES = 128
LOG2E = 1.4426950408889634
MOE_ROWS = 512
VMEM_LIMIT = 56 * 1024 * 1024


def _cparams(*sem):
    return pltpu.CompilerParams(dimension_semantics=sem, vmem_limit_bytes=VMEM_LIMIT)


def _dot(a, b):
    return jnp.dot(a, b, preferred_element_type=F32)


def _dot_nt(a, b):
    return lax.dot_general(a, b, (((1,), (1,)), ((), ())), preferred_element_type=F32)


def _iota2(shape, dim):
    return lax.broadcasted_iota(jnp.int32, shape, dim)


def _linear_kernel(x_ref, w_ref, b_ref, o_ref, *, precision):
    o_ref[...] = jnp.dot(x_ref[...], w_ref[...], preferred_element_type=F32,
                         precision=precision) + b_ref[...]


def _linear(x, w, b, tn, precision=None):
    m, k = x.shape
    n = w.shape[1]
    return pl.pallas_call(
        functools.partial(_linear_kernel, precision=precision),
        grid=(n // tn,),
        in_specs=[pl.BlockSpec((m, k), lambda j: (0, 0)),
                  pl.BlockSpec((k, tn), lambda j: (0, j)),
                  pl.BlockSpec((1, tn), lambda j: (0, j))],
        out_specs=pl.BlockSpec((m, tn), lambda j: (0, j)),
        out_shape=jax.ShapeDtypeStruct((m, n), F32),
        compiler_params=_cparams("arbitrary"),
        name="linear",
    )(x, w, b.reshape(1, n))


def _cmp_mlp_kernel(a_ref, b_ref, pe_ref, w2_ref, o_ref):
    hid = jax.nn.gelu(a_ref[...] + b_ref[...] + pe_ref[...])
    o_ref[...] = _dot(hid.astype(BF16), w2_ref[...])


def _compress(t, pe, w1, w2):
    bn, sn, dh = t.shape
    st = NSA_CMP_STRIDE
    half = st * dh
    nb = sn // st
    t16 = t.reshape(bn * nb, half).astype(BF16)
    w1cat = jnp.concatenate([w1[:half], w1[half:]], axis=1).astype(BF16)
    hidden = w1.shape[1]
    ab = _linear(t16, w1cat, jnp.zeros((2 * hidden,), F32), tn=2 * hidden)
    ab = ab.reshape(bn, nb, 2 * hidden)
    a = ab[:, :, :hidden]
    b_next = jnp.concatenate([ab[:, 1:, hidden:], jnp.zeros((bn, 1, hidden), F32)], axis=1)
    pe_term = jnp.dot(pe.reshape(1, NSA_CMP_LEN * dh), w1, precision=lax.Precision.HIGHEST)
    rows = bn * nb
    tm = min(512, rows)
    out = pl.pallas_call(
        _cmp_mlp_kernel,
        grid=(rows // tm,),
        in_specs=[pl.BlockSpec((tm, hidden), lambda i: (i, 0)),
                  pl.BlockSpec((tm, hidden), lambda i: (i, 0)),
                  pl.BlockSpec((1, hidden), lambda i: (0, 0)),
                  pl.BlockSpec((hidden, dh), lambda i: (0, 0))],
        out_specs=pl.BlockSpec((tm, dh), lambda i: (i, 0)),
        out_shape=jax.ShapeDtypeStruct((rows, dh), F32),
        compiler_params=_cparams("parallel"),
        name="cmp_mlp",
    )(a.reshape(rows, hidden), b_next.reshape(rows, hidden), pe_term, w2.astype(BF16))
    return out.reshape(bn, nb, dh)


def _norm_mod(x, g, sc, sh):
    r = lax.rsqrt(jnp.mean(x * x, axis=-1, keepdims=True) + NORM_EPS)
    return (x * r * g) * (1.0 + sc) + sh


def _proj_kernel(x_ref, g_ref, sc_ref, sh_ref, w_ref, o_ref, h_scr):
    @pl.when(pl.program_id(2) == 0)
    def _():
        h_scr[...] = _norm_mod(x_ref[0], g_ref[...], sc_ref[0], sh_ref[0]).astype(BF16)

    o_ref[0] = _dot(h_scr[...], w_ref[...])


def _in_projection(x, g, sc, sh, w, tm=1024, tn=1536):
    bn, sn, d = x.shape
    n = w.shape[1]
    tm = min(tm, sn)
    return pl.pallas_call(
        _proj_kernel,
        grid=(bn, sn // tm, n // tn),
        in_specs=[pl.BlockSpec((1, tm, d), lambda b, i, j: (b, i, 0)),
                  pl.BlockSpec((1, d), lambda b, i, j: (0, 0)),
                  pl.BlockSpec((1, 1, d), lambda b, i, j: (b, 0, 0)),
                  pl.BlockSpec((1, 1, d), lambda b, i, j: (b, 0, 0)),
                  pl.BlockSpec((d, tn), lambda b, i, j: (0, j))],
        out_specs=pl.BlockSpec((1, tm, tn), lambda b, i, j: (b, i, j)),
        out_shape=jax.ShapeDtypeStruct((bn, sn, n), F32),
        scratch_shapes=[pltpu.VMEM((tm, d), BF16)],
        compiler_params=_cparams("parallel", "parallel", "arbitrary"),
        name="in_proj",
    )(x, g.reshape(1, d), sc, sh, w)


def _banded_kernel(*refs, tile, window, has_sink):
    if has_sink:
        sink_ref, q_ref, kp_ref, kc_ref, vp_ref, vc_ref, o_ref = refs
    else:
        q_ref, kp_ref, kc_ref, vp_ref, vc_ref, o_ref = refs
    i = pl.program_id(2)
    w = window
    upper = _iota2((w, w), 1) > _iota2((w, w), 0)
    first_bias = jnp.where(i > 0, 0.0, NEG_INF)
    for u in range(tile // w):
        q = q_ref[0, 0, u * w:(u + 1) * w, :]
        if u == 0:
            k_prev, v_prev = kp_ref[0, 0], vp_ref[0, 0]
        else:
            k_prev, v_prev = kc_ref[0, 0, (u - 1) * w:u * w, :], vc_ref[0, 0, (u - 1) * w:u * w, :]
        k_cur, v_cur = kc_ref[0, 0, u * w:(u + 1) * w, :], vc_ref[0, 0, u * w:(u + 1) * w, :]
        s_prev = _dot_nt(q, k_prev)
        if u == 0:
            s_prev = s_prev + first_bias
        s = jnp.where(upper, s_prev, _dot_nt(q, k_cur))
        m = jnp.max(s, axis=-1, keepdims=True)
        if has_sink:
            sink = sink_ref[pl.program_id(1)]
            m = jnp.maximum(m, sink)
        p = jnp.exp(s - m)
        den = jnp.sum(p, axis=-1, keepdims=True)
        if has_sink:
            den = den + jnp.exp(sink - m)
        else:
            den = jnp.maximum(den, 1e-30)
        o = (_dot(jnp.where(upper, p, 0.0).astype(BF16), v_prev)
             + _dot(jnp.where(upper, 0.0, p).astype(BF16), v_cur))
        o_ref[0, 0, u * w:(u + 1) * w, :] = o / den


def _banded_attention(q, k, v, window, sinks=None, tile=1024):
    bn, hq, sn, d = q.shape
    grp = hq // k.shape[1]
    tile = min(tile, sn)
    assert tile % window == 0 and sn % tile == 0
    per = tile // window
    has_sink = sinks is not None
    qspec = pl.BlockSpec((1, 1, tile, d), lambda b, h, i: (b, h, i, 0))
    prev = pl.BlockSpec((1, 1, window, d), lambda b, h, i: (b, h // grp, jnp.maximum(i * per - 1, 0), 0))
    cur = pl.BlockSpec((1, 1, tile, d), lambda b, h, i: (b, h // grp, i, 0))
    in_specs = [qspec, prev, cur, prev, cur]
    args = [q, k, k, v, v]
    if has_sink:
        in_specs = [pl.BlockSpec(memory_space=pltpu.SMEM)] + in_specs
        args = [sinks.astype(F32)] + args
    return pl.pallas_call(
        functools.partial(_banded_kernel, tile=tile, window=window, has_sink=has_sink),
        grid=(bn, hq, sn // tile),
        in_specs=in_specs,
        out_specs=pl.BlockSpec((1, 1, tile, d), lambda b, h, i: (b, h, i, 0)),
        out_shape=jax.ShapeDtypeStruct((bn, hq, sn, d), F32),
        compiler_params=_cparams("parallel", "parallel", "parallel"),
        name="banded_attn",
    )(*args)


def _nsa_cmp_kernel(q_ref, kc_ref, vct_ref, ovt_ref, oct_ref, bias_ref, *, tq, n_cmp, n_top, heads):
    i = pl.program_id(1)
    ncp = kc_ref.shape[1]
    nsel = ovt_ref.shape[0]
    dh = q_ref.shape[3]
    t = i * tq + _iota2((ncp, tq), 1)
    n = _iota2((ncp, tq), 0)
    valid = (n * NSA_CMP_STRIDE + (NSA_CMP_LEN - 1) <= t) & (n < n_cmp)
    kc = kc_ref[0]
    vct = vct_ref[0]
    psum = jnp.zeros((ncp, tq), F32)
    for h in range(heads):
        s = jnp.where(valid, _dot_nt(kc, q_ref[0, h]), NEG_INF)
        m = jnp.max(s, axis=0, keepdims=True)
        p = jnp.where(valid, jnp.exp(s - m), 0.0)
        p = p / jnp.maximum(jnp.sum(p, axis=0, keepdims=True), 1e-30)
        oct_ref[0, h * dh:(h + 1) * dh, :] = _dot(vct, p.astype(BF16))
        psum = psum + p
    hi = psum.astype(BF16)
    lo = (psum - hi.astype(F32)).astype(BF16)
    imp = _dot(ovt_ref[...], hi) + _dot(ovt_ref[...], lo)

    tt = i * tq + _iota2((nsel, tq), 1)
    blk = _iota2((nsel, tq), 0)
    cur = tt >> (NSA_SEL_BLOCK.bit_length() - 1)
    forced = (blk == 0) | (blk == cur) | (blk == cur - 1)
    valid_s = blk * NSA_SEL_BLOCK <= tt
    score = jnp.where(forced, NSA_FORCED_SCORE, jnp.where(valid_s, imp, -1.0))
    blk_f = blk.astype(F32)

    def pick(_, carry):
        score, sel = carry
        m = jnp.max(score, axis=0, keepdims=True)
        first = jnp.min(jnp.where(score == m, blk_f, float(nsel)), axis=0, keepdims=True)
        hit = blk_f == first
        return jnp.where(hit, KNOCKOUT, score), jnp.where(hit, 0.0, sel)

    _, bias = lax.fori_loop(0, n_top, pick, (score, jnp.full((nsel, tq), NEG_INF, F32)))
    bias_ref[0] = bias.astype(BF16)


def _nsa_cmp(q, kc, vc, overlap, n_cmp, n_top, tq=256):
    bn, heads, sn, dh = q.shape
    ncp = kc.shape[1]
    nsel = overlap.shape[1]
    oct, bias_t = pl.pallas_call(
        functools.partial(_nsa_cmp_kernel, tq=tq, n_cmp=n_cmp, n_top=n_top, heads=heads),
        grid=(bn, sn // tq),
        in_specs=[pl.BlockSpec((1, heads, tq, dh), lambda b, i: (b, 0, i, 0)),
                  pl.BlockSpec((1, ncp, dh), lambda b, i: (b, 0, 0)),
                  pl.BlockSpec((1, dh, ncp), lambda b, i: (b, 0, 0)),
                  pl.BlockSpec((nsel, ncp), lambda b, i: (0, 0))],
        out_specs=[pl.BlockSpec((1, heads * dh, tq), lambda b, i: (b, 0, i)),
                   pl.BlockSpec((1, nsel, tq), lambda b, i: (b, 0, i))],
        out_shape=[jax.ShapeDtypeStruct((bn, heads * dh, sn), F32),
                   jax.ShapeDtypeStruct((bn, nsel, sn), BF16)],
        compiler_params=_cparams("parallel", "parallel"),
        name="nsa_cmp_topk",
    )(q, kc, vc.transpose(0, 2, 1), overlap.T)
    return oct.transpose(0, 2, 1), bias_t.transpose(0, 2, 1)


def _nsa_sel_kernel(q_ref, kx_ref, v_ref, bias_ref, o_ref, qx_scr, *, tq, tk, heads):
    i = pl.program_id(1)
    dh = v_ref.shape[2]
    bias = bias_ref[0]
    for h in range(heads):
        qx_scr[h] = jnp.concatenate([bias, q_ref[0, h]], axis=1)

    def tile(j, state, diagonal):
        start = pl.multiple_of(j * tk, tk)
        kx = kx_ref[0, pl.ds(start, tk), :]
        v = v_ref[0, pl.ds(start, tk), :]
        if diagonal:
            causal = (start + _iota2((tq, tk), 1)) <= (i * tq + _iota2((tq, tk), 0))
        new_state = []
        for h in range(heads):
            m_old, l_old, acc_old = state[h]
            s = _dot_nt(qx_scr[h], kx)
            if diagonal:
                s = jnp.where(causal, s, NEG_INF)
            m_new = jnp.maximum(m_old, jnp.max(s, axis=-1, keepdims=True))
            alpha = jnp.exp(m_old - m_new)
            p = jnp.exp(s - m_new)
            l_new = alpha * l_old + jnp.sum(p, axis=-1, keepdims=True)
            acc_new = alpha * acc_old + _dot(p.astype(BF16), v)
            new_state.append((m_new, l_new, acc_new))
        return tuple(new_state)

    init = tuple((jnp.full((tq, 1), NEG_INF, F32), jnp.zeros((tq, 1), F32), jnp.zeros((tq, dh), F32))
                 for _ in range(heads))
    n_full = (i * tq) // tk
    state = lax.fori_loop(0, n_full, lambda j, st: tile(j, st, False), init)
    state = tile(n_full, state, True)
    for h in range(heads):
        o_ref[0, :, h * dh:(h + 1) * dh] = state[h][2] / jnp.maximum(state[h][1], 1e-30)


def _nsa_sel(q, kx, v, bias, tq=1024, tk=1024):
    bn, heads, sn, dh = q.shape
    nsel = bias.shape[2]
    tq, tk = min(tq, sn), min(tk, sn)
    assert tk % tq == 0 and sn % tk == 0
    return pl.pallas_call(
        functools.partial(_nsa_sel_kernel, tq=tq, tk=tk, heads=heads),
        grid=(bn, sn // tq),
        in_specs=[pl.BlockSpec((1, heads, tq, dh), lambda b, i: (b, 0, i, 0)),
                  pl.BlockSpec((1, sn, nsel + dh), lambda b, i: (b, 0, 0)),
                  pl.BlockSpec((1, sn, dh), lambda b, i: (b, 0, 0)),
                  pl.BlockSpec((1, tq, nsel), lambda b, i: (b, i, 0))],
        out_specs=pl.BlockSpec((1, tq, heads * dh), lambda b, i: (b, i, 0)),
        out_shape=jax.ShapeDtypeStruct((bn, sn, heads * dh), F32),
        scratch_shapes=[pltpu.VMEM((heads, tq, nsel + dh), BF16)],
        compiler_params=_cparams("parallel", "arbitrary"),
        name="nsa_selected_attn",
    )(q, kx, v, bias)


def _sb_kernel(q_ref, k_ref, v_ref, u_ref, o_ref, *, tq, tk):
    i = pl.program_id(2)
    q = q_ref[0, 0]
    per_q = tq // tk

    def tile(jj, carry, first_row):
        diagonal = first_row is not None
        r0 = first_row if diagonal else 0
        start = pl.multiple_of(jj * tk, tk)
        nz = _dot_nt(q[r0:], k_ref[0, 0, pl.ds(start, tk), :])
        neg_abs = lax.bitcast_convert_type(lax.bitcast_convert_type(nz, jnp.uint32) | jnp.uint32(0x80000000), F32)
        log_keep = jnp.minimum(nz, 0.0) - jnp.log2(1.0 + jnp.exp2(neg_abs))
        if diagonal:
            strict = _iota2(nz.shape, 1) < _iota2(nz.shape, 0)
            log_keep = jnp.where(strict, log_keep, 0.0)
        cum = _dot(log_keep.astype(BF16), u_ref[...])
        a = jnp.exp2(cum + jnp.concatenate([carry[r0:]] * (tk // LANES), axis=1) - nz)
        if diagonal:
            a = jnp.where(strict, a, 0.0)
        out = _dot(a.astype(BF16), v_ref[0, 0, pl.ds(start, tk), :])
        new_carry = carry[r0:] + jnp.broadcast_to(cum[:, 0:1], (tq - r0, LANES))
        if r0:
            out = jnp.concatenate([jnp.zeros((r0, out.shape[1]), F32), out], axis=0)
            new_carry = jnp.concatenate([carry[:r0], new_carry], axis=0)
        return out, new_carry

    def group(first, carry, diagonal):
        total = None
        for r in range(per_q):
            out, carry = tile(first - r, carry, (per_q - 1 - r) * tk if diagonal else None)
            total = out if total is None else total + out
        return total, carry

    acc, carry = group(i * per_q + per_q - 1, jnp.zeros((tq, LANES), F32), True)

    def body(p, state):
        acc, carry = state
        out, carry = group((i - p) * per_q - 1, carry, False)
        return acc + out, carry

    acc, _ = lax.fori_loop(0, i, body, (acc, carry))
    o_ref[0, 0] = acc


def _stick_breaking(q, k, v, tq=1024, tk=256):
    bn, heads, sn, dh = q.shape
    tq = min(tq, sn)
    incl = (np.arange(tk)[:, None] >= np.arange(tk)[None, :]).astype(np.float32)
    u = jnp.asarray(incl, BF16)
    kv = pl.BlockSpec((1, 1, sn, dh), lambda b, h, i: (b, h, 0, 0))
    return pl.pallas_call(
        functools.partial(_sb_kernel, tq=tq, tk=tk),
        grid=(bn, heads, sn // tq),
        in_specs=[pl.BlockSpec((1, 1, tq, dh), lambda b, h, i: (b, h, i, 0)), kv, kv,
                  pl.BlockSpec((tk, tk), lambda b, h, i: (0, 0))],
        out_specs=pl.BlockSpec((1, 1, tq, dh), lambda b, h, i: (b, h, i, 0)),
        out_shape=jax.ShapeDtypeStruct((bn, heads, sn, dh), F32),
        compiler_params=_cparams("parallel", "parallel", "arbitrary"),
        name="stick_breaking_attn",
    )(q, k, v, u)


def _diff_kernel(sc_ref, q_ref, k_ref, v_ref, g_ref, o_ref, *, tq, tk):
    i = pl.program_id(2)
    dv = v_ref.shape[3]

    def tile(j, state, diagonal):
        start = pl.multiple_of(j * tk, tk)
        v = v_ref[0, 0, pl.ds(start, tk), :]
        if diagonal:
            causal = (start + _iota2((tq, tk), 1)) <= (i * tq + _iota2((tq, tk), 0))
        new_state = []
        for mi in range(2):
            m_old, l_old, acc_old = state[mi]
            s = _dot_nt(q_ref[0, mi], k_ref[0, mi, pl.ds(start, tk), :])
            if diagonal:
                s = jnp.where(causal, s, NEG_INF)
            m_new = jnp.maximum(m_old, jnp.max(s, axis=-1, keepdims=True))
            alpha = jnp.exp(m_old - m_new)
            p = jnp.exp(s - m_new)
            l_new = alpha * l_old + jnp.sum(p, axis=-1, keepdims=True)
            acc_new = alpha * acc_old + _dot(p.astype(BF16), v)
            new_state.append((m_new, l_new, acc_new))
        return tuple(new_state)

    init = tuple((jnp.full((tq, 1), NEG_INF, F32), jnp.zeros((tq, 1), F32), jnp.zeros((tq, dv), F32))
                 for _ in range(2))
    n_full = (i * tq) // tk
    state = lax.fori_loop(0, n_full, lambda j, st: tile(j, st, False), init)
    state = tile(n_full, state, True)
    lam = sc_ref[0]
    post = sc_ref[1]
    o = (state[0][2] / jnp.maximum(state[0][1], 1e-30)
         - lam * (state[1][2] / jnp.maximum(state[1][1], 1e-30)))
    r = lax.rsqrt(jnp.mean(o * o, axis=-1, keepdims=True) + NORM_EPS)
    o_ref[0, 0] = (o * r * g_ref[...]) * post


def _diff_attention(q, k, v, subln, lam, post, tq=1024, tk=1024):
    bn, h2, sn, dd = q.shape
    heads = h2 // 2
    dv = v.shape[3]
    tq, tk = min(tq, sn), min(tk, sn)
    assert tk % tq == 0 and sn % tk == 0
    scal = jnp.stack([lam, post]).astype(F32)
    return pl.pallas_call(
        functools.partial(_diff_kernel, tq=tq, tk=tk),
        grid=(bn, heads, sn // tq),
        in_specs=[pl.BlockSpec(memory_space=pltpu.SMEM),
                  pl.BlockSpec((1, 2, tq, dd), lambda b, h, i: (b, h, i, 0)),
                  pl.BlockSpec((1, 2, sn, dd), lambda b, h, i: (b, h, 0, 0)),
                  pl.BlockSpec((1, 1, sn, dv), lambda b, h, i: (b, h, 0, 0)),
                  pl.BlockSpec((1, dv), lambda b, h, i: (0, 0))],
        out_specs=pl.BlockSpec((1, 1, tq, dv), lambda b, h, i: (b, h, i, 0)),
        out_shape=jax.ShapeDtypeStruct((bn, heads, sn, dv), F32),
        compiler_params=_cparams("parallel", "parallel", "arbitrary"),
        name="diff_attn",
    )(scal, q, k, v, subln.reshape(1, dv).astype(F32))


def _merge_kernel(brg_ref, ga0_ref, ga1_ref, ga2_ref, oc_ref, os_ref, ow_ref, ob_ref, ocd_ref, od_ref,
                  wup_ref, wout_ref, x_ref, g1_ref, o_ref, *, d):
    def heads(ref):
        return jnp.concatenate([ref[0, h] for h in range(ref.shape[1])], axis=1)

    o_a = (jax.nn.sigmoid(ga0_ref[0]) * oc_ref[0] + jax.nn.sigmoid(ga1_ref[0]) * os_ref[0]
           + jax.nn.sigmoid(ga2_ref[0]) * heads(ow_ref))
    branches = (o_a, heads(ob_ref), heads(ocd_ref), heads(od_ref))
    merged = None
    for bi, o in enumerate(branches):
        gate = jax.nn.sigmoid(brg_ref[0, :, bi * d:(bi + 1) * d])
        term = gate * _dot(o.astype(BF16), wup_ref[bi])
        merged = term if merged is None else merged + term
    y = _dot(merged.astype(BF16), wout_ref[...])
    o_ref[0] = x_ref[0] + g1_ref[0] * y


def _merge(proj, o_c, o_s, o_w, o_b, o_cd, o_d, w_up, w_out, x, g1, ga_col, tm=256):
    bn, sn, d = x.shape
    bw = o_c.shape[2]
    nh, dh = o_b.shape[1], o_b.shape[3]
    assert ga_col % bw == 0
    gblk = ga_col // bw
    row = lambda b, i: (b, i, 0)
    bspec = pl.BlockSpec((1, tm, bw), row)
    hspec = pl.BlockSpec((1, nh, tm, dh), lambda b, i: (b, 0, i, 0))
    return pl.pallas_call(
        functools.partial(_merge_kernel, d=d),
        grid=(bn, sn // tm),
        in_specs=[pl.BlockSpec((1, tm, N_BRANCH * d), row),
                  pl.BlockSpec((1, tm, bw), lambda b, i: (b, i, gblk)),
                  pl.BlockSpec((1, tm, bw), lambda b, i: (b, i, gblk + 1)),
                  pl.BlockSpec((1, tm, bw), lambda b, i: (b, i, gblk + 2)),
                  bspec, bspec, hspec, hspec, hspec, hspec,
                  pl.BlockSpec((N_BRANCH, bw, d), lambda b, i: (0, 0, 0)),
                  pl.BlockSpec((d, d), lambda b, i: (0, 0)),
                  pl.BlockSpec((1, tm, d), row),
                  pl.BlockSpec((1, 1, d), lambda b, i: (b, 0, 0))],
        out_specs=pl.BlockSpec((1, tm, d), row),
        out_shape=jax.ShapeDtypeStruct((bn, sn, d), F32),
        compiler_params=_cparams("parallel", "parallel"),
        name="branch_merge",
    )(proj, proj, proj, proj, o_c, o_s, o_w, o_b, o_cd, o_d, w_up, w_out, x, g1)


def _router_kernel(x_ref, g_ref, sc_ref, sh_ref, wr_ref, br_ref, tri_ref, h_ref, e_ref, w_ref, rank_ref, cnt_ref,
                   run_scr):
    @pl.when((pl.program_id(0) == 0) & (pl.program_id(1) == 0))
    def _():
        run_scr[...] = jnp.zeros(run_scr.shape, F32)

    h = _norm_mod(x_ref[0], g_ref[...], sc_ref[0], sh_ref[0])
    h_ref[0] = h.astype(BF16)
    logits = jnp.dot(h, wr_ref[...], preferred_element_type=F32,
                     precision=lax.Precision.HIGHEST) + br_ref[...]
    lane = _iota2(logits.shape, 1)
    lane_f = lane.astype(F32)
    cur = logits
    vals, idxs = [], []
    chosen = jnp.zeros(logits.shape, F32)
    for _ in range(TOP_K):
        m = jnp.max(cur, axis=-1, keepdims=True)
        first = jnp.min(jnp.where(cur == m, lane_f, float(LANES)), axis=-1, keepdims=True)
        vals.append(m)
        idxs.append(first)
        hit = lane_f == first
        cur = jnp.where(hit, KNOCKOUT, cur)
        chosen = jnp.where(hit, 1.0, chosen)
    exps = [jnp.exp(v - vals[0]) for v in vals]
    den = exps[0]
    for e in exps[1:]:
        den = den + e
    earlier = _dot(tri_ref[...], chosen.astype(BF16)) + run_scr[0:1, :]
    e_out = jnp.zeros(logits.shape, F32)
    w_out = jnp.zeros(logits.shape, F32)
    r_out = jnp.zeros(logits.shape, F32)
    for k in range(TOP_K):
        rank_k = jnp.sum(jnp.where(lane_f == idxs[k], earlier, 0.0), axis=-1, keepdims=True)
        e_out = jnp.where(lane == k, idxs[k], e_out)
        w_out = jnp.where(lane == k, exps[k] / den, w_out)
        r_out = jnp.where(lane == k, rank_k, r_out)
    e_ref[0] = e_out.astype(jnp.int32)
    w_ref[0] = w_out
    rank_ref[0] = r_out.astype(jnp.int32)
    total = run_scr[...] + jnp.sum(chosen, axis=0, keepdims=True)
    run_scr[...] = total
    cnt_ref[...] = total


def _router(x, g, sc, sh, w_router, b_router, tm=512):
    bn, sn, d = x.shape
    ne = w_router.shape[1]
    wr = jnp.zeros((d, LANES), F32).at[:, :ne].set(w_router)
    br = jnp.full((1, LANES), NEG_INF, F32).at[0, :ne].set(b_router)
    tri = jnp.asarray((np.arange(tm)[:, None] > np.arange(tm)[None, :]).astype(np.float32), BF16)
    row = lambda b, i: (b, i, 0)
    return pl.pallas_call(
        _router_kernel,
        grid=(bn, sn // tm),
        in_specs=[pl.BlockSpec((1, tm, d), row),
                  pl.BlockSpec((1, d), lambda b, i: (0, 0)),
                  pl.BlockSpec((1, 1, d), lambda b, i: (b, 0, 0)),
                  pl.BlockSpec((1, 1, d), lambda b, i: (b, 0, 0)),
                  pl.BlockSpec((d, LANES), lambda b, i: (0, 0)),
                  pl.BlockSpec((1, LANES), lambda b, i: (0, 0)),
                  pl.BlockSpec((tm, tm), lambda b, i: (0, 0))],
        out_specs=[pl.BlockSpec((1, tm, d), row),
                   pl.BlockSpec((1, tm, LANES), row),
                   pl.BlockSpec((1, tm, LANES), row),
                   pl.BlockSpec((1, tm, LANES), row),
                   pl.BlockSpec((8, LANES), lambda b, i: (0, 0))],
        out_shape=[jax.ShapeDtypeStruct((bn, sn, d), BF16),
                   jax.ShapeDtypeStruct((bn, sn, LANES), jnp.int32),
                   jax.ShapeDtypeStruct((bn, sn, LANES), F32),
                   jax.ShapeDtypeStruct((bn, sn, LANES), jnp.int32),
                   jax.ShapeDtypeStruct((8, LANES), F32)],
        scratch_shapes=[pltpu.VMEM((8, LANES), F32)],
        compiler_params=_cparams("arbitrary", "arbitrary"),
        name="moe_router",
    )(x, g.reshape(1, d), sc, sh, wr, br, tri)


def _expert_kernel(ce_ref, x_ref, wgu_ref, bgu_ref, wdn_ref, bdn_ref, rw_ref, *rest, ff, fc, off):
    o_ref, wgu_scr, wdn_scr = rest[-3:]
    c = pl.program_id(0)

    @pl.when((c == 0) | (ce_ref[c + off] != ce_ref[jnp.maximum(c + off - 1, 0)]))
    def _():
        wgu_scr[...] = wgu_ref[0, 0].astype(BF16)
        wdn_scr[...] = wdn_ref[0, 0].astype(BF16)

    x = x_ref[...]
    y = None
    for j in range(ff // fc):
        g = _dot(x, wgu_scr[:, j * fc:(j + 1) * fc]) + bgu_ref[0, 0, :, j * fc:(j + 1) * fc]
        u = _dot(x, wgu_scr[:, ff + j * fc:ff + (j + 1) * fc]) + bgu_ref[0, 0, :, ff + j * fc:ff + (j + 1) * fc]
        g = jnp.minimum(g, SWIGLU_LIMIT)
        u = jnp.clip(u, -SWIGLU_LIMIT, SWIGLU_LIMIT)
        act = g * jax.nn.sigmoid(SWIGLU_ALPHA * g) * (u + 1.0)
        part = _dot(act.astype(BF16), wdn_scr[j * fc:(j + 1) * fc, :])
        y = part if y is None else y + part
    o_ref[...] = (y + bdn_ref[0, 0]) * rw_ref[...]


def _expert_ffn(h_pad, row_tok, chunk_e, layer, w_gu, b_gu, w_dn, b_dn, row_w, tm=MOE_ROWS, fc=512, groups=4):
    n_rows = row_tok.shape[0]
    d = h_pad.shape[1]
    nl, ne, _, ff2 = w_gu.shape
    ff = ff2 // 2
    n_chunks = n_rows // tm
    assert n_chunks % groups == 0
    per = n_chunks // groups
    b_gu3, b_dn3, row_w2 = b_gu.reshape(nl, ne, 1, ff2), b_dn.reshape(nl, ne, 1, d), row_w.reshape(n_rows, 1)
    y = None
    for gi in range(groups):
        off = gi * per
        rows = h_pad[row_tok[off * tm:(off + per) * tm]]
        in_specs = [pl.BlockSpec((tm, d), lambda c, ce: (c, 0)),
                    pl.BlockSpec((1, 1, d, ff2), lambda c, ce, off=off: (layer, ce[c + off], 0, 0)),
                    pl.BlockSpec((1, 1, 1, ff2), lambda c, ce, off=off: (layer, ce[c + off], 0, 0)),
                    pl.BlockSpec((1, 1, ff, d), lambda c, ce, off=off: (layer, ce[c + off], 0, 0)),
                    pl.BlockSpec((1, 1, 1, d), lambda c, ce, off=off: (layer, ce[c + off], 0, 0)),
                    pl.BlockSpec((tm, 1), lambda c, ce, off=off: (c + off, 0))]
        args = [chunk_e, rows, w_gu, b_gu3, w_dn, b_dn3, row_w2]
        aliases = {}
        if y is not None:
            in_specs.append(pl.BlockSpec(memory_space=pl.ANY))
            args.append(y)
            aliases = {len(args) - 1: 0}
        y = pl.pallas_call(
            functools.partial(_expert_kernel, ff=ff, fc=fc, off=off),
            grid_spec=pltpu.PrefetchScalarGridSpec(
                num_scalar_prefetch=1, grid=(per,), in_specs=in_specs,
                out_specs=pl.BlockSpec((tm, d), lambda c, ce, off=off: (c + off, 0)),
                scratch_shapes=[pltpu.VMEM((d, ff2), BF16), pltpu.VMEM((ff, d), BF16)]),
            out_shape=jax.ShapeDtypeStruct((n_rows, d), F32),
            input_output_aliases=aliases,
            compiler_params=_cparams("arbitrary"),
            name="moe_expert_ffn",
        )(*args)
    return y


def _combine_kernel(y_ref, x_ref, g2_ref, o_ref):
    tot = y_ref[0]
    for k in range(1, TOP_K):
        tot = tot + y_ref[k]
    o_ref[0] = x_ref[0] + g2_ref[0] * tot


def _combine(y4, x, g2, tm=512):
    bn, sn, d = x.shape
    row = lambda b, i: (b, i, 0)
    nt = sn // tm
    return pl.pallas_call(
        _combine_kernel,
        grid=(bn, nt),
        in_specs=[pl.BlockSpec((TOP_K, tm, d), lambda b, i: (0, b * nt + i, 0)),
                  pl.BlockSpec((1, tm, d), row),
                  pl.BlockSpec((1, 1, d), lambda b, i: (b, 0, 0))],
        out_specs=pl.BlockSpec((1, tm, d), row),
        out_shape=jax.ShapeDtypeStruct((bn, sn, d), F32),
        compiler_params=_cparams("parallel", "parallel"),
        name="moe_combine",
    )(y4, x, g2)


def _moe(x, g, sc, sh, g2, w_router, b_router, layer, w_gu, b_gu, w_dn, b_dn):
    bn, sn, d = x.shape
    n_tok = bn * sn
    n_asg = n_tok * TOP_K
    tm = MOE_ROWS
    h, e_out, w_out, rank_out, totals = _router(x, g, sc, sh, w_router, b_router)
    e_tok = e_out[:, :, :TOP_K].reshape(n_tok, TOP_K)
    w_flat = w_out[:, :, :TOP_K].reshape(-1)
    rank = rank_out[:, :, :TOP_K].reshape(n_tok, TOP_K)
    counts = totals[0, :N_EXPERTS].astype(jnp.int32)
    starts = jnp.cumsum(counts) - counts
    padded = (counts + tm - 1) // tm * tm
    pad_ends = jnp.cumsum(padded)
    pad_starts = pad_ends - padded
    pos = pad_starts[e_tok] + rank
    n_chunks = n_asg // tm + N_EXPERTS
    n_rows = n_chunks * tm
    chunk_e = jnp.minimum(jnp.searchsorted(pad_ends, jnp.arange(n_chunks, dtype=jnp.int32) * tm, side='right'),
                          N_EXPERTS - 1).astype(jnp.int32)
    order = jnp.argsort(e_tok.reshape(-1))
    in_chunk = jnp.arange(tm, dtype=jnp.int32)[None, :]
    idx_in_e = (jnp.arange(n_chunks, dtype=jnp.int32) * tm - pad_starts[chunk_e])[:, None] + in_chunk
    row_valid = (idx_in_e < counts[chunk_e][:, None]).reshape(-1)
    src = jnp.clip(starts[chunk_e][:, None] + idx_in_e, 0, n_asg - 1).reshape(-1)
    asg = order[src].astype(jnp.int32)
    row_tok = jnp.where(row_valid, asg // TOP_K, n_tok)
    row_w = jnp.where(row_valid, w_flat[asg], 0.0)
    h_pad = jnp.concatenate([h.reshape(n_tok, d), jnp.zeros((1, d), BF16)], axis=0)
    y = _expert_ffn(h_pad, row_tok, chunk_e, layer, w_gu, b_gu, w_dn, b_dn, row_w)
    return _combine(y[pos.T], x, g2)


def _norm_rope(x, g, cos, sin, bd, hd):
    sq = x * x
    hi = sq.astype(BF16)
    lo = (sq - hi.astype(F32)).astype(BF16)
    ss = _dot(hi, bd) + _dot(lo, bd)
    y = x * lax.rsqrt(ss * (1.0 / hd) + NORM_EPS) * g
    half = hd // 2
    first = (_iota2(x.shape, 1) & (hd - 1)) < half
    partner = jnp.where(first, pltpu.roll(y, LANES - half, 1), pltpu.roll(y, half, 1))
    return y * cos + partner * sin


def _prep_kernel(aq_ref, bq_ref, bk_ref, bv_ref, cq_ref, ck_ref, cv_ref, dq_ref, akc_ref, aks_ref, akw_ref,
                 dk_ref, dv_ref, c64_ref, s64_ref, c32_ref, s32_ref, gaq_ref, gak_ref, gcq_ref, gck_ref,
                 gdq_ref, gdk_ref, bd64_ref, bd32_ref,
                 oaq_ref, okc_ref, ovc_ref, okx_ref, ovs_ref, okw_ref, ovw_ref, obq_ref, obk_ref, obv_ref,
                 ocq_ref, ock_ref, ocv_ref, odq_ref, odk_ref, odv_ref, *, ts, sb_scale):
    hd, dd = HEAD_DIM, DIFF_DIM
    c64, s64, c32, s32 = c64_ref[0], s64_ref[0], c32_ref[0], s32_ref[0]
    bd64, bd32 = bd64_ref[...], bd32_ref[...]

    def slabs(ref):
        x = ref[0]
        return [x[:, c * LANES:(c + 1) * LANES] for c in range(x.shape[1] // LANES)]

    def put_heads(o_ref, c, y, width):
        per = LANES // width
        for u in range(per):
            o_ref[0, c * per + u] = y[:, u * width:(u + 1) * width].astype(o_ref.dtype)

    for src, gain, dst in ((aq_ref, gaq_ref, oaq_ref), (dq_ref, gdq_ref, odq_ref), (dk_ref, gdk_ref, odk_ref)):
        for c, x in enumerate(slabs(src)):
            put_heads(dst, c, _norm_rope(x, gain[:, c * LANES:(c + 1) * LANES], c64, s64, bd64, hd), hd)
    kc = _norm_rope(akc_ref[0], gak_ref[0:1, :], c64, s64, bd64, hd)
    ks = _norm_rope(aks_ref[0], gak_ref[1:2, :], c64, s64, bd64, hd)
    kw = _norm_rope(akw_ref[0], gak_ref[2:3, :], c64, s64, bd64, hd)
    okc_ref[0] = kc[:, :hd].astype(BF16)
    ovc_ref[0] = akc_ref[0][:, hd:].astype(BF16)
    nsel = okx_ref.shape[2] - hd
    blk = (pl.program_id(1) * ts + _iota2((ts, nsel), 0)) >> (NSA_SEL_BLOCK.bit_length() - 1)
    okx_ref[0, :, :nsel] = jnp.where(blk == _iota2((ts, nsel), 1), 1.0, 0.0).astype(BF16)
    okx_ref[0, :, nsel:] = ks[:, :hd].astype(BF16)
    ovs_ref[0] = aks_ref[0][:, hd:].astype(BF16)
    okw_ref[0, 0] = kw[:, :hd].astype(BF16)
    ovw_ref[0, 0] = akw_ref[0][:, hd:].astype(BF16)
    for c, x in enumerate(slabs(bq_ref)):
        put_heads(obq_ref, c, x * sb_scale, hd)
    for src, dst in ((bk_ref, obk_ref), (bv_ref, obv_ref), (cv_ref, ocv_ref)):
        for c, x in enumerate(slabs(src)):
            put_heads(dst, c, x, hd)
    put_heads(odv_ref, 0, dv_ref[0], hd)
    for src, gain, dst in ((cq_ref, gcq_ref, ocq_ref), (ck_ref, gck_ref, ock_ref)):
        for c, x in enumerate(slabs(src)):
            put_heads(dst, c, _norm_rope(x, gain[:, c * LANES:(c + 1) * LANES], c32, s32, bd32, dd), dd)


def _rope_tables(positions, hd):
    half = hd // 2
    inv = ROPE_THETA ** (-jnp.arange(half, dtype=F32) * 2.0 / hd)
    ang = positions.astype(F32)[..., None] * inv
    cos, sin = jnp.cos(ang), jnp.sin(ang)
    reps = LANES // hd
    return (jnp.tile(jnp.concatenate([cos, cos], axis=-1), (1, 1, reps)),
            jnp.tile(jnp.concatenate([-sin, sin], axis=-1), (1, 1, reps)))


def _prep(proj, cols, tables, p, n_sel, ts=512):
    bn, sn, _ = proj.shape
    hd, dd = HEAD_DIM, DIFF_DIM
    scale = hd ** -0.5
    c64, s64, c32, s32 = tables

    def cspec(name):
        off, width = cols[name]
        assert off % width == 0
        return pl.BlockSpec((1, ts, width), lambda b, i, blk=off // width: (b, i, blk))

    def tile_gain(g, reps, mult=1.0):
        return (jnp.tile(g.astype(F32), reps) * mult).reshape(1, -1)

    ones = jnp.ones((hd,), F32)
    gak = jnp.stack([jnp.concatenate([p["nsa_kn"][j].astype(F32), ones]) for j in range(3)])
    gains = [tile_gain(p["nsa_qn"], NSA_HEADS, scale), gak,
             tile_gain(p["dif_qn"], 2 * DIFF_HEADS, dd ** -0.5), tile_gain(p["dif_kn"], 2 * DIFF_HEADS),
             tile_gain(p["swa_qn"], SWA_HEADS, scale), tile_gain(p["swa_kn"], SWA_KV_HEADS)]
    lane = np.arange(LANES)
    bd64 = jnp.asarray((lane[:, None] // hd == lane[None, :] // hd).astype(np.float32), BF16)
    bd32 = jnp.asarray((lane[:, None] // dd == lane[None, :] // dd).astype(np.float32), BF16)
    names = ("a_q", "b_q", "b_k", "b_v", "c_q", "c_k", "c_v", "d_q", "a_kcvc", "a_ksvs", "a_kwvw", "d_k", "d_v")
    tab = pl.BlockSpec((1, ts, LANES), lambda b, i: (b, i, 0))
    full = lambda a: pl.BlockSpec(a.shape, lambda b, i: (0,) * a.ndim)

    def hm(nh, w):
        return (jax.ShapeDtypeStruct((bn, nh, sn, w), BF16), pl.BlockSpec((1, nh, ts, w), lambda b, i: (b, 0, i, 0)))

    def tm_(w):
        return (jax.ShapeDtypeStruct((bn, sn, w), BF16), pl.BlockSpec((1, ts, w), lambda b, i: (b, i, 0)))

    outs = [hm(NSA_HEADS, hd), tm_(hd), tm_(hd), tm_(n_sel + hd), tm_(hd), hm(1, hd), hm(1, hd),
            hm(SB_HEADS, hd), hm(SB_HEADS, hd), hm(SB_HEADS, hd),
            hm(2 * DIFF_HEADS, dd), hm(2 * DIFF_HEADS, dd), hm(DIFF_HEADS, 2 * dd),
            hm(SWA_HEADS, hd), hm(SWA_KV_HEADS, hd), hm(SWA_KV_HEADS, hd)]
    consts = gains + [bd64, bd32]
    res = pl.pallas_call(
        functools.partial(_prep_kernel, ts=ts, sb_scale=-scale * LOG2E),
        grid=(bn, sn // ts),
        in_specs=[cspec(n) for n in names] + [tab] * 4 + [full(a) for a in consts],
        out_specs=[o[1] for o in outs],
        out_shape=[o[0] for o in outs],
        compiler_params=_cparams("parallel", "parallel"),
        name="mixer_prep",
    )(*([proj] * len(names)), c64, s64, c32, s32, *consts)
    keys = ("a_q", "kc", "vc", "kx", "vs", "kw", "vw", "b_q", "b_k", "b_v", "c_q", "c_k", "c_v", "d_q", "d_k", "d_v")
    return dict(zip(keys, res))


def _layer_columns(d):
    cols = {}
    off = 0
    for name, width in (("br_g", N_BRANCH * d), ("ga", 3 * NSA_HEADS * HEAD_DIM), ("a_q", 256),
                        ("b_q", 256), ("b_k", 256), ("b_v", 256), ("c_q", 256), ("c_k", 256), ("c_v", 256),
                        ("d_q", 256), ("a_kcvc", 128), ("a_ksvs", 128), ("a_kwvw", 128), ("d_k", 128), ("d_v", 128)):
        cols[name] = (off, width)
        off += width
    return cols, off


def _reorder_w_in(w_in, d, n_pad):
    ref_splits = (256, 128, 128, 128, 12, 256, 256, 256, 256, 256, 256, 256, 128, 128, N_BRANCH * d)
    names = ("a_q", "a_kcvc", "a_ksvs", "a_kwvw", "a_g", "b_q", "b_k", "b_v",
             "c_q", "c_k", "c_v", "d_q", "d_k", "d_v", "br_g")
    starts = np.cumsum((0,) + ref_splits)
    src = {n: (int(starts[i]), ref_splits[i]) for i, n in enumerate(names)}
    cols, total = _layer_columns(d)
    pieces = []
    for name, (off, width) in cols.items():
        if name == "ga":
            g0 = src["a_g"][0]
            gates = w_in[:, g0:g0 + 3 * NSA_HEADS].reshape(-1, NSA_HEADS, 3)
            gates = jnp.broadcast_to(gates.transpose(0, 2, 1)[..., None], (w_in.shape[0], 3, NSA_HEADS, HEAD_DIM))
            pieces.append(gates.reshape(w_in.shape[0], width))
        else:
            pieces.append(w_in[:, src[name][0]:src[name][0] + width])
    pieces.append(jnp.zeros((w_in.shape[0], n_pad - total), w_in.dtype))
    return jnp.concatenate(pieces, axis=1).astype(BF16)


def _mixer_layer(x, positions, tables, mod, p, lam_init):
    bn, sn, d = x.shape
    sh1, sc1, g1 = mod[0], mod[1], mod[2]
    cols, total = _layer_columns(d)
    n_pad = -(-total // 1536) * 1536
    w_in = _reorder_w_in(p["w_in"], d, n_pad)
    proj = _in_projection(x, p["norm1"], sc1, sh1, w_in)
    n_cmp = (sn - NSA_CMP_LEN) // NSA_CMP_STRIDE + 1
    n_sel = sn // NSA_SEL_BLOCK
    n_top = min(NSA_N_SEL, n_sel)
    t = _prep(proj, cols, tables, p, n_sel)
    k_cmp = _compress(t["kc"], p["nsa_pe_k"], p["nsa_w1_k"], p["nsa_w2_k"]).astype(BF16)
    v_cmp = _compress(t["vc"], p["nsa_pe_v"], p["nsa_w1_v"], p["nsa_w2_v"]).astype(BF16)
    cmp_start = np.arange(sn // NSA_CMP_STRIDE) * NSA_CMP_STRIDE
    sel_start = np.arange(n_sel) * NSA_SEL_BLOCK
    overlap = ((cmp_start[:, None] <= (sel_start + NSA_SEL_BLOCK - 1)[None, :]) &
               ((cmp_start + NSA_CMP_LEN - 1)[:, None] >= sel_start[None, :]) &
               (np.arange(sn // NSA_CMP_STRIDE) < n_cmp)[:, None]).astype(np.float32)
    o_c, bias = _nsa_cmp(t["a_q"], k_cmp, v_cmp, jnp.asarray(overlap, BF16), n_cmp, n_top)
    o_s = _nsa_sel(t["a_q"], t["kx"], t["vs"], bias)
    o_w = _banded_attention(t["a_q"], t["kw"], t["vw"], NSA_WINDOW)
    o_b = _stick_breaking(t["b_q"], t["b_k"], t["b_v"])
    lam = (jnp.exp(jnp.sum(p["dif_lq1"] * p["dif_lk1"])) - jnp.exp(jnp.sum(p["dif_lq2"] * p["dif_lk2"]))
           + lam_init)
    o_cd = _diff_attention(t["c_q"], t["c_k"], t["c_v"], p["dif_subln"], lam, jnp.asarray(1.0 - lam_init, F32))
    o_d = _banded_attention(t["d_q"], t["d_k"], t["d_v"], SWA_WINDOW, sinks=p["swa_sinks"])
    return _merge(proj, o_c, o_s, o_w, o_b, o_cd, o_d, p["w_up"].astype(BF16), p["w_out"].astype(BF16),
                  x, g1, cols["ga"][0])


def kernel(x, c, positions, w_ada, b_ada, norm1, norm2, w_in, nsa_qn, nsa_kn, nsa_pe_k, nsa_w1_k, nsa_w2_k,
           nsa_pe_v, nsa_w1_v, nsa_w2_v, dif_qn, dif_kn, dif_lq1, dif_lk1, dif_lq2, dif_lk2, dif_subln,
           swa_qn, swa_kn, swa_sinks, w_up, w_out, w_router, b_router, w_gu, b_gu, w_dn, b_dn):
    bn, sn, d = x.shape
    depth = w_ada.shape[0]
    c_pad = jnp.zeros((8, d), F32).at[:bn].set(c)
    tables = _rope_tables(positions, HEAD_DIM) + _rope_tables(positions, DIFF_DIM)
    for l in range(depth):
        lam_init = 0.8 - 0.6 * math.exp(-0.3 * l)
        mod = _linear(c_pad, w_ada[l], b_ada[l], tn=512, precision=lax.Precision.HIGHEST)[:bn]
        mod = mod.reshape(bn, 6, 1, d).transpose(1, 0, 2, 3)
        p = dict(norm1=norm1[l], w_in=w_in[l], nsa_qn=nsa_qn[l], nsa_kn=nsa_kn[l], nsa_pe_k=nsa_pe_k[l],
                 nsa_w1_k=nsa_w1_k[l], nsa_w2_k=nsa_w2_k[l], nsa_pe_v=nsa_pe_v[l], nsa_w1_v=nsa_w1_v[l],
                 nsa_w2_v=nsa_w2_v[l], dif_qn=dif_qn[l], dif_kn=dif_kn[l], dif_lq1=dif_lq1[l],
                 dif_lk1=dif_lk1[l], dif_lq2=dif_lq2[l], dif_lk2=dif_lk2[l], dif_subln=dif_subln[l],
                 swa_qn=swa_qn[l], swa_kn=swa_kn[l], swa_sinks=swa_sinks[l], w_up=w_up[l], w_out=w_out[l])
        x = _mixer_layer(x, positions, tables, mod, p, lam_init)
        x = _moe(x, norm2[l], mod[4], mod[3], mod[5], w_router[l], b_router[l], l, w_gu, b_gu, w_dn, b_dn)
    return x
```

```python
import functools
import math

import numpy as np
import jax
import jax.numpy as jnp
from jax import lax
from jax.experimental import pallas as pl
from jax.experimental.pallas import tpu as pltpu

F32 = jnp.float32
BF16 = jnp.bfloat16

HEAD_DIM = 64
ROPE_THETA = 10000.0
NORM_EPS = 1e-6
NEG_INF = -1e30
KNOCKOUT = -3e38
N_BRANCH = 4

NSA_HEADS = 4
NSA_CMP_LEN = 32
NSA_CMP_STRIDE = 16
NSA_SEL_BLOCK = 64
NSA_N_SEL = 16
NSA_WINDOW = 512
NSA_FORCED_SCORE = 1e4

SB_HEADS = 4
DIFF_HEADS = 4
DIFF_DIM = 32
SWA_HEADS = 4
SWA_KV_HEADS = 2
SWA_WINDOW = 128

N_EXPERTS = 32
TOP_K = 4
SWIGLU_ALPHA = 1.702
SWIGLU_LIMIT = 7.0

LANES = 128
LOG2E = 1.4426950408889634
MOE_ROWS = 512
VMEM_LIMIT = 56 * 1024 * 1024


def _cparams(*sem):
    return pltpu.CompilerParams(dimension_semantics=sem, vmem_limit_bytes=VMEM_LIMIT)


def _dot(a, b):
    return jnp.dot(a, b, preferred_element_type=F32)


def _dot_nt(a, b):
    return lax.dot_general(a, b, (((1,), (1,)), ((), ())), preferred_element_type=F32)


def _iota2(shape, dim):
    return lax.broadcasted_iota(jnp.int32, shape, dim)


def _linear_kernel(x_ref, w_ref, b_ref, o_ref, *, precision):
    o_ref[...] = jnp.dot(x_ref[...], w_ref[...], preferred_element_type=F32,
                         precision=precision) + b_ref[...]


def _linear(x, w, b, tn, precision=None):
    m, k = x.shape
    n = w.shape[1]
    return pl.pallas_call(
        functools.partial(_linear_kernel, precision=precision),
        grid=(n // tn,),
        in_specs=[pl.BlockSpec((m, k), lambda j: (0, 0)),
                  pl.BlockSpec((k, tn), lambda j: (0, j)),
                  pl.BlockSpec((1, tn), lambda j: (0, j))],
        out_specs=pl.BlockSpec((m, tn), lambda j: (0, j)),
        out_shape=jax.ShapeDtypeStruct((m, n), F32),
        compiler_params=_cparams("arbitrary"),
        name="linear",
    )(x, w, b.reshape(1, n))


def _cmp_mlp_kernel(a_ref, b_ref, pe_ref, w2_ref, o_ref):
    hid = jax.nn.gelu(a_ref[...] + b_ref[...] + pe_ref[...])
    o_ref[...] = _dot(hid.astype(BF16), w2_ref[...])


def _compress(t, pe, w1, w2):
    bn, sn, dh = t.shape
    st = NSA_CMP_STRIDE
    half = st * dh
    nb = sn // st
    t16 = t.reshape(bn * nb, half).astype(BF16)
    w1cat = jnp.concatenate([w1[:half], w1[half:]], axis=1).astype(BF16)
    hidden = w1.shape[1]
    ab = _linear(t16, w1cat, jnp.zeros((2 * hidden,), F32), tn=2 * hidden)
    ab = ab.reshape(bn, nb, 2 * hidden)
    a = ab[:, :, :hidden]
    b_next = jnp.concatenate([ab[:, 1:, hidden:], jnp.zeros((bn, 1, hidden), F32)], axis=1)
    pe_term = jnp.dot(pe.reshape(1, NSA_CMP_LEN * dh), w1, precision=lax.Precision.HIGHEST)
    rows = bn * nb
    tm = min(512, rows)
    out = pl.pallas_call(
        _cmp_mlp_kernel,
        grid=(rows // tm,),
        in_specs=[pl.BlockSpec((tm, hidden), lambda i: (i, 0)),
                  pl.BlockSpec((tm, hidden), lambda i: (i, 0)),
                  pl.BlockSpec((1, hidden), lambda i: (0, 0)),
                  pl.BlockSpec((hidden, dh), lambda i: (0, 0))],
        out_specs=pl.BlockSpec((tm, dh), lambda i: (i, 0)),
        out_shape=jax.ShapeDtypeStruct((rows, dh), F32),
        compiler_params=_cparams("parallel"),
        name="cmp_mlp",
    )(a.reshape(rows, hidden), b_next.reshape(rows, hidden), pe_term, w2.astype(BF16))
    return out.reshape(bn, nb, dh)


def _norm_mod(x, g, sc, sh):
    r = lax.rsqrt(jnp.mean(x * x, axis=-1, keepdims=True) + NORM_EPS)
    return (x * r * g) * (1.0 + sc) + sh


def _proj_kernel(x_ref, g_ref, sc_ref, sh_ref, w_ref, o_ref, h_scr):
    @pl.when(pl.program_id(2) == 0)
    def _():
        h_scr[...] = _norm_mod(x_ref[0], g_ref[...], sc_ref[0], sh_ref[0]).astype(BF16)

    o_ref[0] = _dot(h_scr[...], w_ref[...])


def _in_projection(x, g, sc, sh, w, tm=1024, tn=1536):
    bn, sn, d = x.shape
    n = w.shape[1]
    tm = min(tm, sn)
    return pl.pallas_call(
        _proj_kernel,
        grid=(bn, sn // tm, n // tn),
        in_specs=[pl.BlockSpec((1, tm, d), lambda b, i, j: (b, i, 0)),
                  pl.BlockSpec((1, d), lambda b, i, j: (0, 0)),
                  pl.BlockSpec((1, 1, d), lambda b, i, j: (b, 0, 0)),
                  pl.BlockSpec((1, 1, d), lambda b, i, j: (b, 0, 0)),
                  pl.BlockSpec((d, tn), lambda b, i, j: (0, j))],
        out_specs=pl.BlockSpec((1, tm, tn), lambda b, i, j: (b, i, j)),
        out_shape=jax.ShapeDtypeStruct((bn, sn, n), F32),
        scratch_shapes=[pltpu.VMEM((tm, d), BF16)],
        compiler_params=_cparams("parallel", "parallel", "arbitrary"),
        name="in_proj",
    )(x, g.reshape(1, d), sc, sh, w)


def _banded_kernel(*refs, tile, window, has_sink):
    if has_sink:
        sink_ref, q_ref, kp_ref, kc_ref, vp_ref, vc_ref, o_ref = refs
    else:
        q_ref, kp_ref, kc_ref, vp_ref, vc_ref, o_ref = refs
    i = pl.program_id(2)
    w = window
    upper = _iota2((w, w), 1) > _iota2((w, w), 0)
    first_bias = jnp.where(i > 0, 0.0, NEG_INF)
    for u in range(tile // w):
        q = q_ref[0, 0, u * w:(u + 1) * w, :]
        if u == 0:
            k_prev, v_prev = kp_ref[0, 0], vp_ref[0, 0]
        else:
            k_prev, v_prev = kc_ref[0, 0, (u - 1) * w:u * w, :], vc_ref[0, 0, (u - 1) * w:u * w, :]
        k_cur, v_cur = kc_ref[0, 0, u * w:(u + 1) * w, :], vc_ref[0, 0, u * w:(u + 1) * w, :]
        s_prev = _dot_nt(q, k_prev)
        if u == 0:
            s_prev = s_prev + first_bias
        s = jnp.where(upper, s_prev, _dot_nt(q, k_cur))
        m = jnp.max(s, axis=-1, keepdims=True)
        if has_sink:
            sink = sink_ref[pl.program_id(1)]
            m = jnp.maximum(m, sink)
        p = jnp.exp(s - m)
        den = jnp.sum(p, axis=-1, keepdims=True)
        if has_sink:
            den = den + jnp.exp(sink - m)
        else:
            den = jnp.maximum(den, 1e-30)
        o = (_dot(jnp.where(upper, p, 0.0).astype(BF16), v_prev)
             + _dot(jnp.where(upper, 0.0, p).astype(BF16), v_cur))
        o_ref[0, 0, u * w:(u + 1) * w, :] = o / den


def _banded_attention(q, k, v, window, sinks=None, tile=1024):
    bn, hq, sn, d = q.shape
    grp = hq // k.shape[1]
    tile = min(tile, sn)
    assert tile % window == 0 and sn % tile == 0
    per = tile // window
    has_sink = sinks is not None
    qspec = pl.BlockSpec((1, 1, tile, d), lambda b, h, i: (b, h, i, 0))
    prev = pl.BlockSpec((1, 1, window, d), lambda b, h, i: (b, h // grp, jnp.maximum(i * per - 1, 0), 0))
    cur = pl.BlockSpec((1, 1, tile, d), lambda b, h, i: (b, h // grp, i, 0))
    in_specs = [qspec, prev, cur, prev, cur]
    args = [q, k, k, v, v]
    if has_sink:
        in_specs = [pl.BlockSpec(memory_space=pltpu.SMEM)] + in_specs
        args = [sinks.astype(F32)] + args
    return pl.pallas_call(
        functools.partial(_banded_kernel, tile=tile, window=window, has_sink=has_sink),
        grid=(bn, hq, sn // tile),
        in_specs=in_specs,
        out_specs=pl.BlockSpec((1, 1, tile, d), lambda b, h, i: (b, h, i, 0)),
        out_shape=jax.ShapeDtypeStruct((bn, hq, sn, d), F32),
        compiler_params=_cparams("parallel", "parallel", "parallel"),
        name="banded_attn",
    )(*args)


def _nsa_cmp_kernel(q_ref, kc_ref, vct_ref, ovt_ref, oct_ref, bias_ref, *, tq, n_cmp, n_top, heads):
    i = pl.program_id(1)
    ncp = kc_ref.shape[1]
    nsel = ovt_ref.shape[0]
    dh = q_ref.shape[3]
    t = i * tq + _iota2((ncp, tq), 1)
    n = _iota2((ncp, tq), 0)
    valid = (n * NSA_CMP_STRIDE + (NSA_CMP_LEN - 1) <= t) & (n < n_cmp)
    kc = kc_ref[0]
    vct = vct_ref[0]
    psum = jnp.zeros((ncp, tq), F32)
    for h in range(heads):
        s = jnp.where(valid, _dot_nt(kc, q_ref[0, h]), NEG_INF)
        m = jnp.max(s, axis=0, keepdims=True)
        p = jnp.where(valid, jnp.exp(s - m), 0.0)
        p = p / jnp.maximum(jnp.sum(p, axis=0, keepdims=True), 1e-30)
        oct_ref[0, h * dh:(h + 1) * dh, :] = _dot(vct, p.astype(BF16))
        psum = psum + p
    hi = psum.astype(BF16)
    lo = (psum - hi.astype(F32)).astype(BF16)
    imp = _dot(ovt_ref[...], hi) + _dot(ovt_ref[...], lo)

    tt = i * tq + _iota2((nsel, tq), 1)
    blk = _iota2((nsel, tq), 0)
    cur = tt >> (NSA_SEL_BLOCK.bit_length() - 1)
    forced = (blk == 0) | (blk == cur) | (blk == cur - 1)
    valid_s = blk * NSA_SEL_BLOCK <= tt
    score = jnp.where(forced, NSA_FORCED_SCORE, jnp.where(valid_s, imp, -1.0))
    blk_f = blk.astype(F32)

    def pick(_, carry):
        score, sel = carry
        m = jnp.max(score, axis=0, keepdims=True)
        first = jnp.min(jnp.where(score == m, blk_f, float(nsel)), axis=0, keepdims=True)
        hit = blk_f == first
        return jnp.where(hit, KNOCKOUT, score), jnp.where(hit, 0.0, sel)

    _, bias = lax.fori_loop(0, n_top, pick, (score, jnp.full((nsel, tq), NEG_INF, F32)))
    bias_ref[0] = bias.astype(BF16)


def _nsa_cmp(q, kc, vc, overlap, n_cmp, n_top, tq=256):
    bn, heads, sn, dh = q.shape
    ncp = kc.shape[1]
    nsel = overlap.shape[1]
    oct, bias_t = pl.pallas_call(
        functools.partial(_nsa_cmp_kernel, tq=tq, n_cmp=n_cmp, n_top=n_top, heads=heads),
        grid=(bn, sn // tq),
        in_specs=[pl.BlockSpec((1, heads, tq, dh), lambda b, i: (b, 0, i, 0)),
                  pl.BlockSpec((1, ncp, dh), lambda b, i: (b, 0, 0)),
                  pl.BlockSpec((1, dh, ncp), lambda b, i: (b, 0, 0)),
                  pl.BlockSpec((nsel, ncp), lambda b, i: (0, 0))],
        out_specs=[pl.BlockSpec((1, heads * dh, tq), lambda b, i: (b, 0, i)),
                   pl.BlockSpec((1, nsel, tq), lambda b, i: (b, 0, i))],
        out_shape=[jax.ShapeDtypeStruct((bn, heads * dh, sn), F32),
                   jax.ShapeDtypeStruct((bn, nsel, sn), BF16)],
        compiler_params=_cparams("parallel", "parallel"),
        name="nsa_cmp_topk",
    )(q, kc, vc.transpose(0, 2, 1), overlap.T)
    return oct.transpose(0, 2, 1), bias_t.transpose(0, 2, 1)


def _nsa_sel_kernel(q_ref, kx_ref, v_ref, bias_ref, o_ref, qx_scr, *, tq, tk, heads):
    i = pl.program_id(1)
    dh = v_ref.shape[2]
    bias = bias_ref[0]
    for h in range(heads):
        qx_scr[h] = jnp.concatenate([bias, q_ref[0, h]], axis=1)

    def tile(j, state, diagonal):
        start = pl.multiple_of(j * tk, tk)
        kx = kx_ref[0, pl.ds(start, tk), :]
        v = v_ref[0, pl.ds(start, tk), :]
        if diagonal:
            causal = (start + _iota2((tq, tk), 1)) <= (i * tq + _iota2((tq, tk), 0))
        new_state = []
        for h in range(heads):
            m_old, l_old, acc_old = state[h]
            s = _dot_nt(qx_scr[h], kx)
            if diagonal:
                s = jnp.where(causal, s, NEG_INF)
            m_new = jnp.maximum(m_old, jnp.max(s, axis=-1, keepdims=True))
            alpha = jnp.exp(m_old - m_new)
            p = jnp.exp(s - m_new)
            l_new = alpha * l_old + jnp.sum(p, axis=-1, keepdims=True)
            acc_new = alpha * acc_old + _dot(p.astype(BF16), v)
            new_state.append((m_new, l_new, acc_new))
        return tuple(new_state)

    init = tuple((jnp.full((tq, 1), NEG_INF, F32), jnp.zeros((tq, 1), F32), jnp.zeros((tq, dh), F32))
                 for _ in range(heads))
    n_full = (i * tq) // tk
    state = lax.fori_loop(0, n_full, lambda j, st: tile(j, st, False), init)
    state = tile(n_full, state, True)
    for h in range(heads):
        o_ref[0, :, h * dh:(h + 1) * dh] = state[h][2] / jnp.maximum(state[h][1], 1e-30)


def _nsa_sel(q, kx, v, bias, tq=1024, tk=1024):
    bn, heads, sn, dh = q.shape
    nsel = bias.shape[2]
    tq, tk = min(tq, sn), min(tk, sn)
    assert tk % tq == 0 and sn % tk == 0
    return pl.pallas_call(
        functools.partial(_nsa_sel_kernel, tq=tq, tk=tk, heads=heads),
        grid=(bn, sn // tq),
        in_specs=[pl.BlockSpec((1, heads, tq, dh), lambda b, i: (b, 0, i, 0)),
                  pl.BlockSpec((1, sn, nsel + dh), lambda b, i: (b, 0, 0)),
                  pl.BlockSpec((1, sn, dh), lambda b, i: (b, 0, 0)),
                  pl.BlockSpec((1, tq, nsel), lambda b, i: (b, i, 0))],
        out_specs=pl.BlockSpec((1, tq, heads * dh), lambda b, i: (b, i, 0)),
        out_shape=jax.ShapeDtypeStruct((bn, sn, heads * dh), F32),
        scratch_shapes=[pltpu.VMEM((heads, tq, nsel + dh), BF16)],
        compiler_params=_cparams("parallel", "arbitrary"),
        name="nsa_selected_attn",
    )(q, kx, v, bias)


def _sb_kernel(q_ref, k_ref, v_ref, u_ref, o_ref, *, tq, tk):
    i = pl.program_id(2)
    q = q_ref[0, 0]
    per_q = tq // tk

    def tile(jj, carry, first_row):
        diagonal = first_row is not None
        r0 = first_row if diagonal else 0
        start = pl.multiple_of(jj * tk, tk)
        nz = _dot_nt(q[r0:], k_ref[0, 0, pl.ds(start, tk), :])
        neg_abs = lax.bitcast_convert_type(lax.bitcast_convert_type(nz, jnp.uint32) | jnp.uint32(0x80000000), F32)
        log_keep = jnp.minimum(nz, 0.0) - jnp.log2(1.0 + jnp.exp2(neg_abs))
        if diagonal:
            strict = _iota2(nz.shape, 1) < _iota2(nz.shape, 0)
            log_keep = jnp.where(strict, log_keep, 0.0)
        cum = _dot(log_keep.astype(BF16), u_ref[...])
        a = jnp.exp2(cum + jnp.concatenate([carry[r0:]] * (tk // LANES), axis=1) - nz)
        if diagonal:
            a = jnp.where(strict, a, 0.0)
        out = _dot(a.astype(BF16), v_ref[0, 0, pl.ds(start, tk), :])
        new_carry = carry[r0:] + jnp.broadcast_to(cum[:, 0:1], (tq - r0, LANES))
        if r0:
            out = jnp.concatenate([jnp.zeros((r0, out.shape[1]), F32), out], axis=0)
            new_carry = jnp.concatenate([carry[:r0], new_carry], axis=0)
        return out, new_carry

    def group(first, carry, diagonal):
        total = None
        for r in range(per_q):
            out, carry = tile(first - r, carry, (per_q - 1 - r) * tk if diagonal else None)
            total = out if total is None else total + out
        return total, carry

    acc, carry = group(i * per_q + per_q - 1, jnp.zeros((tq, LANES), F32), True)

    def body(p, state):
        acc, carry = state
        out, carry = group((i - p) * per_q - 1, carry, False)
        return acc + out, carry

    acc, _ = lax.fori_loop(0, i, body, (acc, carry))
    o_ref[0, 0] = acc


def _stick_breaking(q, k, v, tq=1024, tk=256):
    bn, heads, sn, dh = q.shape
    tq = min(tq, sn)
    incl = (np.arange(tk)[:, None] >= np.arange(tk)[None, :]).astype(np.float32)
    u = jnp.asarray(incl, BF16)
    kv = pl.BlockSpec((1, 1, sn, dh), lambda b, h, i: (b, h, 0, 0))
    return pl.pallas_call(
        functools.partial(_sb_kernel, tq=tq, tk=tk),
        grid=(bn, heads, sn // tq),
        in_specs=[pl.BlockSpec((1, 1, tq, dh), lambda b, h, i: (b, h, i, 0)), kv, kv,
                  pl.BlockSpec((tk, tk), lambda b, h, i: (0, 0))],
        out_specs=pl.BlockSpec((1, 1, tq, dh), lambda b, h, i: (b, h, i, 0)),
        out_shape=jax.ShapeDtypeStruct((bn, heads, sn, dh), F32),
        compiler_params=_cparams("parallel", "parallel", "arbitrary"),
        name="stick_breaking_attn",
    )(q, k, v, u)


def _diff_kernel(sc_ref, q_ref, k_ref, v_ref, g_ref, o_ref, *, tq, tk):
    i = pl.program_id(2)
    dv = v_ref.shape[3]

    def tile(j, state, diagonal):
        start = pl.multiple_of(j * tk, tk)
        v = v_ref[0, 0, pl.ds(start, tk), :]
        if diagonal:
            causal = (start + _iota2((tq, tk), 1)) <= (i * tq + _iota2((tq, tk), 0))
        new_state = []
        for mi in range(2):
            m_old, l_old, acc_old = state[mi]
            s = _dot_nt(q_ref[0, mi], k_ref[0, mi, pl.ds(start, tk), :])
            if diagonal:
                s = jnp.where(causal, s, NEG_INF)
            m_new = jnp.maximum(m_old, jnp.max(s, axis=-1, keepdims=True))
            alpha = jnp.exp(m_old - m_new)
            p = jnp.exp(s - m_new)
            l_new = alpha * l_old + jnp.sum(p, axis=-1, keepdims=True)
            acc_new = alpha * acc_old + _dot(p.astype(BF16), v)
            new_state.append((m_new, l_new, acc_new))
        return tuple(new_state)

    init = tuple((jnp.full((tq, 1), NEG_INF, F32), jnp.zeros((tq, 1), F32), jnp.zeros((tq, dv), F32))
                 for _ in range(2))
    n_full = (i * tq) // tk
    state = lax.fori_loop(0, n_full, lambda j, st: tile(j, st, False), init)
    state = tile(n_full, state, True)
    lam = sc_ref[0]
    post = sc_ref[1]
    o = (state[0][2] / jnp.maximum(state[0][1], 1e-30)
         - lam * (state[1][2] / jnp.maximum(state[1][1], 1e-30)))
    r = lax.rsqrt(jnp.mean(o * o, axis=-1, keepdims=True) + NORM_EPS)
    o_ref[0, 0] = (o * r * g_ref[...]) * post


def _diff_attention(q, k, v, subln, lam, post, tq=1024, tk=1024):
    bn, h2, sn, dd = q.shape
    heads = h2 // 2
    dv = v.shape[3]
    tq, tk = min(tq, sn), min(tk, sn)
    assert tk % tq == 0 and sn % tk == 0
    scal = jnp.stack([lam, post]).astype(F32)
    return pl.pallas_call(
        functools.partial(_diff_kernel, tq=tq, tk=tk),
        grid=(bn, heads, sn // tq),
        in_specs=[pl.BlockSpec(memory_space=pltpu.SMEM),
                  pl.BlockSpec((1, 2, tq, dd), lambda b, h, i: (b, h, i, 0)),
                  pl.BlockSpec((1, 2, sn, dd), lambda b, h, i: (b, h, 0, 0)),
                  pl.BlockSpec((1, 1, sn, dv), lambda b, h, i: (b, h, 0, 0)),
                  pl.BlockSpec((1, dv), lambda b, h, i: (0, 0))],
        out_specs=pl.BlockSpec((1, 1, tq, dv), lambda b, h, i: (b, h, i, 0)),
        out_shape=jax.ShapeDtypeStruct((bn, heads, sn, dv), F32),
        compiler_params=_cparams("parallel", "parallel", "arbitrary"),
        name="diff_attn",
    )(scal, q, k, v, subln.reshape(1, dv).astype(F32))


def _merge_kernel(brg_ref, ga0_ref, ga1_ref, ga2_ref, oc_ref, os_ref, ow_ref, ob_ref, ocd_ref, od_ref,
                  wup_ref, wout_ref, x_ref, g1_ref, o_ref, *, d):
    def heads(ref):
        return jnp.concatenate([ref[0, h] for h in range(ref.shape[1])], axis=1)

    o_a = (jax.nn.sigmoid(ga0_ref[0]) * oc_ref[0] + jax.nn.sigmoid(ga1_ref[0]) * os_ref[0]
           + jax.nn.sigmoid(ga2_ref[0]) * heads(ow_ref))
    branches = (o_a, heads(ob_ref), heads(ocd_ref), heads(od_ref))
    merged = None
    for bi, o in enumerate(branches):
        gate = jax.nn.sigmoid(brg_ref[0, :, bi * d:(bi + 1) * d])
        term = gate * _dot(o.astype(BF16), wup_ref[bi])
        merged = term if merged is None else merged + term
    y = _dot(merged.astype(BF16), wout_ref[...])
    o_ref[0] = x_ref[0] + g1_ref[0] * y


def _merge(proj, o_c, o_s, o_w, o_b, o_cd, o_d, w_up, w_out, x, g1, ga_col, tm=256):
    bn, sn, d = x.shape
    bw = o_c.shape[2]
    nh, dh = o_b.shape[1], o_b.shape[3]
    assert ga_col % bw == 0
    gblk = ga_col // bw
    row = lambda b, i: (b, i, 0)
    bspec = pl.BlockSpec((1, tm, bw), row)
    hspec = pl.BlockSpec((1, nh, tm, dh), lambda b, i: (b, 0, i, 0))
    return pl.pallas_call(
        functools.partial(_merge_kernel, d=d),
        grid=(bn, sn // tm),
        in_specs=[pl.BlockSpec((1, tm, N_BRANCH * d), row),
                  pl.BlockSpec((1, tm, bw), lambda b, i: (b, i, gblk)),
                  pl.BlockSpec((1, tm, bw), lambda b, i: (b, i, gblk + 1)),
                  pl.BlockSpec((1, tm, bw), lambda b, i: (b, i, gblk + 2)),
                  bspec, bspec, hspec, hspec, hspec, hspec,
                  pl.BlockSpec((N_BRANCH, bw, d), lambda b, i: (0, 0, 0)),
                  pl.BlockSpec((d, d), lambda b, i: (0, 0)),
                  pl.BlockSpec((1, tm, d), row),
                  pl.BlockSpec((1, 1, d), lambda b, i: (b, 0, 0))],
        out_specs=pl.BlockSpec((1, tm, d), row),
        out_shape=jax.ShapeDtypeStruct((bn, sn, d), F32),
        compiler_params=_cparams("parallel", "parallel"),
        name="branch_merge",
    )(proj, proj, proj, proj, o_c, o_s, o_w, o_b, o_cd, o_d, w_up, w_out, x, g1)


def _router_kernel(x_ref, g_ref, sc_ref, sh_ref, wr_ref, br_ref, tri_ref, h_ref, e_ref, w_ref, rank_ref, cnt_ref,
                   run_scr):
    @pl.when((pl.program_id(0) == 0) & (pl.program_id(1) == 0))
    def _():
        run_scr[...] = jnp.zeros(run_scr.shape, F32)

    h = _norm_mod(x_ref[0], g_ref[...], sc_ref[0], sh_ref[0])
    h_ref[0] = h.astype(BF16)
    logits = jnp.dot(h, wr_ref[...], preferred_element_type=F32,
                     precision=lax.Precision.HIGHEST) + br_ref[...]
    lane = _iota2(logits.shape, 1)
    lane_f = lane.astype(F32)
    cur = logits
    vals, idxs = [], []
    chosen = jnp.zeros(logits.shape, F32)
    for _ in range(TOP_K):
        m = jnp.max(cur, axis=-1, keepdims=True)
        first = jnp.min(jnp.where(cur == m, lane_f, float(LANES)), axis=-1, keepdims=True)
        vals.append(m)
        idxs.append(first)
        hit = lane_f == first
        cur = jnp.where(hit, KNOCKOUT, cur)
        chosen = jnp.where(hit, 1.0, chosen)
    exps = [jnp.exp(v - vals[0]) for v in vals]
    den = exps[0]
    for e in exps[1:]:
        den = den + e
    earlier = _dot(tri_ref[...], chosen.astype(BF16)) + run_scr[0:1, :]
    e_out = jnp.zeros(logits.shape, F32)
    w_out = jnp.zeros(logits.shape, F32)
    r_out = jnp.zeros(logits.shape, F32)
    for k in range(TOP_K):
        rank_k = jnp.sum(jnp.where(lane_f == idxs[k], earlier, 0.0), axis=-1, keepdims=True)
        e_out = jnp.where(lane == k, idxs[k], e_out)
        w_out = jnp.where(lane == k, exps[k] / den, w_out)
        r_out = jnp.where(lane == k, rank_k, r_out)
    e_ref[0] = e_out[:, :TOP_K].astype(jnp.int32)
    w_ref[0] = w_out[:, :TOP_K]
    rank_ref[0] = r_out[:, :TOP_K].astype(jnp.int32)
    total = run_scr[...] + jnp.sum(chosen, axis=0, keepdims=True)
    run_scr[...] = total
    cnt_ref[...] = total


def _router(x, g, sc, sh, w_router, b_router, tm=512):
    bn, sn, d = x.shape
    ne = w_router.shape[1]
    wr = jnp.zeros((d, LANES), F32).at[:, :ne].set(w_router)
    br = jnp.full((1, LANES), NEG_INF, F32).at[0, :ne].set(b_router)
    tri = jnp.asarray((np.arange(tm)[:, None] > np.arange(tm)[None, :]).astype(np.float32), BF16)
    row = lambda b, i: (b, i, 0)
    return pl.pallas_call(
        _router_kernel,
        grid=(bn, sn // tm),
        in_specs=[pl.BlockSpec((1, tm, d), row),
                  pl.BlockSpec((1, d), lambda b, i: (0, 0)),
                  pl.BlockSpec((1, 1, d), lambda b, i: (b, 0, 0)),
                  pl.BlockSpec((1, 1, d), lambda b, i: (b, 0, 0)),
                  pl.BlockSpec((d, LANES), lambda b, i: (0, 0)),
                  pl.BlockSpec((1, LANES), lambda b, i: (0, 0)),
                  pl.BlockSpec((tm, tm), lambda b, i: (0, 0))],
        out_specs=[pl.BlockSpec((1, tm, d), row),
                   pl.BlockSpec((1, tm, TOP_K), row),
                   pl.BlockSpec((1, tm, TOP_K), row),
                   pl.BlockSpec((1, tm, TOP_K), row),
                   pl.BlockSpec((8, LANES), lambda b, i: (0, 0))],
        out_shape=[jax.ShapeDtypeStruct((bn, sn, d), BF16),
                   jax.ShapeDtypeStruct((bn, sn, TOP_K), jnp.int32),
                   jax.ShapeDtypeStruct((bn, sn, TOP_K), F32),
                   jax.ShapeDtypeStruct((bn, sn, TOP_K), jnp.int32),
                   jax.ShapeDtypeStruct((8, LANES), F32)],
        scratch_shapes=[pltpu.VMEM((8, LANES), F32)],
        compiler_params=_cparams("arbitrary", "arbitrary"),
        name="moe_router",
    )(x, g.reshape(1, d), sc, sh, wr, br, tri)


def _expert_kernel(ce_ref, x_ref, wgu_ref, bgu_ref, wdn_ref, bdn_ref, rw_ref, *rest, ff, fc, off):
    o_ref, wgu_scr, wdn_scr = rest[-3:]
    c = pl.program_id(0)

    @pl.when((c == 0) | (ce_ref[c + off] != ce_ref[jnp.maximum(c + off - 1, 0)]))
    def _():
        wgu_scr[...] = wgu_ref[0, 0].astype(BF16)
        wdn_scr[...] = wdn_ref[0, 0].astype(BF16)

    x = x_ref[...]
    y = None
    for j in range(ff // fc):
        g = _dot(x, wgu_scr[:, j * fc:(j + 1) * fc]) + bgu_ref[0, 0, :, j * fc:(j + 1) * fc]
        u = _dot(x, wgu_scr[:, ff + j * fc:ff + (j + 1) * fc]) + bgu_ref[0, 0, :, ff + j * fc:ff + (j + 1) * fc]
        g = jnp.minimum(g, SWIGLU_LIMIT)
        u = jnp.clip(u, -SWIGLU_LIMIT, SWIGLU_LIMIT)
        act = g * jax.nn.sigmoid(SWIGLU_ALPHA * g) * (u + 1.0)
        part = _dot(act.astype(BF16), wdn_scr[j * fc:(j + 1) * fc, :])
        y = part if y is None else y + part
    o_ref[...] = ((y + bdn_ref[0, 0]) * rw_ref[...]).astype(o_ref.dtype)


def _expert_ffn(h_pad, row_tok, chunk_e, layer, w_gu, b_gu, w_dn, b_dn, row_w, tm=MOE_ROWS, fc=512, groups=4):
    n_rows = row_tok.shape[0]
    d = h_pad.shape[1]
    nl, ne, _, ff2 = w_gu.shape
    ff = ff2 // 2
    n_chunks = n_rows // tm
    assert n_chunks % groups == 0
    per = n_chunks // groups
    b_gu3, b_dn3, row_w2 = b_gu.reshape(nl, ne, 1, ff2), b_dn.reshape(nl, ne, 1, d), row_w.reshape(n_rows, 1)
    y = None
    for gi in range(groups):
        off = gi * per
        rows = h_pad[row_tok[off * tm:(off + per) * tm]]
        in_specs = [pl.BlockSpec((tm, d), lambda c, ce: (c, 0)),
                    pl.BlockSpec((1, 1, d, ff2), lambda c, ce, off=off: (layer, ce[c + off], 0, 0)),
                    pl.BlockSpec((1, 1, 1, ff2), lambda c, ce, off=off: (layer, ce[c + off], 0, 0)),
                    pl.BlockSpec((1, 1, ff, d), lambda c, ce, off=off: (layer, ce[c + off], 0, 0)),
                    pl.BlockSpec((1, 1, 1, d), lambda c, ce, off=off: (layer, ce[c + off], 0, 0)),
                    pl.BlockSpec((tm, 1), lambda c, ce, off=off: (c + off, 0))]
        args = [chunk_e, rows, w_gu, b_gu3, w_dn, b_dn3, row_w2]
        aliases = {}
        if y is not None:
            in_specs.append(pl.BlockSpec(memory_space=pl.ANY))
            args.append(y)
            aliases = {len(args) - 1: 0}
        y = pl.pallas_call(
            functools.partial(_expert_kernel, ff=ff, fc=fc, off=off),
            grid_spec=pltpu.PrefetchScalarGridSpec(
                num_scalar_prefetch=1, grid=(per,), in_specs=in_specs,
                out_specs=pl.BlockSpec((tm, d), lambda c, ce, off=off: (c + off, 0)),
                scratch_shapes=[pltpu.VMEM((d, ff2), BF16), pltpu.VMEM((ff, d), BF16)]),
            out_shape=jax.ShapeDtypeStruct((n_rows, d), BF16),
            input_output_aliases=aliases,
            compiler_params=_cparams("arbitrary"),
            name="moe_expert_ffn",
        )(*args)
    return y


def _combine_kernel(y_ref, x_ref, g2_ref, o_ref):
    tot = y_ref[0].astype(F32)
    for k in range(1, TOP_K):
        tot = tot + y_ref[k].astype(F32)
    o_ref[0] = x_ref[0] + g2_ref[0] * tot


def _combine(y4, x, g2, tm=512):
    bn, sn, d = x.shape
    row = lambda b, i: (b, i, 0)
    nt = sn // tm
    return pl.pallas_call(
        _combine_kernel,
        grid=(bn, nt),
        in_specs=[pl.BlockSpec((TOP_K, tm, d), lambda b, i: (0, b * nt + i, 0)),
                  pl.BlockSpec((1, tm, d), row),
                  pl.BlockSpec((1, 1, d), lambda b, i: (b, 0, 0))],
        out_specs=pl.BlockSpec((1, tm, d), row),
        out_shape=jax.ShapeDtypeStruct((bn, sn, d), F32),
        compiler_params=_cparams("parallel", "parallel"),
        name="moe_combine",
    )(y4, x, g2)


def _moe(x, g, sc, sh, g2, w_router, b_router, layer, w_gu, b_gu, w_dn, b_dn):
    bn, sn, d = x.shape
    n_tok = bn * sn
    n_asg = n_tok * TOP_K
    tm = MOE_ROWS
    h, e_out, w_out, rank_out, totals = _router(x, g, sc, sh, w_router, b_router)
    e_tok = e_out.reshape(n_tok, TOP_K)
    w_flat = w_out.reshape(-1)
    counts = totals[0, :N_EXPERTS].astype(jnp.int32)
    starts = jnp.cumsum(counts) - counts
    padded = (counts + tm - 1) // tm * tm
    pad_ends = jnp.cumsum(padded)
    pad_starts = pad_ends - padded
    pos = pad_starts[e_tok] + rank_out.reshape(n_tok, TOP_K)
    n_chunks = n_asg // tm + N_EXPERTS
    chunk_start = jnp.arange(n_chunks, dtype=jnp.int32) * tm
    chunk_e = jnp.minimum(jnp.sum(chunk_start[:, None] >= pad_ends[None, :], axis=1), N_EXPERTS - 1).astype(jnp.int32)
    order = jnp.argsort(e_tok.reshape(-1))
    src = (chunk_start - (pad_starts - starts)[chunk_e])[:, None] + jnp.arange(tm, dtype=jnp.int32)[None, :]
    row_valid = (src < (starts + counts)[chunk_e][:, None]).reshape(-1)
    asg = order[jnp.clip(src, 0, n_asg - 1).reshape(-1)].astype(jnp.int32)
    row_tok = asg // TOP_K
    row_w = jnp.where(row_valid, w_flat[asg], 0.0)
    y = _expert_ffn(h.reshape(n_tok, d), row_tok, chunk_e, layer, w_gu, b_gu, w_dn, b_dn, row_w)
    return _combine(y[pos.T], x, g2)


def _norm_rope(x, g, cos, sin, bd, hd):
    sq = x * x
    hi = sq.astype(BF16)
    lo = (sq - hi.astype(F32)).astype(BF16)
    ss = _dot(hi, bd) + _dot(lo, bd)
    y = x * lax.rsqrt(ss * (1.0 / hd) + NORM_EPS) * g
    half = hd // 2
    first = (_iota2(x.shape, 1) & (hd - 1)) < half
    partner = jnp.where(first, pltpu.roll(y, LANES - half, 1), pltpu.roll(y, half, 1))
    return y * cos + partner * sin


def _prep_kernel(aq_ref, bq_ref, bk_ref, bv_ref, cq_ref, ck_ref, cv_ref, dq_ref, akc_ref, aks_ref, akw_ref,
                 dk_ref, dv_ref, c64_ref, s64_ref, c32_ref, s32_ref, gaq_ref, gak_ref, gcq_ref, gck_ref,
                 gdq_ref, gdk_ref, bd64_ref, bd32_ref,
                 oaq_ref, okc_ref, ovc_ref, okx_ref, ovs_ref, okw_ref, ovw_ref, obq_ref, obk_ref, obv_ref,
                 ocq_ref, ock_ref, ocv_ref, odq_ref, odk_ref, odv_ref, *, ts, sb_scale):
    hd, dd = HEAD_DIM, DIFF_DIM
    c64, s64, c32, s32 = c64_ref[0], s64_ref[0], c32_ref[0], s32_ref[0]
    bd64, bd32 = bd64_ref[...], bd32_ref[...]

    def slabs(ref):
        x = ref[0]
        return [x[:, c * LANES:(c + 1) * LANES] for c in range(x.shape[1] // LANES)]

    def put_heads(o_ref, c, y, width):
        per = LANES // width
        for u in range(per):
            o_ref[0, c * per + u] = y[:, u * width:(u + 1) * width].astype(o_ref.dtype)

    for src, gain, dst in ((aq_ref, gaq_ref, oaq_ref), (dq_ref, gdq_ref, odq_ref), (dk_ref, gdk_ref, odk_ref)):
        for c, x in enumerate(slabs(src)):
            put_heads(dst, c, _norm_rope(x, gain[:, c * LANES:(c + 1) * LANES], c64, s64, bd64, hd), hd)
    kc = _norm_rope(akc_ref[0], gak_ref[0:1, :], c64, s64, bd64, hd)
    ks = _norm_rope(aks_ref[0], gak_ref[1:2, :], c64, s64, bd64, hd)
    kw = _norm_rope(akw_ref[0], gak_ref[2:3, :], c64, s64, bd64, hd)
    okc_ref[0] = kc[:, :hd].astype(BF16)
    ovc_ref[0] = akc_ref[0][:, hd:].astype(BF16)
    nsel = okx_ref.shape[2] - hd
    blk = (pl.program_id(1) * ts + _iota2((ts, nsel), 0)) >> (NSA_SEL_BLOCK.bit_length() - 1)
    okx_ref[0, :, :nsel] = jnp.where(blk == _iota2((ts, nsel), 1), 1.0, 0.0).astype(BF16)
    okx_ref[0, :, nsel:] = ks[:, :hd].astype(BF16)
    ovs_ref[0] = aks_ref[0][:, hd:].astype(BF16)
    okw_ref[0, 0] = kw[:, :hd].astype(BF16)
    ovw_ref[0, 0] = akw_ref[0][:, hd:].astype(BF16)
    for c, x in enumerate(slabs(bq_ref)):
        put_heads(obq_ref, c, x * sb_scale, hd)
    for src, dst in ((bk_ref, obk_ref), (bv_ref, obv_ref), (cv_ref, ocv_ref)):
        for c, x in enumerate(slabs(src)):
            put_heads(dst, c, x, hd)
    put_heads(odv_ref, 0, dv_ref[0], hd)
    for src, gain, dst in ((cq_ref, gcq_ref, ocq_ref), (ck_ref, gck_ref, ock_ref)):
        for c, x in enumerate(slabs(src)):
            put_heads(dst, c, _norm_rope(x, gain[:, c * LANES:(c + 1) * LANES], c32, s32, bd32, dd), dd)


def _rope_tables(positions, hd):
    half = hd // 2
    inv = ROPE_THETA ** (-jnp.arange(half, dtype=F32) * 2.0 / hd)
    ang = positions.astype(F32)[..., None] * inv
    cos, sin = jnp.cos(ang), jnp.sin(ang)
    reps = LANES // hd
    return (jnp.tile(jnp.concatenate([cos, cos], axis=-1), (1, 1, reps)),
            jnp.tile(jnp.concatenate([-sin, sin], axis=-1), (1, 1, reps)))


def _prep(proj, cols, tables, p, n_sel, ts=512):
    bn, sn, _ = proj.shape
    hd, dd = HEAD_DIM, DIFF_DIM
    scale = hd ** -0.5
    c64, s64, c32, s32 = tables

    def cspec(name):
        off, width = cols[name]
        assert off % width == 0
        return pl.BlockSpec((1, ts, width), lambda b, i, blk=off // width: (b, i, blk))

    def tile_gain(g, reps, mult=1.0):
        return (jnp.tile(g.astype(F32), reps) * mult).reshape(1, -1)

    ones = jnp.ones((hd,), F32)
    gak = jnp.stack([jnp.concatenate([p["nsa_kn"][j].astype(F32), ones]) for j in range(3)])
    gains = [tile_gain(p["nsa_qn"], NSA_HEADS, scale), gak,
             tile_gain(p["dif_qn"], 2 * DIFF_HEADS, dd ** -0.5), tile_gain(p["dif_kn"], 2 * DIFF_HEADS),
             tile_gain(p["swa_qn"], SWA_HEADS, scale), tile_gain(p["swa_kn"], SWA_KV_HEADS)]
    lane = np.arange(LANES)
    bd64 = jnp.asarray((lane[:, None] // hd == lane[None, :] // hd).astype(np.float32), BF16)
    bd32 = jnp.asarray((lane[:, None] // dd == lane[None, :] // dd).astype(np.float32), BF16)
    names = ("a_q", "b_q", "b_k", "b_v", "c_q", "c_k", "c_v", "d_q", "a_kcvc", "a_ksvs", "a_kwvw", "d_k", "d_v")
    tab = pl.BlockSpec((1, ts, LANES), lambda b, i: (b, i, 0))
    full = lambda a: pl.BlockSpec(a.shape, lambda b, i: (0,) * a.ndim)

    def hm(nh, w):
        return (jax.ShapeDtypeStruct((bn, nh, sn, w), BF16), pl.BlockSpec((1, nh, ts, w), lambda b, i: (b, 0, i, 0)))

    def tm_(w):
        return (jax.ShapeDtypeStruct((bn, sn, w), BF16), pl.BlockSpec((1, ts, w), lambda b, i: (b, i, 0)))

    outs = [hm(NSA_HEADS, hd), tm_(hd), tm_(hd), tm_(n_sel + hd), tm_(hd), hm(1, hd), hm(1, hd),
            hm(SB_HEADS, hd), hm(SB_HEADS, hd), hm(SB_HEADS, hd),
            hm(2 * DIFF_HEADS, dd), hm(2 * DIFF_HEADS, dd), hm(DIFF_HEADS, 2 * dd),
            hm(SWA_HEADS, hd), hm(SWA_KV_HEADS, hd), hm(SWA_KV_HEADS, hd)]
    consts = gains + [bd64, bd32]
    res = pl.pallas_call(
        functools.partial(_prep_kernel, ts=ts, sb_scale=-scale * LOG2E),
        grid=(bn, sn // ts),
        in_specs=[cspec(n) for n in names] + [tab] * 4 + [full(a) for a in consts],
        out_specs=[o[1] for o in outs],
        out_shape=[o[0] for o in outs],
        compiler_params=_cparams("parallel", "parallel"),
        name="mixer_prep",
    )(*([proj] * len(names)), c64, s64, c32, s32, *consts)
    keys = ("a_q", "kc", "vc", "kx", "vs", "kw", "vw", "b_q", "b_k", "b_v", "c_q", "c_k", "c_v", "d_q", "d_k", "d_v")
    return dict(zip(keys, res))


def _layer_columns(d):
    cols = {}
    off = 0
    for name, width in (("br_g", N_BRANCH * d), ("ga", 3 * NSA_HEADS * HEAD_DIM), ("a_q", 256),
                        ("b_q", 256), ("b_k", 256), ("b_v", 256), ("c_q", 256), ("c_k", 256), ("c_v", 256),
                        ("d_q", 256), ("a_kcvc", 128), ("a_ksvs", 128), ("a_kwvw", 128), ("d_k", 128), ("d_v", 128)):
        cols[name] = (off, width)
        off += width
    return cols, off


def _reorder_w_in(w_in, d, n_pad):
    ref_splits = (256, 128, 128, 128, 12, 256, 256, 256, 256, 256, 256, 256, 128, 128, N_BRANCH * d)
    names = ("a_q", "a_kcvc", "a_ksvs", "a_kwvw", "a_g", "b_q", "b_k", "b_v",
             "c_q", "c_k", "c_v", "d_q", "d_k", "d_v", "br_g")
    starts = np.cumsum((0,) + ref_splits)
    src = {n: (int(starts[i]), ref_splits[i]) for i, n in enumerate(names)}
    cols, total = _layer_columns(d)
    pieces = []
    for name, (off, width) in cols.items():
        if name == "ga":
            g0 = src["a_g"][0]
            gates = w_in[:, g0:g0 + 3 * NSA_HEADS].reshape(-1, NSA_HEADS, 3)
            gates = jnp.broadcast_to(gates.transpose(0, 2, 1)[..., None], (w_in.shape[0], 3, NSA_HEADS, HEAD_DIM))
            pieces.append(gates.reshape(w_in.shape[0], width))
        else:
            pieces.append(w_in[:, src[name][0]:src[name][0] + width])
    pieces.append(jnp.zeros((w_in.shape[0], n_pad - total), w_in.dtype))
    return jnp.concatenate(pieces, axis=1).astype(BF16)


def _mixer_layer(x, positions, tables, mod, p, lam_init):
    bn, sn, d = x.shape
    sh1, sc1, g1 = mod[0], mod[1], mod[2]
    cols, total = _layer_columns(d)
    n_pad = -(-total // 1536) * 1536
    w_in = _reorder_w_in(p["w_in"], d, n_pad)
    proj = _in_projection(x, p["norm1"], sc1, sh1, w_in)
    n_cmp = (sn - NSA_CMP_LEN) // NSA_CMP_STRIDE + 1
    n_sel = sn // NSA_SEL_BLOCK
    n_top = min(NSA_N_SEL, n_sel)
    t = _prep(proj, cols, tables, p, n_sel)
    k_cmp = _compress(t["kc"], p["nsa_pe_k"], p["nsa_w1_k"], p["nsa_w2_k"]).astype(BF16)
    v_cmp = _compress(t["vc"], p["nsa_pe_v"], p["nsa_w1_v"], p["nsa_w2_v"]).astype(BF16)
    cmp_start = np.arange(sn // NSA_CMP_STRIDE) * NSA_CMP_STRIDE
    sel_start = np.arange(n_sel) * NSA_SEL_BLOCK
    overlap = ((cmp_start[:, None] <= (sel_start + NSA_SEL_BLOCK - 1)[None, :]) &
               ((cmp_start + NSA_CMP_LEN - 1)[:, None] >= sel_start[None, :]) &
               (np.arange(sn // NSA_CMP_STRIDE) < n_cmp)[:, None]).astype(np.float32)
    o_c, bias = _nsa_cmp(t["a_q"], k_cmp, v_cmp, jnp.asarray(overlap, BF16), n_cmp, n_top)
    o_s = _nsa_sel(t["a_q"], t["kx"], t["vs"], bias)
    o_w = _banded_attention(t["a_q"], t["kw"], t["vw"], NSA_WINDOW)
    o_b = _stick_breaking(t["b_q"], t["b_k"], t["b_v"])
    lam = (jnp.exp(jnp.sum(p["dif_lq1"] * p["dif_lk1"])) - jnp.exp(jnp.sum(p["dif_lq2"] * p["dif_lk2"]))
           + lam_init)
    o_cd = _diff_attention(t["c_q"], t["c_k"], t["c_v"], p["dif_subln"], lam, jnp.asarray(1.0 - lam_init, F32))
    o_d = _banded_attention(t["d_q"], t["d_k"], t["d_v"], SWA_WINDOW, sinks=p["swa_sinks"])
    return _merge(proj, o_c, o_s, o_w, o_b, o_cd, o_d, p["w_up"].astype(BF16), p["w_out"].astype(BF16),
                  x, g1, cols["ga"][0])


def kernel(x, c, positions, w_ada, b_ada, norm1, norm2, w_in, nsa_qn, nsa_kn, nsa_pe_k, nsa_w1_k, nsa_w2_k,
           nsa_pe_v, nsa_w1_v, nsa_w2_v, dif_qn, dif_kn, dif_lq1, dif_lk1, dif_lq2, dif_lk2, dif_subln,
           swa_qn, swa_kn, swa_sinks, w_up, w_out, w_router, b_router, w_gu, b_gu, w_dn, b_dn):
    bn, sn, d = x.shape
    depth = w_ada.shape[0]
    c_pad = jnp.zeros((8, d), F32).at[:bn].set(c)
    tables = _rope_tables(positions, HEAD_DIM) + _rope_tables(positions, DIFF_DIM)
    for l in range(depth):
        lam_init = 0.8 - 0.6 * math.exp(-0.3 * l)
        mod = _linear(c_pad, w_ada[l], b_ada[l], tn=512, precision=lax.Precision.HIGHEST)[:bn]
        mod = mod.reshape(bn, 6, 1, d).transpose(1, 0, 2, 3)
        p = dict(norm1=norm1[l], w_in=w_in[l], nsa_qn=nsa_qn[l], nsa_kn=nsa_kn[l], nsa_pe_k=nsa_pe_k[l],
                 nsa_w1_k=nsa_w1_k[l], nsa_w2_k=nsa_w2_k[l], nsa_pe_v=nsa_pe_v[l], nsa_w1_v=nsa_w1_v[l],
                 nsa_w2_v=nsa_w2_v[l], dif_qn=dif_qn[l], dif_kn=dif_kn[l], dif_lq1=dif_lq1[l],
                 dif_lk1=dif_lk1[l], dif_lq2=dif_lq2[l], dif_lk2=dif_lk2[l], dif_subln=dif_subln[l],
                 swa_qn=swa_qn[l], swa_kn=swa_kn[l], swa_sinks=swa_sinks[l], w_up=w_up[l], w_out=w_out[l])
        x = _mixer_layer(x, positions, tables, mod, p, lam_init)
        x = _moe(x, norm2[l], mod[4], mod[3], mod[5], w_router[l], b_router[l], l, w_gu, b_gu, w_dn, b_dn)
    return x
```

```python
import functools
import math

import numpy as np
import jax
import jax.numpy as jnp
from jax import lax
from jax.experimental import pallas as pl
from jax.experimental.pallas import tpu as pltpu

F32 = jnp.float32
BF16 = jnp.bfloat16

HEAD_DIM = 64
ROPE_THETA = 10000.0
NORM_EPS = 1e-6
NEG_INF = -1e30
KNOCKOUT = -3e38
N_BRANCH = 4

NSA_HEADS = 4
NSA_CMP_LEN = 32
NSA_CMP_STRIDE = 16
NSA_SEL_BLOCK = 64
NSA_N_SEL = 16
NSA_WINDOW = 512
NSA_FORCED_SCORE = 1e4

SB_HEADS = 4
DIFF_HEADS = 4
DIFF_DIM = 32
SWA_HEADS = 4
SWA_KV_HEADS = 2
SWA_WINDOW = 128

N_EXPERTS = 32
TOP_K = 4
SWIGLU_ALPHA = 1.702
SWIGLU_LIMIT = 7.0

LANES = 128
LOG2E = 1.4426950408889634
MOE_ROWS = 512
VMEM_LIMIT = 56 * 1024 * 1024


def _cparams(*sem):
    return pltpu.CompilerParams(dimension_semantics=sem, vmem_limit_bytes=VMEM_LIMIT)


def _dot(a, b):
    return jnp.dot(a, b, preferred_element_type=F32)


def _dot_nt(a, b):
    return lax.dot_general(a, b, (((1,), (1,)), ((), ())), preferred_element_type=F32)


def _iota2(shape, dim):
    return lax.broadcasted_iota(jnp.int32, shape, dim)


def _linear_kernel(x_ref, w_ref, b_ref, o_ref, *, precision):
    o_ref[...] = jnp.dot(x_ref[...], w_ref[...], preferred_element_type=F32,
                         precision=precision) + b_ref[...]


def _linear(x, w, b, tn, precision=None):
    m, k = x.shape
    n = w.shape[1]
    return pl.pallas_call(
        functools.partial(_linear_kernel, precision=precision),
        grid=(n // tn,),
        in_specs=[pl.BlockSpec((m, k), lambda j: (0, 0)),
                  pl.BlockSpec((k, tn), lambda j: (0, j)),
                  pl.BlockSpec((1, tn), lambda j: (0, j))],
        out_specs=pl.BlockSpec((m, tn), lambda j: (0, j)),
        out_shape=jax.ShapeDtypeStruct((m, n), F32),
        compiler_params=_cparams("arbitrary"),
        name="linear",
    )(x, w, b.reshape(1, n))


def _cmp_mlp_kernel(a_ref, b_ref, pe_ref, w2_ref, o_ref):
    hid = jax.nn.gelu(a_ref[...] + b_ref[...] + pe_ref[...])
    o_ref[...] = _dot(hid.astype(BF16), w2_ref[...])


def _compress(t, pe, w1, w2):
    bn, sn, dh = t.shape
    st = NSA_CMP_STRIDE
    half = st * dh
    nb = sn // st
    t16 = t.reshape(bn * nb, half).astype(BF16)
    w1cat = jnp.concatenate([w1[:half], w1[half:]], axis=1).astype(BF16)
    hidden = w1.shape[1]
    ab = _linear(t16, w1cat, jnp.zeros((2 * hidden,), F32), tn=2 * hidden)
    ab = ab.reshape(bn, nb, 2 * hidden)
    a = ab[:, :, :hidden]
    b_next = jnp.concatenate([ab[:, 1:, hidden:], jnp.zeros((bn, 1, hidden), F32)], axis=1)
    pe_term = jnp.dot(pe.reshape(1, NSA_CMP_LEN * dh), w1, precision=lax.Precision.HIGHEST)
    rows = bn * nb
    tm = min(512, rows)
    out = pl.pallas_call(
        _cmp_mlp_kernel,
        grid=(rows // tm,),
        in_specs=[pl.BlockSpec((tm, hidden), lambda i: (i, 0)),
                  pl.BlockSpec((tm, hidden), lambda i: (i, 0)),
                  pl.BlockSpec((1, hidden), lambda i: (0, 0)),
                  pl.BlockSpec((hidden, dh), lambda i: (0, 0))],
        out_specs=pl.BlockSpec((tm, dh), lambda i: (i, 0)),
        out_shape=jax.ShapeDtypeStruct((rows, dh), F32),
        compiler_params=_cparams("parallel"),
        name="cmp_mlp",
    )(a.reshape(rows, hidden), b_next.reshape(rows, hidden), pe_term, w2.astype(BF16))
    return out.reshape(bn, nb, dh)


def _norm_mod(x, g, sc, sh):
    r = lax.rsqrt(jnp.mean(x * x, axis=-1, keepdims=True) + NORM_EPS)
    return (x * r * g) * (1.0 + sc) + sh


def _proj_kernel(x_ref, g_ref, sc_ref, sh_ref, w_ref, o_ref, h_scr):
    @pl.when(pl.program_id(2) == 0)
    def _():
        h_scr[...] = _norm_mod(x_ref[0], g_ref[...], sc_ref[0], sh_ref[0]).astype(BF16)

    o_ref[0] = _dot(h_scr[...], w_ref[...])


def _in_projection(x, g, sc, sh, w, tm=1024, tn=1536):
    bn, sn, d = x.shape
    n = w.shape[1]
    tm = min(tm, sn)
    return pl.pallas_call(
        _proj_kernel,
        grid=(bn, sn // tm, n // tn),
        in_specs=[pl.BlockSpec((1, tm, d), lambda b, i, j: (b, i, 0)),
                  pl.BlockSpec((1, d), lambda b, i, j: (0, 0)),
                  pl.BlockSpec((1, 1, d), lambda b, i, j: (b, 0, 0)),
                  pl.BlockSpec((1, 1, d), lambda b, i, j: (b, 0, 0)),
                  pl.BlockSpec((d, tn), lambda b, i, j: (0, j))],
        out_specs=pl.BlockSpec((1, tm, tn), lambda b, i, j: (b, i, j)),
        out_shape=jax.ShapeDtypeStruct((bn, sn, n), F32),
        scratch_shapes=[pltpu.VMEM((tm, d), BF16)],
        compiler_params=_cparams("parallel", "parallel", "arbitrary"),
        name="in_proj",
    )(x, g.reshape(1, d), sc, sh, w)


def _banded_kernel(*refs, tile, window, has_sink):
    if has_sink:
        sink_ref, q_ref, kp_ref, kc_ref, vp_ref, vc_ref, o_ref = refs
    else:
        q_ref, kp_ref, kc_ref, vp_ref, vc_ref, o_ref = refs
    i = pl.program_id(2)
    w = window
    upper = _iota2((w, w), 1) > _iota2((w, w), 0)
    first_bias = jnp.where(i > 0, 0.0, NEG_INF)
    for u in range(tile // w):
        q = q_ref[0, 0, u * w:(u + 1) * w, :]
        if u == 0:
            k_prev, v_prev = kp_ref[0, 0], vp_ref[0, 0]
        else:
            k_prev, v_prev = kc_ref[0, 0, (u - 1) * w:u * w, :], vc_ref[0, 0, (u - 1) * w:u * w, :]
        k_cur, v_cur = kc_ref[0, 0, u * w:(u + 1) * w, :], vc_ref[0, 0, u * w:(u + 1) * w, :]
        s_prev = _dot_nt(q, k_prev)
        if u == 0:
            s_prev = s_prev + first_bias
        s = jnp.where(upper, s_prev, _dot_nt(q, k_cur))
        m = jnp.max(s, axis=-1, keepdims=True)
        if has_sink:
            sink = sink_ref[pl.program_id(1)]
            m = jnp.maximum(m, sink)
        p = jnp.exp(s - m)
        den = jnp.sum(p, axis=-1, keepdims=True)
        if has_sink:
            den = den + jnp.exp(sink - m)
        else:
            den = jnp.maximum(den, 1e-30)
        o = (_dot(jnp.where(upper, p, 0.0).astype(BF16), v_prev)
             + _dot(jnp.where(upper, 0.0, p).astype(BF16), v_cur))
        o_ref[0, 0, u * w:(u + 1) * w, :] = o / den


def _banded_attention(q, k, v, window, sinks=None, tile=1024):
    bn, hq, sn, d = q.shape
    grp = hq // k.shape[1]
    tile = min(tile, sn)
    assert tile % window == 0 and sn % tile == 0
    per = tile // window
    has_sink = sinks is not None
    qspec = pl.BlockSpec((1, 1, tile, d), lambda b, h, i: (b, h, i, 0))
    prev = pl.BlockSpec((1, 1, window, d), lambda b, h, i: (b, h // grp, jnp.maximum(i * per - 1, 0), 0))
    cur = pl.BlockSpec((1, 1, tile, d), lambda b, h, i: (b, h // grp, i, 0))
    in_specs = [qspec, prev, cur, prev, cur]
    args = [q, k, k, v, v]
    if has_sink:
        in_specs = [pl.BlockSpec(memory_space=pltpu.SMEM)] + in_specs
        args = [sinks.astype(F32)] + args
    return pl.pallas_call(
        functools.partial(_banded_kernel, tile=tile, window=window, has_sink=has_sink),
        grid=(bn, hq, sn // tile),
        in_specs=in_specs,
        out_specs=pl.BlockSpec((1, 1, tile, d), lambda b, h, i: (b, h, i, 0)),
        out_shape=jax.ShapeDtypeStruct((bn, hq, sn, d), F32),
        compiler_params=_cparams("parallel", "parallel", "parallel"),
        name="banded_attn",
    )(*args)


def _nsa_cmp_kernel(q_ref, kc_ref, vct_ref, ovt_ref, oct_ref, bias_ref, *, tq, n_cmp, n_top, heads):
    i = pl.program_id(1)
    ncp = kc_ref.shape[1]
    nsel = ovt_ref.shape[0]
    dh = q_ref.shape[3]
    t = i * tq + _iota2((ncp, tq), 1)
    n = _iota2((ncp, tq), 0)
    valid = (n * NSA_CMP_STRIDE + (NSA_CMP_LEN - 1) <= t) & (n < n_cmp)
    kc = kc_ref[0]
    vct = vct_ref[0]
    psum = jnp.zeros((ncp, tq), F32)
    for h in range(heads):
        s = jnp.where(valid, _dot_nt(kc, q_ref[0, h]), NEG_INF)
        m = jnp.max(s, axis=0, keepdims=True)
        p = jnp.where(valid, jnp.exp(s - m), 0.0)
        p = p / jnp.maximum(jnp.sum(p, axis=0, keepdims=True), 1e-30)
        oct_ref[0, h * dh:(h + 1) * dh, :] = _dot(vct, p.astype(BF16))
        psum = psum + p
    hi = psum.astype(BF16)
    lo = (psum - hi.astype(F32)).astype(BF16)
    imp = _dot(ovt_ref[...], hi) + _dot(ovt_ref[...], lo)

    tt = i * tq + _iota2((nsel, tq), 1)
    blk = _iota2((nsel, tq), 0)
    cur = tt >> (NSA_SEL_BLOCK.bit_length() - 1)
    forced = (blk == 0) | (blk == cur) | (blk == cur - 1)
    valid_s = blk * NSA_SEL_BLOCK <= tt
    score = jnp.where(forced, NSA_FORCED_SCORE, jnp.where(valid_s, imp, -1.0))
    blk_f = blk.astype(F32)

    def pick(_, carry):
        score, sel = carry
        m = jnp.max(score, axis=0, keepdims=True)
        first = jnp.min(jnp.where(score == m, blk_f, float(nsel)), axis=0, keepdims=True)
        hit = blk_f == first
        return jnp.where(hit, KNOCKOUT, score), jnp.where(hit, 0.0, sel)

    _, bias = lax.fori_loop(0, n_top, pick, (score, jnp.full((nsel, tq), NEG_INF, F32)))
    bias_ref[0] = bias.astype(BF16)


def _nsa_cmp(q, kc, vc, overlap, n_cmp, n_top, tq=256):
    bn, heads, sn, dh = q.shape
    ncp = kc.shape[1]
    nsel = overlap.shape[1]
    oct, bias_t = pl.pallas_call(
        functools.partial(_nsa_cmp_kernel, tq=tq, n_cmp=n_cmp, n_top=n_top, heads=heads),
        grid=(bn, sn // tq),
        in_specs=[pl.BlockSpec((1, heads, tq, dh), lambda b, i: (b, 0, i, 0)),
                  pl.BlockSpec((1, ncp, dh), lambda b, i: (b, 0, 0)),
                  pl.BlockSpec((1, dh, ncp), lambda b, i: (b, 0, 0)),
                  pl.BlockSpec((nsel, ncp), lambda b, i: (0, 0))],
        out_specs=[pl.BlockSpec((1, heads * dh, tq), lambda b, i: (b, 0, i)),
                   pl.BlockSpec((1, nsel, tq), lambda b, i: (b, 0, i))],
        out_shape=[jax.ShapeDtypeStruct((bn, heads * dh, sn), F32),
                   jax.ShapeDtypeStruct((bn, nsel, sn), BF16)],
        compiler_params=_cparams("parallel", "parallel"),
        name="nsa_cmp_topk",
    )(q, kc, vc.transpose(0, 2, 1), overlap.T)
    return oct.transpose(0, 2, 1), bias_t.transpose(0, 2, 1)


def _nsa_sel_kernel(q_ref, kx_ref, v_ref, bias_ref, o_ref, qx_scr, *, tq, tk, heads):
    i = pl.program_id(1)
    dh = v_ref.shape[2]
    bias = bias_ref[0]
    for h in range(heads):
        qx_scr[h] = jnp.concatenate([bias, q_ref[0, h]], axis=1)

    def tile(j, state, diagonal):
        start = pl.multiple_of(j * tk, tk)
        kx = kx_ref[0, pl.ds(start, tk), :]
        v = v_ref[0, pl.ds(start, tk), :]
        if diagonal:
            causal = (start + _iota2((tq, tk), 1)) <= (i * tq + _iota2((tq, tk), 0))
        new_state = []
        for h in range(heads):
            m_old, l_old, acc_old = state[h]
            s = _dot_nt(qx_scr[h], kx)
            if diagonal:
                s = jnp.where(causal, s, NEG_INF)
            m_new = jnp.maximum(m_old, jnp.max(s, axis=-1, keepdims=True))
            alpha = jnp.exp(m_old - m_new)
            p = jnp.exp(s - m_new)
            l_new = alpha * l_old + jnp.sum(p, axis=-1, keepdims=True)
            acc_new = alpha * acc_old + _dot(p.astype(BF16), v)
            new_state.append((m_new, l_new, acc_new))
        return tuple(new_state)

    init = tuple((jnp.full((tq, 1), NEG_INF, F32), jnp.zeros((tq, 1), F32), jnp.zeros((tq, dh), F32))
                 for _ in range(heads))
    n_full = (i * tq) // tk
    state = lax.fori_loop(0, n_full, lambda j, st: tile(j, st, False), init)
    state = tile(n_full, state, True)
    for h in range(heads):
        o_ref[0, :, h * dh:(h + 1) * dh] = state[h][2] / jnp.maximum(state[h][1], 1e-30)


def _nsa_sel(q, kx, v, bias, tq=1024, tk=1024):
    bn, heads, sn, dh = q.shape
    nsel = bias.shape[2]
    tq, tk = min(tq, sn), min(tk, sn)
    assert tk % tq == 0 and sn % tk == 0
    return pl.pallas_call(
        functools.partial(_nsa_sel_kernel, tq=tq, tk=tk, heads=heads),
        grid=(bn, sn // tq),
        in_specs=[pl.BlockSpec((1, heads, tq, dh), lambda b, i: (b, 0, i, 0)),
                  pl.BlockSpec((1, sn, nsel + dh), lambda b, i: (b, 0, 0)),
                  pl.BlockSpec((1, sn, dh), lambda b, i: (b, 0, 0)),
                  pl.BlockSpec((1, tq, nsel), lambda b, i: (b, i, 0))],
        out_specs=pl.BlockSpec((1, tq, heads * dh), lambda b, i: (b, i, 0)),
        out_shape=jax.ShapeDtypeStruct((bn, sn, heads * dh), F32),
        scratch_shapes=[pltpu.VMEM((heads, tq, nsel + dh), BF16)],
        compiler_params=_cparams("parallel", "arbitrary"),
        name="nsa_selected_attn",
    )(q, kx, v, bias)


def _sb_kernel(q_ref, k_ref, v_ref, u_ref, o_ref, *, tq, tk):
    i = pl.program_id(2)
    q = q_ref[0, 0]
    per_q = tq // tk

    def tile(jj, carry, first_row):
        diagonal = first_row is not None
        r0 = first_row if diagonal else 0
        start = pl.multiple_of(jj * tk, tk)
        nz = _dot_nt(q[r0:], k_ref[0, 0, pl.ds(start, tk), :])
        neg_abs = lax.bitcast_convert_type(lax.bitcast_convert_type(nz, jnp.uint32) | jnp.uint32(0x80000000), F32)
        log_keep = jnp.minimum(nz, 0.0) - jnp.log2(1.0 + jnp.exp2(neg_abs))
        if diagonal:
            strict = _iota2(nz.shape, 1) < _iota2(nz.shape, 0)
            log_keep = jnp.where(strict, log_keep, 0.0)
        cum = _dot(log_keep.astype(BF16), u_ref[...])
        a = jnp.exp2(cum + jnp.concatenate([carry[r0:]] * (tk // LANES), axis=1) - nz)
        if diagonal:
            a = jnp.where(strict, a, 0.0)
        out = _dot(a.astype(BF16), v_ref[0, 0, pl.ds(start, tk), :])
        new_carry = carry[r0:] + jnp.broadcast_to(cum[:, 0:1], (tq - r0, LANES))
        if r0:
            out = jnp.concatenate([jnp.zeros((r0, out.shape[1]), F32), out], axis=0)
            new_carry = jnp.concatenate([carry[:r0], new_carry], axis=0)
        return out, new_carry

    def group(first, carry, diagonal):
        total = None
        for r in range(per_q):
            out, carry = tile(first - r, carry, (per_q - 1 - r) * tk if diagonal else None)
            total = out if total is None else total + out
        return total, carry

    acc, carry = group(i * per_q + per_q - 1, jnp.zeros((tq, LANES), F32), True)

    def body(p, state):
        acc, carry = state
        out, carry = group((i - p) * per_q - 1, carry, False)
        return acc + out, carry

    acc, _ = lax.fori_loop(0, i, body, (acc, carry))
    o_ref[0, 0] = acc


def _stick_breaking(q, k, v, tq=1024, tk=256):
    bn, heads, sn, dh = q.shape
    tq = min(tq, sn)
    incl = (np.arange(tk)[:, None] >= np.arange(tk)[None, :]).astype(np.float32)
    u = jnp.asarray(incl, BF16)
    kv = pl.BlockSpec((1, 1, sn, dh), lambda b, h, i: (b, h, 0, 0))
    return pl.pallas_call(
        functools.partial(_sb_kernel, tq=tq, tk=tk),
        grid=(bn, heads, sn // tq),
        in_specs=[pl.BlockSpec((1, 1, tq, dh), lambda b, h, i: (b, h, i, 0)), kv, kv,
                  pl.BlockSpec((tk, tk), lambda b, h, i: (0, 0))],
        out_specs=pl.BlockSpec((1, 1, tq, dh), lambda b, h, i: (b, h, i, 0)),
        out_shape=jax.ShapeDtypeStruct((bn, heads, sn, dh), F32),
        compiler_params=_cparams("parallel", "parallel", "arbitrary"),
        name="stick_breaking_attn",
    )(q, k, v, u)


def _diff_kernel(sc_ref, q_ref, k_ref, v_ref, g_ref, o_ref, *, tq, tk):
    i = pl.program_id(2)
    dv = v_ref.shape[3]

    def tile(j, state, diagonal):
        start = pl.multiple_of(j * tk, tk)
        v = v_ref[0, 0, pl.ds(start, tk), :]
        if diagonal:
            causal = (start + _iota2((tq, tk), 1)) <= (i * tq + _iota2((tq, tk), 0))
        new_state = []
        for mi in range(2):
            m_old, l_old, acc_old = state[mi]
            s = _dot_nt(q_ref[0, mi], k_ref[0, mi, pl.ds(start, tk), :])
            if diagonal:
                s = jnp.where(causal, s, NEG_INF)
            m_new = jnp.maximum(m_old, jnp.max(s, axis=-1, keepdims=True))
            alpha = jnp.exp(m_old - m_new)
            p = jnp.exp(s - m_new)
            l_new = alpha * l_old + jnp.sum(p, axis=-1, keepdims=True)
            acc_new = alpha * acc_old + _dot(p.astype(BF16), v)
            new_state.append((m_new, l_new, acc_new))
        return tuple(new_state)

    init = tuple((jnp.full((tq, 1), NEG_INF, F32), jnp.zeros((tq, 1), F32), jnp.zeros((tq, dv), F32))
                 for _ in range(2))
    n_full = (i * tq) // tk
    state = lax.fori_loop(0, n_full, lambda j, st: tile(j, st, False), init)
    state = tile(n_full, state, True)
    lam = sc_ref[0]
    post = sc_ref[1]
    o = (state[0][2] / jnp.maximum(state[0][1], 1e-30)
         - lam * (state[1][2] / jnp.maximum(state[1][1], 1e-30)))
    r = lax.rsqrt(jnp.mean(o * o, axis=-1, keepdims=True) + NORM_EPS)
    o_ref[0, 0] = (o * r * g_ref[...]) * post


def _diff_attention(q, k, v, subln, lam, post, tq=1024, tk=1024):
    bn, h2, sn, dd = q.shape
    heads = h2 // 2
    dv = v.shape[3]
    tq, tk = min(tq, sn), min(tk, sn)
    assert tk % tq == 0 and sn % tk == 0
    scal = jnp.stack([lam, post]).astype(F32)
    return pl.pallas_call(
        functools.partial(_diff_kernel, tq=tq, tk=tk),
        grid=(bn, heads, sn // tq),
        in_specs=[pl.BlockSpec(memory_space=pltpu.SMEM),
                  pl.BlockSpec((1, 2, tq, dd), lambda b, h, i: (b, h, i, 0)),
                  pl.BlockSpec((1, 2, sn, dd), lambda b, h, i: (b, h, 0, 0)),
                  pl.BlockSpec((1, 1, sn, dv), lambda b, h, i: (b, h, 0, 0)),
                  pl.BlockSpec((1, dv), lambda b, h, i: (0, 0))],
        out_specs=pl.BlockSpec((1, 1, tq, dv), lambda b, h, i: (b, h, i, 0)),
        out_shape=jax.ShapeDtypeStruct((bn, heads, sn, dv), F32),
        compiler_params=_cparams("parallel", "parallel", "arbitrary"),
        name="diff_attn",
    )(scal, q, k, v, subln.reshape(1, dv).astype(F32))


def _merge_kernel(brg_ref, ga0_ref, ga1_ref, ga2_ref, oc_ref, os_ref, ow_ref, ob_ref, ocd_ref, od_ref,
                  wup_ref, wout_ref, x_ref, g1_ref, o_ref, *, d):
    def heads(ref):
        return jnp.concatenate([ref[0, h] for h in range(ref.shape[1])], axis=1)

    o_a = (jax.nn.sigmoid(ga0_ref[0]) * oc_ref[0] + jax.nn.sigmoid(ga1_ref[0]) * os_ref[0]
           + jax.nn.sigmoid(ga2_ref[0]) * heads(ow_ref))
    branches = (o_a, heads(ob_ref), heads(ocd_ref), heads(od_ref))
    merged = None
    for bi, o in enumerate(branches):
        gate = jax.nn.sigmoid(brg_ref[0, :, bi * d:(bi + 1) * d])
        term = gate * _dot(o.astype(BF16), wup_ref[bi])
        merged = term if merged is None else merged + term
    y = _dot(merged.astype(BF16), wout_ref[...])
    o_ref[0] = x_ref[0] + g1_ref[0] * y


def _merge(proj, o_c, o_s, o_w, o_b, o_cd, o_d, w_up, w_out, x, g1, ga_col, tm=256):
    bn, sn, d = x.shape
    bw = o_c.shape[2]
    nh, dh = o_b.shape[1], o_b.shape[3]
    assert ga_col % bw == 0
    gblk = ga_col // bw
    row = lambda b, i: (b, i, 0)
    bspec = pl.BlockSpec((1, tm, bw), row)
    hspec = pl.BlockSpec((1, nh, tm, dh), lambda b, i: (b, 0, i, 0))
    return pl.pallas_call(
        functools.partial(_merge_kernel, d=d),
        grid=(bn, sn // tm),
        in_specs=[pl.BlockSpec((1, tm, N_BRANCH * d), row),
                  pl.BlockSpec((1, tm, bw), lambda b, i: (b, i, gblk)),
                  pl.BlockSpec((1, tm, bw), lambda b, i: (b, i, gblk + 1)),
                  pl.BlockSpec((1, tm, bw), lambda b, i: (b, i, gblk + 2)),
                  bspec, bspec, hspec, hspec, hspec, hspec,
                  pl.BlockSpec((N_BRANCH, bw, d), lambda b, i: (0, 0, 0)),
                  pl.BlockSpec((d, d), lambda b, i: (0, 0)),
                  pl.BlockSpec((1, tm, d), row),
                  pl.BlockSpec((1, 1, d), lambda b, i: (b, 0, 0))],
        out_specs=pl.BlockSpec((1, tm, d), row),
        out_shape=jax.ShapeDtypeStruct((bn, sn, d), F32),
        compiler_params=_cparams("parallel", "parallel"),
        name="branch_merge",
    )(proj, proj, proj, proj, o_c, o_s, o_w, o_b, o_cd, o_d, w_up, w_out, x, g1)


def _router_kernel(x_ref, g_ref, sc_ref, sh_ref, wr_ref, br_ref, tri_ref, h_ref, e_ref, w_ref, rank_ref, cnt_ref,
                   run_scr):
    @pl.when((pl.program_id(0) == 0) & (pl.program_id(1) == 0))
    def _():
        run_scr[...] = jnp.zeros(run_scr.shape, F32)

    h = _norm_mod(x_ref[0], g_ref[...], sc_ref[0], sh_ref[0])
    h_ref[0] = h.astype(BF16)
    logits = jnp.dot(h, wr_ref[...], preferred_element_type=F32,
                     precision=lax.Precision.HIGHEST) + br_ref[...]
    lane = _iota2(logits.shape, 1)
    lane_f = lane.astype(F32)
    cur = logits
    vals, idxs = [], []
    chosen = jnp.zeros(logits.shape, F32)
    for _ in range(TOP_K):
        m = jnp.max(cur, axis=-1, keepdims=True)
        first = jnp.min(jnp.where(cur == m, lane_f, float(LANES)), axis=-1, keepdims=True)
        vals.append(m)
        idxs.append(first)
        hit = lane_f == first
        cur = jnp.where(hit, KNOCKOUT, cur)
        chosen = jnp.where(hit, 1.0, chosen)
    exps = [jnp.exp(v - vals[0]) for v in vals]
    den = exps[0]
    for e in exps[1:]:
        den = den + e
    earlier = _dot(tri_ref[...], chosen.astype(BF16)) + run_scr[0:1, :]
    e_out = jnp.zeros(logits.shape, F32)
    w_out = jnp.zeros(logits.shape, F32)
    r_out = jnp.zeros(logits.shape, F32)
    for k in range(TOP_K):
        rank_k = jnp.sum(jnp.where(lane_f == idxs[k], earlier, 0.0), axis=-1, keepdims=True)
        e_out = jnp.where(lane == k, idxs[k], e_out)
        w_out = jnp.where(lane == k, exps[k] / den, w_out)
        r_out = jnp.where(lane == k, rank_k, r_out)
    e_ref[0] = e_out[:, :TOP_K].astype(jnp.int32)
    w_ref[0] = w_out[:, :TOP_K]
    rank_ref[0] = r_out[:, :TOP_K].astype(jnp.int32)
    total = run_scr[...] + jnp.sum(chosen, axis=0, keepdims=True)
    run_scr[...] = total
    cnt_ref[...] = total


def _router(x, g, sc, sh, w_router, b_router, tm=512):
    bn, sn, d = x.shape
    ne = w_router.shape[1]
    wr = jnp.zeros((d, LANES), F32).at[:, :ne].set(w_router)
    br = jnp.full((1, LANES), NEG_INF, F32).at[0, :ne].set(b_router)
    tri = jnp.asarray((np.arange(tm)[:, None] > np.arange(tm)[None, :]).astype(np.float32), BF16)
    row = lambda b, i: (b, i, 0)
    return pl.pallas_call(
        _router_kernel,
        grid=(bn, sn // tm),
        in_specs=[pl.BlockSpec((1, tm, d), row),
                  pl.BlockSpec((1, d), lambda b, i: (0, 0)),
                  pl.BlockSpec((1, 1, d), lambda b, i: (b, 0, 0)),
                  pl.BlockSpec((1, 1, d), lambda b, i: (b, 0, 0)),
                  pl.BlockSpec((d, LANES), lambda b, i: (0, 0)),
                  pl.BlockSpec((1, LANES), lambda b, i: (0, 0)),
                  pl.BlockSpec((tm, tm), lambda b, i: (0, 0))],
        out_specs=[pl.BlockSpec((1, tm, d), row),
                   pl.BlockSpec((1, tm, TOP_K), row),
                   pl.BlockSpec((1, tm, TOP_K), row),
                   pl.BlockSpec((1, tm, TOP_K), row),
                   pl.BlockSpec((8, LANES), lambda b, i: (0, 0))],
        out_shape=[jax.ShapeDtypeStruct((bn, sn, d), BF16),
                   jax.ShapeDtypeStruct((bn, sn, TOP_K), jnp.int32),
                   jax.ShapeDtypeStruct((bn, sn, TOP_K), F32),
                   jax.ShapeDtypeStruct((bn, sn, TOP_K), jnp.int32),
                   jax.ShapeDtypeStruct((8, LANES), F32)],
        scratch_shapes=[pltpu.VMEM((8, LANES), F32)],
        compiler_params=_cparams("arbitrary", "arbitrary"),
        name="moe_router",
    )(x, g.reshape(1, d), sc, sh, wr, br, tri)


def _expert_kernel(ce_ref, x_ref, wgu_ref, bgu_ref, wdn_ref, bdn_ref, rw_ref, *rest, ff, fc, off):
    o_ref, wgu_scr, wdn_scr = rest[-3:]
    c = pl.program_id(0)

    @pl.when((c == 0) | (ce_ref[c + off] != ce_ref[jnp.maximum(c + off - 1, 0)]))
    def _():
        wgu_scr[...] = wgu_ref[0, 0].astype(BF16)
        wdn_scr[...] = wdn_ref[0, 0].astype(BF16)

    x = x_ref[...]
    y = None
    for j in range(ff // fc):
        g = _dot(x, wgu_scr[:, j * fc:(j + 1) * fc]) + bgu_ref[0, 0, :, j * fc:(j + 1) * fc]
        u = _dot(x, wgu_scr[:, ff + j * fc:ff + (j + 1) * fc]) + bgu_ref[0, 0, :, ff + j * fc:ff + (j + 1) * fc]
        g = jnp.minimum(g, SWIGLU_LIMIT)
        u = jnp.clip(u, -SWIGLU_LIMIT, SWIGLU_LIMIT)
        act = g * jax.nn.sigmoid(SWIGLU_ALPHA * g) * (u + 1.0)
        part = _dot(act.astype(BF16), wdn_scr[j * fc:(j + 1) * fc, :])
        y = part if y is None else y + part
    o_ref[...] = ((y + bdn_ref[0, 0]) * rw_ref[...]).astype(o_ref.dtype)


def _expert_ffn(h, row_tok, chunk_e, layer, w_gu, b_gu, w_dn, b_dn, row_w, tm=MOE_ROWS, fc=512, groups=4):
    n_rows = row_tok.shape[0]
    d = h.shape[1]
    nl, ne, _, ff2 = w_gu.shape
    ff = ff2 // 2
    n_chunks = n_rows // tm
    assert n_chunks % groups == 0
    per = n_chunks // groups
    b_gu3, b_dn3, row_w2 = b_gu.reshape(nl, ne, 1, ff2), b_dn.reshape(nl, ne, 1, d), row_w.reshape(n_rows, 1)
    y = None
    for gi in range(groups):
        off = gi * per
        rows = h[row_tok[off * tm:(off + per) * tm]]
        in_specs = [pl.BlockSpec((tm, d), lambda c, ce: (c, 0)),
                    pl.BlockSpec((1, 1, d, ff2), lambda c, ce, off=off: (layer, ce[c + off], 0, 0)),
                    pl.BlockSpec((1, 1, 1, ff2), lambda c, ce, off=off: (layer, ce[c + off], 0, 0)),
                    pl.BlockSpec((1, 1, ff, d), lambda c, ce, off=off: (layer, ce[c + off], 0, 0)),
                    pl.BlockSpec((1, 1, 1, d), lambda c, ce, off=off: (layer, ce[c + off], 0, 0)),
                    pl.BlockSpec((tm, 1), lambda c, ce, off=off: (c + off, 0))]
        args = [chunk_e, rows, w_gu, b_gu3, w_dn, b_dn3, row_w2]
        aliases = {}
        if y is not None:
            in_specs.append(pl.BlockSpec(memory_space=pl.ANY))
            args.append(y)
            aliases = {len(args) - 1: 0}
        y = pl.pallas_call(
            functools.partial(_expert_kernel, ff=ff, fc=fc, off=off),
            grid_spec=pltpu.PrefetchScalarGridSpec(
                num_scalar_prefetch=1, grid=(per,), in_specs=in_specs,
                out_specs=pl.BlockSpec((tm, d), lambda c, ce, off=off: (c + off, 0)),
                scratch_shapes=[pltpu.VMEM((d, ff2), BF16), pltpu.VMEM((ff, d), BF16)]),
            out_shape=jax.ShapeDtypeStruct((n_rows, d), BF16),
            input_output_aliases=aliases,
            compiler_params=_cparams("arbitrary"),
            name="moe_expert_ffn",
        )(*args)
    return y


def _combine_kernel(y_ref, x_ref, g2_ref, o_ref):
    tot = y_ref[0].astype(F32)
    for k in range(1, TOP_K):
        tot = tot + y_ref[k].astype(F32)
    o_ref[0] = x_ref[0] + g2_ref[0] * tot


def _combine(y4, x, g2, tm=512):
    bn, sn, d = x.shape
    row = lambda b, i: (b, i, 0)
    nt = sn // tm
    return pl.pallas_call(
        _combine_kernel,
        grid=(bn, nt),
        in_specs=[pl.BlockSpec((TOP_K, tm, d), lambda b, i: (0, b * nt + i, 0)),
                  pl.BlockSpec((1, tm, d), row),
                  pl.BlockSpec((1, 1, d), lambda b, i: (b, 0, 0))],
        out_specs=pl.BlockSpec((1, tm, d), row),
        out_shape=jax.ShapeDtypeStruct((bn, sn, d), F32),
        compiler_params=_cparams("parallel", "parallel"),
        name="moe_combine",
    )(y4, x, g2)


def _moe(x, g, sc, sh, g2, w_router, b_router, layer, w_gu, b_gu, w_dn, b_dn):
    bn, sn, d = x.shape
    n_tok = bn * sn
    n_asg = n_tok * TOP_K
    tm = MOE_ROWS
    h, e_out, w_out, rank_out, totals = _router(x, g, sc, sh, w_router, b_router)
    e_tok = e_out.reshape(n_tok, TOP_K)
    w_flat = w_out.reshape(-1)
    counts = totals[0, :N_EXPERTS].astype(jnp.int32)
    starts = jnp.cumsum(counts) - counts
    padded = (counts + tm - 1) // tm * tm
    pad_ends = jnp.cumsum(padded)
    pad_starts = pad_ends - padded
    pos = pad_starts[e_tok] + rank_out.reshape(n_tok, TOP_K)
    n_chunks = n_asg // tm + N_EXPERTS
    chunk_start = jnp.arange(n_chunks, dtype=jnp.int32) * tm
    chunk_e = jnp.minimum(jnp.sum(chunk_start[:, None] >= pad_ends[None, :], axis=1), N_EXPERTS - 1).astype(jnp.int32)
    order = jnp.argsort(e_tok.reshape(-1))
    src = (chunk_start - (pad_starts - starts)[chunk_e])[:, None] + jnp.arange(tm, dtype=jnp.int32)[None, :]
    row_valid = (src < (starts + counts)[chunk_e][:, None]).reshape(-1)
    asg = order[jnp.clip(src, 0, n_asg - 1).reshape(-1)].astype(jnp.int32)
    row_tok = asg // TOP_K
    row_w = jnp.where(row_valid, w_flat[asg], 0.0)
    y = _expert_ffn(h.reshape(n_tok, d), row_tok, chunk_e, layer, w_gu, b_gu, w_dn, b_dn, row_w)
    return _combine(y[pos.T], x, g2)


def _norm_rope(x, g, cos, sin, bd, hd):
    sq = x * x
    hi = sq.astype(BF16)
    lo = (sq - hi.astype(F32)).astype(BF16)
    ss = _dot(hi, bd) + _dot(lo, bd)
    y = x * lax.rsqrt(ss * (1.0 / hd) + NORM_EPS) * g
    half = hd // 2
    first = (_iota2(x.shape, 1) & (hd - 1)) < half
    partner = jnp.where(first, pltpu.roll(y, LANES - half, 1), pltpu.roll(y, half, 1))
    return y * cos + partner * sin


def _prep_kernel(aq_ref, bq_ref, bk_ref, bv_ref, cq_ref, ck_ref, cv_ref, dq_ref, akc_ref, aks_ref, akw_ref,
                 dk_ref, dv_ref, c64_ref, s64_ref, c32_ref, s32_ref, gaq_ref, gak_ref, gcq_ref, gck_ref,
                 gdq_ref, gdk_ref, bd64_ref, bd32_ref,
                 oaq_ref, okc_ref, ovc_ref, okx_ref, ovs_ref, okw_ref, ovw_ref, obq_ref, obk_ref, obv_ref,
                 ocq_ref, ock_ref, ocv_ref, odq_ref, odk_ref, odv_ref, *, ts, sb_scale):
    hd, dd = HEAD_DIM, DIFF_DIM
    c64, s64, c32, s32 = c64_ref[0], s64_ref[0], c32_ref[0], s32_ref[0]
    bd64, bd32 = bd64_ref[...], bd32_ref[...]

    def slabs(ref):
        x = ref[0]
        return [x[:, c * LANES:(c + 1) * LANES] for c in range(x.shape[1] // LANES)]

    def put_heads(o_ref, c, y, width):
        per = LANES // width
        for u in range(per):
            o_ref[0, c * per + u] = y[:, u * width:(u + 1) * width].astype(o_ref.dtype)

    for src, gain, dst in ((aq_ref, gaq_ref, oaq_ref), (dq_ref, gdq_ref, odq_ref), (dk_ref, gdk_ref, odk_ref)):
        for c, x in enumerate(slabs(src)):
            put_heads(dst, c, _norm_rope(x, gain[:, c * LANES:(c + 1) * LANES], c64, s64, bd64, hd), hd)
    kc = _norm_rope(akc_ref[0], gak_ref[0:1, :], c64, s64, bd64, hd)
    ks = _norm_rope(aks_ref[0], gak_ref[1:2, :], c64, s64, bd64, hd)
    kw = _norm_rope(akw_ref[0], gak_ref[2:3, :], c64, s64, bd64, hd)
    okc_ref[0] = kc[:, :hd].astype(BF16)
    ovc_ref[0] = akc_ref[0][:, hd:].astype(BF16)
    nsel = okx_ref.shape[2] - hd
    blk = (pl.program_id(1) * ts + _iota2((ts, nsel), 0)) >> (NSA_SEL_BLOCK.bit_length() - 1)
    okx_ref[0, :, :nsel] = jnp.where(blk == _iota2((ts, nsel), 1), 1.0, 0.0).astype(BF16)
    okx_ref[0, :, nsel:] = ks[:, :hd].astype(BF16)
    ovs_ref[0] = aks_ref[0][:, hd:].astype(BF16)
    okw_ref[0, 0] = kw[:, :hd].astype(BF16)
    ovw_ref[0, 0] = akw_ref[0][:, hd:].astype(BF16)
    for c, x in enumerate(slabs(bq_ref)):
        put_heads(obq_ref, c, x * sb_scale, hd)
    for src, dst in ((bk_ref, obk_ref), (bv_ref, obv_ref), (cv_ref, ocv_ref)):
        for c, x in enumerate(slabs(src)):
            put_heads(dst, c, x, hd)
    put_heads(odv_ref, 0, dv_ref[0], hd)
    for src, gain, dst in ((cq_ref, gcq_ref, ocq_ref), (ck_ref, gck_ref, ock_ref)):
        for c, x in enumerate(slabs(src)):
            put_heads(dst, c, _norm_rope(x, gain[:, c * LANES:(c + 1) * LANES], c32, s32, bd32, dd), dd)


def _rope_tables(positions, hd):
    half = hd // 2
    inv = ROPE_THETA ** (-jnp.arange(half, dtype=F32) * 2.0 / hd)
    ang = positions.astype(F32)[..., None] * inv
    cos, sin = jnp.cos(ang), jnp.sin(ang)
    reps = LANES // hd
    return (jnp.tile(jnp.concatenate([cos, cos], axis=-1), (1, 1, reps)),
            jnp.tile(jnp.concatenate([-sin, sin], axis=-1), (1, 1, reps)))


def _prep(proj, cols, tables, p, n_sel, ts=512):
    bn, sn, _ = proj.shape
    hd, dd = HEAD_DIM, DIFF_DIM
    scale = hd ** -0.5
    c64, s64, c32, s32 = tables

    def cspec(name):
        off, width = cols[name]
        assert off % width == 0
        return pl.BlockSpec((1, ts, width), lambda b, i, blk=off // width: (b, i, blk))

    def tile_gain(g, reps, mult=1.0):
        return (jnp.tile(g.astype(F32), reps) * mult).reshape(1, -1)

    ones = jnp.ones((hd,), F32)
    gak = jnp.stack([jnp.concatenate([p["nsa_kn"][j].astype(F32), ones]) for j in range(3)])
    gains = [tile_gain(p["nsa_qn"], NSA_HEADS, scale), gak,
             tile_gain(p["dif_qn"], 2 * DIFF_HEADS, dd ** -0.5), tile_gain(p["dif_kn"], 2 * DIFF_HEADS),
             tile_gain(p["swa_qn"], SWA_HEADS, scale), tile_gain(p["swa_kn"], SWA_KV_HEADS)]
    lane = np.arange(LANES)
    bd64 = jnp.asarray((lane[:, None] // hd == lane[None, :] // hd).astype(np.float32), BF16)
    bd32 = jnp.asarray((lane[:, None] // dd == lane[None, :] // dd).astype(np.float32), BF16)
    names = ("a_q", "b_q", "b_k", "b_v", "c_q", "c_k", "c_v", "d_q", "a_kcvc", "a_ksvs", "a_kwvw", "d_k", "d_v")
    tab = pl.BlockSpec((1, ts, LANES), lambda b, i: (b, i, 0))
    full = lambda a: pl.BlockSpec(a.shape, lambda b, i: (0,) * a.ndim)

    def hm(nh, w):
        return (jax.ShapeDtypeStruct((bn, nh, sn, w), BF16), pl.BlockSpec((1, nh, ts, w), lambda b, i: (b, 0, i, 0)))

    def tm_(w):
        return (jax.ShapeDtypeStruct((bn, sn, w), BF16), pl.BlockSpec((1, ts, w), lambda b, i: (b, i, 0)))

    outs = [hm(NSA_HEADS, hd), tm_(hd), tm_(hd), tm_(n_sel + hd), tm_(hd), hm(1, hd), hm(1, hd),
            hm(SB_HEADS, hd), hm(SB_HEADS, hd), hm(SB_HEADS, hd),
            hm(2 * DIFF_HEADS, dd), hm(2 * DIFF_HEADS, dd), hm(DIFF_HEADS, 2 * dd),
            hm(SWA_HEADS, hd), hm(SWA_KV_HEADS, hd), hm(SWA_KV_HEADS, hd)]
    consts = gains + [bd64, bd32]
    res = pl.pallas_call(
        functools.partial(_prep_kernel, ts=ts, sb_scale=-scale * LOG2E),
        grid=(bn, sn // ts),
        in_specs=[cspec(n) for n in names] + [tab] * 4 + [full(a) for a in consts],
        out_specs=[o[1] for o in outs],
        out_shape=[o[0] for o in outs],
        compiler_params=_cparams("parallel", "parallel"),
        name="mixer_prep",
    )(*([proj] * len(names)), c64, s64, c32, s32, *consts)
    keys = ("a_q", "kc", "vc", "kx", "vs", "kw", "vw", "b_q", "b_k", "b_v", "c_q", "c_k", "c_v", "d_q", "d_k", "d_v")
    return dict(zip(keys, res))


def _layer_columns(d):
    cols = {}
    off = 0
    for name, width in (("br_g", N_BRANCH * d), ("ga", 3 * NSA_HEADS * HEAD_DIM), ("a_q", 256),
                        ("b_q", 256), ("b_k", 256), ("b_v", 256), ("c_q", 256), ("c_k", 256), ("c_v", 256),
                        ("d_q", 256), ("a_kcvc", 128), ("a_ksvs", 128), ("a_kwvw", 128), ("d_k", 128), ("d_v", 128)):
        cols[name] = (off, width)
        off += width
    return cols, off


def _reorder_w_in(w_in, d, n_pad):
    ref_splits = (256, 128, 128, 128, 12, 256, 256, 256, 256, 256, 256, 256, 128, 128, N_BRANCH * d)
    names = ("a_q", "a_kcvc", "a_ksvs", "a_kwvw", "a_g", "b_q", "b_k", "b_v",
             "c_q", "c_k", "c_v", "d_q", "d_k", "d_v", "br_g")
    starts = np.cumsum((0,) + ref_splits)
    src = {n: (int(starts[i]), ref_splits[i]) for i, n in enumerate(names)}
    cols, total = _layer_columns(d)
    idx = np.zeros((n_pad,), np.int32)
    keep = np.zeros((n_pad,), np.float32)
    for name, (off, width) in cols.items():
        if name == "ga":
            g0 = src["a_g"][0]
            for j in range(3):
                for h in range(NSA_HEADS):
                    base = off + j * NSA_HEADS * HEAD_DIM + h * HEAD_DIM
                    idx[base:base + HEAD_DIM] = g0 + h * 3 + j
        else:
            idx[off:off + width] = src[name][0] + np.arange(width)
        keep[off:off + width] = 1.0
    w = w_in[:, idx] * keep[None, :]
    return w.astype(BF16)


def _mixer_layer(x, positions, tables, mod, p, lam_init):
    bn, sn, d = x.shape
    sh1, sc1, g1 = mod[0], mod[1], mod[2]
    cols, total = _layer_columns(d)
    n_pad = -(-total // 1536) * 1536
    w_in = _reorder_w_in(p["w_in"], d, n_pad)
    proj = _in_projection(x, p["norm1"], sc1, sh1, w_in)
    n_cmp = (sn - NSA_CMP_LEN) // NSA_CMP_STRIDE + 1
    n_sel = sn // NSA_SEL_BLOCK
    n_top = min(NSA_N_SEL, n_sel)
    t = _prep(proj, cols, tables, p, n_sel)
    k_cmp = _compress(t["kc"], p["nsa_pe_k"], p["nsa_w1_k"], p["nsa_w2_k"]).astype(BF16)
    v_cmp = _compress(t["vc"], p["nsa_pe_v"], p["nsa_w1_v"], p["nsa_w2_v"]).astype(BF16)
    cmp_start = np.arange(sn // NSA_CMP_STRIDE) * NSA_CMP_STRIDE
    sel_start = np.arange(n_sel) * NSA_SEL_BLOCK
    overlap = ((cmp_start[:, None] <= (sel_start + NSA_SEL_BLOCK - 1)[None, :]) &
               ((cmp_start + NSA_CMP_LEN - 1)[:, None] >= sel_start[None, :]) &
               (np.arange(sn // NSA_CMP_STRIDE) < n_cmp)[:, None]).astype(np.float32)
    o_c, bias = _nsa_cmp(t["a_q"], k_cmp, v_cmp, jnp.asarray(overlap, BF16), n_cmp, n_top)
    o_s = _nsa_sel(t["a_q"], t["kx"], t["vs"], bias)
    o_w = _banded_attention(t["a_q"], t["kw"], t["vw"], NSA_WINDOW)
    o_b = _stick_breaking(t["b_q"], t["b_k"], t["b_v"])
    lam = (jnp.exp(jnp.sum(p["dif_lq1"] * p["dif_lk1"])) - jnp.exp(jnp.sum(p["dif_lq2"] * p["dif_lk2"]))
           + lam_init)
    o_cd = _diff_attention(t["c_q"], t["c_k"], t["c_v"], p["dif_subln"], lam, jnp.asarray(1.0 - lam_init, F32))
    o_d = _banded_attention(t["d_q"], t["d_k"], t["d_v"], SWA_WINDOW, sinks=p["swa_sinks"])
    return _merge(proj, o_c, o_s, o_w, o_b, o_cd, o_d, p["w_up"].astype(BF16), p["w_out"].astype(BF16),
                  x, g1, cols["ga"][0])


def kernel(x, c, positions, w_ada, b_ada, norm1, norm2, w_in, nsa_qn, nsa_kn, nsa_pe_k, nsa_w1_k, nsa_w2_k,
           nsa_pe_v, nsa_w1_v, nsa_w2_v, dif_qn, dif_kn, dif_lq1, dif_lk1, dif_lq2, dif_lk2, dif_subln,
           swa_qn, swa_kn, swa_sinks, w_up, w_out, w_router, b_router, w_gu, b_gu, w_dn, b_dn):
    bn, sn, d = x.shape
    depth = w_ada.shape[0]
    c_pad = jnp.zeros((8, d), F32).at[:bn].set(c)
    tables = _rope_tables(positions, HEAD_DIM) + _rope_tables(positions, DIFF_DIM)
    for l in range(depth):
        lam_init = 0.8 - 0.6 * math.exp(-0.3 * l)
        mod = _linear(c_pad, w_ada[l], b_ada[l], tn=512, precision=lax.Precision.HIGHEST)[:bn]
        mod = mod.reshape(bn, 6, 1, d).transpose(1, 0, 2, 3)
        p = dict(norm1=norm1[l], w_in=w_in[l], nsa_qn=nsa_qn[l], nsa_kn=nsa_kn[l], nsa_pe_k=nsa_pe_k[l],
                 nsa_w1_k=nsa_w1_k[l], nsa_w2_k=nsa_w2_k[l], nsa_pe_v=nsa_pe_v[l], nsa_w1_v=nsa_w1_v[l],
                 nsa_w2_v=nsa_w2_v[l], dif_qn=dif_qn[l], dif_kn=dif_kn[l], dif_lq1=dif_lq1[l],
                 dif_lk1=dif_lk1[l], dif_lq2=dif_lq2[l], dif_lk2=dif_lk2[l], dif_subln=dif_subln[l],
                 swa_qn=swa_qn[l], swa_kn=swa_kn[l], swa_sinks=swa_sinks[l], w_up=w_up[l], w_out=w_out[l])
        x = _mixer_layer(x, positions, tables, mod, p, lam_init)
        x = _moe(x, norm2[l], mod[4], mod[3], mod[5], w_router[l], b_router[l], l, w_gu, b_gu, w_dn, b_dn)
    return x
```

```python
import functools
import math

import numpy as np
import jax
import jax.numpy as jnp
from jax import lax
from jax.experimental import pallas as pl
from jax.experimental.pallas import tpu as pltpu

F32 = jnp.float32
BF16 = jnp.bfloat16

HEAD_DIM = 64
ROPE_THETA = 10000.0
NORM_EPS = 1e-6
NEG_INF = -1e30
KNOCKOUT = -3e38
N_BRANCH = 4

NSA_HEADS = 4
NSA_CMP_LEN = 32
NSA_CMP_STRIDE = 16
NSA_SEL_BLOCK = 64
NSA_N_SEL = 16
NSA_WINDOW = 512
NSA_FORCED_SCORE = 1e4

SB_HEADS = 4
DIFF_HEADS = 4
DIFF_DIM = 32
SWA_HEADS = 4
SWA_KV_HEADS = 2
SWA_WINDOW = 128

N_EXPERTS = 32
TOP_K = 4
SWIGLU_ALPHA = 1.702
SWIGLU_LIMIT = 7.0

LANES = 128
LOG2E = 1.4426950408889634
MOE_ROWS = 512
VMEM_LIMIT = 56 * 1024 * 1024


def _cparams(*sem):
    return pltpu.CompilerParams(dimension_semantics=sem, vmem_limit_bytes=VMEM_LIMIT)


def _dot(a, b):
    return jnp.dot(a, b, preferred_element_type=F32)


def _dot_nt(a, b):
    return lax.dot_general(a, b, (((1,), (1,)), ((), ())), preferred_element_type=F32)


def _iota2(shape, dim):
    return lax.broadcasted_iota(jnp.int32, shape, dim)


def _linear_kernel(x_ref, w_ref, b_ref, o_ref, *, precision):
    o_ref[...] = jnp.dot(x_ref[...], w_ref[...], preferred_element_type=F32,
                         precision=precision) + b_ref[...]


def _linear(x, w, b, tn, precision=None):
    m, k = x.shape
    n = w.shape[1]
    return pl.pallas_call(
        functools.partial(_linear_kernel, precision=precision),
        grid=(n // tn,),
        in_specs=[pl.BlockSpec((m, k), lambda j: (0, 0)),
                  pl.BlockSpec((k, tn), lambda j: (0, j)),
                  pl.BlockSpec((1, tn), lambda j: (0, j))],
        out_specs=pl.BlockSpec((m, tn), lambda j: (0, j)),
        out_shape=jax.ShapeDtypeStruct((m, n), F32),
        compiler_params=_cparams("arbitrary"),
        name="linear",
    )(x, w, b.reshape(1, n))


def _cmp_mlp_kernel(a_ref, b_ref, pe_ref, w2_ref, o_ref):
    hid = jax.nn.gelu(a_ref[...] + b_ref[...] + pe_ref[...])
    o_ref[...] = _dot(hid.astype(BF16), w2_ref[...])


def _compress(t, pe, w1, w2):
    bn, sn, dh = t.shape
    st = NSA_CMP_STRIDE
    half = st * dh
    nb = sn // st
    t16 = t.reshape(bn * nb, half).astype(BF16)
    w1cat = jnp.concatenate([w1[:half], w1[half:]], axis=1).astype(BF16)
    hidden = w1.shape[1]
    ab = _linear(t16, w1cat, jnp.zeros((2 * hidden,), F32), tn=2 * hidden)
    ab = ab.reshape(bn, nb, 2 * hidden)
    a = ab[:, :, :hidden]
    b_next = jnp.concatenate([ab[:, 1:, hidden:], jnp.zeros((bn, 1, hidden), F32)], axis=1)
    pe_term = jnp.dot(pe.reshape(1, NSA_CMP_LEN * dh), w1, precision=lax.Precision.HIGHEST)
    rows = bn * nb
    tm = min(512, rows)
    out = pl.pallas_call(
        _cmp_mlp_kernel,
        grid=(rows // tm,),
        in_specs=[pl.BlockSpec((tm, hidden), lambda i: (i, 0)),
                  pl.BlockSpec((tm, hidden), lambda i: (i, 0)),
                  pl.BlockSpec((1, hidden), lambda i: (0, 0)),
                  pl.BlockSpec((hidden, dh), lambda i: (0, 0))],
        out_specs=pl.BlockSpec((tm, dh), lambda i: (i, 0)),
        out_shape=jax.ShapeDtypeStruct((rows, dh), F32),
        compiler_params=_cparams("parallel"),
        name="cmp_mlp",
    )(a.reshape(rows, hidden), b_next.reshape(rows, hidden), pe_term, w2.astype(BF16))
    return out.reshape(bn, nb, dh)


def _norm_mod(x, g, sc, sh):
    r = lax.rsqrt(jnp.mean(x * x, axis=-1, keepdims=True) + NORM_EPS)
    return (x * r * g) * (1.0 + sc) + sh


def _proj_kernel(x_ref, g_ref, sc_ref, sh_ref, w_ref, o_ref, h_scr):
    @pl.when(pl.program_id(2) == 0)
    def _():
        h_scr[...] = _norm_mod(x_ref[0], g_ref[...], sc_ref[0], sh_ref[0]).astype(BF16)

    o_ref[0] = _dot(h_scr[...], w_ref[...]).astype(o_ref.dtype)


def _in_projection(x, g, sc, sh, w, tm=1024, tn=1536):
    bn, sn, d = x.shape
    n = w.shape[1]
    tm = min(tm, sn)
    return pl.pallas_call(
        _proj_kernel,
        grid=(bn, sn // tm, n // tn),
        in_specs=[pl.BlockSpec((1, tm, d), lambda b, i, j: (b, i, 0)),
                  pl.BlockSpec((1, d), lambda b, i, j: (0, 0)),
                  pl.BlockSpec((1, 1, d), lambda b, i, j: (b, 0, 0)),
                  pl.BlockSpec((1, 1, d), lambda b, i, j: (b, 0, 0)),
                  pl.BlockSpec((d, tn), lambda b, i, j: (0, j))],
        out_specs=pl.BlockSpec((1, tm, tn), lambda b, i, j: (b, i, j)),
        out_shape=jax.ShapeDtypeStruct((bn, sn, n), BF16),
        scratch_shapes=[pltpu.VMEM((tm, d), BF16)],
        compiler_params=_cparams("parallel", "parallel", "arbitrary"),
        name="in_proj",
    )(x, g.reshape(1, d), sc, sh, w)


def _banded_kernel(*refs, tile, window, has_sink):
    if has_sink:
        sink_ref, q_ref, kp_ref, kc_ref, vp_ref, vc_ref, o_ref = refs
    else:
        q_ref, kp_ref, kc_ref, vp_ref, vc_ref, o_ref = refs
    i = pl.program_id(2)
    w = window
    upper = _iota2((w, w), 1) > _iota2((w, w), 0)
    first_bias = jnp.where(i > 0, 0.0, NEG_INF)
    for u in range(tile // w):
        q = q_ref[0, 0, u * w:(u + 1) * w, :]
        if u == 0:
            k_prev, v_prev = kp_ref[0, 0], vp_ref[0, 0]
        else:
            k_prev, v_prev = kc_ref[0, 0, (u - 1) * w:u * w, :], vc_ref[0, 0, (u - 1) * w:u * w, :]
        k_cur, v_cur = kc_ref[0, 0, u * w:(u + 1) * w, :], vc_ref[0, 0, u * w:(u + 1) * w, :]
        s_prev = _dot_nt(q, k_prev)
        if u == 0:
            s_prev = s_prev + first_bias
        s = jnp.where(upper, s_prev, _dot_nt(q, k_cur))
        m = jnp.max(s, axis=-1, keepdims=True)
        if has_sink:
            sink = sink_ref[pl.program_id(1)]
            m = jnp.maximum(m, sink)
        p = jnp.exp(s - m)
        den = jnp.sum(p, axis=-1, keepdims=True)
        if has_sink:
            den = den + jnp.exp(sink - m)
        else:
            den = jnp.maximum(den, 1e-30)
        o = (_dot(jnp.where(upper, p, 0.0).astype(BF16), v_prev)
             + _dot(jnp.where(upper, 0.0, p).astype(BF16), v_cur))
        o_ref[0, 0, u * w:(u + 1) * w, :] = o / den


def _banded_attention(q, k, v, window, sinks=None, tile=1024):
    bn, hq, sn, d = q.shape
    grp = hq // k.shape[1]
    tile = min(tile, sn)
    assert tile % window == 0 and sn % tile == 0
    per = tile // window
    has_sink = sinks is not None
    qspec = pl.BlockSpec((1, 1, tile, d), lambda b, h, i: (b, h, i, 0))
    prev = pl.BlockSpec((1, 1, window, d), lambda b, h, i: (b, h // grp, jnp.maximum(i * per - 1, 0), 0))
    cur = pl.BlockSpec((1, 1, tile, d), lambda b, h, i: (b, h // grp, i, 0))
    in_specs = [qspec, prev, cur, prev, cur]
    args = [q, k, k, v, v]
    if has_sink:
        in_specs = [pl.BlockSpec(memory_space=pltpu.SMEM)] + in_specs
        args = [sinks.astype(F32)] + args
    return pl.pallas_call(
        functools.partial(_banded_kernel, tile=tile, window=window, has_sink=has_sink),
        grid=(bn, hq, sn // tile),
        in_specs=in_specs,
        out_specs=pl.BlockSpec((1, 1, tile, d), lambda b, h, i: (b, h, i, 0)),
        out_shape=jax.ShapeDtypeStruct((bn, hq, sn, d), F32),
        compiler_params=_cparams("parallel", "parallel", "parallel"),
        name="banded_attn",
    )(*args)


def _nsa_cmp_kernel(q_ref, kc_ref, vct_ref, ovt_ref, oct_ref, bias_ref, *, tq, n_cmp, n_top, heads):
    i = pl.program_id(1)
    ncp = kc_ref.shape[1]
    nsel = ovt_ref.shape[0]
    dh = q_ref.shape[3]
    t = i * tq + _iota2((ncp, tq), 1)
    n = _iota2((ncp, tq), 0)
    valid = (n * NSA_CMP_STRIDE + (NSA_CMP_LEN - 1) <= t) & (n < n_cmp)
    kc = kc_ref[0]
    vct = vct_ref[0]
    psum = jnp.zeros((ncp, tq), F32)
    for h in range(heads):
        s = jnp.where(valid, _dot_nt(kc, q_ref[0, h]), NEG_INF)
        m = jnp.max(s, axis=0, keepdims=True)
        p = jnp.where(valid, jnp.exp(s - m), 0.0)
        p = p / jnp.maximum(jnp.sum(p, axis=0, keepdims=True), 1e-30)
        oct_ref[0, h * dh:(h + 1) * dh, :] = _dot(vct, p.astype(BF16))
        psum = psum + p
    hi = psum.astype(BF16)
    lo = (psum - hi.astype(F32)).astype(BF16)
    imp = _dot(ovt_ref[...], hi) + _dot(ovt_ref[...], lo)

    tt = i * tq + _iota2((nsel, tq), 1)
    blk = _iota2((nsel, tq), 0)
    cur = tt >> (NSA_SEL_BLOCK.bit_length() - 1)
    forced = (blk == 0) | (blk == cur) | (blk == cur - 1)
    valid_s = blk * NSA_SEL_BLOCK <= tt
    score = jnp.where(forced, NSA_FORCED_SCORE, jnp.where(valid_s, imp, -1.0))
    blk_f = blk.astype(F32)

    def pick(_, carry):
        score, sel = carry
        m = jnp.max(score, axis=0, keepdims=True)
        first = jnp.min(jnp.where(score == m, blk_f, float(nsel)), axis=0, keepdims=True)
        hit = blk_f == first
        return jnp.where(hit, KNOCKOUT, score), jnp.where(hit, 0.0, sel)

    _, bias = lax.fori_loop(0, n_top, pick, (score, jnp.full((nsel, tq), NEG_INF, F32)))
    bias_ref[0] = bias.astype(BF16)


def _nsa_cmp(q, kc, vc, overlap, n_cmp, n_top, tq=256):
    bn, heads, sn, dh = q.shape
    ncp = kc.shape[1]
    nsel = overlap.shape[1]
    oct, bias_t = pl.pallas_call(
        functools.partial(_nsa_cmp_kernel, tq=tq, n_cmp=n_cmp, n_top=n_top, heads=heads),
        grid=(bn, sn // tq),
        in_specs=[pl.BlockSpec((1, heads, tq, dh), lambda b, i: (b, 0, i, 0)),
                  pl.BlockSpec((1, ncp, dh), lambda b, i: (b, 0, 0)),
                  pl.BlockSpec((1, dh, ncp), lambda b, i: (b, 0, 0)),
                  pl.BlockSpec((nsel, ncp), lambda b, i: (0, 0))],
        out_specs=[pl.BlockSpec((1, heads * dh, tq), lambda b, i: (b, 0, i)),
                   pl.BlockSpec((1, nsel, tq), lambda b, i: (b, 0, i))],
        out_shape=[jax.ShapeDtypeStruct((bn, heads * dh, sn), F32),
                   jax.ShapeDtypeStruct((bn, nsel, sn), BF16)],
        compiler_params=_cparams("parallel", "parallel"),
        name="nsa_cmp_topk",
    )(q, kc, vc.transpose(0, 2, 1), overlap.T)
    return oct.transpose(0, 2, 1), bias_t.transpose(0, 2, 1)


def _nsa_sel_kernel(q_ref, kx_ref, v_ref, bias_ref, o_ref, qx_scr, *, tq, tk, heads):
    i = pl.program_id(1)
    dh = v_ref.shape[2]
    bias = bias_ref[0]
    for h in range(heads):
        qx_scr[h] = jnp.concatenate([bias, q_ref[0, h]], axis=1)

    def tile(j, state, diagonal):
        start = pl.multiple_of(j * tk, tk)
        kx = kx_ref[0, pl.ds(start, tk), :]
        v = v_ref[0, pl.ds(start, tk), :]
        if diagonal:
            causal = (start + _iota2((tq, tk), 1)) <= (i * tq + _iota2((tq, tk), 0))
        new_state = []
        for h in range(heads):
            m_old, l_old, acc_old = state[h]
            s = _dot_nt(qx_scr[h], kx)
            if diagonal:
                s = jnp.where(causal, s, NEG_INF)
            m_new = jnp.maximum(m_old, jnp.max(s, axis=-1, keepdims=True))
            alpha = jnp.exp(m_old - m_new)
            p = jnp.exp(s - m_new)
            l_new = alpha * l_old + jnp.sum(p, axis=-1, keepdims=True)
            acc_new = alpha * acc_old + _dot(p.astype(BF16), v)
            new_state.append((m_new, l_new, acc_new))
        return tuple(new_state)

    init = tuple((jnp.full((tq, 1), NEG_INF, F32), jnp.zeros((tq, 1), F32), jnp.zeros((tq, dh), F32))
                 for _ in range(heads))
    n_full = (i * tq) // tk
    state = lax.fori_loop(0, n_full, lambda j, st: tile(j, st, False), init)
    state = tile(n_full, state, True)
    for h in range(heads):
        o_ref[0, :, h * dh:(h + 1) * dh] = state[h][2] / jnp.maximum(state[h][1], 1e-30)


def _nsa_sel(q, kx, v, bias, tq=1024, tk=1024):
    bn, heads, sn, dh = q.shape
    nsel = bias.shape[2]
    tq, tk = min(tq, sn), min(tk, sn)
    assert tk % tq == 0 and sn % tk == 0
    return pl.pallas_call(
        functools.partial(_nsa_sel_kernel, tq=tq, tk=tk, heads=heads),
        grid=(bn, sn // tq),
        in_specs=[pl.BlockSpec((1, heads, tq, dh), lambda b, i: (b, 0, i, 0)),
                  pl.BlockSpec((1, sn, nsel + dh), lambda b, i: (b, 0, 0)),
                  pl.BlockSpec((1, sn, dh), lambda b, i: (b, 0, 0)),
                  pl.BlockSpec((1, tq, nsel), lambda b, i: (b, i, 0))],
        out_specs=pl.BlockSpec((1, tq, heads * dh), lambda b, i: (b, i, 0)),
        out_shape=jax.ShapeDtypeStruct((bn, sn, heads * dh), F32),
        scratch_shapes=[pltpu.VMEM((heads, tq, nsel + dh), BF16)],
        compiler_params=_cparams("parallel", "arbitrary"),
        name="nsa_selected_attn",
    )(q, kx, v, bias)


def _sb_kernel(q_ref, k_ref, v_ref, u_ref, o_ref, *, tq, tk):
    i = pl.program_id(2)
    q = q_ref[0, 0]
    per_q = tq // tk

    def tile(jj, carry, first_row):
        diagonal = first_row is not None
        r0 = first_row if diagonal else 0
        start = pl.multiple_of(jj * tk, tk)
        nz = _dot_nt(q[r0:], k_ref[0, 0, pl.ds(start, tk), :])
        neg_abs = lax.bitcast_convert_type(lax.bitcast_convert_type(nz, jnp.uint32) | jnp.uint32(0x80000000), F32)
        log_keep = jnp.minimum(nz, 0.0) - jnp.log2(1.0 + jnp.exp2(neg_abs))
        if diagonal:
            strict = _iota2(nz.shape, 1) < _iota2(nz.shape, 0)
            log_keep = jnp.where(strict, log_keep, 0.0)
        cum = _dot(log_keep.astype(BF16), u_ref[...])
        a = jnp.exp2(cum + jnp.concatenate([carry[r0:]] * (tk // LANES), axis=1) - nz)
        if diagonal:
            a = jnp.where(strict, a, 0.0)
        out = _dot(a.astype(BF16), v_ref[0, 0, pl.ds(start, tk), :])
        new_carry = carry[r0:] + jnp.broadcast_to(cum[:, 0:1], (tq - r0, LANES))
        if r0:
            out = jnp.concatenate([jnp.zeros((r0, out.shape[1]), F32), out], axis=0)
            new_carry = jnp.concatenate([carry[:r0], new_carry], axis=0)
        return out, new_carry

    def group(first, carry, diagonal):
        total = None
        for r in range(per_q):
            out, carry = tile(first - r, carry, (per_q - 1 - r) * tk if diagonal else None)
            total = out if total is None else total + out
        return total, carry

    acc, carry = group(i * per_q + per_q - 1, jnp.zeros((tq, LANES), F32), True)

    def body(p, state):
        acc, carry = state
        out, carry = group((i - p) * per_q - 1, carry, False)
        return acc + out, carry

    acc, _ = lax.fori_loop(0, i, body, (acc, carry))
    o_ref[0, 0] = acc


def _stick_breaking(q, k, v, tq=1024, tk=256):
    bn, heads, sn, dh = q.shape
    tq = min(tq, sn)
    incl = (np.arange(tk)[:, None] >= np.arange(tk)[None, :]).astype(np.float32)
    u = jnp.asarray(incl, BF16)
    kv = pl.BlockSpec((1, 1, sn, dh), lambda b, h, i: (b, h, 0, 0))
    return pl.pallas_call(
        functools.partial(_sb_kernel, tq=tq, tk=tk),
        grid=(bn, heads, sn // tq),
        in_specs=[pl.BlockSpec((1, 1, tq, dh), lambda b, h, i: (b, h, i, 0)), kv, kv,
                  pl.BlockSpec((tk, tk), lambda b, h, i: (0, 0))],
        out_specs=pl.BlockSpec((1, 1, tq, dh), lambda b, h, i: (b, h, i, 0)),
        out_shape=jax.ShapeDtypeStruct((bn, heads, sn, dh), F32),
        compiler_params=_cparams("parallel", "parallel", "arbitrary"),
        name="stick_breaking_attn",
    )(q, k, v, u)


def _diff_kernel(sc_ref, q_ref, k_ref, v_ref, g_ref, o_ref, *, tq, tk):
    i = pl.program_id(2)
    dv = v_ref.shape[3]

    def tile(j, state, diagonal):
        start = pl.multiple_of(j * tk, tk)
        v = v_ref[0, 0, pl.ds(start, tk), :]
        if diagonal:
            causal = (start + _iota2((tq, tk), 1)) <= (i * tq + _iota2((tq, tk), 0))
        new_state = []
        for mi in range(2):
            m_old, l_old, acc_old = state[mi]
            s = _dot_nt(q_ref[0, mi], k_ref[0, mi, pl.ds(start, tk), :])
            if diagonal:
                s = jnp.where(causal, s, NEG_INF)
            m_new = jnp.maximum(m_old, jnp.max(s, axis=-1, keepdims=True))
            alpha = jnp.exp(m_old - m_new)
            p = jnp.exp(s - m_new)
            l_new = alpha * l_old + jnp.sum(p, axis=-1, keepdims=True)
            acc_new = alpha * acc_old + _dot(p.astype(BF16), v)
            new_state.append((m_new, l_new, acc_new))
        return tuple(new_state)

    init = tuple((jnp.full((tq, 1), NEG_INF, F32), jnp.zeros((tq, 1), F32), jnp.zeros((tq, dv), F32))
                 for _ in range(2))
    n_full = (i * tq) // tk
    state = lax.fori_loop(0, n_full, lambda j, st: tile(j, st, False), init)
    state = tile(n_full, state, True)
    lam = sc_ref[0]
    post = sc_ref[1]
    o = (state[0][2] / jnp.maximum(state[0][1], 1e-30)
         - lam * (state[1][2] / jnp.maximum(state[1][1], 1e-30)))
    r = lax.rsqrt(jnp.mean(o * o, axis=-1, keepdims=True) + NORM_EPS)
    o_ref[0, 0] = (o * r * g_ref[...]) * post


def _diff_attention(q, k, v, subln, lam, post, tq=1024, tk=1024):
    bn, h2, sn, dd = q.shape
    heads = h2 // 2
    dv = v.shape[3]
    tq, tk = min(tq, sn), min(tk, sn)
    assert tk % tq == 0 and sn % tk == 0
    scal = jnp.stack([lam, post]).astype(F32)
    return pl.pallas_call(
        functools.partial(_diff_kernel, tq=tq, tk=tk),
        grid=(bn, heads, sn // tq),
        in_specs=[pl.BlockSpec(memory_space=pltpu.SMEM),
                  pl.BlockSpec((1, 2, tq, dd), lambda b, h, i: (b, h, i, 0)),
                  pl.BlockSpec((1, 2, sn, dd), lambda b, h, i: (b, h, 0, 0)),
                  pl.BlockSpec((1, 1, sn, dv), lambda b, h, i: (b, h, 0, 0)),
                  pl.BlockSpec((1, dv), lambda b, h, i: (0, 0))],
        out_specs=pl.BlockSpec((1, 1, tq, dv), lambda b, h, i: (b, h, i, 0)),
        out_shape=jax.ShapeDtypeStruct((bn, heads, sn, dv), F32),
        compiler_params=_cparams("parallel", "parallel", "arbitrary"),
        name="diff_attn",
    )(scal, q, k, v, subln.reshape(1, dv).astype(F32))


def _merge_kernel(brg_ref, ga0_ref, ga1_ref, ga2_ref, oc_ref, os_ref, ow_ref, ob_ref, ocd_ref, od_ref,
                  wup_ref, wout_ref, x_ref, g1_ref, o_ref, *, d):
    def heads(ref):
        return jnp.concatenate([ref[0, h] for h in range(ref.shape[1])], axis=1)

    def gate(x):
        return jax.nn.sigmoid(x.astype(F32))

    o_a = gate(ga0_ref[0]) * oc_ref[0] + gate(ga1_ref[0]) * os_ref[0] + gate(ga2_ref[0]) * heads(ow_ref)
    branches = (o_a, heads(ob_ref), heads(ocd_ref), heads(od_ref))
    merged = None
    for bi, o in enumerate(branches):
        term = gate(brg_ref[0, :, bi * d:(bi + 1) * d]) * _dot(o.astype(BF16), wup_ref[bi])
        merged = term if merged is None else merged + term
    y = _dot(merged.astype(BF16), wout_ref[...])
    o_ref[0] = x_ref[0] + g1_ref[0] * y


def _merge(proj, o_c, o_s, o_w, o_b, o_cd, o_d, w_up, w_out, x, g1, ga_col, tm=256):
    bn, sn, d = x.shape
    bw = o_c.shape[2]
    nh, dh = o_b.shape[1], o_b.shape[3]
    assert ga_col % bw == 0
    gblk = ga_col // bw
    row = lambda b, i: (b, i, 0)
    bspec = pl.BlockSpec((1, tm, bw), row)
    hspec = pl.BlockSpec((1, nh, tm, dh), lambda b, i: (b, 0, i, 0))
    return pl.pallas_call(
        functools.partial(_merge_kernel, d=d),
        grid=(bn, sn // tm),
        in_specs=[pl.BlockSpec((1, tm, N_BRANCH * d), row),
                  pl.BlockSpec((1, tm, bw), lambda b, i: (b, i, gblk)),
                  pl.BlockSpec((1, tm, bw), lambda b, i: (b, i, gblk + 1)),
                  pl.BlockSpec((1, tm, bw), lambda b, i: (b, i, gblk + 2)),
                  bspec, bspec, hspec, hspec, hspec, hspec,
                  pl.BlockSpec((N_BRANCH, bw, d), lambda b, i: (0, 0, 0)),
                  pl.BlockSpec((d, d), lambda b, i: (0, 0)),
                  pl.BlockSpec((1, tm, d), row),
                  pl.BlockSpec((1, 1, d), lambda b, i: (b, 0, 0))],
        out_specs=pl.BlockSpec((1, tm, d), row),
        out_shape=jax.ShapeDtypeStruct((bn, sn, d), F32),
        compiler_params=_cparams("parallel", "parallel"),
        name="branch_merge",
    )(proj, proj, proj, proj, o_c, o_s, o_w, o_b, o_cd, o_d, w_up, w_out, x, g1)


def _router_kernel(x_ref, g_ref, sc_ref, sh_ref, wr_ref, br_ref, tri_ref, h_ref, e_ref, w_ref, rank_ref, cnt_ref,
                   run_scr):
    @pl.when((pl.program_id(0) == 0) & (pl.program_id(1) == 0))
    def _():
        run_scr[...] = jnp.zeros(run_scr.shape, F32)

    h = _norm_mod(x_ref[0], g_ref[...], sc_ref[0], sh_ref[0])
    h_ref[0] = h.astype(BF16)
    logits = jnp.dot(h, wr_ref[...], preferred_element_type=F32,
                     precision=lax.Precision.HIGHEST) + br_ref[...]
    lane = _iota2(logits.shape, 1)
    lane_f = lane.astype(F32)
    cur = logits
    vals, idxs = [], []
    chosen = jnp.zeros(logits.shape, F32)
    for _ in range(TOP_K):
        m = jnp.max(cur, axis=-1, keepdims=True)
        first = jnp.min(jnp.where(cur == m, lane_f, float(LANES)), axis=-1, keepdims=True)
        vals.append(m)
        idxs.append(first)
        hit = lane_f == first
        cur = jnp.where(hit, KNOCKOUT, cur)
        chosen = jnp.where(hit, 1.0, chosen)
    exps = [jnp.exp(v - vals[0]) for v in vals]
    den = exps[0]
    for e in exps[1:]:
        den = den + e
    earlier = _dot(tri_ref[...], chosen.astype(BF16)) + run_scr[0:1, :]
    e_out = jnp.zeros(logits.shape, F32)
    w_out = jnp.zeros(logits.shape, F32)
    r_out = jnp.zeros(logits.shape, F32)
    for k in range(TOP_K):
        rank_k = jnp.sum(jnp.where(lane_f == idxs[k], earlier, 0.0), axis=-1, keepdims=True)
        e_out = jnp.where(lane == k, idxs[k], e_out)
        w_out = jnp.where(lane == k, exps[k] / den, w_out)
        r_out = jnp.where(lane == k, rank_k, r_out)
    e_ref[0] = e_out[:, :TOP_K].astype(jnp.int32)
    w_ref[0] = w_out[:, :TOP_K]
    rank_ref[0] = r_out[:, :TOP_K].astype(jnp.int32)
    total = run_scr[...] + jnp.sum(chosen, axis=0, keepdims=True)
    run_scr[...] = total
    cnt_ref[...] = total


def _router(x, g, sc, sh, w_router, b_router, tm=512):
    bn, sn, d = x.shape
    ne = w_router.shape[1]
    wr = jnp.zeros((d, LANES), F32).at[:, :ne].set(w_router)
    br = jnp.full((1, LANES), NEG_INF, F32).at[0, :ne].set(b_router)
    tri = jnp.asarray((np.arange(tm)[:, None] > np.arange(tm)[None, :]).astype(np.float32), BF16)
    row = lambda b, i: (b, i, 0)
    return pl.pallas_call(
        _router_kernel,
        grid=(bn, sn // tm),
        in_specs=[pl.BlockSpec((1, tm, d), row),
                  pl.BlockSpec((1, d), lambda b, i: (0, 0)),
                  pl.BlockSpec((1, 1, d), lambda b, i: (b, 0, 0)),
                  pl.BlockSpec((1, 1, d), lambda b, i: (b, 0, 0)),
                  pl.BlockSpec((d, LANES), lambda b, i: (0, 0)),
                  pl.BlockSpec((1, LANES), lambda b, i: (0, 0)),
                  pl.BlockSpec((tm, tm), lambda b, i: (0, 0))],
        out_specs=[pl.BlockSpec((1, tm, d), row),
                   pl.BlockSpec((1, tm, TOP_K), row),
                   pl.BlockSpec((1, tm, TOP_K), row),
                   pl.BlockSpec((1, tm, TOP_K), row),
                   pl.BlockSpec((8, LANES), lambda b, i: (0, 0))],
        out_shape=[jax.ShapeDtypeStruct((bn, sn, d), BF16),
                   jax.ShapeDtypeStruct((bn, sn, TOP_K), jnp.int32),
                   jax.ShapeDtypeStruct((bn, sn, TOP_K), F32),
                   jax.ShapeDtypeStruct((bn, sn, TOP_K), jnp.int32),
                   jax.ShapeDtypeStruct((8, LANES), F32)],
        scratch_shapes=[pltpu.VMEM((8, LANES), F32)],
        compiler_params=_cparams("arbitrary", "arbitrary"),
        name="moe_router",
    )(x, g.reshape(1, d), sc, sh, wr, br, tri)


def _expert_kernel(ce_ref, x_ref, wgu_ref, bgu_ref, wdn_ref, bdn_ref, rw_ref, *rest, ff, fc, off):
    o_ref, wgu_scr, wdn_scr = rest[-3:]
    c = pl.program_id(0)

    @pl.when((c == 0) | (ce_ref[c + off] != ce_ref[jnp.maximum(c + off - 1, 0)]))
    def _():
        wgu_scr[...] = wgu_ref[0, 0].astype(BF16)
        wdn_scr[...] = wdn_ref[0, 0].astype(BF16)

    x = x_ref[...]
    y = None
    for j in range(ff // fc):
        g = _dot(x, wgu_scr[:, j * fc:(j + 1) * fc]) + bgu_ref[0, 0, :, j * fc:(j + 1) * fc]
        u = _dot(x, wgu_scr[:, ff + j * fc:ff + (j + 1) * fc]) + bgu_ref[0, 0, :, ff + j * fc:ff + (j + 1) * fc]
        g = jnp.minimum(g, SWIGLU_LIMIT)
        u = jnp.clip(u, -SWIGLU_LIMIT, SWIGLU_LIMIT)
        act = g * jax.nn.sigmoid(SWIGLU_ALPHA * g) * (u + 1.0)
        part = _dot(act.astype(BF16), wdn_scr[j * fc:(j + 1) * fc, :])
        y = part if y is None else y + part
    o_ref[...] = ((y + bdn_ref[0, 0]) * rw_ref[...]).astype(o_ref.dtype)


def _expert_ffn(h, row_tok, chunk_e, layer, w_gu, b_gu, w_dn, b_dn, row_w, tm=MOE_ROWS, fc=512, groups=4):
    n_rows = row_tok.shape[0]
    d = h.shape[1]
    nl, ne, _, ff2 = w_gu.shape
    ff = ff2 // 2
    n_chunks = n_rows // tm
    assert n_chunks % groups == 0
    per = n_chunks // groups
    b_gu3, b_dn3, row_w2 = b_gu.reshape(nl, ne, 1, ff2), b_dn.reshape(nl, ne, 1, d), row_w.reshape(n_rows, 1)
    y = None
    for gi in range(groups):
        off = gi * per
        rows = h[row_tok[off * tm:(off + per) * tm]]
        in_specs = [pl.BlockSpec((tm, d), lambda c, ce: (c, 0)),
                    pl.BlockSpec((1, 1, d, ff2), lambda c, ce, off=off: (layer, ce[c + off], 0, 0)),
                    pl.BlockSpec((1, 1, 1, ff2), lambda c, ce, off=off: (layer, ce[c + off], 0, 0)),
                    pl.BlockSpec((1, 1, ff, d), lambda c, ce, off=off: (layer, ce[c + off], 0, 0)),
                    pl.BlockSpec((1, 1, 1, d), lambda c, ce, off=off: (layer, ce[c + off], 0, 0)),
                    pl.BlockSpec((tm, 1), lambda c, ce, off=off: (c + off, 0))]
        args = [chunk_e, rows, w_gu, b_gu3, w_dn, b_dn3, row_w2]
        aliases = {}
        if y is not None:
            in_specs.append(pl.BlockSpec(memory_space=pl.ANY))
            args.append(y)
            aliases = {len(args) - 1: 0}
        y = pl.pallas_call(
            functools.partial(_expert_kernel, ff=ff, fc=fc, off=off),
            grid_spec=pltpu.PrefetchScalarGridSpec(
                num_scalar_prefetch=1, grid=(per,), in_specs=in_specs,
                out_specs=pl.BlockSpec((tm, d), lambda c, ce, off=off: (c + off, 0)),
                scratch_shapes=[pltpu.VMEM((d, ff2), BF16), pltpu.VMEM((ff, d), BF16)]),
            out_shape=jax.ShapeDtypeStruct((n_rows, d), BF16),
            input_output_aliases=aliases,
            compiler_params=_cparams("arbitrary"),
            name="moe_expert_ffn",
        )(*args)
    return y


def _combine_kernel(y_ref, x_ref, g2_ref, o_ref):
    tot = y_ref[0].astype(F32)
    for k in range(1, TOP_K):
        tot = tot + y_ref[k].astype(F32)
    o_ref[0] = x_ref[0] + g2_ref[0] * tot


def _combine(y4, x, g2, tm=512):
    bn, sn, d = x.shape
    row = lambda b, i: (b, i, 0)
    nt = sn // tm
    return pl.pallas_call(
        _combine_kernel,
        grid=(bn, nt),
        in_specs=[pl.BlockSpec((TOP_K, tm, d), lambda b, i: (0, b * nt + i, 0)),
                  pl.BlockSpec((1, tm, d), row),
                  pl.BlockSpec((1, 1, d), lambda b, i: (b, 0, 0))],
        out_specs=pl.BlockSpec((1, tm, d), row),
        out_shape=jax.ShapeDtypeStruct((bn, sn, d), F32),
        compiler_params=_cparams("parallel", "parallel"),
        name="moe_combine",
    )(y4, x, g2)


def _moe(x, g, sc, sh, g2, w_router, b_router, layer, w_gu, b_gu, w_dn, b_dn):
    bn, sn, d = x.shape
    n_tok = bn * sn
    n_asg = n_tok * TOP_K
    tm = MOE_ROWS
    h, e_out, w_out, rank_out, totals = _router(x, g, sc, sh, w_router, b_router)
    e_tok = e_out.reshape(n_tok, TOP_K)
    w_flat = w_out.reshape(-1)
    counts = totals[0, :N_EXPERTS].astype(jnp.int32)
    starts = jnp.cumsum(counts) - counts
    padded = (counts + tm - 1) // tm * tm
    pad_ends = jnp.cumsum(padded)
    pad_starts = pad_ends - padded
    pos = pad_starts[e_tok] + rank_out.reshape(n_tok, TOP_K)
    n_chunks = n_asg // tm + N_EXPERTS
    chunk_start = jnp.arange(n_chunks, dtype=jnp.int32) * tm
    chunk_e = jnp.minimum(jnp.sum(chunk_start[:, None] >= pad_ends[None, :], axis=1), N_EXPERTS - 1).astype(jnp.int32)
    order = jnp.argsort(e_tok.reshape(-1))
    src = (chunk_start - (pad_starts - starts)[chunk_e])[:, None] + jnp.arange(tm, dtype=jnp.int32)[None, :]
    row_valid = (src < (starts + counts)[chunk_e][:, None]).reshape(-1)
    asg = order[jnp.clip(src, 0, n_asg - 1).reshape(-1)].astype(jnp.int32)
    row_tok = asg // TOP_K
    row_w = jnp.where(row_valid, w_flat[asg], 0.0)
    y = _expert_ffn(h.reshape(n_tok, d), row_tok, chunk_e, layer, w_gu, b_gu, w_dn, b_dn, row_w)
    return _combine(y[pos.T], x, g2)


def _norm_rope(x, g, cos, sin, bd, hd):
    sq = x * x
    hi = sq.astype(BF16)
    lo = (sq - hi.astype(F32)).astype(BF16)
    ss = _dot(hi, bd) + _dot(lo, bd)
    y = x * lax.rsqrt(ss * (1.0 / hd) + NORM_EPS) * g
    half = hd // 2
    first = (_iota2(x.shape, 1) & (hd - 1)) < half
    partner = jnp.where(first, pltpu.roll(y, LANES - half, 1), pltpu.roll(y, half, 1))
    return y * cos + partner * sin


def _prep_kernel(aq_ref, bq_ref, bk_ref, bv_ref, cq_ref, ck_ref, cv_ref, dq_ref, akc_ref, aks_ref, akw_ref,
                 dk_ref, dv_ref, c64_ref, s64_ref, c32_ref, s32_ref, gaq_ref, gak_ref, gcq_ref, gck_ref,
                 gdq_ref, gdk_ref, bd64_ref, bd32_ref,
                 oaq_ref, okc_ref, ovc_ref, okx_ref, ovs_ref, okw_ref, ovw_ref, obq_ref, obk_ref, obv_ref,
                 ocq_ref, ock_ref, ocv_ref, odq_ref, odk_ref, odv_ref, *, ts, sb_scale):
    hd, dd = HEAD_DIM, DIFF_DIM
    c64, s64, c32, s32 = c64_ref[0], s64_ref[0], c32_ref[0], s32_ref[0]
    bd64, bd32 = bd64_ref[...], bd32_ref[...]

    def slabs(ref):
        x = ref[0].astype(F32)
        return [x[:, c * LANES:(c + 1) * LANES] for c in range(x.shape[1] // LANES)]

    def put_heads(o_ref, c, y, width):
        per = LANES // width
        for u in range(per):
            o_ref[0, c * per + u] = y[:, u * width:(u + 1) * width].astype(o_ref.dtype)

    for src, gain, dst in ((aq_ref, gaq_ref, oaq_ref), (dq_ref, gdq_ref, odq_ref), (dk_ref, gdk_ref, odk_ref)):
        for c, x in enumerate(slabs(src)):
            put_heads(dst, c, _norm_rope(x, gain[:, c * LANES:(c + 1) * LANES], c64, s64, bd64, hd), hd)
    kc = _norm_rope(akc_ref[0].astype(F32), gak_ref[0:1, :], c64, s64, bd64, hd)
    ks = _norm_rope(aks_ref[0].astype(F32), gak_ref[1:2, :], c64, s64, bd64, hd)
    kw = _norm_rope(akw_ref[0].astype(F32), gak_ref[2:3, :], c64, s64, bd64, hd)
    okc_ref[0] = kc[:, :hd].astype(BF16)
    ovc_ref[0] = akc_ref[0][:, hd:].astype(BF16)
    nsel = okx_ref.shape[2] - hd
    blk = (pl.program_id(1) * ts + _iota2((ts, nsel), 0)) >> (NSA_SEL_BLOCK.bit_length() - 1)
    okx_ref[0, :, :nsel] = jnp.where(blk == _iota2((ts, nsel), 1), 1.0, 0.0).astype(BF16)
    okx_ref[0, :, nsel:] = ks[:, :hd].astype(BF16)
    ovs_ref[0] = aks_ref[0][:, hd:].astype(BF16)
    okw_ref[0, 0] = kw[:, :hd].astype(BF16)
    ovw_ref[0, 0] = akw_ref[0][:, hd:].astype(BF16)
    for c, x in enumerate(slabs(bq_ref)):
        put_heads(obq_ref, c, x * sb_scale, hd)
    for src, dst in ((bk_ref, obk_ref), (bv_ref, obv_ref), (cv_ref, ocv_ref)):
        for c, x in enumerate(slabs(src)):
            put_heads(dst, c, x, hd)
    put_heads(odv_ref, 0, dv_ref[0].astype(F32), hd)
    for src, gain, dst in ((cq_ref, gcq_ref, ocq_ref), (ck_ref, gck_ref, ock_ref)):
        for c, x in enumerate(slabs(src)):
            put_heads(dst, c, _norm_rope(x, gain[:, c * LANES:(c + 1) * LANES], c32, s32, bd32, dd), dd)


def _rope_tables(positions, hd):
    half = hd // 2
    inv = ROPE_THETA ** (-jnp.arange(half, dtype=F32) * 2.0 / hd)
    ang = positions.astype(F32)[..., None] * inv
    cos, sin = jnp.cos(ang), jnp.sin(ang)
    reps = LANES // hd
    return (jnp.tile(jnp.concatenate([cos, cos], axis=-1), (1, 1, reps)),
            jnp.tile(jnp.concatenate([-sin, sin], axis=-1), (1, 1, reps)))


def _prep(proj, cols, tables, p, n_sel, ts=512):
    bn, sn, _ = proj.shape
    hd, dd = HEAD_DIM, DIFF_DIM
    scale = hd ** -0.5
    c64, s64, c32, s32 = tables

    def cspec(name):
        off, width = cols[name]
        assert off % width == 0
        return pl.BlockSpec((1, ts, width), lambda b, i, blk=off // width: (b, i, blk))

    def tile_gain(g, reps, mult=1.0):
        return (jnp.tile(g.astype(F32), reps) * mult).reshape(1, -1)

    ones = jnp.ones((hd,), F32)
    gak = jnp.stack([jnp.concatenate([p["nsa_kn"][j].astype(F32), ones]) for j in range(3)])
    gains = [tile_gain(p["nsa_qn"], NSA_HEADS, scale), gak,
             tile_gain(p["dif_qn"], 2 * DIFF_HEADS, dd ** -0.5), tile_gain(p["dif_kn"], 2 * DIFF_HEADS),
             tile_gain(p["swa_qn"], SWA_HEADS, scale), tile_gain(p["swa_kn"], SWA_KV_HEADS)]
    lane = np.arange(LANES)
    bd64 = jnp.asarray((lane[:, None] // hd == lane[None, :] // hd).astype(np.float32), BF16)
    bd32 = jnp.asarray((lane[:, None] // dd == lane[None, :] // dd).astype(np.float32), BF16)
    names = ("a_q", "b_q", "b_k", "b_v", "c_q", "c_k", "c_v", "d_q", "a_kcvc", "a_ksvs", "a_kwvw", "d_k", "d_v")
    tab = pl.BlockSpec((1, ts, LANES), lambda b, i: (b, i, 0))
    full = lambda a: pl.BlockSpec(a.shape, lambda b, i: (0,) * a.ndim)

    def hm(nh, w):
        return (jax.ShapeDtypeStruct((bn, nh, sn, w), BF16), pl.BlockSpec((1, nh, ts, w), lambda b, i: (b, 0, i, 0)))

    def tm_(w):
        return (jax.ShapeDtypeStruct((bn, sn, w), BF16), pl.BlockSpec((1, ts, w), lambda b, i: (b, i, 0)))

    outs = [hm(NSA_HEADS, hd), tm_(hd), tm_(hd), tm_(n_sel + hd), tm_(hd), hm(1, hd), hm(1, hd),
            hm(SB_HEADS, hd), hm(SB_HEADS, hd), hm(SB_HEADS, hd),
            hm(2 * DIFF_HEADS, dd), hm(2 * DIFF_HEADS, dd), hm(DIFF_HEADS, 2 * dd),
            hm(SWA_HEADS, hd), hm(SWA_KV_HEADS, hd), hm(SWA_KV_HEADS, hd)]
    consts = gains + [bd64, bd32]
    res = pl.pallas_call(
        functools.partial(_prep_kernel, ts=ts, sb_scale=-scale * LOG2E),
        grid=(bn, sn // ts),
        in_specs=[cspec(n) for n in names] + [tab] * 4 + [full(a) for a in consts],
        out_specs=[o[1] for o in outs],
        out_shape=[o[0] for o in outs],
        compiler_params=_cparams("parallel", "parallel"),
        name="mixer_prep",
    )(*([proj] * len(names)), c64, s64, c32, s32, *consts)
    keys = ("a_q", "kc", "vc", "kx", "vs", "kw", "vw", "b_q", "b_k", "b_v", "c_q", "c_k", "c_v", "d_q", "d_k", "d_v")
    return dict(zip(keys, res))


def _layer_columns(d):
    cols = {}
    off = 0
    for name, width in (("br_g", N_BRANCH * d), ("ga", 3 * NSA_HEADS * HEAD_DIM), ("a_q", 256),
                        ("b_q", 256), ("b_k", 256), ("b_v", 256), ("c_q", 256), ("c_k", 256), ("c_v", 256),
                        ("d_q", 256), ("a_kcvc", 128), ("a_ksvs", 128), ("a_kwvw", 128), ("d_k", 128), ("d_v", 128)):
        cols[name] = (off, width)
        off += width
    return cols, off


def _reorder_w_in(w_in, d, n_pad):
    ref_splits = (256, 128, 128, 128, 12, 256, 256, 256, 256, 256, 256, 256, 128, 128, N_BRANCH * d)
    names = ("a_q", "a_kcvc", "a_ksvs", "a_kwvw", "a_g", "b_q", "b_k", "b_v",
             "c_q", "c_k", "c_v", "d_q", "d_k", "d_v", "br_g")
    starts = np.cumsum((0,) + ref_splits)
    src = {n: (int(starts[i]), ref_splits[i]) for i, n in enumerate(names)}
    cols, total = _layer_columns(d)
    idx = np.zeros((n_pad,), np.int32)
    keep = np.zeros((n_pad,), np.float32)
    for name, (off, width) in cols.items():
        if name == "ga":
            g0 = src["a_g"][0]
            for j in range(3):
                for h in range(NSA_HEADS):
                    base = off + j * NSA_HEADS * HEAD_DIM + h * HEAD_DIM
                    idx[base:base + HEAD_DIM] = g0 + h * 3 + j
        else:
            idx[off:off + width] = src[name][0] + np.arange(width)
        keep[off:off + width] = 1.0
    w = w_in[:, idx] * keep[None, :]
    return w.astype(BF16)


def _mixer_layer(x, positions, tables, mod, p, lam_init):
    bn, sn, d = x.shape
    sh1, sc1, g1 = mod[0], mod[1], mod[2]
    cols, total = _layer_columns(d)
    n_pad = -(-total // 1536) * 1536
    w_in = _reorder_w_in(p["w_in"], d, n_pad)
    proj = _in_projection(x, p["norm1"], sc1, sh1, w_in)
    n_cmp = (sn - NSA_CMP_LEN) // NSA_CMP_STRIDE + 1
    n_sel = sn // NSA_SEL_BLOCK
    n_top = min(NSA_N_SEL, n_sel)
    t = _prep(proj, cols, tables, p, n_sel)
    k_cmp = _compress(t["kc"], p["nsa_pe_k"], p["nsa_w1_k"], p["nsa_w2_k"]).astype(BF16)
    v_cmp = _compress(t["vc"], p["nsa_pe_v"], p["nsa_w1_v"], p["nsa_w2_v"]).astype(BF16)
    cmp_start = np.arange(sn // NSA_CMP_STRIDE) * NSA_CMP_STRIDE
    sel_start = np.arange(n_sel) * NSA_SEL_BLOCK
    overlap = ((cmp_start[:, None] <= (sel_start + NSA_SEL_BLOCK - 1)[None, :]) &
               ((cmp_start + NSA_CMP_LEN - 1)[:, None] >= sel_start[None, :]) &
               (np.arange(sn // NSA_CMP_STRIDE) < n_cmp)[:, None]).astype(np.float32)
    o_c, bias = _nsa_cmp(t["a_q"], k_cmp, v_cmp, jnp.asarray(overlap, BF16), n_cmp, n_top)
    o_s = _nsa_sel(t["a_q"], t["kx"], t["vs"], bias)
    o_w = _banded_attention(t["a_q"], t["kw"], t["vw"], NSA_WINDOW)
    o_b = _stick_breaking(t["b_q"], t["b_k"], t["b_v"])
    lam = (jnp.exp(jnp.sum(p["dif_lq1"] * p["dif_lk1"])) - jnp.exp(jnp.sum(p["dif_lq2"] * p["dif_lk2"]))
           + lam_init)
    o_cd = _diff_attention(t["c_q"], t["c_k"], t["c_v"], p["dif_subln"], lam, jnp.asarray(1.0 - lam_init, F32))
    o_d = _banded_attention(t["d_q"], t["d_k"], t["d_v"], SWA_WINDOW, sinks=p["swa_sinks"])
    return _merge(proj, o_c, o_s, o_w, o_b, o_cd, o_d, p["w_up"].astype(BF16), p["w_out"].astype(BF16),
                  x, g1, cols["ga"][0])


def kernel(x, c, positions, w_ada, b_ada, norm1, norm2, w_in, nsa_qn, nsa_kn, nsa_pe_k, nsa_w1_k, nsa_w2_k,
           nsa_pe_v, nsa_w1_v, nsa_w2_v, dif_qn, dif_kn, dif_lq1, dif_lk1, dif_lq2, dif_lk2, dif_subln,
           swa_qn, swa_kn, swa_sinks, w_up, w_out, w_router, b_router, w_gu, b_gu, w_dn, b_dn):
    bn, sn, d = x.shape
    depth = w_ada.shape[0]
    c_pad = jnp.zeros((8, d), F32).at[:bn].set(c)
    tables = _rope_tables(positions, HEAD_DIM) + _rope_tables(positions, DIFF_DIM)
    for l in range(depth):
        lam_init = 0.8 - 0.6 * math.exp(-0.3 * l)
        mod = _linear(c_pad, w_ada[l], b_ada[l], tn=512, precision=lax.Precision.HIGHEST)[:bn]
        mod = mod.reshape(bn, 6, 1, d).transpose(1, 0, 2, 3)
        p = dict(norm1=norm1[l], w_in=w_in[l], nsa_qn=nsa_qn[l], nsa_kn=nsa_kn[l], nsa_pe_k=nsa_pe_k[l],
                 nsa_w1_k=nsa_w1_k[l], nsa_w2_k=nsa_w2_k[l], nsa_pe_v=nsa_pe_v[l], nsa_w1_v=nsa_w1_v[l],
                 nsa_w2_v=nsa_w2_v[l], dif_qn=dif_qn[l], dif_kn=dif_kn[l], dif_lq1=dif_lq1[l],
                 dif_lk1=dif_lk1[l], dif_lq2=dif_lq2[l], dif_lk2=dif_lk2[l], dif_subln=dif_subln[l],
                 swa_qn=swa_qn[l], swa_kn=swa_kn[l], swa_sinks=swa_sinks[l], w_up=w_up[l], w_out=w_out[l])
        x = _mixer_layer(x, positions, tables, mod, p, lam_init)
        x = _moe(x, norm2[l], mod[4], mod[3], mod[5], w_router[l], b_router[l], l, w_gu, b_gu, w_dn, b_dn)
    return x
```

```python
import functools
import math

import numpy as np
import jax
import jax.numpy as jnp
from jax import lax
from jax.experimental import pallas as pl
from jax.experimental.pallas import tpu as pltpu

F32 = jnp.float32
BF16 = jnp.bfloat16

HEAD_DIM = 64
ROPE_THETA = 10000.0
NORM_EPS = 1e-6
NEG_INF = -1e30
KNOCKOUT = -3e38
N_BRANCH = 4

NSA_HEADS = 4
NSA_CMP_LEN = 32
NSA_CMP_STRIDE = 16
NSA_SEL_BLOCK = 64
NSA_N_SEL = 16
NSA_WINDOW = 512
NSA_FORCED_SCORE = 1e4

SB_HEADS = 4
DIFF_HEADS = 4
DIFF_DIM = 32
SWA_HEADS = 4
SWA_KV_HEADS = 2
SWA_WINDOW = 128

N_EXPERTS = 32
TOP_K = 4
SWIGLU_ALPHA = 1.702
SWIGLU_LIMIT = 7.0

LANES = 128
LOG2E = 1.4426950408889634
VMEM_LIMIT = 56 * 1024 * 1024

FLASH_TILE = 1024
SB_KEY_TILE = 256
BAND_TILE = 1024
CMP_TILE = 512
PROJ_TILE = (1024, 1536)
ROW_TILE = 512
MERGE_TILE = 256
MOE_ROWS = 512
MOE_SLICES = 4


def _cparams(*sem):
    return pltpu.CompilerParams(dimension_semantics=sem, vmem_limit_bytes=VMEM_LIMIT)


def _dot(a, b):
    return jnp.dot(a, b, preferred_element_type=F32)


def _dot_nt(a, b):
    return lax.dot_general(a, b, (((1,), (1,)), ((), ())), preferred_element_type=F32)


def _iota2(shape, dim):
    return lax.broadcasted_iota(jnp.int32, shape, dim)


def _linear_kernel(x_ref, w_ref, b_ref, o_ref, *, precision):
    o_ref[...] = jnp.dot(x_ref[...], w_ref[...], preferred_element_type=F32,
                         precision=precision) + b_ref[...]


def _linear(x, w, b, tn, precision=None):
    m, k = x.shape
    n = w.shape[1]
    return pl.pallas_call(
        functools.partial(_linear_kernel, precision=precision),
        grid=(n // tn,),
        in_specs=[pl.BlockSpec((m, k), lambda j: (0, 0)),
                  pl.BlockSpec((k, tn), lambda j: (0, j)),
                  pl.BlockSpec((1, tn), lambda j: (0, j))],
        out_specs=pl.BlockSpec((m, tn), lambda j: (0, j)),
        out_shape=jax.ShapeDtypeStruct((m, n), F32),
        compiler_params=_cparams("arbitrary"),
        name="linear",
    )(x, w, b.reshape(1, n))


def _cmp_mlp_kernel(a_ref, b_ref, pe_ref, w2_ref, o_ref):
    hid = jax.nn.gelu(a_ref[...] + b_ref[...] + pe_ref[...])
    o_ref[...] = _dot(hid.astype(BF16), w2_ref[...])


def _compress(t, pe, w1, w2):
    bn, sn, dh = t.shape
    st = NSA_CMP_STRIDE
    half = st * dh
    nb = sn // st
    t16 = t.reshape(bn * nb, half).astype(BF16)
    w1cat = jnp.concatenate([w1[:half], w1[half:]], axis=1).astype(BF16)
    hidden = w1.shape[1]
    ab = _linear(t16, w1cat, jnp.zeros((2 * hidden,), F32), tn=2 * hidden)
    ab = ab.reshape(bn, nb, 2 * hidden)
    a = ab[:, :, :hidden]
    b_next = jnp.concatenate([ab[:, 1:, hidden:], jnp.zeros((bn, 1, hidden), F32)], axis=1)
    pe_term = jnp.dot(pe.reshape(1, NSA_CMP_LEN * dh), w1, precision=lax.Precision.HIGHEST)
    rows = bn * nb
    tm = min(512, rows)
    out = pl.pallas_call(
        _cmp_mlp_kernel,
        grid=(rows // tm,),
        in_specs=[pl.BlockSpec((tm, hidden), lambda i: (i, 0)),
                  pl.BlockSpec((tm, hidden), lambda i: (i, 0)),
                  pl.BlockSpec((1, hidden), lambda i: (0, 0)),
                  pl.BlockSpec((hidden, dh), lambda i: (0, 0))],
        out_specs=pl.BlockSpec((tm, dh), lambda i: (i, 0)),
        out_shape=jax.ShapeDtypeStruct((rows, dh), F32),
        compiler_params=_cparams("parallel"),
        name="cmp_mlp",
    )(a.reshape(rows, hidden), b_next.reshape(rows, hidden), pe_term, w2.astype(BF16))
    return out.reshape(bn, nb, dh)


def _norm_mod(x, g, sc, sh):
    r = lax.rsqrt(jnp.mean(x * x, axis=-1, keepdims=True) + NORM_EPS)
    return (x * r * g) * (1.0 + sc) + sh


def _proj_kernel(x_ref, g_ref, sc_ref, sh_ref, w_ref, o_ref, h_scr):
    @pl.when(pl.program_id(2) == 0)
    def _():
        h_scr[...] = _norm_mod(x_ref[0], g_ref[...], sc_ref[0], sh_ref[0]).astype(BF16)

    o_ref[0] = _dot(h_scr[...], w_ref[...]).astype(o_ref.dtype)


def _in_projection(x, g, sc, sh, w, tm=PROJ_TILE[0], tn=PROJ_TILE[1]):
    bn, sn, d = x.shape
    n = w.shape[1]
    tm = min(tm, sn)
    return pl.pallas_call(
        _proj_kernel,
        grid=(bn, sn // tm, n // tn),
        in_specs=[pl.BlockSpec((1, tm, d), lambda b, i, j: (b, i, 0)),
                  pl.BlockSpec((1, d), lambda b, i, j: (0, 0)),
                  pl.BlockSpec((1, 1, d), lambda b, i, j: (b, 0, 0)),
                  pl.BlockSpec((1, 1, d), lambda b, i, j: (b, 0, 0)),
                  pl.BlockSpec((d, tn), lambda b, i, j: (0, j))],
        out_specs=pl.BlockSpec((1, tm, tn), lambda b, i, j: (b, i, j)),
        out_shape=jax.ShapeDtypeStruct((bn, sn, n), BF16),
        scratch_shapes=[pltpu.VMEM((tm, d), BF16)],
        compiler_params=_cparams("parallel", "parallel", "arbitrary"),
        name="in_proj",
    )(x, g.reshape(1, d), sc, sh, w)


def _banded_kernel(*refs, tile, window, has_sink):
    if has_sink:
        sink_ref, q_ref, kp_ref, kc_ref, vp_ref, vc_ref, o_ref = refs
    else:
        q_ref, kp_ref, kc_ref, vp_ref, vc_ref, o_ref = refs
    i = pl.program_id(2)
    w = window
    upper = _iota2((w, w), 1) > _iota2((w, w), 0)
    first_bias = jnp.where(i > 0, 0.0, NEG_INF)
    for u in range(tile // w):
        q = q_ref[0, 0, u * w:(u + 1) * w, :]
        if u == 0:
            k_prev, v_prev = kp_ref[0, 0], vp_ref[0, 0]
        else:
            k_prev, v_prev = kc_ref[0, 0, (u - 1) * w:u * w, :], vc_ref[0, 0, (u - 1) * w:u * w, :]
        k_cur, v_cur = kc_ref[0, 0, u * w:(u + 1) * w, :], vc_ref[0, 0, u * w:(u + 1) * w, :]
        s_prev = _dot_nt(q, k_prev)
        if u == 0:
            s_prev = s_prev + first_bias
        s = jnp.where(upper, s_prev, _dot_nt(q, k_cur))
        m = jnp.max(s, axis=-1, keepdims=True)
        if has_sink:
            sink = sink_ref[pl.program_id(1)]
            m = jnp.maximum(m, sink)
        p = jnp.exp(s - m)
        den = jnp.sum(p, axis=-1, keepdims=True)
        if has_sink:
            den = den + jnp.exp(sink - m)
        else:
            den = jnp.maximum(den, 1e-30)
        o = (_dot(jnp.where(upper, p, 0.0).astype(BF16), v_prev)
             + _dot(jnp.where(upper, 0.0, p).astype(BF16), v_cur))
        o_ref[0, 0, u * w:(u + 1) * w, :] = o / den


def _banded_attention(q, k, v, window, sinks=None, tile=BAND_TILE):
    bn, hq, sn, d = q.shape
    grp = hq // k.shape[1]
    tile = min(tile, sn)
    assert tile % window == 0 and sn % tile == 0
    per = tile // window
    has_sink = sinks is not None
    qspec = pl.BlockSpec((1, 1, tile, d), lambda b, h, i: (b, h, i, 0))
    prev = pl.BlockSpec((1, 1, window, d), lambda b, h, i: (b, h // grp, jnp.maximum(i * per - 1, 0), 0))
    cur = pl.BlockSpec((1, 1, tile, d), lambda b, h, i: (b, h // grp, i, 0))
    in_specs = [qspec, prev, cur, prev, cur]
    args = [q, k, k, v, v]
    if has_sink:
        in_specs = [pl.BlockSpec(memory_space=pltpu.SMEM)] + in_specs
        args = [sinks.astype(F32)] + args
    return pl.pallas_call(
        functools.partial(_banded_kernel, tile=tile, window=window, has_sink=has_sink),
        grid=(bn, hq, sn // tile),
        in_specs=in_specs,
        out_specs=pl.BlockSpec((1, 1, tile, d), lambda b, h, i: (b, h, i, 0)),
        out_shape=jax.ShapeDtypeStruct((bn, hq, sn, d), F32),
        compiler_params=_cparams("parallel", "parallel", "parallel"),
        name="banded_attn",
    )(*args)


def _nsa_cmp_kernel(q_ref, kc_ref, vct_ref, ovt_ref, oct_ref, bias_ref, *, tq, n_cmp, n_top, heads):
    i = pl.program_id(1)
    ncp = kc_ref.shape[1]
    nsel = ovt_ref.shape[0]
    dh = q_ref.shape[3]
    t = i * tq + _iota2((ncp, tq), 1)
    n = _iota2((ncp, tq), 0)
    valid = (n * NSA_CMP_STRIDE + (NSA_CMP_LEN - 1) <= t) & (n < n_cmp)
    kc = kc_ref[0]
    vct = vct_ref[0]
    psum = jnp.zeros((ncp, tq), F32)
    for h in range(heads):
        s = jnp.where(valid, _dot_nt(kc, q_ref[0, h]), NEG_INF)
        m = jnp.max(s, axis=0, keepdims=True)
        p = jnp.where(valid, jnp.exp(s - m), 0.0)
        p = p / jnp.maximum(jnp.sum(p, axis=0, keepdims=True), 1e-30)
        oct_ref[0, h * dh:(h + 1) * dh, :] = _dot(vct, p.astype(BF16))
        psum = psum + p
    hi = psum.astype(BF16)
    lo = (psum - hi.astype(F32)).astype(BF16)
    imp = _dot(ovt_ref[...], hi) + _dot(ovt_ref[...], lo)

    tt = i * tq + _iota2((nsel, tq), 1)
    blk = _iota2((nsel, tq), 0)
    cur = tt >> (NSA_SEL_BLOCK.bit_length() - 1)
    forced = (blk == 0) | (blk == cur) | (blk == cur - 1)
    valid_s = blk * NSA_SEL_BLOCK <= tt
    score = jnp.where(forced, NSA_FORCED_SCORE, jnp.where(valid_s, imp, -1.0))
    blk_f = blk.astype(F32)

    def pick(_, carry):
        score, sel = carry
        m = jnp.max(score, axis=0, keepdims=True)
        first = jnp.min(jnp.where(score == m, blk_f, float(nsel)), axis=0, keepdims=True)
        hit = blk_f == first
        return jnp.where(hit, KNOCKOUT, score), jnp.where(hit, 0.0, sel)

    _, bias = lax.fori_loop(0, n_top, pick, (score, jnp.full((nsel, tq), NEG_INF, F32)))
    bias_ref[0] = bias.astype(BF16)


def _nsa_cmp(q, kc, vc, overlap, n_cmp, n_top, tq=CMP_TILE):
    bn, heads, sn, dh = q.shape
    tq = min(tq, sn)
    ncp = kc.shape[1]
    nsel = overlap.shape[1]
    oct, bias_t = pl.pallas_call(
        functools.partial(_nsa_cmp_kernel, tq=tq, n_cmp=n_cmp, n_top=n_top, heads=heads),
        grid=(bn, sn // tq),
        in_specs=[pl.BlockSpec((1, heads, tq, dh), lambda b, i: (b, 0, i, 0)),
                  pl.BlockSpec((1, ncp, dh), lambda b, i: (b, 0, 0)),
                  pl.BlockSpec((1, dh, ncp), lambda b, i: (b, 0, 0)),
                  pl.BlockSpec((nsel, ncp), lambda b, i: (0, 0))],
        out_specs=[pl.BlockSpec((1, heads * dh, tq), lambda b, i: (b, 0, i)),
                   pl.BlockSpec((1, nsel, tq), lambda b, i: (b, 0, i))],
        out_shape=[jax.ShapeDtypeStruct((bn, heads * dh, sn), F32),
                   jax.ShapeDtypeStruct((bn, nsel, sn), BF16)],
        compiler_params=_cparams("parallel", "parallel"),
        name="nsa_cmp_topk",
    )(q, kc, vc.transpose(0, 2, 1), overlap.T)
    return oct.transpose(0, 2, 1), bias_t.transpose(0, 2, 1)


def _nsa_sel_kernel(q_ref, kx_ref, v_ref, bias_ref, o_ref, qx_scr, *, tq, tk, heads):
    i = pl.program_id(1)
    dh = v_ref.shape[2]
    bias = bias_ref[0]
    for h in range(heads):
        qx_scr[h] = jnp.concatenate([bias, q_ref[0, h]], axis=1)

    def tile(j, state, diagonal):
        start = pl.multiple_of(j * tk, tk)
        kx = kx_ref[0, pl.ds(start, tk), :]
        v = v_ref[0, pl.ds(start, tk), :]
        if diagonal:
            causal = (start + _iota2((tq, tk), 1)) <= (i * tq + _iota2((tq, tk), 0))
        new_state = []
        for h in range(heads):
            m_old, l_old, acc_old = state[h]
            s = _dot_nt(qx_scr[h], kx)
            if diagonal:
                s = jnp.where(causal, s, NEG_INF)
            m_new = jnp.maximum(m_old, jnp.max(s, axis=-1, keepdims=True))
            alpha = jnp.exp(m_old - m_new)
            p = jnp.exp(s - m_new)
            l_new = alpha * l_old + jnp.sum(p, axis=-1, keepdims=True)
            acc_new = alpha * acc_old + _dot(p.astype(BF16), v)
            new_state.append((m_new, l_new, acc_new))
        return tuple(new_state)

    init = tuple((jnp.full((tq, 1), NEG_INF, F32), jnp.zeros((tq, 1), F32), jnp.zeros((tq, dh), F32))
                 for _ in range(heads))
    n_full = (i * tq) // tk
    state = lax.fori_loop(0, n_full, lambda j, st: tile(j, st, False), init)
    state = tile(n_full, state, True)
    for h in range(heads):
        o_ref[0, :, h * dh:(h + 1) * dh] = state[h][2] / jnp.maximum(state[h][1], 1e-30)


def _nsa_sel(q, kx, v, bias, tq=FLASH_TILE, tk=FLASH_TILE):
    bn, heads, sn, dh = q.shape
    nsel = bias.shape[2]
    tq, tk = min(tq, sn), min(tk, sn)
    assert tk % tq == 0 and sn % tk == 0
    return pl.pallas_call(
        functools.partial(_nsa_sel_kernel, tq=tq, tk=tk, heads=heads),
        grid=(bn, sn // tq),
        in_specs=[pl.BlockSpec((1, heads, tq, dh), lambda b, i: (b, 0, i, 0)),
                  pl.BlockSpec((1, sn, nsel + dh), lambda b, i: (b, 0, 0)),
                  pl.BlockSpec((1, sn, dh), lambda b, i: (b, 0, 0)),
                  pl.BlockSpec((1, tq, nsel), lambda b, i: (b, i, 0))],
        out_specs=pl.BlockSpec((1, tq, heads * dh), lambda b, i: (b, i, 0)),
        out_shape=jax.ShapeDtypeStruct((bn, sn, heads * dh), F32),
        scratch_shapes=[pltpu.VMEM((heads, tq, nsel + dh), BF16)],
        compiler_params=_cparams("parallel", "arbitrary"),
        name="nsa_selected_attn",
    )(q, kx, v, bias)


def _sb_kernel(q_ref, k_ref, v_ref, u_ref, o_ref, *, tq, tk):
    i = pl.program_id(2)
    q = q_ref[0, 0]
    per_q = tq // tk

    def tile(jj, carry, first_row):
        diagonal = first_row is not None
        r0 = first_row if diagonal else 0
        start = pl.multiple_of(jj * tk, tk)
        nz = _dot_nt(q[r0:], k_ref[0, 0, pl.ds(start, tk), :])
        neg_abs = lax.bitcast_convert_type(lax.bitcast_convert_type(nz, jnp.uint32) | jnp.uint32(0x80000000), F32)
        log_keep = jnp.minimum(nz, 0.0) - jnp.log2(1.0 + jnp.exp2(neg_abs))
        if diagonal:
            strict = _iota2(nz.shape, 1) < _iota2(nz.shape, 0)
            log_keep = jnp.where(strict, log_keep, 0.0)
        cum = _dot(log_keep.astype(BF16), u_ref[...])
        a = jnp.exp2(cum + jnp.concatenate([carry[r0:]] * (tk // LANES), axis=1) - nz)
        if diagonal:
            a = jnp.where(strict, a, 0.0)
        out = _dot(a.astype(BF16), v_ref[0, 0, pl.ds(start, tk), :])
        new_carry = carry[r0:] + jnp.broadcast_to(cum[:, 0:1], (tq - r0, LANES))
        if r0:
            out = jnp.concatenate([jnp.zeros((r0, out.shape[1]), F32), out], axis=0)
            new_carry = jnp.concatenate([carry[:r0], new_carry], axis=0)
        return out, new_carry

    def group(first, carry, diagonal):
        total = None
        for r in range(per_q):
            out, carry = tile(first - r, carry, (per_q - 1 - r) * tk if diagonal else None)
            total = out if total is None else total + out
        return total, carry

    acc, carry = group(i * per_q + per_q - 1, jnp.zeros((tq, LANES), F32), True)

    def body(p, state):
        acc, carry = state
        out, carry = group((i - p) * per_q - 1, carry, False)
        return acc + out, carry

    acc, _ = lax.fori_loop(0, i, body, (acc, carry))
    o_ref[0, 0] = acc


def _stick_breaking(q, k, v, tq=FLASH_TILE, tk=SB_KEY_TILE):
    bn, heads, sn, dh = q.shape
    tq = min(tq, sn)
    incl = (np.arange(tk)[:, None] >= np.arange(tk)[None, :]).astype(np.float32)
    u = jnp.asarray(incl, BF16)
    kv = pl.BlockSpec((1, 1, sn, dh), lambda b, h, i: (b, h, 0, 0))
    return pl.pallas_call(
        functools.partial(_sb_kernel, tq=tq, tk=tk),
        grid=(bn, heads, sn // tq),
        in_specs=[pl.BlockSpec((1, 1, tq, dh), lambda b, h, i: (b, h, i, 0)), kv, kv,
                  pl.BlockSpec((tk, tk), lambda b, h, i: (0, 0))],
        out_specs=pl.BlockSpec((1, 1, tq, dh), lambda b, h, i: (b, h, i, 0)),
        out_shape=jax.ShapeDtypeStruct((bn, heads, sn, dh), F32),
        compiler_params=_cparams("parallel", "parallel", "arbitrary"),
        name="stick_breaking_attn",
    )(q, k, v, u)


def _diff_kernel(sc_ref, q_ref, k_ref, v_ref, g_ref, o_ref, *, tq, tk):
    i = pl.program_id(2)
    dv = v_ref.shape[3]

    def tile(j, state, diagonal):
        start = pl.multiple_of(j * tk, tk)
        v = v_ref[0, 0, pl.ds(start, tk), :]
        if diagonal:
            causal = (start + _iota2((tq, tk), 1)) <= (i * tq + _iota2((tq, tk), 0))
        new_state = []
        for mi in range(2):
            m_old, l_old, acc_old = state[mi]
            s = _dot_nt(q_ref[0, mi], k_ref[0, mi, pl.ds(start, tk), :])
            if diagonal:
                s = jnp.where(causal, s, NEG_INF)
            m_new = jnp.maximum(m_old, jnp.max(s, axis=-1, keepdims=True))
            alpha = jnp.exp(m_old - m_new)
            p = jnp.exp(s - m_new)
            l_new = alpha * l_old + jnp.sum(p, axis=-1, keepdims=True)
            acc_new = alpha * acc_old + _dot(p.astype(BF16), v)
            new_state.append((m_new, l_new, acc_new))
        return tuple(new_state)

    init = tuple((jnp.full((tq, 1), NEG_INF, F32), jnp.zeros((tq, 1), F32), jnp.zeros((tq, dv), F32))
                 for _ in range(2))
    n_full = (i * tq) // tk
    state = lax.fori_loop(0, n_full, lambda j, st: tile(j, st, False), init)
    state = tile(n_full, state, True)
    lam = sc_ref[0]
    post = sc_ref[1]
    o = (state[0][2] / jnp.maximum(state[0][1], 1e-30)
         - lam * (state[1][2] / jnp.maximum(state[1][1], 1e-30)))
    r = lax.rsqrt(jnp.mean(o * o, axis=-1, keepdims=True) + NORM_EPS)
    o_ref[0, 0] = (o * r * g_ref[...]) * post


def _diff_attention(q, k, v, subln, lam, post, tq=FLASH_TILE, tk=FLASH_TILE):
    bn, h2, sn, dd = q.shape
    heads = h2 // 2
    dv = v.shape[3]
    tq, tk = min(tq, sn), min(tk, sn)
    assert tk % tq == 0 and sn % tk == 0
    scal = jnp.stack([lam, post]).astype(F32)
    return pl.pallas_call(
        functools.partial(_diff_kernel, tq=tq, tk=tk),
        grid=(bn, heads, sn // tq),
        in_specs=[pl.BlockSpec(memory_space=pltpu.SMEM),
                  pl.BlockSpec((1, 2, tq, dd), lambda b, h, i: (b, h, i, 0)),
                  pl.BlockSpec((1, 2, sn, dd), lambda b, h, i: (b, h, 0, 0)),
                  pl.BlockSpec((1, 1, sn, dv), lambda b, h, i: (b, h, 0, 0)),
                  pl.BlockSpec((1, dv), lambda b, h, i: (0, 0))],
        out_specs=pl.BlockSpec((1, 1, tq, dv), lambda b, h, i: (b, h, i, 0)),
        out_shape=jax.ShapeDtypeStruct((bn, heads, sn, dv), F32),
        compiler_params=_cparams("parallel", "parallel", "arbitrary"),
        name="diff_attn",
    )(scal, q, k, v, subln.reshape(1, dv).astype(F32))


def _merge_kernel(brg_ref, ga0_ref, ga1_ref, ga2_ref, oc_ref, os_ref, ow_ref, ob_ref, ocd_ref, od_ref,
                  wup_ref, wout_ref, x_ref, g1_ref, o_ref, *, d):
    def heads(ref):
        return jnp.concatenate([ref[0, h] for h in range(ref.shape[1])], axis=1)

    def gate(x):
        return jax.nn.sigmoid(x.astype(F32))

    o_a = gate(ga0_ref[0]) * oc_ref[0] + gate(ga1_ref[0]) * os_ref[0] + gate(ga2_ref[0]) * heads(ow_ref)
    branches = (o_a, heads(ob_ref), heads(ocd_ref), heads(od_ref))
    merged = None
    for bi, o in enumerate(branches):
        term = gate(brg_ref[0, :, bi * d:(bi + 1) * d]) * _dot(o.astype(BF16), wup_ref[bi])
        merged = term if merged is None else merged + term
    y = _dot(merged.astype(BF16), wout_ref[...])
    o_ref[0] = x_ref[0] + g1_ref[0] * y


def _merge(proj, o_c, o_s, o_w, o_b, o_cd, o_d, w_up, w_out, x, g1, ga_col, tm=MERGE_TILE):
    bn, sn, d = x.shape
    bw = o_c.shape[2]
    nh, dh = o_b.shape[1], o_b.shape[3]
    assert ga_col % bw == 0
    gblk = ga_col // bw
    row = lambda b, i: (b, i, 0)
    bspec = pl.BlockSpec((1, tm, bw), row)
    hspec = pl.BlockSpec((1, nh, tm, dh), lambda b, i: (b, 0, i, 0))
    return pl.pallas_call(
        functools.partial(_merge_kernel, d=d),
        grid=(bn, sn // tm),
        in_specs=[pl.BlockSpec((1, tm, N_BRANCH * d), row),
                  pl.BlockSpec((1, tm, bw), lambda b, i: (b, i, gblk)),
                  pl.BlockSpec((1, tm, bw), lambda b, i: (b, i, gblk + 1)),
                  pl.BlockSpec((1, tm, bw), lambda b, i: (b, i, gblk + 2)),
                  bspec, bspec, hspec, hspec, hspec, hspec,
                  pl.BlockSpec((N_BRANCH, bw, d), lambda b, i: (0, 0, 0)),
                  pl.BlockSpec((d, d), lambda b, i: (0, 0)),
                  pl.BlockSpec((1, tm, d), row),
                  pl.BlockSpec((1, 1, d), lambda b, i: (b, 0, 0))],
        out_specs=pl.BlockSpec((1, tm, d), row),
        out_shape=jax.ShapeDtypeStruct((bn, sn, d), F32),
        compiler_params=_cparams("parallel", "parallel"),
        name="branch_merge",
    )(proj, proj, proj, proj, o_c, o_s, o_w, o_b, o_cd, o_d, w_up, w_out, x, g1)


def _router_kernel(x_ref, g_ref, sc_ref, sh_ref, wr_ref, br_ref, tri_ref, h_ref, e_ref, w_ref, rank_ref, cnt_ref,
                   run_scr):
    @pl.when((pl.program_id(0) == 0) & (pl.program_id(1) == 0))
    def _():
        run_scr[...] = jnp.zeros(run_scr.shape, F32)

    h = _norm_mod(x_ref[0], g_ref[...], sc_ref[0], sh_ref[0])
    h_ref[0] = h.astype(BF16)
    logits = jnp.dot(h, wr_ref[...], preferred_element_type=F32,
                     precision=lax.Precision.HIGHEST) + br_ref[...]
    lane = _iota2(logits.shape, 1)
    lane_f = lane.astype(F32)
    cur = logits
    vals, idxs = [], []
    chosen = jnp.zeros(logits.shape, F32)
    for _ in range(TOP_K):
        m = jnp.max(cur, axis=-1, keepdims=True)
        first = jnp.min(jnp.where(cur == m, lane_f, float(LANES)), axis=-1, keepdims=True)
        vals.append(m)
        idxs.append(first)
        hit = lane_f == first
        cur = jnp.where(hit, KNOCKOUT, cur)
        chosen = jnp.where(hit, 1.0, chosen)
    exps = [jnp.exp(v - vals[0]) for v in vals]
    den = exps[0]
    for e in exps[1:]:
        den = den + e
    earlier = _dot(tri_ref[...], chosen.astype(BF16)) + run_scr[0:1, :]
    e_out = jnp.zeros(logits.shape, F32)
    w_out = jnp.zeros(logits.shape, F32)
    r_out = jnp.zeros(logits.shape, F32)
    for k in range(TOP_K):
        rank_k = jnp.sum(jnp.where(lane_f == idxs[k], earlier, 0.0), axis=-1, keepdims=True)
        e_out = jnp.where(lane == k, idxs[k], e_out)
        w_out = jnp.where(lane == k, exps[k] / den, w_out)
        r_out = jnp.where(lane == k, rank_k, r_out)
    e_ref[0] = e_out[:, :TOP_K].astype(jnp.int32)
    w_ref[0] = w_out[:, :TOP_K]
    rank_ref[0] = r_out[:, :TOP_K].astype(jnp.int32)
    total = run_scr[...] + jnp.sum(chosen, axis=0, keepdims=True)
    run_scr[...] = total
    cnt_ref[...] = total


def _router(x, g, sc, sh, w_router, b_router, tm=ROW_TILE):
    bn, sn, d = x.shape
    ne = w_router.shape[1]
    wr = jnp.zeros((d, LANES), F32).at[:, :ne].set(w_router)
    br = jnp.full((1, LANES), NEG_INF, F32).at[0, :ne].set(b_router)
    tri = jnp.asarray((np.arange(tm)[:, None] > np.arange(tm)[None, :]).astype(np.float32), BF16)
    row = lambda b, i: (b, i, 0)
    return pl.pallas_call(
        _router_kernel,
        grid=(bn, sn // tm),
        in_specs=[pl.BlockSpec((1, tm, d), row),
                  pl.BlockSpec((1, d), lambda b, i: (0, 0)),
                  pl.BlockSpec((1, 1, d), lambda b, i: (b, 0, 0)),
                  pl.BlockSpec((1, 1, d), lambda b, i: (b, 0, 0)),
                  pl.BlockSpec((d, LANES), lambda b, i: (0, 0)),
                  pl.BlockSpec((1, LANES), lambda b, i: (0, 0)),
                  pl.BlockSpec((tm, tm), lambda b, i: (0, 0))],
        out_specs=[pl.BlockSpec((1, tm, d), row),
                   pl.BlockSpec((1, tm, TOP_K), row),
                   pl.BlockSpec((1, tm, TOP_K), row),
                   pl.BlockSpec((1, tm, TOP_K), row),
                   pl.BlockSpec((8, LANES), lambda b, i: (0, 0))],
        out_shape=[jax.ShapeDtypeStruct((bn, sn, d), BF16),
                   jax.ShapeDtypeStruct((bn, sn, TOP_K), jnp.int32),
                   jax.ShapeDtypeStruct((bn, sn, TOP_K), F32),
                   jax.ShapeDtypeStruct((bn, sn, TOP_K), jnp.int32),
                   jax.ShapeDtypeStruct((8, LANES), F32)],
        scratch_shapes=[pltpu.VMEM((8, LANES), F32)],
        compiler_params=_cparams("arbitrary", "arbitrary"),
        name="moe_router",
    )(x, g.reshape(1, d), sc, sh, wr, br, tri)


def _expert_kernel(ce_ref, x_ref, wgu_ref, bgu_ref, wdn_ref, bdn_ref, rw_ref, *rest, ff, fc, off):
    o_ref, wgu_scr, wdn_scr = rest[-3:]
    c = pl.program_id(0)

    @pl.when((c == 0) | (ce_ref[c + off] != ce_ref[jnp.maximum(c + off - 1, 0)]))
    def _():
        wgu_scr[...] = wgu_ref[0, 0].astype(BF16)
        wdn_scr[...] = wdn_ref[0, 0].astype(BF16)

    x = x_ref[...]
    y = None
    for j in range(ff // fc):
        g = _dot(x, wgu_scr[:, j * fc:(j + 1) * fc]) + bgu_ref[0, 0, :, j * fc:(j + 1) * fc]
        u = _dot(x, wgu_scr[:, ff + j * fc:ff + (j + 1) * fc]) + bgu_ref[0, 0, :, ff + j * fc:ff + (j + 1) * fc]
        g = jnp.minimum(g, SWIGLU_LIMIT)
        u = jnp.clip(u, -SWIGLU_LIMIT, SWIGLU_LIMIT)
        act = g * jax.nn.sigmoid(SWIGLU_ALPHA * g) * (u + 1.0)
        part = _dot(act.astype(BF16), wdn_scr[j * fc:(j + 1) * fc, :])
        y = part if y is None else y + part
    o_ref[...] = ((y + bdn_ref[0, 0]) * rw_ref[...]).astype(o_ref.dtype)


def _expert_ffn(h, row_tok, chunk_e, layer, w_gu, b_gu, w_dn, b_dn, row_w, tm=MOE_ROWS, fc=512, groups=MOE_SLICES):
    n_rows = row_tok.shape[0]
    d = h.shape[1]
    nl, ne, _, ff2 = w_gu.shape
    ff = ff2 // 2
    n_chunks = n_rows // tm
    assert n_chunks % groups == 0
    per = n_chunks // groups
    b_gu3, b_dn3, row_w2 = b_gu.reshape(nl, ne, 1, ff2), b_dn.reshape(nl, ne, 1, d), row_w.reshape(n_rows, 1)
    y = None
    for gi in range(groups):
        off = gi * per
        rows = h[row_tok[off * tm:(off + per) * tm]]
        in_specs = [pl.BlockSpec((tm, d), lambda c, ce: (c, 0)),
                    pl.BlockSpec((1, 1, d, ff2), lambda c, ce, off=off: (layer, ce[c + off], 0, 0)),
                    pl.BlockSpec((1, 1, 1, ff2), lambda c, ce, off=off: (layer, ce[c + off], 0, 0)),
                    pl.BlockSpec((1, 1, ff, d), lambda c, ce, off=off: (layer, ce[c + off], 0, 0)),
                    pl.BlockSpec((1, 1, 1, d), lambda c, ce, off=off: (layer, ce[c + off], 0, 0)),
                    pl.BlockSpec((tm, 1), lambda c, ce, off=off: (c + off, 0))]
        args = [chunk_e, rows, w_gu, b_gu3, w_dn, b_dn3, row_w2]
        aliases = {}
        if y is not None:
            in_specs.append(pl.BlockSpec(memory_space=pl.ANY))
            args.append(y)
            aliases = {len(args) - 1: 0}
        y = pl.pallas_call(
            functools.partial(_expert_kernel, ff=ff, fc=fc, off=off),
            grid_spec=pltpu.PrefetchScalarGridSpec(
                num_scalar_prefetch=1, grid=(per,), in_specs=in_specs,
                out_specs=pl.BlockSpec((tm, d), lambda c, ce, off=off: (c + off, 0)),
                scratch_shapes=[pltpu.VMEM((d, ff2), BF16), pltpu.VMEM((ff, d), BF16)]),
            out_shape=jax.ShapeDtypeStruct((n_rows, d), BF16),
            input_output_aliases=aliases,
            compiler_params=_cparams("arbitrary"),
            name="moe_expert_ffn",
        )(*args)
    return y


def _combine_kernel(y_ref, x_ref, g2_ref, o_ref):
    tot = y_ref[0].astype(F32)
    for k in range(1, TOP_K):
        tot = tot + y_ref[k].astype(F32)
    o_ref[0] = x_ref[0] + g2_ref[0] * tot


def _combine(y4, x, g2, tm=ROW_TILE):
    bn, sn, d = x.shape
    row = lambda b, i: (b, i, 0)
    nt = sn // tm
    return pl.pallas_call(
        _combine_kernel,
        grid=(bn, nt),
        in_specs=[pl.BlockSpec((TOP_K, tm, d), lambda b, i: (0, b * nt + i, 0)),
                  pl.BlockSpec((1, tm, d), row),
                  pl.BlockSpec((1, 1, d), lambda b, i: (b, 0, 0))],
        out_specs=pl.BlockSpec((1, tm, d), row),
        out_shape=jax.ShapeDtypeStruct((bn, sn, d), F32),
        compiler_params=_cparams("parallel", "parallel"),
        name="moe_combine",
    )(y4, x, g2)


def _moe(x, g, sc, sh, g2, w_router, b_router, layer, w_gu, b_gu, w_dn, b_dn):
    bn, sn, d = x.shape
    n_tok = bn * sn
    n_asg = n_tok * TOP_K
    tm = MOE_ROWS
    h, e_out, w_out, rank_out, totals = _router(x, g, sc, sh, w_router, b_router)
    e_tok = e_out.reshape(n_tok, TOP_K)
    w_flat = w_out.reshape(-1)
    counts = totals[0, :N_EXPERTS].astype(jnp.int32)
    starts = jnp.cumsum(counts) - counts
    padded = (counts + tm - 1) // tm * tm
    pad_ends = jnp.cumsum(padded)
    pad_starts = pad_ends - padded
    pos = pad_starts[e_tok] + rank_out.reshape(n_tok, TOP_K)
    n_chunks = n_asg // tm + N_EXPERTS
    chunk_start = jnp.arange(n_chunks, dtype=jnp.int32) * tm
    chunk_e = jnp.minimum(jnp.sum(chunk_start[:, None] >= pad_ends[None, :], axis=1), N_EXPERTS - 1).astype(jnp.int32)
    order = jnp.argsort(e_tok.reshape(-1))
    src = (chunk_start - (pad_starts - starts)[chunk_e])[:, None] + jnp.arange(tm, dtype=jnp.int32)[None, :]
    row_valid = (src < (starts + counts)[chunk_e][:, None]).reshape(-1)
    asg = order[jnp.clip(src, 0, n_asg - 1).reshape(-1)].astype(jnp.int32)
    row_tok = asg // TOP_K
    row_w = jnp.where(row_valid, w_flat[asg], 0.0)
    y = _expert_ffn(h.reshape(n_tok, d), row_tok, chunk_e, layer, w_gu, b_gu, w_dn, b_dn, row_w)
    return _combine(y[pos.T], x, g2)


def _norm_rope(x, g, cos, sin, bd, hd):
    sq = x * x
    hi = sq.astype(BF16)
    lo = (sq - hi.astype(F32)).astype(BF16)
    ss = _dot(hi, bd) + _dot(lo, bd)
    y = x * lax.rsqrt(ss * (1.0 / hd) + NORM_EPS) * g
    half = hd // 2
    first = (_iota2(x.shape, 1) & (hd - 1)) < half
    partner = jnp.where(first, pltpu.roll(y, LANES - half, 1), pltpu.roll(y, half, 1))
    return y * cos + partner * sin


def _prep_kernel(aq_ref, bq_ref, bk_ref, bv_ref, cq_ref, ck_ref, cv_ref, dq_ref, akc_ref, aks_ref, akw_ref,
                 dk_ref, dv_ref, c64_ref, s64_ref, c32_ref, s32_ref, gaq_ref, gak_ref, gcq_ref, gck_ref,
                 gdq_ref, gdk_ref, bd64_ref, bd32_ref,
                 oaq_ref, okc_ref, ovc_ref, okx_ref, ovs_ref, okw_ref, ovw_ref, obq_ref, obk_ref, obv_ref,
                 ocq_ref, ock_ref, ocv_ref, odq_ref, odk_ref, odv_ref, *, ts, sb_scale):
    hd, dd = HEAD_DIM, DIFF_DIM
    c64, s64, c32, s32 = c64_ref[0], s64_ref[0], c32_ref[0], s32_ref[0]
    bd64, bd32 = bd64_ref[...], bd32_ref[...]

    def slabs(ref):
        x = ref[0].astype(F32)
        return [x[:, c * LANES:(c + 1) * LANES] for c in range(x.shape[1] // LANES)]

    def put_heads(o_ref, c, y, width):
        per = LANES // width
        for u in range(per):
            o_ref[0, c * per + u] = y[:, u * width:(u + 1) * width].astype(o_ref.dtype)

    for src, gain, dst in ((aq_ref, gaq_ref, oaq_ref), (dq_ref, gdq_ref, odq_ref), (dk_ref, gdk_ref, odk_ref)):
        for c, x in enumerate(slabs(src)):
            put_heads(dst, c, _norm_rope(x, gain[:, c * LANES:(c + 1) * LANES], c64, s64, bd64, hd), hd)
    kc = _norm_rope(akc_ref[0].astype(F32), gak_ref[0:1, :], c64, s64, bd64, hd)
    ks = _norm_rope(aks_ref[0].astype(F32), gak_ref[1:2, :], c64, s64, bd64, hd)
    kw = _norm_rope(akw_ref[0].astype(F32), gak_ref[2:3, :], c64, s64, bd64, hd)
    okc_ref[0] = kc[:, :hd].astype(BF16)
    ovc_ref[0] = akc_ref[0][:, hd:].astype(BF16)
    nsel = okx_ref.shape[2] - hd
    blk = (pl.program_id(1) * ts + _iota2((ts, nsel), 0)) >> (NSA_SEL_BLOCK.bit_length() - 1)
    okx_ref[0, :, :nsel] = jnp.where(blk == _iota2((ts, nsel), 1), 1.0, 0.0).astype(BF16)
    okx_ref[0, :, nsel:] = ks[:, :hd].astype(BF16)
    ovs_ref[0] = aks_ref[0][:, hd:].astype(BF16)
    okw_ref[0, 0] = kw[:, :hd].astype(BF16)
    ovw_ref[0, 0] = akw_ref[0][:, hd:].astype(BF16)
    for c, x in enumerate(slabs(bq_ref)):
        put_heads(obq_ref, c, x * sb_scale, hd)
    for src, dst in ((bk_ref, obk_ref), (bv_ref, obv_ref), (cv_ref, ocv_ref)):
        for c, x in enumerate(slabs(src)):
            put_heads(dst, c, x, hd)
    put_heads(odv_ref, 0, dv_ref[0].astype(F32), hd)
    for src, gain, dst in ((cq_ref, gcq_ref, ocq_ref), (ck_ref, gck_ref, ock_ref)):
        for c, x in enumerate(slabs(src)):
            put_heads(dst, c, _norm_rope(x, gain[:, c * LANES:(c + 1) * LANES], c32, s32, bd32, dd), dd)


def _rope_tables(positions, hd):
    half = hd // 2
    inv = ROPE_THETA ** (-jnp.arange(half, dtype=F32) * 2.0 / hd)
    ang = positions.astype(F32)[..., None] * inv
    cos, sin = jnp.cos(ang), jnp.sin(ang)
    reps = LANES // hd
    return (jnp.tile(jnp.concatenate([cos, cos], axis=-1), (1, 1, reps)),
            jnp.tile(jnp.concatenate([-sin, sin], axis=-1), (1, 1, reps)))


def _prep(proj, cols, tables, p, n_sel, ts=ROW_TILE):
    bn, sn, _ = proj.shape
    hd, dd = HEAD_DIM, DIFF_DIM
    scale = hd ** -0.5
    c64, s64, c32, s32 = tables

    def cspec(name):
        off, width = cols[name]
        assert off % width == 0
        return pl.BlockSpec((1, ts, width), lambda b, i, blk=off // width: (b, i, blk))

    def tile_gain(g, reps, mult=1.0):
        return (jnp.tile(g.astype(F32), reps) * mult).reshape(1, -1)

    ones = jnp.ones((hd,), F32)
    gak = jnp.stack([jnp.concatenate([p["nsa_kn"][j].astype(F32), ones]) for j in range(3)])
    gains = [tile_gain(p["nsa_qn"], NSA_HEADS, scale), gak,
             tile_gain(p["dif_qn"], 2 * DIFF_HEADS, dd ** -0.5), tile_gain(p["dif_kn"], 2 * DIFF_HEADS),
             tile_gain(p["swa_qn"], SWA_HEADS, scale), tile_gain(p["swa_kn"], SWA_KV_HEADS)]
    lane = np.arange(LANES)
    bd64 = jnp.asarray((lane[:, None] // hd == lane[None, :] // hd).astype(np.float32), BF16)
    bd32 = jnp.asarray((lane[:, None] // dd == lane[None, :] // dd).astype(np.float32), BF16)
    names = ("a_q", "b_q", "b_k", "b_v", "c_q", "c_k", "c_v", "d_q", "a_kcvc", "a_ksvs", "a_kwvw", "d_k", "d_v")
    tab = pl.BlockSpec((1, ts, LANES), lambda b, i: (b, i, 0))
    full = lambda a: pl.BlockSpec(a.shape, lambda b, i: (0,) * a.ndim)

    def hm(nh, w):
        return (jax.ShapeDtypeStruct((bn, nh, sn, w), BF16), pl.BlockSpec((1, nh, ts, w), lambda b, i: (b, 0, i, 0)))

    def tm_(w):
        return (jax.ShapeDtypeStruct((bn, sn, w), BF16), pl.BlockSpec((1, ts, w), lambda b, i: (b, i, 0)))

    outs = [hm(NSA_HEADS, hd), tm_(hd), tm_(hd), tm_(n_sel + hd), tm_(hd), hm(1, hd), hm(1, hd),
            hm(SB_HEADS, hd), hm(SB_HEADS, hd), hm(SB_HEADS, hd),
            hm(2 * DIFF_HEADS, dd), hm(2 * DIFF_HEADS, dd), hm(DIFF_HEADS, 2 * dd),
            hm(SWA_HEADS, hd), hm(SWA_KV_HEADS, hd), hm(SWA_KV_HEADS, hd)]
    consts = gains + [bd64, bd32]
    res = pl.pallas_call(
        functools.partial(_prep_kernel, ts=ts, sb_scale=-scale * LOG2E),
        grid=(bn, sn // ts),
        in_specs=[cspec(n) for n in names] + [tab] * 4 + [full(a) for a in consts],
        out_specs=[o[1] for o in outs],
        out_shape=[o[0] for o in outs],
        compiler_params=_cparams("parallel", "parallel"),
        name="mixer_prep",
    )(*([proj] * len(names)), c64, s64, c32, s32, *consts)
    keys = ("a_q", "kc", "vc", "kx", "vs", "kw", "vw", "b_q", "b_k", "b_v", "c_q", "c_k", "c_v", "d_q", "d_k", "d_v")
    return dict(zip(keys, res))


def _layer_columns(d):
    cols = {}
    off = 0
    for name, width in (("br_g", N_BRANCH * d), ("ga", 3 * NSA_HEADS * HEAD_DIM), ("a_q", 256),
                        ("b_q", 256), ("b_k", 256), ("b_v", 256), ("c_q", 256), ("c_k", 256), ("c_v", 256),
                        ("d_q", 256), ("a_kcvc", 128), ("a_ksvs", 128), ("a_kwvw", 128), ("d_k", 128), ("d_v", 128)):
        cols[name] = (off, width)
        off += width
    return cols, off


def _reorder_w_in(w_in, d, n_pad):
    ref_splits = (256, 128, 128, 128, 12, 256, 256, 256, 256, 256, 256, 256, 128, 128, N_BRANCH * d)
    names = ("a_q", "a_kcvc", "a_ksvs", "a_kwvw", "a_g", "b_q", "b_k", "b_v",
             "c_q", "c_k", "c_v", "d_q", "d_k", "d_v", "br_g")
    starts = np.cumsum((0,) + ref_splits)
    src = {n: (int(starts[i]), ref_splits[i]) for i, n in enumerate(names)}
    cols, total = _layer_columns(d)
    idx = np.zeros((n_pad,), np.int32)
    keep = np.zeros((n_pad,), np.float32)
    for name, (off, width) in cols.items():
        if name == "ga":
            g0 = src["a_g"][0]
            for j in range(3):
                for h in range(NSA_HEADS):
                    base = off + j * NSA_HEADS * HEAD_DIM + h * HEAD_DIM
                    idx[base:base + HEAD_DIM] = g0 + h * 3 + j
        else:
            idx[off:off + width] = src[name][0] + np.arange(width)
        keep[off:off + width] = 1.0
    w = w_in[:, idx] * keep[None, :]
    return w.astype(BF16)


def _mixer_layer(x, positions, tables, mod, p, lam_init):
    bn, sn, d = x.shape
    sh1, sc1, g1 = mod[0], mod[1], mod[2]
    cols, total = _layer_columns(d)
    n_pad = -(-total // PROJ_TILE[1]) * PROJ_TILE[1]
    w_in = _reorder_w_in(p["w_in"], d, n_pad)
    proj = _in_projection(x, p["norm1"], sc1, sh1, w_in)
    n_cmp = (sn - NSA_CMP_LEN) // NSA_CMP_STRIDE + 1
    n_sel = sn // NSA_SEL_BLOCK
    n_top = min(NSA_N_SEL, n_sel)
    t = _prep(proj, cols, tables, p, n_sel)
    k_cmp = _compress(t["kc"], p["nsa_pe_k"], p["nsa_w1_k"], p["nsa_w2_k"]).astype(BF16)
    v_cmp = _compress(t["vc"], p["nsa_pe_v"], p["nsa_w1_v"], p["nsa_w2_v"]).astype(BF16)
    cmp_start = np.arange(sn // NSA_CMP_STRIDE) * NSA_CMP_STRIDE
    sel_start = np.arange(n_sel) * NSA_SEL_BLOCK
    overlap = ((cmp_start[:, None] <= (sel_start + NSA_SEL_BLOCK - 1)[None, :]) &
               ((cmp_start + NSA_CMP_LEN - 1)[:, None] >= sel_start[None, :]) &
               (np.arange(sn // NSA_CMP_STRIDE) < n_cmp)[:, None]).astype(np.float32)
    o_c, bias = _nsa_cmp(t["a_q"], k_cmp, v_cmp, jnp.asarray(overlap, BF16), n_cmp, n_top)
    o_s = _nsa_sel(t["a_q"], t["kx"], t["vs"], bias)
    o_w = _banded_attention(t["a_q"], t["kw"], t["vw"], NSA_WINDOW)
    o_b = _stick_breaking(t["b_q"], t["b_k"], t["b_v"])
    lam = (jnp.exp(jnp.sum(p["dif_lq1"] * p["dif_lk1"])) - jnp.exp(jnp.sum(p["dif_lq2"] * p["dif_lk2"]))
           + lam_init)
    o_cd = _diff_attention(t["c_q"], t["c_k"], t["c_v"], p["dif_subln"], lam, jnp.asarray(1.0 - lam_init, F32))
    o_d = _banded_attention(t["d_q"], t["d_k"], t["d_v"], SWA_WINDOW, sinks=p["swa_sinks"])
    return _merge(proj, o_c, o_s, o_w, o_b, o_cd, o_d, p["w_up"].astype(BF16), p["w_out"].astype(BF16),
                  x, g1, cols["ga"][0])


def kernel(x, c, positions, w_ada, b_ada, norm1, norm2, w_in, nsa_qn, nsa_kn, nsa_pe_k, nsa_w1_k, nsa_w2_k,
           nsa_pe_v, nsa_w1_v, nsa_w2_v, dif_qn, dif_kn, dif_lq1, dif_lk1, dif_lq2, dif_lk2, dif_subln,
           swa_qn, swa_kn, swa_sinks, w_up, w_out, w_router, b_router, w_gu, b_gu, w_dn, b_dn):
    bn, sn, d = x.shape
    depth = w_ada.shape[0]
    c_pad = jnp.zeros((8, d), F32).at[:bn].set(c)
    tables = _rope_tables(positions, HEAD_DIM) + _rope_tables(positions, DIFF_DIM)
    for l in range(depth):
        lam_init = 0.8 - 0.6 * math.exp(-0.3 * l)
        mod = _linear(c_pad, w_ada[l], b_ada[l], tn=512, precision=lax.Precision.HIGHEST)[:bn]
        mod = mod.reshape(bn, 6, 1, d).transpose(1, 0, 2, 3)
        p = dict(norm1=norm1[l], w_in=w_in[l], nsa_qn=nsa_qn[l], nsa_kn=nsa_kn[l], nsa_pe_k=nsa_pe_k[l],
                 nsa_w1_k=nsa_w1_k[l], nsa_w2_k=nsa_w2_k[l], nsa_pe_v=nsa_pe_v[l], nsa_w1_v=nsa_w1_v[l],
                 nsa_w2_v=nsa_w2_v[l], dif_qn=dif_qn[l], dif_kn=dif_kn[l], dif_lq1=dif_lq1[l],
                 dif_lk1=dif_lk1[l], dif_lq2=dif_lq2[l], dif_lk2=dif_lk2[l], dif_subln=dif_subln[l],
                 swa_qn=swa_qn[l], swa_kn=swa_kn[l], swa_sinks=swa_sinks[l], w_up=w_up[l], w_out=w_out[l])
        x = _mixer_layer(x, positions, tables, mod, p, lam_init)
        x = _moe(x, norm2[l], mod[4], mod[3], mod[5], w_router[l], b_router[l], l, w_gu, b_gu, w_dn, b_dn)
    return x
```

```python
import functools
import math

import numpy as np
import jax
import jax.numpy as jnp
from jax import lax
from jax.experimental import pallas as pl
from jax.experimental.pallas import tpu as pltpu

F32 = jnp.float32
BF16 = jnp.bfloat16

HEAD_DIM = 64
ROPE_THETA = 10000.0
NORM_EPS = 1e-6
NEG_INF = -1e30
KNOCKOUT = -3e38
N_BRANCH = 4

NSA_HEADS = 4
NSA_CMP_LEN = 32
NSA_CMP_STRIDE = 16
NSA_SEL_BLOCK = 64
NSA_N_SEL = 16
NSA_WINDOW = 512
NSA_FORCED_SCORE = 1e4

SB_HEADS = 4
DIFF_HEADS = 4
DIFF_DIM = 32
SWA_HEADS = 4
SWA_KV_HEADS = 2
SWA_WINDOW = 128

N_EXPERTS = 32
TOP_K = 4
SWIGLU_ALPHA = 1.702
SWIGLU_LIMIT = 7.0

LANES = 128
LOG2E = 1.4426950408889634
VMEM_LIMIT = 56 * 1024 * 1024

FLASH_TILE = 1024
SB_KEY_TILE = 256
BAND_TILE = 1024
CMP_TILE = 512
PROJ_TILE = (1024, 2560)
ROW_TILE = 512
MERGE_TILE = 512
MOE_ROWS = 512
MOE_SLICES = 4


def _cparams(*sem):
    return pltpu.CompilerParams(dimension_semantics=sem, vmem_limit_bytes=VMEM_LIMIT)


def _dot(a, b):
    return jnp.dot(a, b, preferred_element_type=F32)


def _dot_nt(a, b):
    return lax.dot_general(a, b, (((1,), (1,)), ((), ())), preferred_element_type=F32)


def _iota2(shape, dim):
    return lax.broadcasted_iota(jnp.int32, shape, dim)


def _linear_kernel(x_ref, w_ref, b_ref, o_ref, *, precision):
    o_ref[...] = jnp.dot(x_ref[...], w_ref[...], preferred_element_type=F32,
                         precision=precision) + b_ref[...]


def _linear(x, w, b, tn, precision=None):
    m, k = x.shape
    n = w.shape[1]
    return pl.pallas_call(
        functools.partial(_linear_kernel, precision=precision),
        grid=(n // tn,),
        in_specs=[pl.BlockSpec((m, k), lambda j: (0, 0)),
                  pl.BlockSpec((k, tn), lambda j: (0, j)),
                  pl.BlockSpec((1, tn), lambda j: (0, j))],
        out_specs=pl.BlockSpec((m, tn), lambda j: (0, j)),
        out_shape=jax.ShapeDtypeStruct((m, n), F32),
        compiler_params=_cparams("arbitrary"),
        name="linear",
    )(x, w, b.reshape(1, n))


def _cmp_mlp_kernel(a_ref, b_ref, pe_ref, w2_ref, o_ref):
    hid = jax.nn.gelu(a_ref[...] + b_ref[...] + pe_ref[...])
    o_ref[...] = _dot(hid.astype(BF16), w2_ref[...])


def _compress(t, pe, w1, w2):
    bn, sn, dh = t.shape
    st = NSA_CMP_STRIDE
    half = st * dh
    nb = sn // st
    t16 = t.reshape(bn * nb, half).astype(BF16)
    w1cat = jnp.concatenate([w1[:half], w1[half:]], axis=1).astype(BF16)
    hidden = w1.shape[1]
    ab = _linear(t16, w1cat, jnp.zeros((2 * hidden,), F32), tn=2 * hidden)
    ab = ab.reshape(bn, nb, 2 * hidden)
    a = ab[:, :, :hidden]
    b_next = jnp.concatenate([ab[:, 1:, hidden:], jnp.zeros((bn, 1, hidden), F32)], axis=1)
    pe_term = jnp.dot(pe.reshape(1, NSA_CMP_LEN * dh), w1, precision=lax.Precision.HIGHEST)
    rows = bn * nb
    tm = min(512, rows)
    out = pl.pallas_call(
        _cmp_mlp_kernel,
        grid=(rows // tm,),
        in_specs=[pl.BlockSpec((tm, hidden), lambda i: (i, 0)),
                  pl.BlockSpec((tm, hidden), lambda i: (i, 0)),
                  pl.BlockSpec((1, hidden), lambda i: (0, 0)),
                  pl.BlockSpec((hidden, dh), lambda i: (0, 0))],
        out_specs=pl.BlockSpec((tm, dh), lambda i: (i, 0)),
        out_shape=jax.ShapeDtypeStruct((rows, dh), F32),
        compiler_params=_cparams("parallel"),
        name="cmp_mlp",
    )(a.reshape(rows, hidden), b_next.reshape(rows, hidden), pe_term, w2.astype(BF16))
    return out.reshape(bn, nb, dh)


def _norm_mod(x, g, sc, sh):
    r = lax.rsqrt(jnp.mean(x * x, axis=-1, keepdims=True) + NORM_EPS)
    return (x * r * g) * (1.0 + sc) + sh


def _proj_kernel(x_ref, g_ref, sc_ref, sh_ref, w_ref, o_ref, h_scr):
    @pl.when(pl.program_id(2) == 0)
    def _():
        h_scr[...] = _norm_mod(x_ref[0], g_ref[...], sc_ref[0], sh_ref[0]).astype(BF16)

    o_ref[0] = _dot(h_scr[...], w_ref[...]).astype(o_ref.dtype)


def _in_projection(x, g, sc, sh, w, tm=PROJ_TILE[0], tn=PROJ_TILE[1]):
    bn, sn, d = x.shape
    n = w.shape[1]
    tm = min(tm, sn)
    return pl.pallas_call(
        _proj_kernel,
        grid=(bn, sn // tm, n // tn),
        in_specs=[pl.BlockSpec((1, tm, d), lambda b, i, j: (b, i, 0)),
                  pl.BlockSpec((1, d), lambda b, i, j: (0, 0)),
                  pl.BlockSpec((1, 1, d), lambda b, i, j: (b, 0, 0)),
                  pl.BlockSpec((1, 1, d), lambda b, i, j: (b, 0, 0)),
                  pl.BlockSpec((d, tn), lambda b, i, j: (0, j))],
        out_specs=pl.BlockSpec((1, tm, tn), lambda b, i, j: (b, i, j)),
        out_shape=jax.ShapeDtypeStruct((bn, sn, n), BF16),
        scratch_shapes=[pltpu.VMEM((tm, d), BF16)],
        compiler_params=_cparams("parallel", "parallel", "arbitrary"),
        name="in_proj",
    )(x, g.reshape(1, d), sc, sh, w)


def _banded_kernel(*refs, tile, window, has_sink):
    if has_sink:
        sink_ref, q_ref, kp_ref, kc_ref, vp_ref, vc_ref, o_ref = refs
    else:
        q_ref, kp_ref, kc_ref, vp_ref, vc_ref, o_ref = refs
    i = pl.program_id(2)
    w = window
    upper = _iota2((w, w), 1) > _iota2((w, w), 0)
    first_bias = jnp.where(i > 0, 0.0, NEG_INF)
    for u in range(tile // w):
        q = q_ref[0, 0, u * w:(u + 1) * w, :]
        if u == 0:
            k_prev, v_prev = kp_ref[0, 0], vp_ref[0, 0]
        else:
            k_prev, v_prev = kc_ref[0, 0, (u - 1) * w:u * w, :], vc_ref[0, 0, (u - 1) * w:u * w, :]
        k_cur, v_cur = kc_ref[0, 0, u * w:(u + 1) * w, :], vc_ref[0, 0, u * w:(u + 1) * w, :]
        s_prev = _dot_nt(q, k_prev)
        if u == 0:
            s_prev = s_prev + first_bias
        s = jnp.where(upper, s_prev, _dot_nt(q, k_cur))
        m = jnp.max(s, axis=-1, keepdims=True)
        if has_sink:
            sink = sink_ref[pl.program_id(1)]
            m = jnp.maximum(m, sink)
        p = jnp.exp(s - m)
        den = jnp.sum(p, axis=-1, keepdims=True)
        if has_sink:
            den = den + jnp.exp(sink - m)
        else:
            den = jnp.maximum(den, 1e-30)
        o = (_dot(jnp.where(upper, p, 0.0).astype(BF16), v_prev)
             + _dot(jnp.where(upper, 0.0, p).astype(BF16), v_cur))
        o_ref[0, 0, u * w:(u + 1) * w, :] = o / den


def _banded_attention(q, k, v, window, sinks=None, tile=BAND_TILE):
    bn, hq, sn, d = q.shape
    grp = hq // k.shape[1]
    tile = min(tile, sn)
    assert tile % window == 0 and sn % tile == 0
    per = tile // window
    has_sink = sinks is not None
    qspec = pl.BlockSpec((1, 1, tile, d), lambda b, h, i: (b, h, i, 0))
    prev = pl.BlockSpec((1, 1, window, d), lambda b, h, i: (b, h // grp, jnp.maximum(i * per - 1, 0), 0))
    cur = pl.BlockSpec((1, 1, tile, d), lambda b, h, i: (b, h // grp, i, 0))
    in_specs = [qspec, prev, cur, prev, cur]
    args = [q, k, k, v, v]
    if has_sink:
        in_specs = [pl.BlockSpec(memory_space=pltpu.SMEM)] + in_specs
        args = [sinks.astype(F32)] + args
    return pl.pallas_call(
        functools.partial(_banded_kernel, tile=tile, window=window, has_sink=has_sink),
        grid=(bn, hq, sn // tile),
        in_specs=in_specs,
        out_specs=pl.BlockSpec((1, 1, tile, d), lambda b, h, i: (b, h, i, 0)),
        out_shape=jax.ShapeDtypeStruct((bn, hq, sn, d), F32),
        compiler_params=_cparams("parallel", "parallel", "parallel"),
        name="banded_attn",
    )(*args)


def _nsa_cmp_kernel(q_ref, kc_ref, vct_ref, ovt_ref, oct_ref, bias_ref, *, tq, n_cmp, n_top, heads):
    i = pl.program_id(1)
    ncp = kc_ref.shape[1]
    nsel = ovt_ref.shape[0]
    dh = q_ref.shape[3]
    t = i * tq + _iota2((ncp, tq), 1)
    n = _iota2((ncp, tq), 0)
    valid = (n * NSA_CMP_STRIDE + (NSA_CMP_LEN - 1) <= t) & (n < n_cmp)
    kc = kc_ref[0]
    vct = vct_ref[0]
    psum = jnp.zeros((ncp, tq), F32)
    for h in range(heads):
        s = jnp.where(valid, _dot_nt(kc, q_ref[0, h]), NEG_INF)
        m = jnp.max(s, axis=0, keepdims=True)
        p = jnp.where(valid, jnp.exp(s - m), 0.0)
        p = p / jnp.maximum(jnp.sum(p, axis=0, keepdims=True), 1e-30)
        oct_ref[0, h * dh:(h + 1) * dh, :] = _dot(vct, p.astype(BF16))
        psum = psum + p
    hi = psum.astype(BF16)
    lo = (psum - hi.astype(F32)).astype(BF16)
    imp = _dot(ovt_ref[...], hi) + _dot(ovt_ref[...], lo)

    tt = i * tq + _iota2((nsel, tq), 1)
    blk = _iota2((nsel, tq), 0)
    cur = tt >> (NSA_SEL_BLOCK.bit_length() - 1)
    forced = (blk == 0) | (blk == cur) | (blk == cur - 1)
    valid_s = blk * NSA_SEL_BLOCK <= tt
    score = jnp.where(forced, NSA_FORCED_SCORE, jnp.where(valid_s, imp, -1.0))
    blk_f = blk.astype(F32)

    def pick(_, carry):
        score, sel = carry
        m = jnp.max(score, axis=0, keepdims=True)
        first = jnp.min(jnp.where(score == m, blk_f, float(nsel)), axis=0, keepdims=True)
        hit = blk_f == first
        return jnp.where(hit, KNOCKOUT, score), jnp.where(hit, 0.0, sel)

    _, bias = lax.fori_loop(0, n_top, pick, (score, jnp.full((nsel, tq), NEG_INF, F32)))
    bias_ref[0] = bias.astype(BF16)


def _nsa_cmp(q, kc, vc, overlap, n_cmp, n_top, tq=CMP_TILE):
    bn, heads, sn, dh = q.shape
    tq = min(tq, sn)
    ncp = kc.shape[1]
    nsel = overlap.shape[1]
    oct, bias_t = pl.pallas_call(
        functools.partial(_nsa_cmp_kernel, tq=tq, n_cmp=n_cmp, n_top=n_top, heads=heads),
        grid=(bn, sn // tq),
        in_specs=[pl.BlockSpec((1, heads, tq, dh), lambda b, i: (b, 0, i, 0)),
                  pl.BlockSpec((1, ncp, dh), lambda b, i: (b, 0, 0)),
                  pl.BlockSpec((1, dh, ncp), lambda b, i: (b, 0, 0)),
                  pl.BlockSpec((nsel, ncp), lambda b, i: (0, 0))],
        out_specs=[pl.BlockSpec((1, heads * dh, tq), lambda b, i: (b, 0, i)),
                   pl.BlockSpec((1, nsel, tq), lambda b, i: (b, 0, i))],
        out_shape=[jax.ShapeDtypeStruct((bn, heads * dh, sn), F32),
                   jax.ShapeDtypeStruct((bn, nsel, sn), BF16)],
        compiler_params=_cparams("parallel", "parallel"),
        name="nsa_cmp_topk",
    )(q, kc, vc.transpose(0, 2, 1), overlap.T)
    return oct.transpose(0, 2, 1), bias_t.transpose(0, 2, 1)


def _nsa_sel_kernel(q_ref, kx_ref, v_ref, bias_ref, o_ref, qx_scr, *, tq, tk, heads):
    i = pl.program_id(1)
    dh = v_ref.shape[2]
    bias = bias_ref[0]
    for h in range(heads):
        qx_scr[h] = jnp.concatenate([bias, q_ref[0, h]], axis=1)

    def tile(j, state, diagonal):
        start = pl.multiple_of(j * tk, tk)
        kx = kx_ref[0, pl.ds(start, tk), :]
        v = v_ref[0, pl.ds(start, tk), :]
        if diagonal:
            causal = (start + _iota2((tq, tk), 1)) <= (i * tq + _iota2((tq, tk), 0))
        new_state = []
        for h in range(heads):
            m_old, l_old, acc_old = state[h]
            s = _dot_nt(qx_scr[h], kx)
            if diagonal:
                s = jnp.where(causal, s, NEG_INF)
            m_new = jnp.maximum(m_old, jnp.max(s, axis=-1, keepdims=True))
            alpha = jnp.exp(m_old - m_new)
            p = jnp.exp(s - m_new)
            l_new = alpha * l_old + jnp.sum(p, axis=-1, keepdims=True)
            acc_new = alpha * acc_old + _dot(p.astype(BF16), v)
            new_state.append((m_new, l_new, acc_new))
        return tuple(new_state)

    init = tuple((jnp.full((tq, 1), NEG_INF, F32), jnp.zeros((tq, 1), F32), jnp.zeros((tq, dh), F32))
                 for _ in range(heads))
    n_full = (i * tq) // tk
    state = lax.fori_loop(0, n_full, lambda j, st: tile(j, st, False), init)
    state = tile(n_full, state, True)
    for h in range(heads):
        o_ref[0, :, h * dh:(h + 1) * dh] = state[h][2] / jnp.maximum(state[h][1], 1e-30)


def _nsa_sel(q, kx, v, bias, tq=FLASH_TILE, tk=FLASH_TILE):
    bn, heads, sn, dh = q.shape
    nsel = bias.shape[2]
    tq, tk = min(tq, sn), min(tk, sn)
    assert tk % tq == 0 and sn % tk == 0
    return pl.pallas_call(
        functools.partial(_nsa_sel_kernel, tq=tq, tk=tk, heads=heads),
        grid=(bn, sn // tq),
        in_specs=[pl.BlockSpec((1, heads, tq, dh), lambda b, i: (b, 0, i, 0)),
                  pl.BlockSpec((1, sn, nsel + dh), lambda b, i: (b, 0, 0)),
                  pl.BlockSpec((1, sn, dh), lambda b, i: (b, 0, 0)),
                  pl.BlockSpec((1, tq, nsel), lambda b, i: (b, i, 0))],
        out_specs=pl.BlockSpec((1, tq, heads * dh), lambda b, i: (b, i, 0)),
        out_shape=jax.ShapeDtypeStruct((bn, sn, heads * dh), F32),
        scratch_shapes=[pltpu.VMEM((heads, tq, nsel + dh), BF16)],
        compiler_params=_cparams("parallel", "arbitrary"),
        name="nsa_selected_attn",
    )(q, kx, v, bias)


def _sb_kernel(q_ref, k_ref, v_ref, u_ref, o_ref, *, tq, tk):
    i = pl.program_id(2)
    q = q_ref[0, 0]
    per_q = tq // tk

    def tile(jj, carry, first_row):
        diagonal = first_row is not None
        r0 = first_row if diagonal else 0
        start = pl.multiple_of(jj * tk, tk)
        nz = _dot_nt(q[r0:], k_ref[0, 0, pl.ds(start, tk), :])
        neg_abs = lax.bitcast_convert_type(lax.bitcast_convert_type(nz, jnp.uint32) | jnp.uint32(0x80000000), F32)
        log_keep = jnp.minimum(nz, 0.0) - jnp.log2(1.0 + jnp.exp2(neg_abs))
        if diagonal:
            strict = _iota2(nz.shape, 1) < _iota2(nz.shape, 0)
            log_keep = jnp.where(strict, log_keep, 0.0)
        cum = _dot(log_keep.astype(BF16), u_ref[...])
        a = jnp.exp2(cum + jnp.concatenate([carry[r0:]] * (tk // LANES), axis=1) - nz)
        if diagonal:
            a = jnp.where(strict, a, 0.0)
        out = _dot(a.astype(BF16), v_ref[0, 0, pl.ds(start, tk), :])
        new_carry = carry[r0:] + jnp.broadcast_to(cum[:, 0:1], (tq - r0, LANES))
        if r0:
            out = jnp.concatenate([jnp.zeros((r0, out.shape[1]), F32), out], axis=0)
            new_carry = jnp.concatenate([carry[:r0], new_carry], axis=0)
        return out, new_carry

    def group(first, carry, diagonal):
        total = None
        for r in range(per_q):
            out, carry = tile(first - r, carry, (per_q - 1 - r) * tk if diagonal else None)
            total = out if total is None else total + out
        return total, carry

    acc, carry = group(i * per_q + per_q - 1, jnp.zeros((tq, LANES), F32), True)

    def body(p, state):
        acc, carry = state
        out, carry = group((i - p) * per_q - 1, carry, False)
        return acc + out, carry

    acc, _ = lax.fori_loop(0, i, body, (acc, carry))
    o_ref[0, 0] = acc


def _stick_breaking(q, k, v, tq=FLASH_TILE, tk=SB_KEY_TILE):
    bn, heads, sn, dh = q.shape
    tq = min(tq, sn)
    incl = (np.arange(tk)[:, None] >= np.arange(tk)[None, :]).astype(np.float32)
    u = jnp.asarray(incl, BF16)
    kv = pl.BlockSpec((1, 1, sn, dh), lambda b, h, i: (b, h, 0, 0))
    return pl.pallas_call(
        functools.partial(_sb_kernel, tq=tq, tk=tk),
        grid=(bn, heads, sn // tq),
        in_specs=[pl.BlockSpec((1, 1, tq, dh), lambda b, h, i: (b, h, i, 0)), kv, kv,
                  pl.BlockSpec((tk, tk), lambda b, h, i: (0, 0))],
        out_specs=pl.BlockSpec((1, 1, tq, dh), lambda b, h, i: (b, h, i, 0)),
        out_shape=jax.ShapeDtypeStruct((bn, heads, sn, dh), F32),
        compiler_params=_cparams("parallel", "parallel", "arbitrary"),
        name="stick_breaking_attn",
    )(q, k, v, u)


def _diff_kernel(sc_ref, q_ref, k_ref, v_ref, g_ref, o_ref, *, tq, tk):
    i = pl.program_id(2)
    dv = v_ref.shape[3]

    def tile(j, state, diagonal):
        start = pl.multiple_of(j * tk, tk)
        v = v_ref[0, 0, pl.ds(start, tk), :]
        if diagonal:
            causal = (start + _iota2((tq, tk), 1)) <= (i * tq + _iota2((tq, tk), 0))
        new_state = []
        for mi in range(2):
            m_old, l_old, acc_old = state[mi]
            s = _dot_nt(q_ref[0, mi], k_ref[0, mi, pl.ds(start, tk), :])
            if diagonal:
                s = jnp.where(causal, s, NEG_INF)
            m_new = jnp.maximum(m_old, jnp.max(s, axis=-1, keepdims=True))
            alpha = jnp.exp(m_old - m_new)
            p = jnp.exp(s - m_new)
            l_new = alpha * l_old + jnp.sum(p, axis=-1, keepdims=True)
            acc_new = alpha * acc_old + _dot(p.astype(BF16), v)
            new_state.append((m_new, l_new, acc_new))
        return tuple(new_state)

    init = tuple((jnp.full((tq, 1), NEG_INF, F32), jnp.zeros((tq, 1), F32), jnp.zeros((tq, dv), F32))
                 for _ in range(2))
    n_full = (i * tq) // tk
    state = lax.fori_loop(0, n_full, lambda j, st: tile(j, st, False), init)
    state = tile(n_full, state, True)
    lam = sc_ref[0]
    post = sc_ref[1]
    o = (state[0][2] / jnp.maximum(state[0][1], 1e-30)
         - lam * (state[1][2] / jnp.maximum(state[1][1], 1e-30)))
    r = lax.rsqrt(jnp.mean(o * o, axis=-1, keepdims=True) + NORM_EPS)
    o_ref[0, 0] = (o * r * g_ref[...]) * post


def _diff_attention(q, k, v, subln, lam, post, tq=FLASH_TILE, tk=FLASH_TILE):
    bn, h2, sn, dd = q.shape
    heads = h2 // 2
    dv = v.shape[3]
    tq, tk = min(tq, sn), min(tk, sn)
    assert tk % tq == 0 and sn % tk == 0
    scal = jnp.stack([lam, post]).astype(F32)
    return pl.pallas_call(
        functools.partial(_diff_kernel, tq=tq, tk=tk),
        grid=(bn, heads, sn // tq),
        in_specs=[pl.BlockSpec(memory_space=pltpu.SMEM),
                  pl.BlockSpec((1, 2, tq, dd), lambda b, h, i: (b, h, i, 0)),
                  pl.BlockSpec((1, 2, sn, dd), lambda b, h, i: (b, h, 0, 0)),
                  pl.BlockSpec((1, 1, sn, dv), lambda b, h, i: (b, h, 0, 0)),
                  pl.BlockSpec((1, dv), lambda b, h, i: (0, 0))],
        out_specs=pl.BlockSpec((1, 1, tq, dv), lambda b, h, i: (b, h, i, 0)),
        out_shape=jax.ShapeDtypeStruct((bn, heads, sn, dv), F32),
        compiler_params=_cparams("parallel", "parallel", "arbitrary"),
        name="diff_attn",
    )(scal, q, k, v, subln.reshape(1, dv).astype(F32))


def _merge_kernel(brg_ref, ga0_ref, ga1_ref, ga2_ref, oc_ref, os_ref, ow_ref, ob_ref, ocd_ref, od_ref,
                  wup_ref, wout_ref, x_ref, g1_ref, o_ref, *, d):
    def heads(ref):
        return jnp.concatenate([ref[0, h] for h in range(ref.shape[1])], axis=1)

    def gate(x):
        return jax.nn.sigmoid(x.astype(F32))

    o_a = gate(ga0_ref[0]) * oc_ref[0] + gate(ga1_ref[0]) * os_ref[0] + gate(ga2_ref[0]) * heads(ow_ref)
    branches = (o_a, heads(ob_ref), heads(ocd_ref), heads(od_ref))
    merged = None
    for bi, o in enumerate(branches):
        term = gate(brg_ref[0, :, bi * d:(bi + 1) * d]) * _dot(o.astype(BF16), wup_ref[bi])
        merged = term if merged is None else merged + term
    y = _dot(merged.astype(BF16), wout_ref[...])
    o_ref[0] = x_ref[0] + g1_ref[0] * y


def _merge(proj, o_c, o_s, o_w, o_b, o_cd, o_d, w_up, w_out, x, g1, ga_col, tm=MERGE_TILE):
    bn, sn, d = x.shape
    bw = o_c.shape[2]
    nh, dh = o_b.shape[1], o_b.shape[3]
    assert ga_col % bw == 0
    gblk = ga_col // bw
    row = lambda b, i: (b, i, 0)
    bspec = pl.BlockSpec((1, tm, bw), row)
    hspec = pl.BlockSpec((1, nh, tm, dh), lambda b, i: (b, 0, i, 0))
    return pl.pallas_call(
        functools.partial(_merge_kernel, d=d),
        grid=(bn, sn // tm),
        in_specs=[pl.BlockSpec((1, tm, N_BRANCH * d), row),
                  pl.BlockSpec((1, tm, bw), lambda b, i: (b, i, gblk)),
                  pl.BlockSpec((1, tm, bw), lambda b, i: (b, i, gblk + 1)),
                  pl.BlockSpec((1, tm, bw), lambda b, i: (b, i, gblk + 2)),
                  bspec, bspec, hspec, hspec, hspec, hspec,
                  pl.BlockSpec((N_BRANCH, bw, d), lambda b, i: (0, 0, 0)),
                  pl.BlockSpec((d, d), lambda b, i: (0, 0)),
                  pl.BlockSpec((1, tm, d), row),
                  pl.BlockSpec((1, 1, d), lambda b, i: (b, 0, 0))],
        out_specs=pl.BlockSpec((1, tm, d), row),
        out_shape=jax.ShapeDtypeStruct((bn, sn, d), F32),
        compiler_params=_cparams("parallel", "parallel"),
        name="branch_merge",
    )(proj, proj, proj, proj, o_c, o_s, o_w, o_b, o_cd, o_d, w_up, w_out, x, g1)


def _router_kernel(x_ref, g_ref, sc_ref, sh_ref, wr_ref, br_ref, tri_ref, h_ref, e_ref, w_ref, rank_ref, cnt_ref,
                   run_scr):
    @pl.when((pl.program_id(0) == 0) & (pl.program_id(1) == 0))
    def _():
        run_scr[...] = jnp.zeros(run_scr.shape, F32)

    h = _norm_mod(x_ref[0], g_ref[...], sc_ref[0], sh_ref[0])
    h_ref[0] = h.astype(BF16)
    logits = jnp.dot(h, wr_ref[...], preferred_element_type=F32,
                     precision=lax.Precision.HIGHEST) + br_ref[...]
    lane = _iota2(logits.shape, 1)
    lane_f = lane.astype(F32)
    cur = logits
    vals, idxs = [], []
    chosen = jnp.zeros(logits.shape, F32)
    for _ in range(TOP_K):
        m = jnp.max(cur, axis=-1, keepdims=True)
        first = jnp.min(jnp.where(cur == m, lane_f, float(LANES)), axis=-1, keepdims=True)
        vals.append(m)
        idxs.append(first)
        hit = lane_f == first
        cur = jnp.where(hit, KNOCKOUT, cur)
        chosen = jnp.where(hit, 1.0, chosen)
    exps = [jnp.exp(v - vals[0]) for v in vals]
    den = exps[0]
    for e in exps[1:]:
        den = den + e
    earlier = _dot(tri_ref[...], chosen.astype(BF16)) + run_scr[0:1, :]
    e_out = jnp.zeros(logits.shape, F32)
    w_out = jnp.zeros(logits.shape, F32)
    r_out = jnp.zeros(logits.shape, F32)
    for k in range(TOP_K):
        rank_k = jnp.sum(jnp.where(lane_f == idxs[k], earlier, 0.0), axis=-1, keepdims=True)
        e_out = jnp.where(lane == k, idxs[k], e_out)
        w_out = jnp.where(lane == k, exps[k] / den, w_out)
        r_out = jnp.where(lane == k, rank_k, r_out)
    e_ref[0] = e_out[:, :TOP_K].astype(jnp.int32)
    w_ref[0] = w_out[:, :TOP_K]
    rank_ref[0] = r_out[:, :TOP_K].astype(jnp.int32)
    total = run_scr[...] + jnp.sum(chosen, axis=0, keepdims=True)
    run_scr[...] = total
    cnt_ref[...] = total


def _router(x, g, sc, sh, w_router, b_router, tm=ROW_TILE):
    bn, sn, d = x.shape
    ne = w_router.shape[1]
    wr = jnp.zeros((d, LANES), F32).at[:, :ne].set(w_router)
    br = jnp.full((1, LANES), NEG_INF, F32).at[0, :ne].set(b_router)
    tri = jnp.asarray((np.arange(tm)[:, None] > np.arange(tm)[None, :]).astype(np.float32), BF16)
    row = lambda b, i: (b, i, 0)
    return pl.pallas_call(
        _router_kernel,
        grid=(bn, sn // tm),
        in_specs=[pl.BlockSpec((1, tm, d), row),
                  pl.BlockSpec((1, d), lambda b, i: (0, 0)),
                  pl.BlockSpec((1, 1, d), lambda b, i: (b, 0, 0)),
                  pl.BlockSpec((1, 1, d), lambda b, i: (b, 0, 0)),
                  pl.BlockSpec((d, LANES), lambda b, i: (0, 0)),
                  pl.BlockSpec((1, LANES), lambda b, i: (0, 0)),
                  pl.BlockSpec((tm, tm), lambda b, i: (0, 0))],
        out_specs=[pl.BlockSpec((1, tm, d), row),
                   pl.BlockSpec((1, tm, TOP_K), row),
                   pl.BlockSpec((1, tm, TOP_K), row),
                   pl.BlockSpec((1, tm, TOP_K), row),
                   pl.BlockSpec((8, LANES), lambda b, i: (0, 0))],
        out_shape=[jax.ShapeDtypeStruct((bn, sn, d), BF16),
                   jax.ShapeDtypeStruct((bn, sn, TOP_K), jnp.int32),
                   jax.ShapeDtypeStruct((bn, sn, TOP_K), F32),
                   jax.ShapeDtypeStruct((bn, sn, TOP_K), jnp.int32),
                   jax.ShapeDtypeStruct((8, LANES), F32)],
        scratch_shapes=[pltpu.VMEM((8, LANES), F32)],
        compiler_params=_cparams("arbitrary", "arbitrary"),
        name="moe_router",
    )(x, g.reshape(1, d), sc, sh, wr, br, tri)


def _expert_kernel(ce_ref, x_ref, wgu_ref, bgu_ref, wdn_ref, bdn_ref, rw_ref, *rest, ff, fc, off):
    o_ref, wgu_scr, wdn_scr = rest[-3:]
    c = pl.program_id(0)

    @pl.when((c == 0) | (ce_ref[c + off] != ce_ref[jnp.maximum(c + off - 1, 0)]))
    def _():
        wgu_scr[...] = wgu_ref[0, 0].astype(BF16)
        wdn_scr[...] = wdn_ref[0, 0].astype(BF16)

    x = x_ref[...]
    y = None
    for j in range(ff // fc):
        g = _dot(x, wgu_scr[:, j * fc:(j + 1) * fc]) + bgu_ref[0, 0, :, j * fc:(j + 1) * fc]
        u = _dot(x, wgu_scr[:, ff + j * fc:ff + (j + 1) * fc]) + bgu_ref[0, 0, :, ff + j * fc:ff + (j + 1) * fc]
        g = jnp.minimum(g, SWIGLU_LIMIT)
        u = jnp.clip(u, -SWIGLU_LIMIT, SWIGLU_LIMIT)
        act = g * jax.nn.sigmoid(SWIGLU_ALPHA * g) * (u + 1.0)
        part = _dot(act.astype(BF16), wdn_scr[j * fc:(j + 1) * fc, :])
        y = part if y is None else y + part
    o_ref[...] = ((y + bdn_ref[0, 0]) * rw_ref[...]).astype(o_ref.dtype)


def _expert_ffn(h, row_tok, chunk_e, layer, w_gu, b_gu, w_dn, b_dn, row_w, tm=MOE_ROWS, fc=512, groups=MOE_SLICES):
    n_rows = row_tok.shape[0]
    d = h.shape[1]
    nl, ne, _, ff2 = w_gu.shape
    ff = ff2 // 2
    n_chunks = n_rows // tm
    assert n_chunks % groups == 0
    per = n_chunks // groups
    b_gu3, b_dn3, row_w2 = b_gu.reshape(nl, ne, 1, ff2), b_dn.reshape(nl, ne, 1, d), row_w.reshape(n_rows, 1)
    y = None
    for gi in range(groups):
        off = gi * per
        rows = h[row_tok[off * tm:(off + per) * tm]]
        in_specs = [pl.BlockSpec((tm, d), lambda c, ce: (c, 0)),
                    pl.BlockSpec((1, 1, d, ff2), lambda c, ce, off=off: (layer, ce[c + off], 0, 0)),
                    pl.BlockSpec((1, 1, 1, ff2), lambda c, ce, off=off: (layer, ce[c + off], 0, 0)),
                    pl.BlockSpec((1, 1, ff, d), lambda c, ce, off=off: (layer, ce[c + off], 0, 0)),
                    pl.BlockSpec((1, 1, 1, d), lambda c, ce, off=off: (layer, ce[c + off], 0, 0)),
                    pl.BlockSpec((tm, 1), lambda c, ce, off=off: (c + off, 0))]
        args = [chunk_e, rows, w_gu, b_gu3, w_dn, b_dn3, row_w2]
        aliases = {}
        if y is not None:
            in_specs.append(pl.BlockSpec(memory_space=pl.ANY))
            args.append(y)
            aliases = {len(args) - 1: 0}
        y = pl.pallas_call(
            functools.partial(_expert_kernel, ff=ff, fc=fc, off=off),
            grid_spec=pltpu.PrefetchScalarGridSpec(
                num_scalar_prefetch=1, grid=(per,), in_specs=in_specs,
                out_specs=pl.BlockSpec((tm, d), lambda c, ce, off=off: (c + off, 0)),
                scratch_shapes=[pltpu.VMEM((d, ff2), BF16), pltpu.VMEM((ff, d), BF16)]),
            out_shape=jax.ShapeDtypeStruct((n_rows, d), BF16),
            input_output_aliases=aliases,
            compiler_params=_cparams("arbitrary"),
            name="moe_expert_ffn",
        )(*args)
    return y


def _combine_kernel(y_ref, x_ref, g2_ref, o_ref):
    tot = y_ref[0].astype(F32)
    for k in range(1, TOP_K):
        tot = tot + y_ref[k].astype(F32)
    o_ref[0] = x_ref[0] + g2_ref[0] * tot


def _combine(y4, x, g2, tm=ROW_TILE):
    bn, sn, d = x.shape
    row = lambda b, i: (b, i, 0)
    nt = sn // tm
    return pl.pallas_call(
        _combine_kernel,
        grid=(bn, nt),
        in_specs=[pl.BlockSpec((TOP_K, tm, d), lambda b, i: (0, b * nt + i, 0)),
                  pl.BlockSpec((1, tm, d), row),
                  pl.BlockSpec((1, 1, d), lambda b, i: (b, 0, 0))],
        out_specs=pl.BlockSpec((1, tm, d), row),
        out_shape=jax.ShapeDtypeStruct((bn, sn, d), F32),
        compiler_params=_cparams("parallel", "parallel"),
        name="moe_combine",
    )(y4, x, g2)


def _moe(x, g, sc, sh, g2, w_router, b_router, layer, w_gu, b_gu, w_dn, b_dn):
    bn, sn, d = x.shape
    n_tok = bn * sn
    n_asg = n_tok * TOP_K
    tm = MOE_ROWS
    h, e_out, w_out, rank_out, totals = _router(x, g, sc, sh, w_router, b_router)
    e_tok = e_out.reshape(n_tok, TOP_K)
    w_flat = w_out.reshape(-1)
    counts = totals[0, :N_EXPERTS].astype(jnp.int32)
    starts = jnp.cumsum(counts) - counts
    padded = (counts + tm - 1) // tm * tm
    pad_ends = jnp.cumsum(padded)
    pad_starts = pad_ends - padded
    pos = pad_starts[e_tok] + rank_out.reshape(n_tok, TOP_K)
    n_chunks = n_asg // tm + N_EXPERTS
    chunk_start = jnp.arange(n_chunks, dtype=jnp.int32) * tm
    chunk_e = jnp.minimum(jnp.sum(chunk_start[:, None] >= pad_ends[None, :], axis=1), N_EXPERTS - 1).astype(jnp.int32)
    order = jnp.argsort(e_tok.reshape(-1))
    src = (chunk_start - (pad_starts - starts)[chunk_e])[:, None] + jnp.arange(tm, dtype=jnp.int32)[None, :]
    row_valid = (src < (starts + counts)[chunk_e][:, None]).reshape(-1)
    asg = order[jnp.clip(src, 0, n_asg - 1).reshape(-1)].astype(jnp.int32)
    row_tok = asg // TOP_K
    row_w = jnp.where(row_valid, w_flat[asg], 0.0)
    y = _expert_ffn(h.reshape(n_tok, d), row_tok, chunk_e, layer, w_gu, b_gu, w_dn, b_dn, row_w)
    return _combine(y[pos.T], x, g2)


def _norm_rope(x, g, cos, sin, bd, hd):
    sq = x * x
    hi = sq.astype(BF16)
    lo = (sq - hi.astype(F32)).astype(BF16)
    ss = _dot(hi, bd) + _dot(lo, bd)
    y = x * lax.rsqrt(ss * (1.0 / hd) + NORM_EPS) * g
    half = hd // 2
    first = (_iota2(x.shape, 1) & (hd - 1)) < half
    partner = jnp.where(first, pltpu.roll(y, LANES - half, 1), pltpu.roll(y, half, 1))
    return y * cos + partner * sin


def _prep_kernel(aq_ref, bq_ref, bk_ref, bv_ref, cq_ref, ck_ref, cv_ref, dq_ref, akc_ref, aks_ref, akw_ref,
                 dk_ref, dv_ref, c64_ref, s64_ref, c32_ref, s32_ref, gaq_ref, gak_ref, gcq_ref, gck_ref,
                 gdq_ref, gdk_ref, bd64_ref, bd32_ref,
                 oaq_ref, okc_ref, ovc_ref, okx_ref, ovs_ref, okw_ref, ovw_ref, obq_ref, obk_ref, obv_ref,
                 ocq_ref, ock_ref, ocv_ref, odq_ref, odk_ref, odv_ref, *, ts, sb_scale):
    hd, dd = HEAD_DIM, DIFF_DIM
    c64, s64, c32, s32 = c64_ref[0], s64_ref[0], c32_ref[0], s32_ref[0]
    bd64, bd32 = bd64_ref[...], bd32_ref[...]

    def slabs(ref):
        x = ref[0].astype(F32)
        return [x[:, c * LANES:(c + 1) * LANES] for c in range(x.shape[1] // LANES)]

    def put_heads(o_ref, c, y, width):
        per = LANES // width
        for u in range(per):
            o_ref[0, c * per + u] = y[:, u * width:(u + 1) * width].astype(o_ref.dtype)

    for src, gain, dst in ((aq_ref, gaq_ref, oaq_ref), (dq_ref, gdq_ref, odq_ref), (dk_ref, gdk_ref, odk_ref)):
        for c, x in enumerate(slabs(src)):
            put_heads(dst, c, _norm_rope(x, gain[:, c * LANES:(c + 1) * LANES], c64, s64, bd64, hd), hd)
    kc = _norm_rope(akc_ref[0].astype(F32), gak_ref[0:1, :], c64, s64, bd64, hd)
    ks = _norm_rope(aks_ref[0].astype(F32), gak_ref[1:2, :], c64, s64, bd64, hd)
    kw = _norm_rope(akw_ref[0].astype(F32), gak_ref[2:3, :], c64, s64, bd64, hd)
    okc_ref[0] = kc[:, :hd].astype(BF16)
    ovc_ref[0] = akc_ref[0][:, hd:].astype(BF16)
    nsel = okx_ref.shape[2] - hd
    blk = (pl.program_id(1) * ts + _iota2((ts, nsel), 0)) >> (NSA_SEL_BLOCK.bit_length() - 1)
    okx_ref[0, :, :nsel] = jnp.where(blk == _iota2((ts, nsel), 1), 1.0, 0.0).astype(BF16)
    okx_ref[0, :, nsel:] = ks[:, :hd].astype(BF16)
    ovs_ref[0] = aks_ref[0][:, hd:].astype(BF16)
    okw_ref[0, 0] = kw[:, :hd].astype(BF16)
    ovw_ref[0, 0] = akw_ref[0][:, hd:].astype(BF16)
    for c, x in enumerate(slabs(bq_ref)):
        put_heads(obq_ref, c, x * sb_scale, hd)
    for src, dst in ((bk_ref, obk_ref), (bv_ref, obv_ref), (cv_ref, ocv_ref)):
        for c, x in enumerate(slabs(src)):
            put_heads(dst, c, x, hd)
    put_heads(odv_ref, 0, dv_ref[0].astype(F32), hd)
    for src, gain, dst in ((cq_ref, gcq_ref, ocq_ref), (ck_ref, gck_ref, ock_ref)):
        for c, x in enumerate(slabs(src)):
            put_heads(dst, c, _norm_rope(x, gain[:, c * LANES:(c + 1) * LANES], c32, s32, bd32, dd), dd)


def _rope_tables(positions, hd):
    half = hd // 2
    inv = ROPE_THETA ** (-jnp.arange(half, dtype=F32) * 2.0 / hd)
    ang = positions.astype(F32)[..., None] * inv
    cos, sin = jnp.cos(ang), jnp.sin(ang)
    reps = LANES // hd
    return (jnp.tile(jnp.concatenate([cos, cos], axis=-1), (1, 1, reps)),
            jnp.tile(jnp.concatenate([-sin, sin], axis=-1), (1, 1, reps)))


def _prep(proj, cols, tables, p, n_sel, ts=ROW_TILE):
    bn, sn, _ = proj.shape
    hd, dd = HEAD_DIM, DIFF_DIM
    scale = hd ** -0.5
    c64, s64, c32, s32 = tables

    def cspec(name):
        off, width = cols[name]
        assert off % width == 0
        return pl.BlockSpec((1, ts, width), lambda b, i, blk=off // width: (b, i, blk))

    def tile_gain(g, reps, mult=1.0):
        return (jnp.tile(g.astype(F32), reps) * mult).reshape(1, -1)

    ones = jnp.ones((hd,), F32)
    gak = jnp.stack([jnp.concatenate([p["nsa_kn"][j].astype(F32), ones]) for j in range(3)])
    gains = [tile_gain(p["nsa_qn"], NSA_HEADS, scale), gak,
             tile_gain(p["dif_qn"], 2 * DIFF_HEADS, dd ** -0.5), tile_gain(p["dif_kn"], 2 * DIFF_HEADS),
             tile_gain(p["swa_qn"], SWA_HEADS, scale), tile_gain(p["swa_kn"], SWA_KV_HEADS)]
    lane = np.arange(LANES)
    bd64 = jnp.asarray((lane[:, None] // hd == lane[None, :] // hd).astype(np.float32), BF16)
    bd32 = jnp.asarray((lane[:, None] // dd == lane[None, :] // dd).astype(np.float32), BF16)
    names = ("a_q", "b_q", "b_k", "b_v", "c_q", "c_k", "c_v", "d_q", "a_kcvc", "a_ksvs", "a_kwvw", "d_k", "d_v")
    tab = pl.BlockSpec((1, ts, LANES), lambda b, i: (b, i, 0))
    full = lambda a: pl.BlockSpec(a.shape, lambda b, i: (0,) * a.ndim)

    def hm(nh, w):
        return (jax.ShapeDtypeStruct((bn, nh, sn, w), BF16), pl.BlockSpec((1, nh, ts, w), lambda b, i: (b, 0, i, 0)))

    def tm_(w):
        return (jax.ShapeDtypeStruct((bn, sn, w), BF16), pl.BlockSpec((1, ts, w), lambda b, i: (b, i, 0)))

    outs = [hm(NSA_HEADS, hd), tm_(hd), tm_(hd), tm_(n_sel + hd), tm_(hd), hm(1, hd), hm(1, hd),
            hm(SB_HEADS, hd), hm(SB_HEADS, hd), hm(SB_HEADS, hd),
            hm(2 * DIFF_HEADS, dd), hm(2 * DIFF_HEADS, dd), hm(DIFF_HEADS, 2 * dd),
            hm(SWA_HEADS, hd), hm(SWA_KV_HEADS, hd), hm(SWA_KV_HEADS, hd)]
    consts = gains + [bd64, bd32]
    res = pl.pallas_call(
        functools.partial(_prep_kernel, ts=ts, sb_scale=-scale * LOG2E),
        grid=(bn, sn // ts),
        in_specs=[cspec(n) for n in names] + [tab] * 4 + [full(a) for a in consts],
        out_specs=[o[1] for o in outs],
        out_shape=[o[0] for o in outs],
        compiler_params=_cparams("parallel", "parallel"),
        name="mixer_prep",
    )(*([proj] * len(names)), c64, s64, c32, s32, *consts)
    keys = ("a_q", "kc", "vc", "kx", "vs", "kw", "vw", "b_q", "b_k", "b_v", "c_q", "c_k", "c_v", "d_q", "d_k", "d_v")
    return dict(zip(keys, res))


def _layer_columns(d):
    cols = {}
    off = 0
    for name, width in (("br_g", N_BRANCH * d), ("ga", 3 * NSA_HEADS * HEAD_DIM), ("a_q", 256),
                        ("b_q", 256), ("b_k", 256), ("b_v", 256), ("c_q", 256), ("c_k", 256), ("c_v", 256),
                        ("d_q", 256), ("a_kcvc", 128), ("a_ksvs", 128), ("a_kwvw", 128), ("d_k", 128), ("d_v", 128)):
        cols[name] = (off, width)
        off += width
    return cols, off


def _reorder_w_in(w_in, d, n_pad):
    ref_splits = (256, 128, 128, 128, 12, 256, 256, 256, 256, 256, 256, 256, 128, 128, N_BRANCH * d)
    names = ("a_q", "a_kcvc", "a_ksvs", "a_kwvw", "a_g", "b_q", "b_k", "b_v",
             "c_q", "c_k", "c_v", "d_q", "d_k", "d_v", "br_g")
    starts = np.cumsum((0,) + ref_splits)
    src = {n: (int(starts[i]), ref_splits[i]) for i, n in enumerate(names)}
    cols, total = _layer_columns(d)
    idx = np.zeros((n_pad,), np.int32)
    keep = np.zeros((n_pad,), np.float32)
    for name, (off, width) in cols.items():
        if name == "ga":
            g0 = src["a_g"][0]
            for j in range(3):
                for h in range(NSA_HEADS):
                    base = off + j * NSA_HEADS * HEAD_DIM + h * HEAD_DIM
                    idx[base:base + HEAD_DIM] = g0 + h * 3 + j
        else:
            idx[off:off + width] = src[name][0] + np.arange(width)
        keep[off:off + width] = 1.0
    w = w_in[:, idx] * keep[None, :]
    return w.astype(BF16)


def _mixer_layer(x, positions, tables, mod, p, lam_init):
    bn, sn, d = x.shape
    sh1, sc1, g1 = mod[0], mod[1], mod[2]
    cols, total = _layer_columns(d)
    n_pad = -(-total // PROJ_TILE[1]) * PROJ_TILE[1]
    w_in = _reorder_w_in(p["w_in"], d, n_pad)
    proj = _in_projection(x, p["norm1"], sc1, sh1, w_in)
    n_cmp = (sn - NSA_CMP_LEN) // NSA_CMP_STRIDE + 1
    n_sel = sn // NSA_SEL_BLOCK
    n_top = min(NSA_N_SEL, n_sel)
    t = _prep(proj, cols, tables, p, n_sel)
    k_cmp = _compress(t["kc"], p["nsa_pe_k"], p["nsa_w1_k"], p["nsa_w2_k"]).astype(BF16)
    v_cmp = _compress(t["vc"], p["nsa_pe_v"], p["nsa_w1_v"], p["nsa_w2_v"]).astype(BF16)
    cmp_start = np.arange(sn // NSA_CMP_STRIDE) * NSA_CMP_STRIDE
    sel_start = np.arange(n_sel) * NSA_SEL_BLOCK
    overlap = ((cmp_start[:, None] <= (sel_start + NSA_SEL_BLOCK - 1)[None, :]) &
               ((cmp_start + NSA_CMP_LEN - 1)[:, None] >= sel_start[None, :]) &
               (np.arange(sn // NSA_CMP_STRIDE) < n_cmp)[:, None]).astype(np.float32)
    o_c, bias = _nsa_cmp(t["a_q"], k_cmp, v_cmp, jnp.asarray(overlap, BF16), n_cmp, n_top)
    o_s = _nsa_sel(t["a_q"], t["kx"], t["vs"], bias)
    o_w = _banded_attention(t["a_q"], t["kw"], t["vw"], NSA_WINDOW)
    o_b = _stick_breaking(t["b_q"], t["b_k"], t["b_v"])
    lam = (jnp.exp(jnp.sum(p["dif_lq1"] * p["dif_lk1"])) - jnp.exp(jnp.sum(p["dif_lq2"] * p["dif_lk2"]))
           + lam_init)
    o_cd = _diff_attention(t["c_q"], t["c_k"], t["c_v"], p["dif_subln"], lam, jnp.asarray(1.0 - lam_init, F32))
    o_d = _banded_attention(t["d_q"], t["d_k"], t["d_v"], SWA_WINDOW, sinks=p["swa_sinks"])
    return _merge(proj, o_c, o_s, o_w, o_b, o_cd, o_d, p["w_up"].astype(BF16), p["w_out"].astype(BF16),
                  x, g1, cols["ga"][0])


def kernel(x, c, positions, w_ada, b_ada, norm1, norm2, w_in, nsa_qn, nsa_kn, nsa_pe_k, nsa_w1_k, nsa_w2_k,
           nsa_pe_v, nsa_w1_v, nsa_w2_v, dif_qn, dif_kn, dif_lq1, dif_lk1, dif_lq2, dif_lk2, dif_subln,
           swa_qn, swa_kn, swa_sinks, w_up, w_out, w_router, b_router, w_gu, b_gu, w_dn, b_dn):
    bn, sn, d = x.shape
    depth = w_ada.shape[0]
    c_pad = jnp.zeros((8, d), F32).at[:bn].set(c)
    tables = _rope_tables(positions, HEAD_DIM) + _rope_tables(positions, DIFF_DIM)
    for l in range(depth):
        lam_init = 0.8 - 0.6 * math.exp(-0.3 * l)
        mod = _linear(c_pad, w_ada[l], b_ada[l], tn=512, precision=lax.Precision.HIGHEST)[:bn]
        mod = mod.reshape(bn, 6, 1, d).transpose(1, 0, 2, 3)
        p = dict(norm1=norm1[l], w_in=w_in[l], nsa_qn=nsa_qn[l], nsa_kn=nsa_kn[l], nsa_pe_k=nsa_pe_k[l],
                 nsa_w1_k=nsa_w1_k[l], nsa_w2_k=nsa_w2_k[l], nsa_pe_v=nsa_pe_v[l], nsa_w1_v=nsa_w1_v[l],
                 nsa_w2_v=nsa_w2_v[l], dif_qn=dif_qn[l], dif_kn=dif_kn[l], dif_lq1=dif_lq1[l],
                 dif_lk1=dif_lk1[l], dif_lq2=dif_lq2[l], dif_lk2=dif_lk2[l], dif_subln=dif_subln[l],
                 swa_qn=swa_qn[l], swa_kn=swa_kn[l], swa_sinks=swa_sinks[l], w_up=w_up[l], w_out=w_out[l])
        x = _mixer_layer(x, positions, tables, mod, p, lam_init)
        x = _moe(x, norm2[l], mod[4], mod[3], mod[5], w_router[l], b_router[l], l, w_gu, b_gu, w_dn, b_dn)
    return x
```

```python
import functools
import math

import numpy as np
import jax
import jax.numpy as jnp
from jax import lax
from jax.experimental import pallas as pl
from jax.experimental.pallas import tpu as pltpu

F32 = jnp.float32
BF16 = jnp.bfloat16

HEAD_DIM = 64
ROPE_THETA = 10000.0
NORM_EPS = 1e-6
NEG_INF = -1e30
KNOCKOUT = -3e38
N_BRANCH = 4

NSA_HEADS = 4
NSA_CMP_LEN = 32
NSA_CMP_STRIDE = 16
NSA_SEL_BLOCK = 64
NSA_N_SEL = 16
NSA_WINDOW = 512
NSA_FORCED_SCORE = 1e4

SB_HEADS = 4
DIFF_HEADS = 4
DIFF_DIM = 32
SWA_HEADS = 4
SWA_KV_HEADS = 2
SWA_WINDOW = 128

N_EXPERTS = 32
TOP_K = 4
SWIGLU_ALPHA = 1.702
SWIGLU_LIMIT = 7.0

LANES = 128
LOG2E = 1.4426950408889634
VMEM_LIMIT = 56 * 1024 * 1024

FLASH_TILE = 1024
SB_KEY_TILE = 256
BAND_TILE = 1024
CMP_TILE = 512
PROJ_TILE = (1024, 2560)
ROW_TILE = 512
MERGE_TILE = 512
MOE_ROWS = 512
MOE_SLICES = 4


def _cparams(*sem):
    return pltpu.CompilerParams(dimension_semantics=sem, vmem_limit_bytes=VMEM_LIMIT)


def _dot(a, b):
    return jnp.dot(a, b, preferred_element_type=F32)


def _dot_nt(a, b):
    return lax.dot_general(a, b, (((1,), (1,)), ((), ())), preferred_element_type=F32)


def _iota2(shape, dim):
    return lax.broadcasted_iota(jnp.int32, shape, dim)


def _linear_kernel(x_ref, w_ref, b_ref, o_ref, *, precision):
    o_ref[...] = jnp.dot(x_ref[...], w_ref[...], preferred_element_type=F32,
                         precision=precision) + b_ref[...]


def _linear(x, w, b, tn, precision=None):
    m, k = x.shape
    n = w.shape[1]
    return pl.pallas_call(
        functools.partial(_linear_kernel, precision=precision),
        grid=(n // tn,),
        in_specs=[pl.BlockSpec((m, k), lambda j: (0, 0)),
                  pl.BlockSpec((k, tn), lambda j: (0, j)),
                  pl.BlockSpec((1, tn), lambda j: (0, j))],
        out_specs=pl.BlockSpec((m, tn), lambda j: (0, j)),
        out_shape=jax.ShapeDtypeStruct((m, n), F32),
        compiler_params=_cparams("arbitrary"),
        name="linear",
    )(x, w, b.reshape(1, n))


def _cmp_mlp_kernel(a_ref, b_ref, pe_ref, w2_ref, o_ref):
    hid = jax.nn.gelu(a_ref[...] + b_ref[...] + pe_ref[...])
    o_ref[...] = _dot(hid.astype(BF16), w2_ref[...])


def _compress(t, pe, w1, w2):
    bn, sn, dh = t.shape
    st = NSA_CMP_STRIDE
    half = st * dh
    nb = sn // st
    t16 = t.reshape(bn * nb, half).astype(BF16)
    w1cat = jnp.concatenate([w1[:half], w1[half:]], axis=1).astype(BF16)
    hidden = w1.shape[1]
    ab = _linear(t16, w1cat, jnp.zeros((2 * hidden,), F32), tn=2 * hidden)
    ab = ab.reshape(bn, nb, 2 * hidden)
    a = ab[:, :, :hidden]
    b_next = jnp.concatenate([ab[:, 1:, hidden:], jnp.zeros((bn, 1, hidden), F32)], axis=1)
    pe_term = jnp.dot(pe.reshape(1, NSA_CMP_LEN * dh), w1, precision=lax.Precision.HIGHEST)
    rows = bn * nb
    tm = min(512, rows)
    out = pl.pallas_call(
        _cmp_mlp_kernel,
        grid=(rows // tm,),
        in_specs=[pl.BlockSpec((tm, hidden), lambda i: (i, 0)),
                  pl.BlockSpec((tm, hidden), lambda i: (i, 0)),
                  pl.BlockSpec((1, hidden), lambda i: (0, 0)),
                  pl.BlockSpec((hidden, dh), lambda i: (0, 0))],
        out_specs=pl.BlockSpec((tm, dh), lambda i: (i, 0)),
        out_shape=jax.ShapeDtypeStruct((rows, dh), F32),
        compiler_params=_cparams("parallel"),
        name="cmp_mlp",
    )(a.reshape(rows, hidden), b_next.reshape(rows, hidden), pe_term, w2.astype(BF16))
    return out.reshape(bn, nb, dh)


def _norm_mod(x, g, sc, sh):
    r = lax.rsqrt(jnp.mean(x * x, axis=-1, keepdims=True) + NORM_EPS)
    return (x * r * g) * (1.0 + sc) + sh


def _proj_kernel(x_ref, g_ref, sc_ref, sh_ref, w_ref, o_ref, h_scr):
    @pl.when(pl.program_id(2) == 0)
    def _():
        h_scr[...] = _norm_mod(x_ref[0], g_ref[...], sc_ref[0], sh_ref[0]).astype(BF16)

    o_ref[0] = _dot(h_scr[...], w_ref[...]).astype(o_ref.dtype)


def _in_projection(x, g, sc, sh, w, tm=PROJ_TILE[0], tn=PROJ_TILE[1]):
    bn, sn, d = x.shape
    n = w.shape[1]
    tm = min(tm, sn)
    return pl.pallas_call(
        _proj_kernel,
        grid=(bn, sn // tm, n // tn),
        in_specs=[pl.BlockSpec((1, tm, d), lambda b, i, j: (b, i, 0)),
                  pl.BlockSpec((1, d), lambda b, i, j: (0, 0)),
                  pl.BlockSpec((1, 1, d), lambda b, i, j: (b, 0, 0)),
                  pl.BlockSpec((1, 1, d), lambda b, i, j: (b, 0, 0)),
                  pl.BlockSpec((d, tn), lambda b, i, j: (0, j))],
        out_specs=pl.BlockSpec((1, tm, tn), lambda b, i, j: (b, i, j)),
        out_shape=jax.ShapeDtypeStruct((bn, sn, n), BF16),
        scratch_shapes=[pltpu.VMEM((tm, d), BF16)],
        compiler_params=_cparams("parallel", "parallel", "arbitrary"),
        name="in_proj",
    )(x, g.reshape(1, d), sc, sh, w)


def _banded_kernel(*refs, tile, window, has_sink, grp):
    if has_sink:
        sink_ref, q_ref, kp_ref, kc_ref, vp_ref, vc_ref, o_ref = refs
    else:
        q_ref, kp_ref, kc_ref, vp_ref, vc_ref, o_ref = refs
    i = pl.program_id(2)
    w = window
    rows = grp * w
    row = _iota2((rows, w), 0)
    upper = _iota2((rows, w), 1) > (row & (w - 1))
    first_bias = jnp.where(i > 0, 0.0, NEG_INF)
    if has_sink:
        head0 = pl.program_id(1) * grp
        sink = jnp.full((rows, 1), sink_ref[head0], F32)
        for hh in range(1, grp):
            sink = jnp.where(_iota2((rows, 1), 0) >= hh * w, sink_ref[head0 + hh], sink)
    for u in range(tile // w):
        q = jnp.concatenate([q_ref[0, hh, u * w:(u + 1) * w, :] for hh in range(grp)], axis=0)
        if u == 0:
            k_prev, v_prev = kp_ref[0, 0], vp_ref[0, 0]
        else:
            k_prev, v_prev = kc_ref[0, 0, (u - 1) * w:u * w, :], vc_ref[0, 0, (u - 1) * w:u * w, :]
        k_cur, v_cur = kc_ref[0, 0, u * w:(u + 1) * w, :], vc_ref[0, 0, u * w:(u + 1) * w, :]
        s_prev = _dot_nt(q, k_prev)
        if u == 0:
            s_prev = s_prev + first_bias
        s = jnp.where(upper, s_prev, _dot_nt(q, k_cur))
        m = jnp.max(s, axis=-1, keepdims=True)
        if has_sink:
            m = jnp.maximum(m, sink)
        p = jnp.exp(s - m)
        den = jnp.sum(p, axis=-1, keepdims=True)
        if has_sink:
            den = den + jnp.exp(sink - m)
        else:
            den = jnp.maximum(den, 1e-30)
        o = (_dot(jnp.where(upper, p, 0.0).astype(BF16), v_prev)
             + _dot(jnp.where(upper, 0.0, p).astype(BF16), v_cur)) / den
        for hh in range(grp):
            o_ref[0, hh, u * w:(u + 1) * w, :] = o[hh * w:(hh + 1) * w]


def _banded_attention(q, k, v, window, sinks=None, tile=BAND_TILE):
    bn, hq, sn, d = q.shape
    hkv = k.shape[1]
    grp = hq // hkv
    tile = min(tile, sn)
    assert tile % window == 0 and sn % tile == 0 and window & (window - 1) == 0
    per = tile // window
    has_sink = sinks is not None
    qspec = pl.BlockSpec((1, grp, tile, d), lambda b, g, i: (b, g, i, 0))
    prev = pl.BlockSpec((1, 1, window, d), lambda b, g, i: (b, g, jnp.maximum(i * per - 1, 0), 0))
    cur = pl.BlockSpec((1, 1, tile, d), lambda b, g, i: (b, g, i, 0))
    in_specs = [qspec, prev, cur, prev, cur]
    args = [q, k, k, v, v]
    if has_sink:
        in_specs = [pl.BlockSpec(memory_space=pltpu.SMEM)] + in_specs
        args = [sinks.astype(F32)] + args
    return pl.pallas_call(
        functools.partial(_banded_kernel, tile=tile, window=window, has_sink=has_sink, grp=grp),
        grid=(bn, hkv, sn // tile),
        in_specs=in_specs,
        out_specs=pl.BlockSpec((1, grp, tile, d), lambda b, g, i: (b, g, i, 0)),
        out_shape=jax.ShapeDtypeStruct((bn, hq, sn, d), F32),
        compiler_params=_cparams("parallel", "parallel", "parallel"),
        name="banded_attn",
    )(*args)


def _nsa_cmp_kernel(q_ref, kc_ref, vct_ref, ovt_ref, oct_ref, bias_ref, *, tq, n_cmp, n_top, heads):
    i = pl.program_id(1)
    ncp = kc_ref.shape[1]
    nsel = ovt_ref.shape[0]
    dh = q_ref.shape[3]
    t = i * tq + _iota2((ncp, tq), 1)
    n = _iota2((ncp, tq), 0)
    valid = (n * NSA_CMP_STRIDE + (NSA_CMP_LEN - 1) <= t) & (n < n_cmp)
    kc = kc_ref[0]
    vct = vct_ref[0]
    psum = jnp.zeros((ncp, tq), F32)
    for h in range(heads):
        s = jnp.where(valid, _dot_nt(kc, q_ref[0, h]), NEG_INF)
        m = jnp.max(s, axis=0, keepdims=True)
        p = jnp.where(valid, jnp.exp(s - m), 0.0)
        p = p / jnp.maximum(jnp.sum(p, axis=0, keepdims=True), 1e-30)
        oct_ref[0, h * dh:(h + 1) * dh, :] = _dot(vct, p.astype(BF16))
        psum = psum + p
    hi = psum.astype(BF16)
    lo = (psum - hi.astype(F32)).astype(BF16)
    imp = _dot(ovt_ref[...], hi) + _dot(ovt_ref[...], lo)

    tt = i * tq + _iota2((nsel, tq), 1)
    blk = _iota2((nsel, tq), 0)
    cur = tt >> (NSA_SEL_BLOCK.bit_length() - 1)
    forced = (blk == 0) | (blk == cur) | (blk == cur - 1)
    valid_s = blk * NSA_SEL_BLOCK <= tt
    score = jnp.where(forced, NSA_FORCED_SCORE, jnp.where(valid_s, imp, -1.0))
    blk_f = blk.astype(F32)

    def pick(_, carry):
        score, sel = carry
        m = jnp.max(score, axis=0, keepdims=True)
        first = jnp.min(jnp.where(score == m, blk_f, float(nsel)), axis=0, keepdims=True)
        hit = blk_f == first
        return jnp.where(hit, KNOCKOUT, score), jnp.where(hit, 0.0, sel)

    _, bias = lax.fori_loop(0, n_top, pick, (score, jnp.full((nsel, tq), NEG_INF, F32)))
    bias_ref[0] = bias.astype(BF16)


def _nsa_cmp(q, kc, vc, overlap, n_cmp, n_top, tq=CMP_TILE):
    bn, heads, sn, dh = q.shape
    tq = min(tq, sn)
    ncp = kc.shape[1]
    nsel = overlap.shape[1]
    oct, bias_t = pl.pallas_call(
        functools.partial(_nsa_cmp_kernel, tq=tq, n_cmp=n_cmp, n_top=n_top, heads=heads),
        grid=(bn, sn // tq),
        in_specs=[pl.BlockSpec((1, heads, tq, dh), lambda b, i: (b, 0, i, 0)),
                  pl.BlockSpec((1, ncp, dh), lambda b, i: (b, 0, 0)),
                  pl.BlockSpec((1, dh, ncp), lambda b, i: (b, 0, 0)),
                  pl.BlockSpec((nsel, ncp), lambda b, i: (0, 0))],
        out_specs=[pl.BlockSpec((1, heads * dh, tq), lambda b, i: (b, 0, i)),
                   pl.BlockSpec((1, nsel, tq), lambda b, i: (b, 0, i))],
        out_shape=[jax.ShapeDtypeStruct((bn, heads * dh, sn), F32),
                   jax.ShapeDtypeStruct((bn, nsel, sn), BF16)],
        compiler_params=_cparams("parallel", "parallel"),
        name="nsa_cmp_topk",
    )(q, kc, vc.transpose(0, 2, 1), overlap.T)
    return oct.transpose(0, 2, 1), bias_t.transpose(0, 2, 1)


def _nsa_sel_kernel(q_ref, kx_ref, v_ref, bias_ref, o_ref, qx_scr, *, tq, tk, heads):
    i = pl.program_id(1)
    dh = v_ref.shape[2]
    bias = bias_ref[0]
    for h in range(heads):
        qx_scr[h] = jnp.concatenate([bias, q_ref[0, h]], axis=1)

    def tile(j, state, diagonal):
        start = pl.multiple_of(j * tk, tk)
        kx = kx_ref[0, pl.ds(start, tk), :]
        v = v_ref[0, pl.ds(start, tk), :]
        if diagonal:
            causal = (start + _iota2((tq, tk), 1)) <= (i * tq + _iota2((tq, tk), 0))
        new_state = []
        for h in range(heads):
            m_old, l_old, acc_old = state[h]
            s = _dot_nt(qx_scr[h], kx)
            if diagonal:
                s = jnp.where(causal, s, NEG_INF)
            m_new = jnp.maximum(m_old, jnp.max(s, axis=-1, keepdims=True))
            alpha = jnp.exp(m_old - m_new)
            p = jnp.exp(s - m_new)
            l_new = alpha * l_old + jnp.sum(p, axis=-1, keepdims=True)
            acc_new = alpha * acc_old + _dot(p.astype(BF16), v)
            new_state.append((m_new, l_new, acc_new))
        return tuple(new_state)

    init = tuple((jnp.full((tq, 1), NEG_INF, F32), jnp.zeros((tq, 1), F32), jnp.zeros((tq, dh), F32))
                 for _ in range(heads))
    n_full = (i * tq) // tk
    state = lax.fori_loop(0, n_full, lambda j, st: tile(j, st, False), init)
    state = tile(n_full, state, True)
    for h in range(heads):
        o_ref[0, :, h * dh:(h + 1) * dh] = state[h][2] / jnp.maximum(state[h][1], 1e-30)


def _nsa_sel(q, kx, v, bias, tq=FLASH_TILE, tk=FLASH_TILE):
    bn, heads, sn, dh = q.shape
    nsel = bias.shape[2]
    tq, tk = min(tq, sn), min(tk, sn)
    assert tk % tq == 0 and sn % tk == 0
    return pl.pallas_call(
        functools.partial(_nsa_sel_kernel, tq=tq, tk=tk, heads=heads),
        grid=(bn, sn // tq),
        in_specs=[pl.BlockSpec((1, heads, tq, dh), lambda b, i: (b, 0, i, 0)),
                  pl.BlockSpec((1, sn, nsel + dh), lambda b, i: (b, 0, 0)),
                  pl.BlockSpec((1, sn, dh), lambda b, i: (b, 0, 0)),
                  pl.BlockSpec((1, tq, nsel), lambda b, i: (b, i, 0))],
        out_specs=pl.BlockSpec((1, tq, heads * dh), lambda b, i: (b, i, 0)),
        out_shape=jax.ShapeDtypeStruct((bn, sn, heads * dh), F32),
        scratch_shapes=[pltpu.VMEM((heads, tq, nsel + dh), BF16)],
        compiler_params=_cparams("parallel", "arbitrary"),
        name="nsa_selected_attn",
    )(q, kx, v, bias)


def _sb_kernel(q_ref, k_ref, v_ref, u_ref, o_ref, *, tq, tk):
    i = pl.program_id(2)
    q = q_ref[0, 0]
    per_q = tq // tk

    def tile(jj, carry, first_row):
        diagonal = first_row is not None
        r0 = first_row if diagonal else 0
        start = pl.multiple_of(jj * tk, tk)
        nz = _dot_nt(q[r0:], k_ref[0, 0, pl.ds(start, tk), :])
        neg_abs = lax.bitcast_convert_type(lax.bitcast_convert_type(nz, jnp.uint32) | jnp.uint32(0x80000000), F32)
        log_keep = jnp.minimum(nz, 0.0) - jnp.log2(1.0 + jnp.exp2(neg_abs))
        if diagonal:
            strict = _iota2(nz.shape, 1) < _iota2(nz.shape, 0)
            log_keep = jnp.where(strict, log_keep, 0.0)
        cum = _dot(log_keep.astype(BF16), u_ref[...])
        a = jnp.exp2(cum + jnp.concatenate([carry[r0:]] * (tk // LANES), axis=1) - nz)
        if diagonal:
            a = jnp.where(strict, a, 0.0)
        out = _dot(a.astype(BF16), v_ref[0, 0, pl.ds(start, tk), :])
        new_carry = carry[r0:] + jnp.broadcast_to(cum[:, 0:1], (tq - r0, LANES))
        if r0:
            out = jnp.concatenate([jnp.zeros((r0, out.shape[1]), F32), out], axis=0)
            new_carry = jnp.concatenate([carry[:r0], new_carry], axis=0)
        return out, new_carry

    def group(first, carry, diagonal):
        total = None
        for r in range(per_q):
            out, carry = tile(first - r, carry, (per_q - 1 - r) * tk if diagonal else None)
            total = out if total is None else total + out
        return total, carry

    acc, carry = group(i * per_q + per_q - 1, jnp.zeros((tq, LANES), F32), True)

    def body(p, state):
        acc, carry = state
        out, carry = group((i - p) * per_q - 1, carry, False)
        return acc + out, carry

    acc, _ = lax.fori_loop(0, i, body, (acc, carry))
    o_ref[0, 0] = acc


def _stick_breaking(q, k, v, tq=FLASH_TILE, tk=SB_KEY_TILE):
    bn, heads, sn, dh = q.shape
    tq = min(tq, sn)
    incl = (np.arange(tk)[:, None] >= np.arange(tk)[None, :]).astype(np.float32)
    u = jnp.asarray(incl, BF16)
    kv = pl.BlockSpec((1, 1, sn, dh), lambda b, h, i: (b, h, 0, 0))
    return pl.pallas_call(
        functools.partial(_sb_kernel, tq=tq, tk=tk),
        grid=(bn, heads, sn // tq),
        in_specs=[pl.BlockSpec((1, 1, tq, dh), lambda b, h, i: (b, h, i, 0)), kv, kv,
                  pl.BlockSpec((tk, tk), lambda b, h, i: (0, 0))],
        out_specs=pl.BlockSpec((1, 1, tq, dh), lambda b, h, i: (b, h, i, 0)),
        out_shape=jax.ShapeDtypeStruct((bn, heads, sn, dh), F32),
        compiler_params=_cparams("parallel", "parallel", "arbitrary"),
        name="stick_breaking_attn",
    )(q, k, v, u)


def _diff_kernel(sc_ref, q_ref, k_ref, v_ref, g_ref, o_ref, *, tq, tk):
    i = pl.program_id(2)
    dv = v_ref.shape[3]

    def tile(j, state, diagonal):
        start = pl.multiple_of(j * tk, tk)
        v = v_ref[0, 0, pl.ds(start, tk), :]
        if diagonal:
            causal = (start + _iota2((tq, tk), 1)) <= (i * tq + _iota2((tq, tk), 0))
        new_state = []
        for mi in range(2):
            m_old, l_old, acc_old = state[mi]
            s = _dot_nt(q_ref[0, mi], k_ref[0, mi, pl.ds(start, tk), :])
            if diagonal:
                s = jnp.where(causal, s, NEG_INF)
            m_new = jnp.maximum(m_old, jnp.max(s, axis=-1, keepdims=True))
            alpha = jnp.exp(m_old - m_new)
            p = jnp.exp(s - m_new)
            l_new = alpha * l_old + jnp.sum(p, axis=-1, keepdims=True)
            acc_new = alpha * acc_old + _dot(p.astype(BF16), v)
            new_state.append((m_new, l_new, acc_new))
        return tuple(new_state)

    init = tuple((jnp.full((tq, 1), NEG_INF, F32), jnp.zeros((tq, 1), F32), jnp.zeros((tq, dv), F32))
                 for _ in range(2))
    n_full = (i * tq) // tk
    state = lax.fori_loop(0, n_full, lambda j, st: tile(j, st, False), init)
    state = tile(n_full, state, True)
    lam = sc_ref[0]
    post = sc_ref[1]
    o = (state[0][2] / jnp.maximum(state[0][1], 1e-30)
         - lam * (state[1][2] / jnp.maximum(state[1][1], 1e-30)))
    r = lax.rsqrt(jnp.mean(o * o, axis=-1, keepdims=True) + NORM_EPS)
    o_ref[0, 0] = (o * r * g_ref[...]) * post


def _diff_attention(q, k, v, subln, lam, post, tq=FLASH_TILE, tk=FLASH_TILE):
    bn, h2, sn, dd = q.shape
    heads = h2 // 2
    dv = v.shape[3]
    tq, tk = min(tq, sn), min(tk, sn)
    assert tk % tq == 0 and sn % tk == 0
    scal = jnp.stack([lam, post]).astype(F32)
    return pl.pallas_call(
        functools.partial(_diff_kernel, tq=tq, tk=tk),
        grid=(bn, heads, sn // tq),
        in_specs=[pl.BlockSpec(memory_space=pltpu.SMEM),
                  pl.BlockSpec((1, 2, tq, dd), lambda b, h, i: (b, h, i, 0)),
                  pl.BlockSpec((1, 2, sn, dd), lambda b, h, i: (b, h, 0, 0)),
                  pl.BlockSpec((1, 1, sn, dv), lambda b, h, i: (b, h, 0, 0)),
                  pl.BlockSpec((1, dv), lambda b, h, i: (0, 0))],
        out_specs=pl.BlockSpec((1, 1, tq, dv), lambda b, h, i: (b, h, i, 0)),
        out_shape=jax.ShapeDtypeStruct((bn, heads, sn, dv), F32),
        compiler_params=_cparams("parallel", "parallel", "arbitrary"),
        name="diff_attn",
    )(scal, q, k, v, subln.reshape(1, dv).astype(F32))


def _merge_kernel(brg_ref, ga0_ref, ga1_ref, ga2_ref, oc_ref, os_ref, ow_ref, ob_ref, ocd_ref, od_ref,
                  wup_ref, wout_ref, x_ref, g1_ref, o_ref, *, d):
    def heads(ref):
        return jnp.concatenate([ref[0, h] for h in range(ref.shape[1])], axis=1)

    def gate(x):
        return jax.nn.sigmoid(x.astype(F32))

    o_a = gate(ga0_ref[0]) * oc_ref[0] + gate(ga1_ref[0]) * os_ref[0] + gate(ga2_ref[0]) * heads(ow_ref)
    branches = (o_a, heads(ob_ref), heads(ocd_ref), heads(od_ref))
    merged = None
    for bi, o in enumerate(branches):
        term = gate(brg_ref[0, :, bi * d:(bi + 1) * d]) * _dot(o.astype(BF16), wup_ref[bi])
        merged = term if merged is None else merged + term
    y = _dot(merged.astype(BF16), wout_ref[...])
    o_ref[0] = x_ref[0] + g1_ref[0] * y


def _merge(proj, o_c, o_s, o_w, o_b, o_cd, o_d, w_up, w_out, x, g1, ga_col, tm=MERGE_TILE):
    bn, sn, d = x.shape
    bw = o_c.shape[2]
    nh, dh = o_b.shape[1], o_b.shape[3]
    assert ga_col % bw == 0
    gblk = ga_col // bw
    row = lambda b, i: (b, i, 0)
    bspec = pl.BlockSpec((1, tm, bw), row)
    hspec = pl.BlockSpec((1, nh, tm, dh), lambda b, i: (b, 0, i, 0))
    return pl.pallas_call(
        functools.partial(_merge_kernel, d=d),
        grid=(bn, sn // tm),
        in_specs=[pl.BlockSpec((1, tm, N_BRANCH * d), row),
                  pl.BlockSpec((1, tm, bw), lambda b, i: (b, i, gblk)),
                  pl.BlockSpec((1, tm, bw), lambda b, i: (b, i, gblk + 1)),
                  pl.BlockSpec((1, tm, bw), lambda b, i: (b, i, gblk + 2)),
                  bspec, bspec, hspec, hspec, hspec, hspec,
                  pl.BlockSpec((N_BRANCH, bw, d), lambda b, i: (0, 0, 0)),
                  pl.BlockSpec((d, d), lambda b, i: (0, 0)),
                  pl.BlockSpec((1, tm, d), row),
                  pl.BlockSpec((1, 1, d), lambda b, i: (b, 0, 0))],
        out_specs=pl.BlockSpec((1, tm, d), row),
        out_shape=jax.ShapeDtypeStruct((bn, sn, d), F32),
        compiler_params=_cparams("parallel", "parallel"),
        name="branch_merge",
    )(proj, proj, proj, proj, o_c, o_s, o_w, o_b, o_cd, o_d, w_up, w_out, x, g1)


def _router_kernel(x_ref, g_ref, sc_ref, sh_ref, wr_ref, br_ref, tri_ref, h_ref, e_ref, w_ref, rank_ref, cnt_ref,
                   run_scr):
    @pl.when((pl.program_id(0) == 0) & (pl.program_id(1) == 0))
    def _():
        run_scr[...] = jnp.zeros(run_scr.shape, F32)

    h = _norm_mod(x_ref[0], g_ref[...], sc_ref[0], sh_ref[0])
    h_ref[0] = h.astype(BF16)
    logits = jnp.dot(h, wr_ref[...], preferred_element_type=F32,
                     precision=lax.Precision.HIGHEST) + br_ref[...]
    lane = _iota2(logits.shape, 1)
    lane_f = lane.astype(F32)
    cur = logits
    vals, idxs = [], []
    chosen = jnp.zeros(logits.shape, F32)
    for _ in range(TOP_K):
        m = jnp.max(cur, axis=-1, keepdims=True)
        first = jnp.min(jnp.where(cur == m, lane_f, float(LANES)), axis=-1, keepdims=True)
        vals.append(m)
        idxs.append(first)
        hit = lane_f == first
        cur = jnp.where(hit, KNOCKOUT, cur)
        chosen = jnp.where(hit, 1.0, chosen)
    exps = [jnp.exp(v - vals[0]) for v in vals]
    den = exps[0]
    for e in exps[1:]:
        den = den + e
    earlier = _dot(tri_ref[...], chosen.astype(BF16)) + run_scr[0:1, :]
    e_out = jnp.zeros(logits.shape, F32)
    w_out = jnp.zeros(logits.shape, F32)
    r_out = jnp.zeros(logits.shape, F32)
    for k in range(TOP_K):
        rank_k = jnp.sum(jnp.where(lane_f == idxs[k], earlier, 0.0), axis=-1, keepdims=True)
        e_out = jnp.where(lane == k, idxs[k], e_out)
        w_out = jnp.where(lane == k, exps[k] / den, w_out)
        r_out = jnp.where(lane == k, rank_k, r_out)
    e_ref[0] = e_out[:, :TOP_K].astype(jnp.int32)
    w_ref[0] = w_out[:, :TOP_K]
    rank_ref[0] = r_out[:, :TOP_K].astype(jnp.int32)
    total = run_scr[...] + jnp.sum(chosen, axis=0, keepdims=True)
    run_scr[...] = total
    cnt_ref[...] = total


def _router(x, g, sc, sh, w_router, b_router, tm=ROW_TILE):
    bn, sn, d = x.shape
    ne = w_router.shape[1]
    wr = jnp.zeros((d, LANES), F32).at[:, :ne].set(w_router)
    br = jnp.full((1, LANES), NEG_INF, F32).at[0, :ne].set(b_router)
    tri = jnp.asarray((np.arange(tm)[:, None] > np.arange(tm)[None, :]).astype(np.float32), BF16)
    row = lambda b, i: (b, i, 0)
    return pl.pallas_call(
        _router_kernel,
        grid=(bn, sn // tm),
        in_specs=[pl.BlockSpec((1, tm, d), row),
                  pl.BlockSpec((1, d), lambda b, i: (0, 0)),
                  pl.BlockSpec((1, 1, d), lambda b, i: (b, 0, 0)),
                  pl.BlockSpec((1, 1, d), lambda b, i: (b, 0, 0)),
                  pl.BlockSpec((d, LANES), lambda b, i: (0, 0)),
                  pl.BlockSpec((1, LANES), lambda b, i: (0, 0)),
                  pl.BlockSpec((tm, tm), lambda b, i: (0, 0))],
        out_specs=[pl.BlockSpec((1, tm, d), row),
                   pl.BlockSpec((1, tm, TOP_K), row),
                   pl.BlockSpec((1, tm, TOP_K), row),
                   pl.BlockSpec((1, tm, TOP_K), row),
                   pl.BlockSpec((8, LANES), lambda b, i: (0, 0))],
        out_shape=[jax.ShapeDtypeStruct((bn, sn, d), BF16),
                   jax.ShapeDtypeStruct((bn, sn, TOP_K), jnp.int32),
                   jax.ShapeDtypeStruct((bn, sn, TOP_K), F32),
                   jax.ShapeDtypeStruct((bn, sn, TOP_K), jnp.int32),
                   jax.ShapeDtypeStruct((8, LANES), F32)],
        scratch_shapes=[pltpu.VMEM((8, LANES), F32)],
        compiler_params=_cparams("arbitrary", "arbitrary"),
        name="moe_router",
    )(x, g.reshape(1, d), sc, sh, wr, br, tri)


def _expert_kernel(ce_ref, x_ref, wgu_ref, bgu_ref, wdn_ref, bdn_ref, rw_ref, *rest, ff, fc, off):
    o_ref, wgu_scr, wdn_scr = rest[-3:]
    c = pl.program_id(0)

    @pl.when((c == 0) | (ce_ref[c + off] != ce_ref[jnp.maximum(c + off - 1, 0)]))
    def _():
        wgu_scr[...] = wgu_ref[0, 0].astype(BF16)
        wdn_scr[...] = wdn_ref[0, 0].astype(BF16)

    x = x_ref[...]
    y = None
    for j in range(ff // fc):
        g = _dot(x, wgu_scr[:, j * fc:(j + 1) * fc]) + bgu_ref[0, 0, :, j * fc:(j + 1) * fc]
        u = _dot(x, wgu_scr[:, ff + j * fc:ff + (j + 1) * fc]) + bgu_ref[0, 0, :, ff + j * fc:ff + (j + 1) * fc]
        g = jnp.minimum(g, SWIGLU_LIMIT)
        u = jnp.clip(u, -SWIGLU_LIMIT, SWIGLU_LIMIT)
        act = g * jax.nn.sigmoid(SWIGLU_ALPHA * g) * (u + 1.0)
        part = _dot(act.astype(BF16), wdn_scr[j * fc:(j + 1) * fc, :])
        y = part if y is None else y + part
    o_ref[...] = ((y + bdn_ref[0, 0]) * rw_ref[...]).astype(o_ref.dtype)


def _expert_ffn(h, row_tok, chunk_e, layer, w_gu, b_gu, w_dn, b_dn, row_w, tm=MOE_ROWS, fc=512, groups=MOE_SLICES):
    n_rows = row_tok.shape[0]
    d = h.shape[1]
    nl, ne, _, ff2 = w_gu.shape
    ff = ff2 // 2
    n_chunks = n_rows // tm
    assert n_chunks % groups == 0
    per = n_chunks // groups
    b_gu3, b_dn3, row_w2 = b_gu.reshape(nl, ne, 1, ff2), b_dn.reshape(nl, ne, 1, d), row_w.reshape(n_rows, 1)
    y = None
    for gi in range(groups):
        off = gi * per
        rows = h[row_tok[off * tm:(off + per) * tm]]
        in_specs = [pl.BlockSpec((tm, d), lambda c, ce: (c, 0)),
                    pl.BlockSpec((1, 1, d, ff2), lambda c, ce, off=off: (layer, ce[c + off], 0, 0)),
                    pl.BlockSpec((1, 1, 1, ff2), lambda c, ce, off=off: (layer, ce[c + off], 0, 0)),
                    pl.BlockSpec((1, 1, ff, d), lambda c, ce, off=off: (layer, ce[c + off], 0, 0)),
                    pl.BlockSpec((1, 1, 1, d), lambda c, ce, off=off: (layer, ce[c + off], 0, 0)),
                    pl.BlockSpec((tm, 1), lambda c, ce, off=off: (c + off, 0))]
        args = [chunk_e, rows, w_gu, b_gu3, w_dn, b_dn3, row_w2]
        aliases = {}
        if y is not None:
            in_specs.append(pl.BlockSpec(memory_space=pl.ANY))
            args.append(y)
            aliases = {len(args) - 1: 0}
        y = pl.pallas_call(
            functools.partial(_expert_kernel, ff=ff, fc=fc, off=off),
            grid_spec=pltpu.PrefetchScalarGridSpec(
                num_scalar_prefetch=1, grid=(per,), in_specs=in_specs,
                out_specs=pl.BlockSpec((tm, d), lambda c, ce, off=off: (c + off, 0)),
                scratch_shapes=[pltpu.VMEM((d, ff2), BF16), pltpu.VMEM((ff, d), BF16)]),
            out_shape=jax.ShapeDtypeStruct((n_rows, d), BF16),
            input_output_aliases=aliases,
            compiler_params=_cparams("arbitrary"),
            name="moe_expert_ffn",
        )(*args)
    return y


def _combine_kernel(y_ref, x_ref, g2_ref, o_ref):
    tot = y_ref[0].astype(F32)
    for k in range(1, TOP_K):
        tot = tot + y_ref[k].astype(F32)
    o_ref[0] = x_ref[0] + g2_ref[0] * tot


def _combine(y4, x, g2, tm=ROW_TILE):
    bn, sn, d = x.shape
    row = lambda b, i: (b, i, 0)
    nt = sn // tm
    return pl.pallas_call(
        _combine_kernel,
        grid=(bn, nt),
        in_specs=[pl.BlockSpec((TOP_K, tm, d), lambda b, i: (0, b * nt + i, 0)),
                  pl.BlockSpec((1, tm, d), row),
                  pl.BlockSpec((1, 1, d), lambda b, i: (b, 0, 0))],
        out_specs=pl.BlockSpec((1, tm, d), row),
        out_shape=jax.ShapeDtypeStruct((bn, sn, d), F32),
        compiler_params=_cparams("parallel", "parallel"),
        name="moe_combine",
    )(y4, x, g2)


def _moe(x, g, sc, sh, g2, w_router, b_router, layer, w_gu, b_gu, w_dn, b_dn):
    bn, sn, d = x.shape
    n_tok = bn * sn
    n_asg = n_tok * TOP_K
    tm = MOE_ROWS
    h, e_out, w_out, rank_out, totals = _router(x, g, sc, sh, w_router, b_router)
    e_tok = e_out.reshape(n_tok, TOP_K)
    w_flat = w_out.reshape(-1)
    counts = totals[0, :N_EXPERTS].astype(jnp.int32)
    starts = jnp.cumsum(counts) - counts
    padded = (counts + tm - 1) // tm * tm
    pad_ends = jnp.cumsum(padded)
    pad_starts = pad_ends - padded
    pos = pad_starts[e_tok] + rank_out.reshape(n_tok, TOP_K)
    n_chunks = n_asg // tm + N_EXPERTS
    chunk_start = jnp.arange(n_chunks, dtype=jnp.int32) * tm
    chunk_e = jnp.minimum(jnp.sum(chunk_start[:, None] >= pad_ends[None, :], axis=1), N_EXPERTS - 1).astype(jnp.int32)
    order = jnp.argsort(e_tok.reshape(-1))
    src = (chunk_start - (pad_starts - starts)[chunk_e])[:, None] + jnp.arange(tm, dtype=jnp.int32)[None, :]
    row_valid = (src < (starts + counts)[chunk_e][:, None]).reshape(-1)
    asg = order[jnp.clip(src, 0, n_asg - 1).reshape(-1)].astype(jnp.int32)
    row_tok = asg // TOP_K
    row_w = jnp.where(row_valid, w_flat[asg], 0.0)
    y = _expert_ffn(h.reshape(n_tok, d), row_tok, chunk_e, layer, w_gu, b_gu, w_dn, b_dn, row_w)
    return _combine(y[pos.T], x, g2)


def _norm_rope(x, g, cos, sin, bd, hd):
    sq = x * x
    hi = sq.astype(BF16)
    lo = (sq - hi.astype(F32)).astype(BF16)
    ss = _dot(hi, bd) + _dot(lo, bd)
    y = x * lax.rsqrt(ss * (1.0 / hd) + NORM_EPS) * g
    half = hd // 2
    first = (_iota2(x.shape, 1) & (hd - 1)) < half
    partner = jnp.where(first, pltpu.roll(y, LANES - half, 1), pltpu.roll(y, half, 1))
    return y * cos + partner * sin


def _prep_kernel(aq_ref, bq_ref, bk_ref, bv_ref, cq_ref, ck_ref, cv_ref, dq_ref, akc_ref, aks_ref, akw_ref,
                 dk_ref, dv_ref, c64_ref, s64_ref, c32_ref, s32_ref, gaq_ref, gak_ref, gcq_ref, gck_ref,
                 gdq_ref, gdk_ref, bd64_ref, bd32_ref,
                 oaq_ref, okc_ref, ovc_ref, okx_ref, ovs_ref, okw_ref, ovw_ref, obq_ref, obk_ref, obv_ref,
                 ocq_ref, ock_ref, ocv_ref, odq_ref, odk_ref, odv_ref, *, ts, sb_scale):
    hd, dd = HEAD_DIM, DIFF_DIM
    c64, s64, c32, s32 = c64_ref[0], s64_ref[0], c32_ref[0], s32_ref[0]
    bd64, bd32 = bd64_ref[...], bd32_ref[...]

    def slabs(ref):
        x = ref[0].astype(F32)
        return [x[:, c * LANES:(c + 1) * LANES] for c in range(x.shape[1] // LANES)]

    def put_heads(o_ref, c, y, width):
        per = LANES // width
        for u in range(per):
            o_ref[0, c * per + u] = y[:, u * width:(u + 1) * width].astype(o_ref.dtype)

    for src, gain, dst in ((aq_ref, gaq_ref, oaq_ref), (dq_ref, gdq_ref, odq_ref), (dk_ref, gdk_ref, odk_ref)):
        for c, x in enumerate(slabs(src)):
            put_heads(dst, c, _norm_rope(x, gain[:, c * LANES:(c + 1) * LANES], c64, s64, bd64, hd), hd)
    kc = _norm_rope(akc_ref[0].astype(F32), gak_ref[0:1, :], c64, s64, bd64, hd)
    ks = _norm_rope(aks_ref[0].astype(F32), gak_ref[1:2, :], c64, s64, bd64, hd)
    kw = _norm_rope(akw_ref[0].astype(F32), gak_ref[2:3, :], c64, s64, bd64, hd)
    okc_ref[0] = kc[:, :hd].astype(BF16)
    ovc_ref[0] = akc_ref[0][:, hd:].astype(BF16)
    nsel = okx_ref.shape[2] - hd
    blk = (pl.program_id(1) * ts + _iota2((ts, nsel), 0)) >> (NSA_SEL_BLOCK.bit_length() - 1)
    okx_ref[0, :, :nsel] = jnp.where(blk == _iota2((ts, nsel), 1), 1.0, 0.0).astype(BF16)
    okx_ref[0, :, nsel:] = ks[:, :hd].astype(BF16)
    ovs_ref[0] = aks_ref[0][:, hd:].astype(BF16)
    okw_ref[0, 0] = kw[:, :hd].astype(BF16)
    ovw_ref[0, 0] = akw_ref[0][:, hd:].astype(BF16)
    for c, x in enumerate(slabs(bq_ref)):
        put_heads(obq_ref, c, x * sb_scale, hd)
    for src, dst in ((bk_ref, obk_ref), (bv_ref, obv_ref), (cv_ref, ocv_ref)):
        for c, x in enumerate(slabs(src)):
            put_heads(dst, c, x, hd)
    put_heads(odv_ref, 0, dv_ref[0].astype(F32), hd)
    for src, gain, dst in ((cq_ref, gcq_ref, ocq_ref), (ck_ref, gck_ref, ock_ref)):
        for c, x in enumerate(slabs(src)):
            put_heads(dst, c, _norm_rope(x, gain[:, c * LANES:(c + 1) * LANES], c32, s32, bd32, dd), dd)


def _rope_tables(positions, hd):
    half = hd // 2
    inv = ROPE_THETA ** (-jnp.arange(half, dtype=F32) * 2.0 / hd)
    ang = positions.astype(F32)[..., None] * inv
    cos, sin = jnp.cos(ang), jnp.sin(ang)
    reps = LANES // hd
    return (jnp.tile(jnp.concatenate([cos, cos], axis=-1), (1, 1, reps)),
            jnp.tile(jnp.concatenate([-sin, sin], axis=-1), (1, 1, reps)))


def _prep(proj, cols, tables, p, n_sel, ts=ROW_TILE):
    bn, sn, _ = proj.shape
    hd, dd = HEAD_DIM, DIFF_DIM
    scale = hd ** -0.5
    c64, s64, c32, s32 = tables

    def cspec(name):
        off, width = cols[name]
        assert off % width == 0
        return pl.BlockSpec((1, ts, width), lambda b, i, blk=off // width: (b, i, blk))

    def tile_gain(g, reps, mult=1.0):
        return (jnp.tile(g.astype(F32), reps) * mult).reshape(1, -1)

    ones = jnp.ones((hd,), F32)
    gak = jnp.stack([jnp.concatenate([p["nsa_kn"][j].astype(F32), ones]) for j in range(3)])
    gains = [tile_gain(p["nsa_qn"], NSA_HEADS, scale), gak,
             tile_gain(p["dif_qn"], 2 * DIFF_HEADS, dd ** -0.5), tile_gain(p["dif_kn"], 2 * DIFF_HEADS),
             tile_gain(p["swa_qn"], SWA_HEADS, scale), tile_gain(p["swa_kn"], SWA_KV_HEADS)]
    lane = np.arange(LANES)
    bd64 = jnp.asarray((lane[:, None] // hd == lane[None, :] // hd).astype(np.float32), BF16)
    bd32 = jnp.asarray((lane[:, None] // dd == lane[None, :] // dd).astype(np.float32), BF16)
    names = ("a_q", "b_q", "b_k", "b_v", "c_q", "c_k", "c_v", "d_q", "a_kcvc", "a_ksvs", "a_kwvw", "d_k", "d_v")
    tab = pl.BlockSpec((1, ts, LANES), lambda b, i: (b, i, 0))
    full = lambda a: pl.BlockSpec(a.shape, lambda b, i: (0,) * a.ndim)

    def hm(nh, w):
        return (jax.ShapeDtypeStruct((bn, nh, sn, w), BF16), pl.BlockSpec((1, nh, ts, w), lambda b, i: (b, 0, i, 0)))

    def tm_(w):
        return (jax.ShapeDtypeStruct((bn, sn, w), BF16), pl.BlockSpec((1, ts, w), lambda b, i: (b, i, 0)))

    outs = [hm(NSA_HEADS, hd), tm_(hd), tm_(hd), tm_(n_sel + hd), tm_(hd), hm(1, hd), hm(1, hd),
            hm(SB_HEADS, hd), hm(SB_HEADS, hd), hm(SB_HEADS, hd),
            hm(2 * DIFF_HEADS, dd), hm(2 * DIFF_HEADS, dd), hm(DIFF_HEADS, 2 * dd),
            hm(SWA_HEADS, hd), hm(SWA_KV_HEADS, hd), hm(SWA_KV_HEADS, hd)]
    consts = gains + [bd64, bd32]
    res = pl.pallas_call(
        functools.partial(_prep_kernel, ts=ts, sb_scale=-scale * LOG2E),
        grid=(bn, sn // ts),
        in_specs=[cspec(n) for n in names] + [tab] * 4 + [full(a) for a in consts],
        out_specs=[o[1] for o in outs],
        out_shape=[o[0] for o in outs],
        compiler_params=_cparams("parallel", "parallel"),
        name="mixer_prep",
    )(*([proj] * len(names)), c64, s64, c32, s32, *consts)
    keys = ("a_q", "kc", "vc", "kx", "vs", "kw", "vw", "b_q", "b_k", "b_v", "c_q", "c_k", "c_v", "d_q", "d_k", "d_v")
    return dict(zip(keys, res))


def _layer_columns(d):
    cols = {}
    off = 0
    for name, width in (("br_g", N_BRANCH * d), ("ga", 3 * NSA_HEADS * HEAD_DIM), ("a_q", 256),
                        ("b_q", 256), ("b_k", 256), ("b_v", 256), ("c_q", 256), ("c_k", 256), ("c_v", 256),
                        ("d_q", 256), ("a_kcvc", 128), ("a_ksvs", 128), ("a_kwvw", 128), ("d_k", 128), ("d_v", 128)):
        cols[name] = (off, width)
        off += width
    return cols, off


def _reorder_w_in(w_in, d, n_pad):
    ref_splits = (256, 128, 128, 128, 12, 256, 256, 256, 256, 256, 256, 256, 128, 128, N_BRANCH * d)
    names = ("a_q", "a_kcvc", "a_ksvs", "a_kwvw", "a_g", "b_q", "b_k", "b_v",
             "c_q", "c_k", "c_v", "d_q", "d_k", "d_v", "br_g")
    starts = np.cumsum((0,) + ref_splits)
    src = {n: (int(starts[i]), ref_splits[i]) for i, n in enumerate(names)}
    cols, total = _layer_columns(d)
    idx = np.zeros((n_pad,), np.int32)
    keep = np.zeros((n_pad,), np.float32)
    for name, (off, width) in cols.items():
        if name == "ga":
            g0 = src["a_g"][0]
            for j in range(3):
                for h in range(NSA_HEADS):
                    base = off + j * NSA_HEADS * HEAD_DIM + h * HEAD_DIM
                    idx[base:base + HEAD_DIM] = g0 + h * 3 + j
        else:
            idx[off:off + width] = src[name][0] + np.arange(width)
        keep[off:off + width] = 1.0
    w = w_in[:, idx] * keep[None, :]
    return w.astype(BF16)


def _mixer_layer(x, positions, tables, mod, p, lam_init):
    bn, sn, d = x.shape
    sh1, sc1, g1 = mod[0], mod[1], mod[2]
    cols, total = _layer_columns(d)
    n_pad = -(-total // PROJ_TILE[1]) * PROJ_TILE[1]
    w_in = _reorder_w_in(p["w_in"], d, n_pad)
    proj = _in_projection(x, p["norm1"], sc1, sh1, w_in)
    n_cmp = (sn - NSA_CMP_LEN) // NSA_CMP_STRIDE + 1
    n_sel = sn // NSA_SEL_BLOCK
    n_top = min(NSA_N_SEL, n_sel)
    t = _prep(proj, cols, tables, p, n_sel)
    k_cmp = _compress(t["kc"], p["nsa_pe_k"], p["nsa_w1_k"], p["nsa_w2_k"]).astype(BF16)
    v_cmp = _compress(t["vc"], p["nsa_pe_v"], p["nsa_w1_v"], p["nsa_w2_v"]).astype(BF16)
    cmp_start = np.arange(sn // NSA_CMP_STRIDE) * NSA_CMP_STRIDE
    sel_start = np.arange(n_sel) * NSA_SEL_BLOCK
    overlap = ((cmp_start[:, None] <= (sel_start + NSA_SEL_BLOCK - 1)[None, :]) &
               ((cmp_start + NSA_CMP_LEN - 1)[:, None] >= sel_start[None, :]) &
               (np.arange(sn // NSA_CMP_STRIDE) < n_cmp)[:, None]).astype(np.float32)
    o_c, bias = _nsa_cmp(t["a_q"], k_cmp, v_cmp, jnp.asarray(overlap, BF16), n_cmp, n_top)
    o_s = _nsa_sel(t["a_q"], t["kx"], t["vs"], bias)
    o_w = _banded_attention(t["a_q"], t["kw"], t["vw"], NSA_WINDOW)
    o_b = _stick_breaking(t["b_q"], t["b_k"], t["b_v"])
    lam = (jnp.exp(jnp.sum(p["dif_lq1"] * p["dif_lk1"])) - jnp.exp(jnp.sum(p["dif_lq2"] * p["dif_lk2"]))
           + lam_init)
    o_cd = _diff_attention(t["c_q"], t["c_k"], t["c_v"], p["dif_subln"], lam, jnp.asarray(1.0 - lam_init, F32))
    o_d = _banded_attention(t["d_q"], t["d_k"], t["d_v"], SWA_WINDOW, sinks=p["swa_sinks"])
    return _merge(proj, o_c, o_s, o_w, o_b, o_cd, o_d, p["w_up"].astype(BF16), p["w_out"].astype(BF16),
                  x, g1, cols["ga"][0])


def kernel(x, c, positions, w_ada, b_ada, norm1, norm2, w_in, nsa_qn, nsa_kn, nsa_pe_k, nsa_w1_k, nsa_w2_k,
           nsa_pe_v, nsa_w1_v, nsa_w2_v, dif_qn, dif_kn, dif_lq1, dif_lk1, dif_lq2, dif_lk2, dif_subln,
           swa_qn, swa_kn, swa_sinks, w_up, w_out, w_router, b_router, w_gu, b_gu, w_dn, b_dn):
    bn, sn, d = x.shape
    depth = w_ada.shape[0]
    c_pad = jnp.zeros((8, d), F32).at[:bn].set(c)
    tables = _rope_tables(positions, HEAD_DIM) + _rope_tables(positions, DIFF_DIM)
    for l in range(depth):
        lam_init = 0.8 - 0.6 * math.exp(-0.3 * l)
        mod = _linear(c_pad, w_ada[l], b_ada[l], tn=512, precision=lax.Precision.HIGHEST)[:bn]
        mod = mod.reshape(bn, 6, 1, d).transpose(1, 0, 2, 3)
        p = dict(norm1=norm1[l], w_in=w_in[l], nsa_qn=nsa_qn[l], nsa_kn=nsa_kn[l], nsa_pe_k=nsa_pe_k[l],
                 nsa_w1_k=nsa_w1_k[l], nsa_w2_k=nsa_w2_k[l], nsa_pe_v=nsa_pe_v[l], nsa_w1_v=nsa_w1_v[l],
                 nsa_w2_v=nsa_w2_v[l], dif_qn=dif_qn[l], dif_kn=dif_kn[l], dif_lq1=dif_lq1[l],
                 dif_lk1=dif_lk1[l], dif_lq2=dif_lq2[l], dif_lk2=dif_lk2[l], dif_subln=dif_subln[l],
                 swa_qn=swa_qn[l], swa_kn=swa_kn[l], swa_sinks=swa_sinks[l], w_up=w_up[l], w_out=w_out[l])
        x = _mixer_layer(x, positions, tables, mod, p, lam_init)
        x = _moe(x, norm2[l], mod[4], mod[3], mod[5], w_router[l], b_router[l], l, w_gu, b_gu, w_dn, b_dn)
    return x
```

```python
import functools
import math

import numpy as np
import jax
import jax.numpy as jnp
from jax import lax
from jax.experimental import pallas as pl
from jax.experimental.pallas import tpu as pltpu

F32 = jnp.float32
BF16 = jnp.bfloat16

HEAD_DIM = 64
ROPE_THETA = 10000.0
NORM_EPS = 1e-6
NEG_INF = -1e30
KNOCKOUT = -3e38
N_BRANCH = 4

NSA_HEADS = 4
NSA_CMP_LEN = 32
NSA_CMP_STRIDE = 16
NSA_SEL_BLOCK = 64
NSA_N_SEL = 16
NSA_WINDOW = 512
NSA_FORCED_SCORE = 1e4

SB_HEADS = 4
DIFF_HEADS = 4
DIFF_DIM = 32
SWA_HEADS = 4
SWA_KV_HEADS = 2
SWA_WINDOW = 128

N_EXPERTS = 32
TOP_K = 4
SWIGLU_ALPHA = 1.702
SWIGLU_LIMIT = 7.0

LANES = 128
LOG2E = 1.4426950408889634
VMEM_LIMIT = 56 * 1024 * 1024

FLASH_TILE = 1024
SB_KEY_TILE = 256
BAND_TILE = 1024
CMP_TILE = 1024
PROJ_TILE = (1024, 2560)
ROW_TILE = 512
MERGE_TILE = 512
MOE_ROWS = 512
MOE_SLICES = 4


def _cparams(*sem):
    return pltpu.CompilerParams(dimension_semantics=sem, vmem_limit_bytes=VMEM_LIMIT)


def _dot(a, b):
    return jnp.dot(a, b, preferred_element_type=F32)


def _dot_nt(a, b):
    return lax.dot_general(a, b, (((1,), (1,)), ((), ())), preferred_element_type=F32)


def _iota2(shape, dim):
    return lax.broadcasted_iota(jnp.int32, shape, dim)


def _linear_kernel(x_ref, w_ref, b_ref, o_ref, *, precision):
    o_ref[...] = jnp.dot(x_ref[...], w_ref[...], preferred_element_type=F32,
                         precision=precision) + b_ref[...]


def _linear(x, w, b, tn, precision=None):
    m, k = x.shape
    n = w.shape[1]
    return pl.pallas_call(
        functools.partial(_linear_kernel, precision=precision),
        grid=(n // tn,),
        in_specs=[pl.BlockSpec((m, k), lambda j: (0, 0)),
                  pl.BlockSpec((k, tn), lambda j: (0, j)),
                  pl.BlockSpec((1, tn), lambda j: (0, j))],
        out_specs=pl.BlockSpec((m, tn), lambda j: (0, j)),
        out_shape=jax.ShapeDtypeStruct((m, n), F32),
        compiler_params=_cparams("arbitrary"),
        name="linear",
    )(x, w, b.reshape(1, n))


def _cmp_mlp_kernel(a_ref, b_ref, pe_ref, w2_ref, o_ref):
    hid = jax.nn.gelu(a_ref[...] + b_ref[...] + pe_ref[...])
    o_ref[...] = _dot(hid.astype(BF16), w2_ref[...])


def _compress(t, pe, w1, w2):
    bn, sn, dh = t.shape
    st = NSA_CMP_STRIDE
    half = st * dh
    nb = sn // st
    t16 = t.reshape(bn * nb, half).astype(BF16)
    w1cat = jnp.concatenate([w1[:half], w1[half:]], axis=1).astype(BF16)
    hidden = w1.shape[1]
    ab = _linear(t16, w1cat, jnp.zeros((2 * hidden,), F32), tn=2 * hidden)
    ab = ab.reshape(bn, nb, 2 * hidden)
    a = ab[:, :, :hidden]
    b_next = jnp.concatenate([ab[:, 1:, hidden:], jnp.zeros((bn, 1, hidden), F32)], axis=1)
    pe_term = jnp.dot(pe.reshape(1, NSA_CMP_LEN * dh), w1, precision=lax.Precision.HIGHEST)
    rows = bn * nb
    tm = min(512, rows)
    out = pl.pallas_call(
        _cmp_mlp_kernel,
        grid=(rows // tm,),
        in_specs=[pl.BlockSpec((tm, hidden), lambda i: (i, 0)),
                  pl.BlockSpec((tm, hidden), lambda i: (i, 0)),
                  pl.BlockSpec((1, hidden), lambda i: (0, 0)),
                  pl.BlockSpec((hidden, dh), lambda i: (0, 0))],
        out_specs=pl.BlockSpec((tm, dh), lambda i: (i, 0)),
        out_shape=jax.ShapeDtypeStruct((rows, dh), F32),
        compiler_params=_cparams("parallel"),
        name="cmp_mlp",
    )(a.reshape(rows, hidden), b_next.reshape(rows, hidden), pe_term, w2.astype(BF16))
    return out.reshape(bn, nb, dh)


def _norm_mod(x, g, sc, sh):
    r = lax.rsqrt(jnp.mean(x * x, axis=-1, keepdims=True) + NORM_EPS)
    return (x * r * g) * (1.0 + sc) + sh


def _proj_kernel(x_ref, g_ref, sc_ref, sh_ref, w_ref, o_ref, h_scr):
    @pl.when(pl.program_id(2) == 0)
    def _():
        h_scr[...] = _norm_mod(x_ref[0], g_ref[...], sc_ref[0], sh_ref[0]).astype(BF16)

    o_ref[0] = _dot(h_scr[...], w_ref[...]).astype(o_ref.dtype)


def _in_projection(x, g, sc, sh, w, tm=PROJ_TILE[0], tn=PROJ_TILE[1]):
    bn, sn, d = x.shape
    n = w.shape[1]
    tm = min(tm, sn)
    return pl.pallas_call(
        _proj_kernel,
        grid=(bn, sn // tm, n // tn),
        in_specs=[pl.BlockSpec((1, tm, d), lambda b, i, j: (b, i, 0)),
                  pl.BlockSpec((1, d), lambda b, i, j: (0, 0)),
                  pl.BlockSpec((1, 1, d), lambda b, i, j: (b, 0, 0)),
                  pl.BlockSpec((1, 1, d), lambda b, i, j: (b, 0, 0)),
                  pl.BlockSpec((d, tn), lambda b, i, j: (0, j))],
        out_specs=pl.BlockSpec((1, tm, tn), lambda b, i, j: (b, i, j)),
        out_shape=jax.ShapeDtypeStruct((bn, sn, n), BF16),
        scratch_shapes=[pltpu.VMEM((tm, d), BF16)],
        compiler_params=_cparams("parallel", "parallel", "arbitrary"),
        name="in_proj",
    )(x, g.reshape(1, d), sc, sh, w)


def _banded_kernel(*refs, tile, window, has_sink, grp):
    if has_sink:
        sink_ref, q_ref, kp_ref, kc_ref, vp_ref, vc_ref, o_ref = refs
    else:
        q_ref, kp_ref, kc_ref, vp_ref, vc_ref, o_ref = refs
    i = pl.program_id(2)
    w = window
    rows = grp * w
    row = _iota2((rows, w), 0)
    upper = _iota2((rows, w), 1) > (row & (w - 1))
    first_bias = jnp.where(i > 0, 0.0, NEG_INF)
    if has_sink:
        head0 = pl.program_id(1) * grp
        sink = jnp.full((rows, 1), sink_ref[head0], F32)
        for hh in range(1, grp):
            sink = jnp.where(_iota2((rows, 1), 0) >= hh * w, sink_ref[head0 + hh], sink)
    for u in range(tile // w):
        q = jnp.concatenate([q_ref[0, hh, u * w:(u + 1) * w, :] for hh in range(grp)], axis=0)
        if u == 0:
            k_prev, v_prev = kp_ref[0, 0], vp_ref[0, 0]
        else:
            k_prev, v_prev = kc_ref[0, 0, (u - 1) * w:u * w, :], vc_ref[0, 0, (u - 1) * w:u * w, :]
        k_cur, v_cur = kc_ref[0, 0, u * w:(u + 1) * w, :], vc_ref[0, 0, u * w:(u + 1) * w, :]
        s_prev = _dot_nt(q, k_prev)
        if u == 0:
            s_prev = s_prev + first_bias
        s = jnp.where(upper, s_prev, _dot_nt(q, k_cur))
        m = jnp.max(s, axis=-1, keepdims=True)
        if has_sink:
            m = jnp.maximum(m, sink)
        p = jnp.exp(s - m)
        den = jnp.sum(p, axis=-1, keepdims=True)
        if has_sink:
            den = den + jnp.exp(sink - m)
        else:
            den = jnp.maximum(den, 1e-30)
        o = (_dot(jnp.where(upper, p, 0.0).astype(BF16), v_prev)
             + _dot(jnp.where(upper, 0.0, p).astype(BF16), v_cur)) / den
        for hh in range(grp):
            o_ref[0, hh, u * w:(u + 1) * w, :] = o[hh * w:(hh + 1) * w]


def _banded_attention(q, k, v, window, sinks=None, tile=BAND_TILE):
    bn, hq, sn, d = q.shape
    hkv = k.shape[1]
    grp = hq // hkv
    tile = min(tile, sn)
    assert tile % window == 0 and sn % tile == 0 and window & (window - 1) == 0
    per = tile // window
    has_sink = sinks is not None
    qspec = pl.BlockSpec((1, grp, tile, d), lambda b, g, i: (b, g, i, 0))
    prev = pl.BlockSpec((1, 1, window, d), lambda b, g, i: (b, g, jnp.maximum(i * per - 1, 0), 0))
    cur = pl.BlockSpec((1, 1, tile, d), lambda b, g, i: (b, g, i, 0))
    in_specs = [qspec, prev, cur, prev, cur]
    args = [q, k, k, v, v]
    if has_sink:
        in_specs = [pl.BlockSpec(memory_space=pltpu.SMEM)] + in_specs
        args = [sinks.astype(F32)] + args
    return pl.pallas_call(
        functools.partial(_banded_kernel, tile=tile, window=window, has_sink=has_sink, grp=grp),
        grid=(bn, hkv, sn // tile),
        in_specs=in_specs,
        out_specs=pl.BlockSpec((1, grp, tile, d), lambda b, g, i: (b, g, i, 0)),
        out_shape=jax.ShapeDtypeStruct((bn, hq, sn, d), F32),
        compiler_params=_cparams("parallel", "parallel", "parallel"),
        name="banded_attn",
    )(*args)


def _nsa_cmp_kernel(q_ref, kc_ref, vct_ref, ovt_ref, oct_ref, bias_ref, *, tq, n_cmp, n_top, heads):
    i = pl.program_id(1)
    ncp = kc_ref.shape[1]
    nsel = ovt_ref.shape[0]
    dh = q_ref.shape[3]
    t = i * tq + _iota2((ncp, tq), 1)
    n = _iota2((ncp, tq), 0)
    valid = (n * NSA_CMP_STRIDE + (NSA_CMP_LEN - 1) <= t) & (n < n_cmp)
    kc = kc_ref[0]
    vct = vct_ref[0]
    psum = jnp.zeros((ncp, tq), F32)
    for h in range(heads):
        s = jnp.where(valid, _dot_nt(kc, q_ref[0, h]), NEG_INF)
        m = jnp.max(s, axis=0, keepdims=True)
        p = jnp.where(valid, jnp.exp(s - m), 0.0)
        p = p / jnp.maximum(jnp.sum(p, axis=0, keepdims=True), 1e-30)
        oct_ref[0, h * dh:(h + 1) * dh, :] = _dot(vct, p.astype(BF16))
        psum = psum + p
    hi = psum.astype(BF16)
    lo = (psum - hi.astype(F32)).astype(BF16)
    imp = _dot(ovt_ref[...], hi) + _dot(ovt_ref[...], lo)

    tt = i * tq + _iota2((nsel, tq), 1)
    blk = _iota2((nsel, tq), 0)
    cur = tt >> (NSA_SEL_BLOCK.bit_length() - 1)
    forced = (blk == 0) | (blk == cur) | (blk == cur - 1)
    valid_s = blk * NSA_SEL_BLOCK <= tt
    score = jnp.where(forced, NSA_FORCED_SCORE, jnp.where(valid_s, imp, -1.0))
    blk_f = blk.astype(F32)

    def pick(_, carry):
        score, sel = carry
        m = jnp.max(score, axis=0, keepdims=True)
        first = jnp.min(jnp.where(score == m, blk_f, float(nsel)), axis=0, keepdims=True)
        hit = blk_f == first
        return jnp.where(hit, KNOCKOUT, score), jnp.where(hit, 0.0, sel)

    _, bias = lax.fori_loop(0, n_top, pick, (score, jnp.full((nsel, tq), NEG_INF, F32)))
    bias_ref[0] = bias.astype(BF16)


def _nsa_cmp(q, kc, vc, overlap, n_cmp, n_top, tq=CMP_TILE):
    bn, heads, sn, dh = q.shape
    tq = min(tq, sn)
    ncp = kc.shape[1]
    nsel = overlap.shape[1]
    oct, bias_t = pl.pallas_call(
        functools.partial(_nsa_cmp_kernel, tq=tq, n_cmp=n_cmp, n_top=n_top, heads=heads),
        grid=(bn, sn // tq),
        in_specs=[pl.BlockSpec((1, heads, tq, dh), lambda b, i: (b, 0, i, 0)),
                  pl.BlockSpec((1, ncp, dh), lambda b, i: (b, 0, 0)),
                  pl.BlockSpec((1, dh, ncp), lambda b, i: (b, 0, 0)),
                  pl.BlockSpec((nsel, ncp), lambda b, i: (0, 0))],
        out_specs=[pl.BlockSpec((1, heads * dh, tq), lambda b, i: (b, 0, i)),
                   pl.BlockSpec((1, nsel, tq), lambda b, i: (b, 0, i))],
        out_shape=[jax.ShapeDtypeStruct((bn, heads * dh, sn), F32),
                   jax.ShapeDtypeStruct((bn, nsel, sn), BF16)],
        compiler_params=_cparams("parallel", "parallel"),
        name="nsa_cmp_topk",
    )(q, kc, vc.transpose(0, 2, 1), overlap.T)
    return oct.transpose(0, 2, 1), bias_t.transpose(0, 2, 1)


def _nsa_sel_kernel(q_ref, kx_ref, v_ref, bias_ref, o_ref, qx_scr, *, tq, tk, heads):
    i = pl.program_id(1)
    dh = v_ref.shape[2]
    bias = bias_ref[0]
    for h in range(heads):
        qx_scr[h] = jnp.concatenate([bias, q_ref[0, h]], axis=1)

    def tile(j, state, diagonal):
        start = pl.multiple_of(j * tk, tk)
        kx = kx_ref[0, pl.ds(start, tk), :]
        v = v_ref[0, pl.ds(start, tk), :]
        if diagonal:
            causal = (start + _iota2((tq, tk), 1)) <= (i * tq + _iota2((tq, tk), 0))
        new_state = []
        for h in range(heads):
            m_old, l_old, acc_old = state[h]
            s = _dot_nt(qx_scr[h], kx)
            if diagonal:
                s = jnp.where(causal, s, NEG_INF)
            m_new = jnp.maximum(m_old, jnp.max(s, axis=-1, keepdims=True))
            alpha = jnp.exp(m_old - m_new)
            p = jnp.exp(s - m_new)
            l_new = alpha * l_old + jnp.sum(p, axis=-1, keepdims=True)
            acc_new = alpha * acc_old + _dot(p.astype(BF16), v)
            new_state.append((m_new, l_new, acc_new))
        return tuple(new_state)

    init = tuple((jnp.full((tq, 1), NEG_INF, F32), jnp.zeros((tq, 1), F32), jnp.zeros((tq, dh), F32))
                 for _ in range(heads))
    n_full = (i * tq) // tk
    state = lax.fori_loop(0, n_full, lambda j, st: tile(j, st, False), init)
    state = tile(n_full, state, True)
    for h in range(heads):
        o_ref[0, :, h * dh:(h + 1) * dh] = state[h][2] / jnp.maximum(state[h][1], 1e-30)


def _nsa_sel(q, kx, v, bias, tq=FLASH_TILE, tk=FLASH_TILE):
    bn, heads, sn, dh = q.shape
    nsel = bias.shape[2]
    tq, tk = min(tq, sn), min(tk, sn)
    assert tk % tq == 0 and sn % tk == 0
    return pl.pallas_call(
        functools.partial(_nsa_sel_kernel, tq=tq, tk=tk, heads=heads),
        grid=(bn, sn // tq),
        in_specs=[pl.BlockSpec((1, heads, tq, dh), lambda b, i: (b, 0, i, 0)),
                  pl.BlockSpec((1, sn, nsel + dh), lambda b, i: (b, 0, 0)),
                  pl.BlockSpec((1, sn, dh), lambda b, i: (b, 0, 0)),
                  pl.BlockSpec((1, tq, nsel), lambda b, i: (b, i, 0))],
        out_specs=pl.BlockSpec((1, tq, heads * dh), lambda b, i: (b, i, 0)),
        out_shape=jax.ShapeDtypeStruct((bn, sn, heads * dh), F32),
        scratch_shapes=[pltpu.VMEM((heads, tq, nsel + dh), BF16)],
        compiler_params=_cparams("parallel", "arbitrary"),
        name="nsa_selected_attn",
    )(q, kx, v, bias)


def _sb_kernel(q_ref, k_ref, v_ref, u_ref, o_ref, *, tq, tk):
    i = pl.program_id(2)
    q = q_ref[0, 0]
    per_q = tq // tk

    def tile(jj, carry, first_row):
        diagonal = first_row is not None
        r0 = first_row if diagonal else 0
        start = pl.multiple_of(jj * tk, tk)
        nz = _dot_nt(q[r0:], k_ref[0, 0, pl.ds(start, tk), :])
        neg_abs = lax.bitcast_convert_type(lax.bitcast_convert_type(nz, jnp.uint32) | jnp.uint32(0x80000000), F32)
        log_keep = jnp.minimum(nz, 0.0) - jnp.log2(1.0 + jnp.exp2(neg_abs))
        if diagonal:
            strict = _iota2(nz.shape, 1) < _iota2(nz.shape, 0)
            log_keep = jnp.where(strict, log_keep, 0.0)
        cum = _dot(log_keep.astype(BF16), u_ref[...])
        a = jnp.exp2(cum + jnp.concatenate([carry[r0:]] * (tk // LANES), axis=1) - nz)
        if diagonal:
            a = jnp.where(strict, a, 0.0)
        out = _dot(a.astype(BF16), v_ref[0, 0, pl.ds(start, tk), :])
        new_carry = carry[r0:] + jnp.broadcast_to(cum[:, 0:1], (tq - r0, LANES))
        if r0:
            out = jnp.concatenate([jnp.zeros((r0, out.shape[1]), F32), out], axis=0)
            new_carry = jnp.concatenate([carry[:r0], new_carry], axis=0)
        return out, new_carry

    def group(first, carry, diagonal):
        total = None
        for r in range(per_q):
            out, carry = tile(first - r, carry, (per_q - 1 - r) * tk if diagonal else None)
            total = out if total is None else total + out
        return total, carry

    acc, carry = group(i * per_q + per_q - 1, jnp.zeros((tq, LANES), F32), True)

    def body(p, state):
        acc, carry = state
        out, carry = group((i - p) * per_q - 1, carry, False)
        return acc + out, carry

    acc, _ = lax.fori_loop(0, i, body, (acc, carry))
    o_ref[0, 0] = acc


def _stick_breaking(q, k, v, tq=FLASH_TILE, tk=SB_KEY_TILE):
    bn, heads, sn, dh = q.shape
    tq = min(tq, sn)
    incl = (np.arange(tk)[:, None] >= np.arange(tk)[None, :]).astype(np.float32)
    u = jnp.asarray(incl, BF16)
    kv = pl.BlockSpec((1, 1, sn, dh), lambda b, h, i: (b, h, 0, 0))
    return pl.pallas_call(
        functools.partial(_sb_kernel, tq=tq, tk=tk),
        grid=(bn, heads, sn // tq),
        in_specs=[pl.BlockSpec((1, 1, tq, dh), lambda b, h, i: (b, h, i, 0)), kv, kv,
                  pl.BlockSpec((tk, tk), lambda b, h, i: (0, 0))],
        out_specs=pl.BlockSpec((1, 1, tq, dh), lambda b, h, i: (b, h, i, 0)),
        out_shape=jax.ShapeDtypeStruct((bn, heads, sn, dh), F32),
        compiler_params=_cparams("parallel", "parallel", "arbitrary"),
        name="stick_breaking_attn",
    )(q, k, v, u)


def _diff_kernel(sc_ref, q_ref, k_ref, v_ref, g_ref, o_ref, *, tq, tk):
    i = pl.program_id(2)
    dv = v_ref.shape[3]

    def tile(j, state, diagonal):
        start = pl.multiple_of(j * tk, tk)
        v = v_ref[0, 0, pl.ds(start, tk), :]
        if diagonal:
            causal = (start + _iota2((tq, tk), 1)) <= (i * tq + _iota2((tq, tk), 0))
        new_state = []
        for mi in range(2):
            m_old, l_old, acc_old = state[mi]
            s = _dot_nt(q_ref[0, mi], k_ref[0, mi, pl.ds(start, tk), :])
            if diagonal:
                s = jnp.where(causal, s, NEG_INF)
            m_new = jnp.maximum(m_old, jnp.max(s, axis=-1, keepdims=True))
            alpha = jnp.exp(m_old - m_new)
            p = jnp.exp(s - m_new)
            l_new = alpha * l_old + jnp.sum(p, axis=-1, keepdims=True)
            acc_new = alpha * acc_old + _dot(p.astype(BF16), v)
            new_state.append((m_new, l_new, acc_new))
        return tuple(new_state)

    init = tuple((jnp.full((tq, 1), NEG_INF, F32), jnp.zeros((tq, 1), F32), jnp.zeros((tq, dv), F32))
                 for _ in range(2))
    n_full = (i * tq) // tk
    state = lax.fori_loop(0, n_full, lambda j, st: tile(j, st, False), init)
    state = tile(n_full, state, True)
    lam = sc_ref[0]
    post = sc_ref[1]
    o = (state[0][2] / jnp.maximum(state[0][1], 1e-30)
         - lam * (state[1][2] / jnp.maximum(state[1][1], 1e-30)))
    r = lax.rsqrt(jnp.mean(o * o, axis=-1, keepdims=True) + NORM_EPS)
    o_ref[0, 0] = (o * r * g_ref[...]) * post


def _diff_attention(q, k, v, subln, lam, post, tq=FLASH_TILE, tk=FLASH_TILE):
    bn, h2, sn, dd = q.shape
    heads = h2 // 2
    dv = v.shape[3]
    tq, tk = min(tq, sn), min(tk, sn)
    assert tk % tq == 0 and sn % tk == 0
    scal = jnp.stack([lam, post]).astype(F32)
    return pl.pallas_call(
        functools.partial(_diff_kernel, tq=tq, tk=tk),
        grid=(bn, heads, sn // tq),
        in_specs=[pl.BlockSpec(memory_space=pltpu.SMEM),
                  pl.BlockSpec((1, 2, tq, dd), lambda b, h, i: (b, h, i, 0)),
                  pl.BlockSpec((1, 2, sn, dd), lambda b, h, i: (b, h, 0, 0)),
                  pl.BlockSpec((1, 1, sn, dv), lambda b, h, i: (b, h, 0, 0)),
                  pl.BlockSpec((1, dv), lambda b, h, i: (0, 0))],
        out_specs=pl.BlockSpec((1, 1, tq, dv), lambda b, h, i: (b, h, i, 0)),
        out_shape=jax.ShapeDtypeStruct((bn, heads, sn, dv), F32),
        compiler_params=_cparams("parallel", "parallel", "arbitrary"),
        name="diff_attn",
    )(scal, q, k, v, subln.reshape(1, dv).astype(F32))


def _merge_kernel(brg_ref, ga0_ref, ga1_ref, ga2_ref, oc_ref, os_ref, ow_ref, ob_ref, ocd_ref, od_ref,
                  wup_ref, wout_ref, x_ref, g1_ref, o_ref, *, d):
    def heads(ref):
        return jnp.concatenate([ref[0, h] for h in range(ref.shape[1])], axis=1)

    def gate(x):
        return jax.nn.sigmoid(x.astype(F32))

    o_a = gate(ga0_ref[0]) * oc_ref[0] + gate(ga1_ref[0]) * os_ref[0] + gate(ga2_ref[0]) * heads(ow_ref)
    branches = (o_a, heads(ob_ref), heads(ocd_ref), heads(od_ref))
    merged = None
    for bi, o in enumerate(branches):
        term = gate(brg_ref[0, :, bi * d:(bi + 1) * d]) * _dot(o.astype(BF16), wup_ref[bi])
        merged = term if merged is None else merged + term
    y = _dot(merged.astype(BF16), wout_ref[...])
    o_ref[0] = x_ref[0] + g1_ref[0] * y


def _merge(proj, o_c, o_s, o_w, o_b, o_cd, o_d, w_up, w_out, x, g1, ga_col, tm=MERGE_TILE):
    bn, sn, d = x.shape
    bw = o_c.shape[2]
    nh, dh = o_b.shape[1], o_b.shape[3]
    assert ga_col % bw == 0
    gblk = ga_col // bw
    row = lambda b, i: (b, i, 0)
    bspec = pl.BlockSpec((1, tm, bw), row)
    hspec = pl.BlockSpec((1, nh, tm, dh), lambda b, i: (b, 0, i, 0))
    return pl.pallas_call(
        functools.partial(_merge_kernel, d=d),
        grid=(bn, sn // tm),
        in_specs=[pl.BlockSpec((1, tm, N_BRANCH * d), row),
                  pl.BlockSpec((1, tm, bw), lambda b, i: (b, i, gblk)),
                  pl.BlockSpec((1, tm, bw), lambda b, i: (b, i, gblk + 1)),
                  pl.BlockSpec((1, tm, bw), lambda b, i: (b, i, gblk + 2)),
                  bspec, bspec, hspec, hspec, hspec, hspec,
                  pl.BlockSpec((N_BRANCH, bw, d), lambda b, i: (0, 0, 0)),
                  pl.BlockSpec((d, d), lambda b, i: (0, 0)),
                  pl.BlockSpec((1, tm, d), row),
                  pl.BlockSpec((1, 1, d), lambda b, i: (b, 0, 0))],
        out_specs=pl.BlockSpec((1, tm, d), row),
        out_shape=jax.ShapeDtypeStruct((bn, sn, d), F32),
        compiler_params=_cparams("parallel", "parallel"),
        name="branch_merge",
    )(proj, proj, proj, proj, o_c, o_s, o_w, o_b, o_cd, o_d, w_up, w_out, x, g1)


def _router_kernel(x_ref, g_ref, sc_ref, sh_ref, wr_ref, br_ref, tri_ref, h_ref, e_ref, w_ref, rank_ref, cnt_ref,
                   run_scr):
    @pl.when((pl.program_id(0) == 0) & (pl.program_id(1) == 0))
    def _():
        run_scr[...] = jnp.zeros(run_scr.shape, F32)

    h = _norm_mod(x_ref[0], g_ref[...], sc_ref[0], sh_ref[0])
    h_ref[0] = h.astype(BF16)
    logits = jnp.dot(h, wr_ref[...], preferred_element_type=F32,
                     precision=lax.Precision.HIGHEST) + br_ref[...]
    lane = _iota2(logits.shape, 1)
    lane_f = lane.astype(F32)
    cur = logits
    vals, idxs = [], []
    chosen = jnp.zeros(logits.shape, F32)
    for _ in range(TOP_K):
        m = jnp.max(cur, axis=-1, keepdims=True)
        first = jnp.min(jnp.where(cur == m, lane_f, float(LANES)), axis=-1, keepdims=True)
        vals.append(m)
        idxs.append(first)
        hit = lane_f == first
        cur = jnp.where(hit, KNOCKOUT, cur)
        chosen = jnp.where(hit, 1.0, chosen)
    exps = [jnp.exp(v - vals[0]) for v in vals]
    den = exps[0]
    for e in exps[1:]:
        den = den + e
    earlier = _dot(tri_ref[...], chosen.astype(BF16)) + run_scr[0:1, :]
    e_out = jnp.zeros(logits.shape, F32)
    w_out = jnp.zeros(logits.shape, F32)
    r_out = jnp.zeros(logits.shape, F32)
    for k in range(TOP_K):
        rank_k = jnp.sum(jnp.where(lane_f == idxs[k], earlier, 0.0), axis=-1, keepdims=True)
        e_out = jnp.where(lane == k, idxs[k], e_out)
        w_out = jnp.where(lane == k, exps[k] / den, w_out)
        r_out = jnp.where(lane == k, rank_k, r_out)
    e_ref[0] = e_out[:, :TOP_K].astype(jnp.int32)
    w_ref[0] = w_out[:, :TOP_K]
    rank_ref[0] = r_out[:, :TOP_K].astype(jnp.int32)
    total = run_scr[...] + jnp.sum(chosen, axis=0, keepdims=True)
    run_scr[...] = total
    cnt_ref[...] = total


def _router(x, g, sc, sh, w_router, b_router, tm=ROW_TILE):
    bn, sn, d = x.shape
    ne = w_router.shape[1]
    wr = jnp.zeros((d, LANES), F32).at[:, :ne].set(w_router)
    br = jnp.full((1, LANES), NEG_INF, F32).at[0, :ne].set(b_router)
    tri = jnp.asarray((np.arange(tm)[:, None] > np.arange(tm)[None, :]).astype(np.float32), BF16)
    row = lambda b, i: (b, i, 0)
    return pl.pallas_call(
        _router_kernel,
        grid=(bn, sn // tm),
        in_specs=[pl.BlockSpec((1, tm, d), row),
                  pl.BlockSpec((1, d), lambda b, i: (0, 0)),
                  pl.BlockSpec((1, 1, d), lambda b, i: (b, 0, 0)),
                  pl.BlockSpec((1, 1, d), lambda b, i: (b, 0, 0)),
                  pl.BlockSpec((d, LANES), lambda b, i: (0, 0)),
                  pl.BlockSpec((1, LANES), lambda b, i: (0, 0)),
                  pl.BlockSpec((tm, tm), lambda b, i: (0, 0))],
        out_specs=[pl.BlockSpec((1, tm, d), row),
                   pl.BlockSpec((1, tm, TOP_K), row),
                   pl.BlockSpec((1, tm, TOP_K), row),
                   pl.BlockSpec((1, tm, TOP_K), row),
                   pl.BlockSpec((8, LANES), lambda b, i: (0, 0))],
        out_shape=[jax.ShapeDtypeStruct((bn, sn, d), BF16),
                   jax.ShapeDtypeStruct((bn, sn, TOP_K), jnp.int32),
                   jax.ShapeDtypeStruct((bn, sn, TOP_K), F32),
                   jax.ShapeDtypeStruct((bn, sn, TOP_K), jnp.int32),
                   jax.ShapeDtypeStruct((8, LANES), F32)],
        scratch_shapes=[pltpu.VMEM((8, LANES), F32)],
        compiler_params=_cparams("arbitrary", "arbitrary"),
        name="moe_router",
    )(x, g.reshape(1, d), sc, sh, wr, br, tri)


def _expert_kernel(ce_ref, x_ref, wgu_ref, bgu_ref, wdn_ref, bdn_ref, rw_ref, *rest, ff, fc, off):
    o_ref, wgu_scr, wdn_scr = rest[-3:]
    c = pl.program_id(0)

    @pl.when((c == 0) | (ce_ref[c + off] != ce_ref[jnp.maximum(c + off - 1, 0)]))
    def _():
        wgu_scr[...] = wgu_ref[0, 0].astype(BF16)
        wdn_scr[...] = wdn_ref[0, 0].astype(BF16)

    x = x_ref[...]
    y = None
    for j in range(ff // fc):
        g = _dot(x, wgu_scr[:, j * fc:(j + 1) * fc]) + bgu_ref[0, 0, :, j * fc:(j + 1) * fc]
        u = _dot(x, wgu_scr[:, ff + j * fc:ff + (j + 1) * fc]) + bgu_ref[0, 0, :, ff + j * fc:ff + (j + 1) * fc]
        g = jnp.minimum(g, SWIGLU_LIMIT)
        u = jnp.clip(u, -SWIGLU_LIMIT, SWIGLU_LIMIT)
        act = g * jax.nn.sigmoid(SWIGLU_ALPHA * g) * (u + 1.0)
        part = _dot(act.astype(BF16), wdn_scr[j * fc:(j + 1) * fc, :])
        y = part if y is None else y + part
    o_ref[...] = ((y + bdn_ref[0, 0]) * rw_ref[...]).astype(o_ref.dtype)


def _expert_ffn(h, row_tok, chunk_e, layer, w_gu, b_gu, w_dn, b_dn, row_w, tm=MOE_ROWS, fc=512, groups=MOE_SLICES):
    n_rows = row_tok.shape[0]
    d = h.shape[1]
    nl, ne, _, ff2 = w_gu.shape
    ff = ff2 // 2
    n_chunks = n_rows // tm
    assert n_chunks % groups == 0
    per = n_chunks // groups
    b_gu3, b_dn3, row_w2 = b_gu.reshape(nl, ne, 1, ff2), b_dn.reshape(nl, ne, 1, d), row_w.reshape(n_rows, 1)
    y = None
    for gi in range(groups):
        off = gi * per
        rows = h[row_tok[off * tm:(off + per) * tm]]
        in_specs = [pl.BlockSpec((tm, d), lambda c, ce: (c, 0)),
                    pl.BlockSpec((1, 1, d, ff2), lambda c, ce, off=off: (layer, ce[c + off], 0, 0)),
                    pl.BlockSpec((1, 1, 1, ff2), lambda c, ce, off=off: (layer, ce[c + off], 0, 0)),
                    pl.BlockSpec((1, 1, ff, d), lambda c, ce, off=off: (layer, ce[c + off], 0, 0)),
                    pl.BlockSpec((1, 1, 1, d), lambda c, ce, off=off: (layer, ce[c + off], 0, 0)),
                    pl.BlockSpec((tm, 1), lambda c, ce, off=off: (c + off, 0))]
        args = [chunk_e, rows, w_gu, b_gu3, w_dn, b_dn3, row_w2]
        aliases = {}
        if y is not None:
            in_specs.append(pl.BlockSpec(memory_space=pl.ANY))
            args.append(y)
            aliases = {len(args) - 1: 0}
        y = pl.pallas_call(
            functools.partial(_expert_kernel, ff=ff, fc=fc, off=off),
            grid_spec=pltpu.PrefetchScalarGridSpec(
                num_scalar_prefetch=1, grid=(per,), in_specs=in_specs,
                out_specs=pl.BlockSpec((tm, d), lambda c, ce, off=off: (c + off, 0)),
                scratch_shapes=[pltpu.VMEM((d, ff2), BF16), pltpu.VMEM((ff, d), BF16)]),
            out_shape=jax.ShapeDtypeStruct((n_rows, d), BF16),
            input_output_aliases=aliases,
            compiler_params=_cparams("arbitrary"),
            name="moe_expert_ffn",
        )(*args)
    return y


def _combine_kernel(y_ref, x_ref, g2_ref, o_ref):
    tot = y_ref[0].astype(F32)
    for k in range(1, TOP_K):
        tot = tot + y_ref[k].astype(F32)
    o_ref[0] = x_ref[0] + g2_ref[0] * tot


def _combine(y4, x, g2, tm=ROW_TILE):
    bn, sn, d = x.shape
    row = lambda b, i: (b, i, 0)
    nt = sn // tm
    return pl.pallas_call(
        _combine_kernel,
        grid=(bn, nt),
        in_specs=[pl.BlockSpec((TOP_K, tm, d), lambda b, i: (0, b * nt + i, 0)),
                  pl.BlockSpec((1, tm, d), row),
                  pl.BlockSpec((1, 1, d), lambda b, i: (b, 0, 0))],
        out_specs=pl.BlockSpec((1, tm, d), row),
        out_shape=jax.ShapeDtypeStruct((bn, sn, d), F32),
        compiler_params=_cparams("parallel", "parallel"),
        name="moe_combine",
    )(y4, x, g2)


def _moe(x, g, sc, sh, g2, w_router, b_router, layer, w_gu, b_gu, w_dn, b_dn):
    bn, sn, d = x.shape
    n_tok = bn * sn
    n_asg = n_tok * TOP_K
    tm = MOE_ROWS
    h, e_out, w_out, rank_out, totals = _router(x, g, sc, sh, w_router, b_router)
    e_tok = e_out.reshape(n_tok, TOP_K)
    w_flat = w_out.reshape(-1)
    counts = totals[0, :N_EXPERTS].astype(jnp.int32)
    starts = jnp.cumsum(counts) - counts
    padded = (counts + tm - 1) // tm * tm
    pad_ends = jnp.cumsum(padded)
    pad_starts = pad_ends - padded
    pos = pad_starts[e_tok] + rank_out.reshape(n_tok, TOP_K)
    n_chunks = n_asg // tm + N_EXPERTS
    chunk_start = jnp.arange(n_chunks, dtype=jnp.int32) * tm
    chunk_e = jnp.minimum(jnp.sum(chunk_start[:, None] >= pad_ends[None, :], axis=1), N_EXPERTS - 1).astype(jnp.int32)
    order = jnp.argsort(e_tok.reshape(-1))
    src = (chunk_start - (pad_starts - starts)[chunk_e])[:, None] + jnp.arange(tm, dtype=jnp.int32)[None, :]
    row_valid = (src < (starts + counts)[chunk_e][:, None]).reshape(-1)
    asg = order[jnp.clip(src, 0, n_asg - 1).reshape(-1)].astype(jnp.int32)
    row_tok = asg // TOP_K
    row_w = jnp.where(row_valid, w_flat[asg], 0.0)
    y = _expert_ffn(h.reshape(n_tok, d), row_tok, chunk_e, layer, w_gu, b_gu, w_dn, b_dn, row_w)
    return _combine(y[pos.T], x, g2)


def _norm_rope(x, g, cos, sin, bd, hd):
    sq = x * x
    hi = sq.astype(BF16)
    lo = (sq - hi.astype(F32)).astype(BF16)
    ss = _dot(hi, bd) + _dot(lo, bd)
    y = x * lax.rsqrt(ss * (1.0 / hd) + NORM_EPS) * g
    half = hd // 2
    first = (_iota2(x.shape, 1) & (hd - 1)) < half
    partner = jnp.where(first, pltpu.roll(y, LANES - half, 1), pltpu.roll(y, half, 1))
    return y * cos + partner * sin


def _prep_kernel(aq_ref, bq_ref, bk_ref, bv_ref, cq_ref, ck_ref, cv_ref, dq_ref, akc_ref, aks_ref, akw_ref,
                 dk_ref, dv_ref, c64_ref, s64_ref, c32_ref, s32_ref, gaq_ref, gak_ref, gcq_ref, gck_ref,
                 gdq_ref, gdk_ref, bd64_ref, bd32_ref,
                 oaq_ref, okc_ref, ovc_ref, okx_ref, ovs_ref, okw_ref, ovw_ref, obq_ref, obk_ref, obv_ref,
                 ocq_ref, ock_ref, ocv_ref, odq_ref, odk_ref, odv_ref, *, ts, sb_scale):
    hd, dd = HEAD_DIM, DIFF_DIM
    c64, s64, c32, s32 = c64_ref[0], s64_ref[0], c32_ref[0], s32_ref[0]
    bd64, bd32 = bd64_ref[...], bd32_ref[...]

    def slabs(ref):
        x = ref[0].astype(F32)
        return [x[:, c * LANES:(c + 1) * LANES] for c in range(x.shape[1] // LANES)]

    def put_heads(o_ref, c, y, width):
        per = LANES // width
        for u in range(per):
            o_ref[0, c * per + u] = y[:, u * width:(u + 1) * width].astype(o_ref.dtype)

    for src, gain, dst in ((aq_ref, gaq_ref, oaq_ref), (dq_ref, gdq_ref, odq_ref), (dk_ref, gdk_ref, odk_ref)):
        for c, x in enumerate(slabs(src)):
            put_heads(dst, c, _norm_rope(x, gain[:, c * LANES:(c + 1) * LANES], c64, s64, bd64, hd), hd)
    kc = _norm_rope(akc_ref[0].astype(F32), gak_ref[0:1, :], c64, s64, bd64, hd)
    ks = _norm_rope(aks_ref[0].astype(F32), gak_ref[1:2, :], c64, s64, bd64, hd)
    kw = _norm_rope(akw_ref[0].astype(F32), gak_ref[2:3, :], c64, s64, bd64, hd)
    okc_ref[0] = kc[:, :hd].astype(BF16)
    ovc_ref[0] = akc_ref[0][:, hd:].astype(BF16)
    nsel = okx_ref.shape[2] - hd
    blk = (pl.program_id(1) * ts + _iota2((ts, nsel), 0)) >> (NSA_SEL_BLOCK.bit_length() - 1)
    okx_ref[0, :, :nsel] = jnp.where(blk == _iota2((ts, nsel), 1), 1.0, 0.0).astype(BF16)
    okx_ref[0, :, nsel:] = ks[:, :hd].astype(BF16)
    ovs_ref[0] = aks_ref[0][:, hd:].astype(BF16)
    okw_ref[0, 0] = kw[:, :hd].astype(BF16)
    ovw_ref[0, 0] = akw_ref[0][:, hd:].astype(BF16)
    for c, x in enumerate(slabs(bq_ref)):
        put_heads(obq_ref, c, x * sb_scale, hd)
    for src, dst in ((bk_ref, obk_ref), (bv_ref, obv_ref), (cv_ref, ocv_ref)):
        for c, x in enumerate(slabs(src)):
            put_heads(dst, c, x, hd)
    put_heads(odv_ref, 0, dv_ref[0].astype(F32), hd)
    for src, gain, dst in ((cq_ref, gcq_ref, ocq_ref), (ck_ref, gck_ref, ock_ref)):
        for c, x in enumerate(slabs(src)):
            put_heads(dst, c, _norm_rope(x, gain[:, c * LANES:(c + 1) * LANES], c32, s32, bd32, dd), dd)


def _rope_tables(positions, hd):
    half = hd // 2
    inv = ROPE_THETA ** (-jnp.arange(half, dtype=F32) * 2.0 / hd)
    ang = positions.astype(F32)[..., None] * inv
    cos, sin = jnp.cos(ang), jnp.sin(ang)
    reps = LANES // hd
    return (jnp.tile(jnp.concatenate([cos, cos], axis=-1), (1, 1, reps)),
            jnp.tile(jnp.concatenate([-sin, sin], axis=-1), (1, 1, reps)))


def _prep(proj, cols, tables, p, n_sel, ts=ROW_TILE):
    bn, sn, _ = proj.shape
    hd, dd = HEAD_DIM, DIFF_DIM
    scale = hd ** -0.5
    c64, s64, c32, s32 = tables

    def cspec(name):
        off, width = cols[name]
        assert off % width == 0
        return pl.BlockSpec((1, ts, width), lambda b, i, blk=off // width: (b, i, blk))

    def tile_gain(g, reps, mult=1.0):
        return (jnp.tile(g.astype(F32), reps) * mult).reshape(1, -1)

    ones = jnp.ones((hd,), F32)
    gak = jnp.stack([jnp.concatenate([p["nsa_kn"][j].astype(F32), ones]) for j in range(3)])
    gains = [tile_gain(p["nsa_qn"], NSA_HEADS, scale), gak,
             tile_gain(p["dif_qn"], 2 * DIFF_HEADS, dd ** -0.5), tile_gain(p["dif_kn"], 2 * DIFF_HEADS),
             tile_gain(p["swa_qn"], SWA_HEADS, scale), tile_gain(p["swa_kn"], SWA_KV_HEADS)]
    lane = np.arange(LANES)
    bd64 = jnp.asarray((lane[:, None] // hd == lane[None, :] // hd).astype(np.float32), BF16)
    bd32 = jnp.asarray((lane[:, None] // dd == lane[None, :] // dd).astype(np.float32), BF16)
    names = ("a_q", "b_q", "b_k", "b_v", "c_q", "c_k", "c_v", "d_q", "a_kcvc", "a_ksvs", "a_kwvw", "d_k", "d_v")
    tab = pl.BlockSpec((1, ts, LANES), lambda b, i: (b, i, 0))
    full = lambda a: pl.BlockSpec(a.shape, lambda b, i: (0,) * a.ndim)

    def hm(nh, w):
        return (jax.ShapeDtypeStruct((bn, nh, sn, w), BF16), pl.BlockSpec((1, nh, ts, w), lambda b, i: (b, 0, i, 0)))

    def tm_(w):
        return (jax.ShapeDtypeStruct((bn, sn, w), BF16), pl.BlockSpec((1, ts, w), lambda b, i: (b, i, 0)))

    outs = [hm(NSA_HEADS, hd), tm_(hd), tm_(hd), tm_(n_sel + hd), tm_(hd), hm(1, hd), hm(1, hd),
            hm(SB_HEADS, hd), hm(SB_HEADS, hd), hm(SB_HEADS, hd),
            hm(2 * DIFF_HEADS, dd), hm(2 * DIFF_HEADS, dd), hm(DIFF_HEADS, 2 * dd),
            hm(SWA_HEADS, hd), hm(SWA_KV_HEADS, hd), hm(SWA_KV_HEADS, hd)]
    consts = gains + [bd64, bd32]
    res = pl.pallas_call(
        functools.partial(_prep_kernel, ts=ts, sb_scale=-scale * LOG2E),
        grid=(bn, sn // ts),
        in_specs=[cspec(n) for n in names] + [tab] * 4 + [full(a) for a in consts],
        out_specs=[o[1] for o in outs],
        out_shape=[o[0] for o in outs],
        compiler_params=_cparams("parallel", "parallel"),
        name="mixer_prep",
    )(*([proj] * len(names)), c64, s64, c32, s32, *consts)
    keys = ("a_q", "kc", "vc", "kx", "vs", "kw", "vw", "b_q", "b_k", "b_v", "c_q", "c_k", "c_v", "d_q", "d_k", "d_v")
    return dict(zip(keys, res))


def _layer_columns(d):
    cols = {}
    off = 0
    for name, width in (("br_g", N_BRANCH * d), ("ga", 3 * NSA_HEADS * HEAD_DIM), ("a_q", 256),
                        ("b_q", 256), ("b_k", 256), ("b_v", 256), ("c_q", 256), ("c_k", 256), ("c_v", 256),
                        ("d_q", 256), ("a_kcvc", 128), ("a_ksvs", 128), ("a_kwvw", 128), ("d_k", 128), ("d_v", 128)):
        cols[name] = (off, width)
        off += width
    return cols, off


def _reorder_w_in(w_in, d, n_pad):
    ref_splits = (256, 128, 128, 128, 12, 256, 256, 256, 256, 256, 256, 256, 128, 128, N_BRANCH * d)
    names = ("a_q", "a_kcvc", "a_ksvs", "a_kwvw", "a_g", "b_q", "b_k", "b_v",
             "c_q", "c_k", "c_v", "d_q", "d_k", "d_v", "br_g")
    starts = np.cumsum((0,) + ref_splits)
    src = {n: (int(starts[i]), ref_splits[i]) for i, n in enumerate(names)}
    cols, total = _layer_columns(d)
    idx = np.zeros((n_pad,), np.int32)
    keep = np.zeros((n_pad,), np.float32)
    for name, (off, width) in cols.items():
        if name == "ga":
            g0 = src["a_g"][0]
            for j in range(3):
                for h in range(NSA_HEADS):
                    base = off + j * NSA_HEADS * HEAD_DIM + h * HEAD_DIM
                    idx[base:base + HEAD_DIM] = g0 + h * 3 + j
        else:
            idx[off:off + width] = src[name][0] + np.arange(width)
        keep[off:off + width] = 1.0
    w = w_in[:, idx] * keep[None, :]
    return w.astype(BF16)


def _mixer_layer(x, positions, tables, mod, p, lam_init):
    bn, sn, d = x.shape
    sh1, sc1, g1 = mod[0], mod[1], mod[2]
    cols, total = _layer_columns(d)
    n_pad = -(-total // PROJ_TILE[1]) * PROJ_TILE[1]
    w_in = _reorder_w_in(p["w_in"], d, n_pad)
    proj = _in_projection(x, p["norm1"], sc1, sh1, w_in)
    n_cmp = (sn - NSA_CMP_LEN) // NSA_CMP_STRIDE + 1
    n_sel = sn // NSA_SEL_BLOCK
    n_top = min(NSA_N_SEL, n_sel)
    t = _prep(proj, cols, tables, p, n_sel)
    k_cmp = _compress(t["kc"], p["nsa_pe_k"], p["nsa_w1_k"], p["nsa_w2_k"]).astype(BF16)
    v_cmp = _compress(t["vc"], p["nsa_pe_v"], p["nsa_w1_v"], p["nsa_w2_v"]).astype(BF16)
    cmp_start = np.arange(sn // NSA_CMP_STRIDE) * NSA_CMP_STRIDE
    sel_start = np.arange(n_sel) * NSA_SEL_BLOCK
    overlap = ((cmp_start[:, None] <= (sel_start + NSA_SEL_BLOCK - 1)[None, :]) &
               ((cmp_start + NSA_CMP_LEN - 1)[:, None] >= sel_start[None, :]) &
               (np.arange(sn // NSA_CMP_STRIDE) < n_cmp)[:, None]).astype(np.float32)
    o_c, bias = _nsa_cmp(t["a_q"], k_cmp, v_cmp, jnp.asarray(overlap, BF16), n_cmp, n_top)
    o_s = _nsa_sel(t["a_q"], t["kx"], t["vs"], bias)
    o_w = _banded_attention(t["a_q"], t["kw"], t["vw"], NSA_WINDOW)
    o_b = _stick_breaking(t["b_q"], t["b_k"], t["b_v"])
    lam = (jnp.exp(jnp.sum(p["dif_lq1"] * p["dif_lk1"])) - jnp.exp(jnp.sum(p["dif_lq2"] * p["dif_lk2"]))
           + lam_init)
    o_cd = _diff_attention(t["c_q"], t["c_k"], t["c_v"], p["dif_subln"], lam, jnp.asarray(1.0 - lam_init, F32))
    o_d = _banded_attention(t["d_q"], t["d_k"], t["d_v"], SWA_WINDOW, sinks=p["swa_sinks"])
    return _merge(proj, o_c, o_s, o_w, o_b, o_cd, o_d, p["w_up"].astype(BF16), p["w_out"].astype(BF16),
                  x, g1, cols["ga"][0])


def kernel(x, c, positions, w_ada, b_ada, norm1, norm2, w_in, nsa_qn, nsa_kn, nsa_pe_k, nsa_w1_k, nsa_w2_k,
           nsa_pe_v, nsa_w1_v, nsa_w2_v, dif_qn, dif_kn, dif_lq1, dif_lk1, dif_lq2, dif_lk2, dif_subln,
           swa_qn, swa_kn, swa_sinks, w_up, w_out, w_router, b_router, w_gu, b_gu, w_dn, b_dn):
    bn, sn, d = x.shape
    depth = w_ada.shape[0]
    c_pad = jnp.zeros((8, d), F32).at[:bn].set(c)
    tables = _rope_tables(positions, HEAD_DIM) + _rope_tables(positions, DIFF_DIM)
    for l in range(depth):
        lam_init = 0.8 - 0.6 * math.exp(-0.3 * l)
        mod = _linear(c_pad, w_ada[l], b_ada[l], tn=512, precision=lax.Precision.HIGHEST)[:bn]
        mod = mod.reshape(bn, 6, 1, d).transpose(1, 0, 2, 3)
        p = dict(norm1=norm1[l], w_in=w_in[l], nsa_qn=nsa_qn[l], nsa_kn=nsa_kn[l], nsa_pe_k=nsa_pe_k[l],
                 nsa_w1_k=nsa_w1_k[l], nsa_w2_k=nsa_w2_k[l], nsa_pe_v=nsa_pe_v[l], nsa_w1_v=nsa_w1_v[l],
                 nsa_w2_v=nsa_w2_v[l], dif_qn=dif_qn[l], dif_kn=dif_kn[l], dif_lq1=dif_lq1[l],
                 dif_lk1=dif_lk1[l], dif_lq2=dif_lq2[l], dif_lk2=dif_lk2[l], dif_subln=dif_subln[l],
                 swa_qn=swa_qn[l], swa_kn=swa_kn[l], swa_sinks=swa_sinks[l], w_up=w_up[l], w_out=w_out[l])
        x = _mixer_layer(x, positions, tables, mod, p, lam_init)
        x = _moe(x, norm2[l], mod[4], mod[3], mod[5], w_router[l], b_router[l], l, w_gu, b_gu, w_dn, b_dn)
    return x
```

```python
import functools
import math

import numpy as np
import jax
import jax.numpy as jnp
from jax import lax
from jax.experimental import pallas as pl
from jax.experimental.pallas import tpu as pltpu

F32 = jnp.float32
BF16 = jnp.bfloat16

HEAD_DIM = 64
ROPE_THETA = 10000.0
NORM_EPS = 1e-6
NEG_INF = -1e30
KNOCKOUT = -3e38
N_BRANCH = 4

NSA_HEADS = 4
NSA_CMP_LEN = 32
NSA_CMP_STRIDE = 16
NSA_SEL_BLOCK = 64
NSA_N_SEL = 16
NSA_WINDOW = 512
NSA_FORCED_SCORE = 1e4

SB_HEADS = 4
DIFF_HEADS = 4
DIFF_DIM = 32
SWA_HEADS = 4
SWA_KV_HEADS = 2
SWA_WINDOW = 128

N_EXPERTS = 32
TOP_K = 4
SWIGLU_ALPHA = 1.702
SWIGLU_LIMIT = 7.0

LANES = 128
LOG2E = 1.4426950408889634
VMEM_LIMIT = 56 * 1024 * 1024

FLASH_TILE = 1024
SB_KEY_TILE = 256
BAND_TILE = 1024
CMP_TILE = 1024
PROJ_TILE = (1024, 2560)
ROW_TILE = 512
MERGE_TILE = 512
MOE_ROWS = 512
MOE_SLICES = 4


def _cparams(*sem):
    return pltpu.CompilerParams(dimension_semantics=sem, vmem_limit_bytes=VMEM_LIMIT)


def _dot(a, b):
    return jnp.dot(a, b, preferred_element_type=F32)


def _dot_nt(a, b):
    return lax.dot_general(a, b, (((1,), (1,)), ((), ())), preferred_element_type=F32)


def _iota2(shape, dim):
    return lax.broadcasted_iota(jnp.int32, shape, dim)


def _linear_kernel(x_ref, w_ref, b_ref, o_ref, *, precision):
    o_ref[...] = jnp.dot(x_ref[...], w_ref[...], preferred_element_type=F32,
                         precision=precision) + b_ref[...]


def _linear(x, w, b, tn, precision=None):
    m, k = x.shape
    n = w.shape[1]
    return pl.pallas_call(
        functools.partial(_linear_kernel, precision=precision),
        grid=(n // tn,),
        in_specs=[pl.BlockSpec((m, k), lambda j: (0, 0)),
                  pl.BlockSpec((k, tn), lambda j: (0, j)),
                  pl.BlockSpec((1, tn), lambda j: (0, j))],
        out_specs=pl.BlockSpec((m, tn), lambda j: (0, j)),
        out_shape=jax.ShapeDtypeStruct((m, n), F32),
        compiler_params=_cparams("arbitrary"),
        name="linear",
    )(x, w, b.reshape(1, n))


def _cmp_mlp_kernel(a_ref, b_ref, pe_ref, w2_ref, o_ref):
    hid = jax.nn.gelu(a_ref[...] + b_ref[...] + pe_ref[...])
    o_ref[...] = _dot(hid.astype(BF16), w2_ref[...])


def _compress(t, pe, w1, w2):
    bn, sn, dh = t.shape
    st = NSA_CMP_STRIDE
    half = st * dh
    nb = sn // st
    t16 = t.reshape(bn * nb, half).astype(BF16)
    w1cat = jnp.concatenate([w1[:half], w1[half:]], axis=1).astype(BF16)
    hidden = w1.shape[1]
    ab = _linear(t16, w1cat, jnp.zeros((2 * hidden,), F32), tn=2 * hidden)
    ab = ab.reshape(bn, nb, 2 * hidden)
    a = ab[:, :, :hidden]
    b_next = jnp.concatenate([ab[:, 1:, hidden:], jnp.zeros((bn, 1, hidden), F32)], axis=1)
    pe_term = jnp.dot(pe.reshape(1, NSA_CMP_LEN * dh), w1, precision=lax.Precision.HIGHEST)
    rows = bn * nb
    tm = min(512, rows)
    out = pl.pallas_call(
        _cmp_mlp_kernel,
        grid=(rows // tm,),
        in_specs=[pl.BlockSpec((tm, hidden), lambda i: (i, 0)),
                  pl.BlockSpec((tm, hidden), lambda i: (i, 0)),
                  pl.BlockSpec((1, hidden), lambda i: (0, 0)),
                  pl.BlockSpec((hidden, dh), lambda i: (0, 0))],
        out_specs=pl.BlockSpec((tm, dh), lambda i: (i, 0)),
        out_shape=jax.ShapeDtypeStruct((rows, dh), F32),
        compiler_params=_cparams("parallel"),
        name="cmp_mlp",
    )(a.reshape(rows, hidden), b_next.reshape(rows, hidden), pe_term, w2.astype(BF16))
    return out.reshape(bn, nb, dh)


def _norm_mod(x, g, sc, sh):
    r = lax.rsqrt(jnp.mean(x * x, axis=-1, keepdims=True) + NORM_EPS)
    return (x * r * g) * (1.0 + sc) + sh


def _proj_kernel(x_ref, g_ref, sc_ref, sh_ref, w_ref, o_ref, h_scr):
    @pl.when(pl.program_id(2) == 0)
    def _():
        h_scr[...] = _norm_mod(x_ref[0], g_ref[...], sc_ref[0], sh_ref[0]).astype(BF16)

    o_ref[0] = _dot(h_scr[...], w_ref[...]).astype(o_ref.dtype)


def _in_projection(x, g, sc, sh, w, tm=PROJ_TILE[0], tn=PROJ_TILE[1]):
    bn, sn, d = x.shape
    n = w.shape[1]
    tm = min(tm, sn)
    return pl.pallas_call(
        _proj_kernel,
        grid=(bn, sn // tm, n // tn),
        in_specs=[pl.BlockSpec((1, tm, d), lambda b, i, j: (b, i, 0)),
                  pl.BlockSpec((1, d), lambda b, i, j: (0, 0)),
                  pl.BlockSpec((1, 1, d), lambda b, i, j: (b, 0, 0)),
                  pl.BlockSpec((1, 1, d), lambda b, i, j: (b, 0, 0)),
                  pl.BlockSpec((d, tn), lambda b, i, j: (0, j))],
        out_specs=pl.BlockSpec((1, tm, tn), lambda b, i, j: (b, i, j)),
        out_shape=jax.ShapeDtypeStruct((bn, sn, n), BF16),
        scratch_shapes=[pltpu.VMEM((tm, d), BF16)],
        compiler_params=_cparams("parallel", "parallel", "arbitrary"),
        name="in_proj",
    )(x, g.reshape(1, d), sc, sh, w)


def _banded_kernel(*refs, tile, window, has_sink, grp):
    if has_sink:
        sink_ref, q_ref, kp_ref, kc_ref, vp_ref, vc_ref, o_ref = refs
    else:
        q_ref, kp_ref, kc_ref, vp_ref, vc_ref, o_ref = refs
    i = pl.program_id(2)
    w = window
    rows = grp * w
    row = _iota2((rows, w), 0)
    upper = _iota2((rows, w), 1) > (row & (w - 1))
    first_bias = jnp.where(i > 0, 0.0, NEG_INF)
    if has_sink:
        head0 = pl.program_id(1) * grp
        sink = jnp.full((rows, 1), sink_ref[head0], F32)
        for hh in range(1, grp):
            sink = jnp.where(_iota2((rows, 1), 0) >= hh * w, sink_ref[head0 + hh], sink)
    for u in range(tile // w):
        q = jnp.concatenate([q_ref[0, hh, u * w:(u + 1) * w, :] for hh in range(grp)], axis=0)
        if u == 0:
            k_prev, v_prev = kp_ref[0, 0], vp_ref[0, 0]
        else:
            k_prev, v_prev = kc_ref[0, 0, (u - 1) * w:u * w, :], vc_ref[0, 0, (u - 1) * w:u * w, :]
        k_cur, v_cur = kc_ref[0, 0, u * w:(u + 1) * w, :], vc_ref[0, 0, u * w:(u + 1) * w, :]
        s_prev = _dot_nt(q, k_prev)
        if u == 0:
            s_prev = s_prev + first_bias
        s = jnp.where(upper, s_prev, _dot_nt(q, k_cur))
        m = jnp.max(s, axis=-1, keepdims=True)
        if has_sink:
            m = jnp.maximum(m, sink)
        p = jnp.exp(s - m)
        den = jnp.sum(p, axis=-1, keepdims=True)
        if has_sink:
            den = den + jnp.exp(sink - m)
        else:
            den = jnp.maximum(den, 1e-30)
        o = (_dot(jnp.where(upper, p, 0.0).astype(BF16), v_prev)
             + _dot(jnp.where(upper, 0.0, p).astype(BF16), v_cur)) / den
        for hh in range(grp):
            o_ref[0, hh, u * w:(u + 1) * w, :] = o[hh * w:(hh + 1) * w]


def _banded_attention(q, k, v, window, sinks=None, tile=BAND_TILE):
    bn, hq, sn, d = q.shape
    hkv = k.shape[1]
    grp = hq // hkv
    tile = min(tile, sn)
    assert tile % window == 0 and sn % tile == 0 and window & (window - 1) == 0
    per = tile // window
    has_sink = sinks is not None
    qspec = pl.BlockSpec((1, grp, tile, d), lambda b, g, i: (b, g, i, 0))
    prev = pl.BlockSpec((1, 1, window, d), lambda b, g, i: (b, g, jnp.maximum(i * per - 1, 0), 0))
    cur = pl.BlockSpec((1, 1, tile, d), lambda b, g, i: (b, g, i, 0))
    in_specs = [qspec, prev, cur, prev, cur]
    args = [q, k, k, v, v]
    if has_sink:
        in_specs = [pl.BlockSpec(memory_space=pltpu.SMEM)] + in_specs
        args = [sinks.astype(F32)] + args
    return pl.pallas_call(
        functools.partial(_banded_kernel, tile=tile, window=window, has_sink=has_sink, grp=grp),
        grid=(bn, hkv, sn // tile),
        in_specs=in_specs,
        out_specs=pl.BlockSpec((1, grp, tile, d), lambda b, g, i: (b, g, i, 0)),
        out_shape=jax.ShapeDtypeStruct((bn, hq, sn, d), F32),
        compiler_params=_cparams("parallel", "parallel", "parallel"),
        name="banded_attn",
    )(*args)


def _nsa_cmp_kernel(q_ref, kc_ref, vct_ref, ovt_ref, oct_ref, bias_ref, *, tq, n_cmp, n_top, heads):
    i = pl.program_id(1)
    ncp = kc_ref.shape[1]
    nsel = ovt_ref.shape[0]
    dh = q_ref.shape[3]
    t = i * tq + _iota2((ncp, tq), 1)
    n = _iota2((ncp, tq), 0)
    valid = (n * NSA_CMP_STRIDE + (NSA_CMP_LEN - 1) <= t) & (n < n_cmp)
    kc = kc_ref[0]
    vct = vct_ref[0]
    psum = jnp.zeros((ncp, tq), F32)
    for h in range(heads):
        s = jnp.where(valid, _dot_nt(kc, q_ref[0, h]), NEG_INF)
        m = jnp.max(s, axis=0, keepdims=True)
        p = jnp.where(valid, jnp.exp(s - m), 0.0)
        p = p / jnp.maximum(jnp.sum(p, axis=0, keepdims=True), 1e-30)
        oct_ref[0, h * dh:(h + 1) * dh, :] = _dot(vct, p.astype(BF16))
        psum = psum + p
    hi = psum.astype(BF16)
    lo = (psum - hi.astype(F32)).astype(BF16)
    imp = _dot(ovt_ref[...], hi) + _dot(ovt_ref[...], lo)

    tt = i * tq + _iota2((nsel, tq), 1)
    blk = _iota2((nsel, tq), 0)
    cur = tt >> (NSA_SEL_BLOCK.bit_length() - 1)
    forced = (blk == 0) | (blk == cur) | (blk == cur - 1)
    valid_s = blk * NSA_SEL_BLOCK <= tt
    score = jnp.where(forced, NSA_FORCED_SCORE, jnp.where(valid_s, imp, -1.0))
    blk_f = blk.astype(F32)

    def pick(_, carry):
        score, sel = carry
        m = jnp.max(score, axis=0, keepdims=True)
        first = jnp.min(jnp.where(score == m, blk_f, float(nsel)), axis=0, keepdims=True)
        hit = blk_f == first
        return jnp.where(hit, KNOCKOUT, score), jnp.where(hit, 0.0, sel)

    _, bias = lax.fori_loop(0, n_top, pick, (score, jnp.full((nsel, tq), NEG_INF, F32)))
    bias_ref[0] = bias.astype(BF16)


def _nsa_cmp(q, kc, vc, overlap, n_cmp, n_top, tq=CMP_TILE):
    bn, heads, sn, dh = q.shape
    tq = min(tq, sn)
    ncp = kc.shape[1]
    nsel = overlap.shape[1]
    oct, bias_t = pl.pallas_call(
        functools.partial(_nsa_cmp_kernel, tq=tq, n_cmp=n_cmp, n_top=n_top, heads=heads),
        grid=(bn, sn // tq),
        in_specs=[pl.BlockSpec((1, heads, tq, dh), lambda b, i: (b, 0, i, 0)),
                  pl.BlockSpec((1, ncp, dh), lambda b, i: (b, 0, 0)),
                  pl.BlockSpec((1, dh, ncp), lambda b, i: (b, 0, 0)),
                  pl.BlockSpec((nsel, ncp), lambda b, i: (0, 0))],
        out_specs=[pl.BlockSpec((1, heads * dh, tq), lambda b, i: (b, 0, i)),
                   pl.BlockSpec((1, nsel, tq), lambda b, i: (b, 0, i))],
        out_shape=[jax.ShapeDtypeStruct((bn, heads * dh, sn), F32),
                   jax.ShapeDtypeStruct((bn, nsel, sn), BF16)],
        compiler_params=_cparams("parallel", "parallel"),
        name="nsa_cmp_topk",
    )(q, kc, vc.transpose(0, 2, 1), overlap.T)
    return oct.transpose(0, 2, 1), bias_t.transpose(0, 2, 1)


def _nsa_sel_kernel(q_ref, kx_ref, v_ref, bias_ref, o_ref, qx_scr, *, tq, tk, heads):
    i = pl.program_id(1)
    dh = v_ref.shape[2]
    bias = bias_ref[0]
    for h in range(heads):
        qx_scr[h] = jnp.concatenate([bias, q_ref[0, h]], axis=1)

    def tile(j, state, diagonal):
        start = pl.multiple_of(j * tk, tk)
        kx = kx_ref[0, pl.ds(start, tk), :]
        v = v_ref[0, pl.ds(start, tk), :]
        if diagonal:
            causal = (start + _iota2((tq, tk), 1)) <= (i * tq + _iota2((tq, tk), 0))
        new_state = []
        for h in range(heads):
            m_old, l_old, acc_old = state[h]
            s = _dot_nt(qx_scr[h], kx)
            if diagonal:
                s = jnp.where(causal, s, NEG_INF)
            m_new = jnp.maximum(m_old, jnp.max(s, axis=-1, keepdims=True))
            alpha = jnp.exp(m_old - m_new)
            p = jnp.exp(s - m_new)
            l_new = alpha * l_old + jnp.sum(p, axis=-1, keepdims=True)
            acc_new = alpha * acc_old + _dot(p.astype(BF16), v)
            new_state.append((m_new, l_new, acc_new))
        return tuple(new_state)

    init = tuple((jnp.full((tq, 1), NEG_INF, F32), jnp.zeros((tq, 1), F32), jnp.zeros((tq, dh), F32))
                 for _ in range(heads))
    n_full = (i * tq) // tk
    state = lax.fori_loop(0, n_full, lambda j, st: tile(j, st, False), init)
    state = tile(n_full, state, True)
    for h in range(heads):
        o_ref[0, :, h * dh:(h + 1) * dh] = state[h][2] / jnp.maximum(state[h][1], 1e-30)


def _nsa_sel(q, kx, v, bias, tq=FLASH_TILE, tk=FLASH_TILE):
    bn, heads, sn, dh = q.shape
    nsel = bias.shape[2]
    tq, tk = min(tq, sn), min(tk, sn)
    assert tk % tq == 0 and sn % tk == 0
    return pl.pallas_call(
        functools.partial(_nsa_sel_kernel, tq=tq, tk=tk, heads=heads),
        grid=(bn, sn // tq),
        in_specs=[pl.BlockSpec((1, heads, tq, dh), lambda b, i: (b, 0, i, 0)),
                  pl.BlockSpec((1, sn, nsel + dh), lambda b, i: (b, 0, 0)),
                  pl.BlockSpec((1, sn, dh), lambda b, i: (b, 0, 0)),
                  pl.BlockSpec((1, tq, nsel), lambda b, i: (b, i, 0))],
        out_specs=pl.BlockSpec((1, tq, heads * dh), lambda b, i: (b, i, 0)),
        out_shape=jax.ShapeDtypeStruct((bn, sn, heads * dh), F32),
        scratch_shapes=[pltpu.VMEM((heads, tq, nsel + dh), BF16)],
        compiler_params=_cparams("parallel", "arbitrary"),
        name="nsa_selected_attn",
    )(q, kx, v, bias)


def _sb_kernel(q_ref, k_ref, v_ref, u_ref, o_ref, *, tq, tk):
    i = pl.program_id(2)
    q = q_ref[0, 0]
    per_q = tq // tk

    def tile(jj, carry, first_row):
        diagonal = first_row is not None
        r0 = first_row if diagonal else 0
        start = pl.multiple_of(jj * tk, tk)
        nz = _dot_nt(q[r0:], k_ref[0, 0, pl.ds(start, tk), :])
        neg_abs = lax.bitcast_convert_type(lax.bitcast_convert_type(nz, jnp.uint32) | jnp.uint32(0x80000000), F32)
        log_keep = jnp.minimum(nz, 0.0) - jnp.log2(1.0 + jnp.exp2(neg_abs))
        if diagonal:
            strict = _iota2(nz.shape, 1) < _iota2(nz.shape, 0)
            log_keep = jnp.where(strict, log_keep, 0.0)
        cum = _dot(log_keep.astype(BF16), u_ref[...])
        a = jnp.exp2(cum + jnp.concatenate([carry[r0:]] * (tk // LANES), axis=1) - nz)
        if diagonal:
            a = jnp.where(strict, a, 0.0)
        out = _dot(a.astype(BF16), v_ref[0, 0, pl.ds(start, tk), :])
        new_carry = carry[r0:] + jnp.broadcast_to(cum[:, 0:1], (tq - r0, LANES))
        if r0:
            out = jnp.concatenate([jnp.zeros((r0, out.shape[1]), F32), out], axis=0)
            new_carry = jnp.concatenate([carry[:r0], new_carry], axis=0)
        return out, new_carry

    def group(first, carry, diagonal):
        total = None
        for r in range(per_q):
            out, carry = tile(first - r, carry, (per_q - 1 - r) * tk if diagonal else None)
            total = out if total is None else total + out
        return total, carry

    acc, carry = group(i * per_q + per_q - 1, jnp.zeros((tq, LANES), F32), True)

    def body(p, state):
        acc, carry = state
        out, carry = group((i - p) * per_q - 1, carry, False)
        return acc + out, carry

    acc, _ = lax.fori_loop(0, i, body, (acc, carry))
    o_ref[0, 0] = acc


def _stick_breaking(q, k, v, tq=FLASH_TILE, tk=SB_KEY_TILE):
    bn, heads, sn, dh = q.shape
    tq = min(tq, sn)
    incl = (np.arange(tk)[:, None] >= np.arange(tk)[None, :]).astype(np.float32)
    u = jnp.asarray(incl, BF16)
    kv = pl.BlockSpec((1, 1, sn, dh), lambda b, h, i: (b, h, 0, 0))
    return pl.pallas_call(
        functools.partial(_sb_kernel, tq=tq, tk=tk),
        grid=(bn, heads, sn // tq),
        in_specs=[pl.BlockSpec((1, 1, tq, dh), lambda b, h, i: (b, h, i, 0)), kv, kv,
                  pl.BlockSpec((tk, tk), lambda b, h, i: (0, 0))],
        out_specs=pl.BlockSpec((1, 1, tq, dh), lambda b, h, i: (b, h, i, 0)),
        out_shape=jax.ShapeDtypeStruct((bn, heads, sn, dh), F32),
        compiler_params=_cparams("parallel", "parallel", "arbitrary"),
        name="stick_breaking_attn",
    )(q, k, v, u)


def _diff_kernel(sc_ref, q_ref, k_ref, v_ref, g_ref, o_ref, *, tq, tk):
    i = pl.program_id(2)
    dv = v_ref.shape[3]

    def tile(j, state, diagonal):
        start = pl.multiple_of(j * tk, tk)
        v = jnp.concatenate([v_ref[0, 0, pl.ds(start, tk), :], jnp.ones((tk, dv), BF16)], axis=1)
        if diagonal:
            causal = (start + _iota2((tq, tk), 1)) <= (i * tq + _iota2((tq, tk), 0))
        new_state = []
        for mi in range(2):
            m_old, l_old, acc_old = state[mi]
            s = _dot_nt(q_ref[0, mi], k_ref[0, mi, pl.ds(start, tk), :])
            if diagonal:
                s = jnp.where(causal, s, NEG_INF)
            m_new = jnp.maximum(m_old, jnp.max(s, axis=-1, keepdims=True))
            alpha = jnp.exp(m_old - m_new)
            p = jnp.exp(s - m_new)
            pv = _dot(p.astype(BF16), v)
            l_new = alpha * l_old + pv[:, dv:dv + 1]
            acc_new = alpha * acc_old + pv[:, :dv]
            new_state.append((m_new, l_new, acc_new))
        return tuple(new_state)

    init = tuple((jnp.full((tq, 1), NEG_INF, F32), jnp.zeros((tq, 1), F32), jnp.zeros((tq, dv), F32))
                 for _ in range(2))
    n_full = (i * tq) // tk
    state = lax.fori_loop(0, n_full, lambda j, st: tile(j, st, False), init)
    state = tile(n_full, state, True)
    lam = sc_ref[0]
    post = sc_ref[1]
    o = (state[0][2] / jnp.maximum(state[0][1], 1e-30)
         - lam * (state[1][2] / jnp.maximum(state[1][1], 1e-30)))
    r = lax.rsqrt(jnp.mean(o * o, axis=-1, keepdims=True) + NORM_EPS)
    o_ref[0, 0] = (o * r * g_ref[...]) * post


def _diff_attention(q, k, v, subln, lam, post, tq=FLASH_TILE, tk=FLASH_TILE):
    bn, h2, sn, dd = q.shape
    heads = h2 // 2
    dv = v.shape[3]
    tq, tk = min(tq, sn), min(tk, sn)
    assert tk % tq == 0 and sn % tk == 0
    scal = jnp.stack([lam, post]).astype(F32)
    return pl.pallas_call(
        functools.partial(_diff_kernel, tq=tq, tk=tk),
        grid=(bn, heads, sn // tq),
        in_specs=[pl.BlockSpec(memory_space=pltpu.SMEM),
                  pl.BlockSpec((1, 2, tq, dd), lambda b, h, i: (b, h, i, 0)),
                  pl.BlockSpec((1, 2, sn, dd), lambda b, h, i: (b, h, 0, 0)),
                  pl.BlockSpec((1, 1, sn, dv), lambda b, h, i: (b, h, 0, 0)),
                  pl.BlockSpec((1, dv), lambda b, h, i: (0, 0))],
        out_specs=pl.BlockSpec((1, 1, tq, dv), lambda b, h, i: (b, h, i, 0)),
        out_shape=jax.ShapeDtypeStruct((bn, heads, sn, dv), F32),
        compiler_params=_cparams("parallel", "parallel", "arbitrary"),
        name="diff_attn",
    )(scal, q, k, v, subln.reshape(1, dv).astype(F32))


def _merge_kernel(brg_ref, ga0_ref, ga1_ref, ga2_ref, oc_ref, os_ref, ow_ref, ob_ref, ocd_ref, od_ref,
                  wup_ref, wout_ref, x_ref, g1_ref, o_ref, *, d):
    def heads(ref):
        return jnp.concatenate([ref[0, h] for h in range(ref.shape[1])], axis=1)

    def gate(x):
        return jax.nn.sigmoid(x.astype(F32))

    o_a = gate(ga0_ref[0]) * oc_ref[0] + gate(ga1_ref[0]) * os_ref[0] + gate(ga2_ref[0]) * heads(ow_ref)
    branches = (o_a, heads(ob_ref), heads(ocd_ref), heads(od_ref))
    merged = None
    for bi, o in enumerate(branches):
        term = gate(brg_ref[0, :, bi * d:(bi + 1) * d]) * _dot(o.astype(BF16), wup_ref[bi])
        merged = term if merged is None else merged + term
    y = _dot(merged.astype(BF16), wout_ref[...])
    o_ref[0] = x_ref[0] + g1_ref[0] * y


def _merge(proj, o_c, o_s, o_w, o_b, o_cd, o_d, w_up, w_out, x, g1, ga_col, tm=MERGE_TILE):
    bn, sn, d = x.shape
    bw = o_c.shape[2]
    nh, dh = o_b.shape[1], o_b.shape[3]
    assert ga_col % bw == 0
    gblk = ga_col // bw
    row = lambda b, i: (b, i, 0)
    bspec = pl.BlockSpec((1, tm, bw), row)
    hspec = pl.BlockSpec((1, nh, tm, dh), lambda b, i: (b, 0, i, 0))
    return pl.pallas_call(
        functools.partial(_merge_kernel, d=d),
        grid=(bn, sn // tm),
        in_specs=[pl.BlockSpec((1, tm, N_BRANCH * d), row),
                  pl.BlockSpec((1, tm, bw), lambda b, i: (b, i, gblk)),
                  pl.BlockSpec((1, tm, bw), lambda b, i: (b, i, gblk + 1)),
                  pl.BlockSpec((1, tm, bw), lambda b, i: (b, i, gblk + 2)),
                  bspec, bspec, hspec, hspec, hspec, hspec,
                  pl.BlockSpec((N_BRANCH, bw, d), lambda b, i: (0, 0, 0)),
                  pl.BlockSpec((d, d), lambda b, i: (0, 0)),
                  pl.BlockSpec((1, tm, d), row),
                  pl.BlockSpec((1, 1, d), lambda b, i: (b, 0, 0))],
        out_specs=pl.BlockSpec((1, tm, d), row),
        out_shape=jax.ShapeDtypeStruct((bn, sn, d), F32),
        compiler_params=_cparams("parallel", "parallel"),
        name="branch_merge",
    )(proj, proj, proj, proj, o_c, o_s, o_w, o_b, o_cd, o_d, w_up, w_out, x, g1)


def _router_kernel(x_ref, g_ref, sc_ref, sh_ref, wr_ref, br_ref, tri_ref, h_ref, e_ref, w_ref, rank_ref, cnt_ref,
                   run_scr):
    @pl.when((pl.program_id(0) == 0) & (pl.program_id(1) == 0))
    def _():
        run_scr[...] = jnp.zeros(run_scr.shape, F32)

    h = _norm_mod(x_ref[0], g_ref[...], sc_ref[0], sh_ref[0])
    h_ref[0] = h.astype(BF16)
    logits = jnp.dot(h, wr_ref[...], preferred_element_type=F32,
                     precision=lax.Precision.HIGHEST) + br_ref[...]
    lane = _iota2(logits.shape, 1)
    lane_f = lane.astype(F32)
    cur = logits
    vals, idxs = [], []
    chosen = jnp.zeros(logits.shape, F32)
    for _ in range(TOP_K):
        m = jnp.max(cur, axis=-1, keepdims=True)
        first = jnp.min(jnp.where(cur == m, lane_f, float(LANES)), axis=-1, keepdims=True)
        vals.append(m)
        idxs.append(first)
        hit = lane_f == first
        cur = jnp.where(hit, KNOCKOUT, cur)
        chosen = jnp.where(hit, 1.0, chosen)
    exps = [jnp.exp(v - vals[0]) for v in vals]
    den = exps[0]
    for e in exps[1:]:
        den = den + e
    earlier = _dot(tri_ref[...], chosen.astype(BF16)) + run_scr[0:1, :]
    e_out = jnp.zeros(logits.shape, F32)
    w_out = jnp.zeros(logits.shape, F32)
    r_out = jnp.zeros(logits.shape, F32)
    for k in range(TOP_K):
        rank_k = jnp.sum(jnp.where(lane_f == idxs[k], earlier, 0.0), axis=-1, keepdims=True)
        e_out = jnp.where(lane == k, idxs[k], e_out)
        w_out = jnp.where(lane == k, exps[k] / den, w_out)
        r_out = jnp.where(lane == k, rank_k, r_out)
    e_ref[0] = e_out[:, :TOP_K].astype(jnp.int32)
    w_ref[0] = w_out[:, :TOP_K]
    rank_ref[0] = r_out[:, :TOP_K].astype(jnp.int32)
    total = run_scr[...] + jnp.sum(chosen, axis=0, keepdims=True)
    run_scr[...] = total
    cnt_ref[...] = total


def _router(x, g, sc, sh, w_router, b_router, tm=ROW_TILE):
    bn, sn, d = x.shape
    ne = w_router.shape[1]
    wr = jnp.zeros((d, LANES), F32).at[:, :ne].set(w_router)
    br = jnp.full((1, LANES), NEG_INF, F32).at[0, :ne].set(b_router)
    tri = jnp.asarray((np.arange(tm)[:, None] > np.arange(tm)[None, :]).astype(np.float32), BF16)
    row = lambda b, i: (b, i, 0)
    return pl.pallas_call(
        _router_kernel,
        grid=(bn, sn // tm),
        in_specs=[pl.BlockSpec((1, tm, d), row),
                  pl.BlockSpec((1, d), lambda b, i: (0, 0)),
                  pl.BlockSpec((1, 1, d), lambda b, i: (b, 0, 0)),
                  pl.BlockSpec((1, 1, d), lambda b, i: (b, 0, 0)),
                  pl.BlockSpec((d, LANES), lambda b, i: (0, 0)),
                  pl.BlockSpec((1, LANES), lambda b, i: (0, 0)),
                  pl.BlockSpec((tm, tm), lambda b, i: (0, 0))],
        out_specs=[pl.BlockSpec((1, tm, d), row),
                   pl.BlockSpec((1, tm, TOP_K), row),
                   pl.BlockSpec((1, tm, TOP_K), row),
                   pl.BlockSpec((1, tm, TOP_K), row),
                   pl.BlockSpec((8, LANES), lambda b, i: (0, 0))],
        out_shape=[jax.ShapeDtypeStruct((bn, sn, d), BF16),
                   jax.ShapeDtypeStruct((bn, sn, TOP_K), jnp.int32),
                   jax.ShapeDtypeStruct((bn, sn, TOP_K), F32),
                   jax.ShapeDtypeStruct((bn, sn, TOP_K), jnp.int32),
                   jax.ShapeDtypeStruct((8, LANES), F32)],
        scratch_shapes=[pltpu.VMEM((8, LANES), F32)],
        compiler_params=_cparams("arbitrary", "arbitrary"),
        name="moe_router",
    )(x, g.reshape(1, d), sc, sh, wr, br, tri)


def _expert_kernel(ce_ref, x_ref, wgu_ref, bgu_ref, wdn_ref, bdn_ref, rw_ref, *rest, ff, fc, off):
    o_ref, wgu_scr, wdn_scr = rest[-3:]
    c = pl.program_id(0)

    @pl.when((c == 0) | (ce_ref[c + off] != ce_ref[jnp.maximum(c + off - 1, 0)]))
    def _():
        wgu_scr[...] = wgu_ref[0, 0].astype(BF16)
        wdn_scr[...] = wdn_ref[0, 0].astype(BF16)

    x = x_ref[...]
    y = None
    for j in range(ff // fc):
        g = _dot(x, wgu_scr[:, j * fc:(j + 1) * fc]) + bgu_ref[0, 0, :, j * fc:(j + 1) * fc]
        u = _dot(x, wgu_scr[:, ff + j * fc:ff + (j + 1) * fc]) + bgu_ref[0, 0, :, ff + j * fc:ff + (j + 1) * fc]
        g = jnp.minimum(g, SWIGLU_LIMIT)
        u = jnp.clip(u, -SWIGLU_LIMIT, SWIGLU_LIMIT)
        act = g * jax.nn.sigmoid(SWIGLU_ALPHA * g) * (u + 1.0)
        part = _dot(act.astype(BF16), wdn_scr[j * fc:(j + 1) * fc, :])
        y = part if y is None else y + part
    o_ref[...] = ((y + bdn_ref[0, 0]) * rw_ref[...]).astype(o_ref.dtype)


def _expert_ffn(h, row_tok, chunk_e, layer, w_gu, b_gu, w_dn, b_dn, row_w, tm=MOE_ROWS, fc=512, groups=MOE_SLICES):
    n_rows = row_tok.shape[0]
    d = h.shape[1]
    nl, ne, _, ff2 = w_gu.shape
    ff = ff2 // 2
    n_chunks = n_rows // tm
    assert n_chunks % groups == 0
    per = n_chunks // groups
    b_gu3, b_dn3, row_w2 = b_gu.reshape(nl, ne, 1, ff2), b_dn.reshape(nl, ne, 1, d), row_w.reshape(n_rows, 1)
    y = None
    for gi in range(groups):
        off = gi * per
        rows = h[row_tok[off * tm:(off + per) * tm]]
        in_specs = [pl.BlockSpec((tm, d), lambda c, ce: (c, 0)),
                    pl.BlockSpec((1, 1, d, ff2), lambda c, ce, off=off: (layer, ce[c + off], 0, 0)),
                    pl.BlockSpec((1, 1, 1, ff2), lambda c, ce, off=off: (layer, ce[c + off], 0, 0)),
                    pl.BlockSpec((1, 1, ff, d), lambda c, ce, off=off: (layer, ce[c + off], 0, 0)),
                    pl.BlockSpec((1, 1, 1, d), lambda c, ce, off=off: (layer, ce[c + off], 0, 0)),
                    pl.BlockSpec((tm, 1), lambda c, ce, off=off: (c + off, 0))]
        args = [chunk_e, rows, w_gu, b_gu3, w_dn, b_dn3, row_w2]
        aliases = {}
        if y is not None:
            in_specs.append(pl.BlockSpec(memory_space=pl.ANY))
            args.append(y)
            aliases = {len(args) - 1: 0}
        y = pl.pallas_call(
            functools.partial(_expert_kernel, ff=ff, fc=fc, off=off),
            grid_spec=pltpu.PrefetchScalarGridSpec(
                num_scalar_prefetch=1, grid=(per,), in_specs=in_specs,
                out_specs=pl.BlockSpec((tm, d), lambda c, ce, off=off: (c + off, 0)),
                scratch_shapes=[pltpu.VMEM((d, ff2), BF16), pltpu.VMEM((ff, d), BF16)]),
            out_shape=jax.ShapeDtypeStruct((n_rows, d), BF16),
            input_output_aliases=aliases,
            compiler_params=_cparams("arbitrary"),
            name="moe_expert_ffn",
        )(*args)
    return y


def _combine_kernel(y_ref, x_ref, g2_ref, o_ref):
    tot = y_ref[0].astype(F32)
    for k in range(1, TOP_K):
        tot = tot + y_ref[k].astype(F32)
    o_ref[0] = x_ref[0] + g2_ref[0] * tot


def _combine(y4, x, g2, tm=ROW_TILE):
    bn, sn, d = x.shape
    row = lambda b, i: (b, i, 0)
    nt = sn // tm
    return pl.pallas_call(
        _combine_kernel,
        grid=(bn, nt),
        in_specs=[pl.BlockSpec((TOP_K, tm, d), lambda b, i: (0, b * nt + i, 0)),
                  pl.BlockSpec((1, tm, d), row),
                  pl.BlockSpec((1, 1, d), lambda b, i: (b, 0, 0))],
        out_specs=pl.BlockSpec((1, tm, d), row),
        out_shape=jax.ShapeDtypeStruct((bn, sn, d), F32),
        compiler_params=_cparams("parallel", "parallel"),
        name="moe_combine",
    )(y4, x, g2)


def _moe(x, g, sc, sh, g2, w_router, b_router, layer, w_gu, b_gu, w_dn, b_dn):
    bn, sn, d = x.shape
    n_tok = bn * sn
    n_asg = n_tok * TOP_K
    tm = MOE_ROWS
    h, e_out, w_out, rank_out, totals = _router(x, g, sc, sh, w_router, b_router)
    e_tok = e_out.reshape(n_tok, TOP_K)
    w_flat = w_out.reshape(-1)
    counts = totals[0, :N_EXPERTS].astype(jnp.int32)
    starts = jnp.cumsum(counts) - counts
    padded = (counts + tm - 1) // tm * tm
    pad_ends = jnp.cumsum(padded)
    pad_starts = pad_ends - padded
    pos = pad_starts[e_tok] + rank_out.reshape(n_tok, TOP_K)
    n_chunks = n_asg // tm + N_EXPERTS
    chunk_start = jnp.arange(n_chunks, dtype=jnp.int32) * tm
    chunk_e = jnp.minimum(jnp.sum(chunk_start[:, None] >= pad_ends[None, :], axis=1), N_EXPERTS - 1).astype(jnp.int32)
    order = jnp.argsort(e_tok.reshape(-1))
    src = (chunk_start - (pad_starts - starts)[chunk_e])[:, None] + jnp.arange(tm, dtype=jnp.int32)[None, :]
    row_valid = (src < (starts + counts)[chunk_e][:, None]).reshape(-1)
    asg = order[jnp.clip(src, 0, n_asg - 1).reshape(-1)].astype(jnp.int32)
    row_tok = asg // TOP_K
    row_w = jnp.where(row_valid, w_flat[asg], 0.0)
    y = _expert_ffn(h.reshape(n_tok, d), row_tok, chunk_e, layer, w_gu, b_gu, w_dn, b_dn, row_w)
    return _combine(y[pos.T], x, g2)


def _norm_rope(x, g, cos, sin, bd, hd):
    sq = x * x
    hi = sq.astype(BF16)
    lo = (sq - hi.astype(F32)).astype(BF16)
    ss = _dot(hi, bd) + _dot(lo, bd)
    y = x * lax.rsqrt(ss * (1.0 / hd) + NORM_EPS) * g
    half = hd // 2
    first = (_iota2(x.shape, 1) & (hd - 1)) < half
    partner = jnp.where(first, pltpu.roll(y, LANES - half, 1), pltpu.roll(y, half, 1))
    return y * cos + partner * sin


def _prep_kernel(aq_ref, bq_ref, bk_ref, bv_ref, cq_ref, ck_ref, cv_ref, dq_ref, akc_ref, aks_ref, akw_ref,
                 dk_ref, dv_ref, c64_ref, s64_ref, c32_ref, s32_ref, gaq_ref, gak_ref, gcq_ref, gck_ref,
                 gdq_ref, gdk_ref, bd64_ref, bd32_ref,
                 oaq_ref, okc_ref, ovc_ref, okx_ref, ovs_ref, okw_ref, ovw_ref, obq_ref, obk_ref, obv_ref,
                 ocq_ref, ock_ref, ocv_ref, odq_ref, odk_ref, odv_ref, *, ts, sb_scale):
    hd, dd = HEAD_DIM, DIFF_DIM
    c64, s64, c32, s32 = c64_ref[0], s64_ref[0], c32_ref[0], s32_ref[0]
    bd64, bd32 = bd64_ref[...], bd32_ref[...]

    def slabs(ref):
        x = ref[0].astype(F32)
        return [x[:, c * LANES:(c + 1) * LANES] for c in range(x.shape[1] // LANES)]

    def put_heads(o_ref, c, y, width):
        per = LANES // width
        for u in range(per):
            o_ref[0, c * per + u] = y[:, u * width:(u + 1) * width].astype(o_ref.dtype)

    for src, gain, dst in ((aq_ref, gaq_ref, oaq_ref), (dq_ref, gdq_ref, odq_ref), (dk_ref, gdk_ref, odk_ref)):
        for c, x in enumerate(slabs(src)):
            put_heads(dst, c, _norm_rope(x, gain[:, c * LANES:(c + 1) * LANES], c64, s64, bd64, hd), hd)
    kc = _norm_rope(akc_ref[0].astype(F32), gak_ref[0:1, :], c64, s64, bd64, hd)
    ks = _norm_rope(aks_ref[0].astype(F32), gak_ref[1:2, :], c64, s64, bd64, hd)
    kw = _norm_rope(akw_ref[0].astype(F32), gak_ref[2:3, :], c64, s64, bd64, hd)
    okc_ref[0] = kc[:, :hd].astype(BF16)
    ovc_ref[0] = akc_ref[0][:, hd:].astype(BF16)
    nsel = okx_ref.shape[2] - hd
    blk = (pl.program_id(1) * ts + _iota2((ts, nsel), 0)) >> (NSA_SEL_BLOCK.bit_length() - 1)
    okx_ref[0, :, :nsel] = jnp.where(blk == _iota2((ts, nsel), 1), 1.0, 0.0).astype(BF16)
    okx_ref[0, :, nsel:] = ks[:, :hd].astype(BF16)
    ovs_ref[0] = aks_ref[0][:, hd:].astype(BF16)
    okw_ref[0, 0] = kw[:, :hd].astype(BF16)
    ovw_ref[0, 0] = akw_ref[0][:, hd:].astype(BF16)
    for c, x in enumerate(slabs(bq_ref)):
        put_heads(obq_ref, c, x * sb_scale, hd)
    for src, dst in ((bk_ref, obk_ref), (bv_ref, obv_ref), (cv_ref, ocv_ref)):
        for c, x in enumerate(slabs(src)):
            put_heads(dst, c, x, hd)
    put_heads(odv_ref, 0, dv_ref[0].astype(F32), hd)
    for src, gain, dst in ((cq_ref, gcq_ref, ocq_ref), (ck_ref, gck_ref, ock_ref)):
        for c, x in enumerate(slabs(src)):
            put_heads(dst, c, _norm_rope(x, gain[:, c * LANES:(c + 1) * LANES], c32, s32, bd32, dd), dd)


def _rope_tables(positions, hd):
    half = hd // 2
    inv = ROPE_THETA ** (-jnp.arange(half, dtype=F32) * 2.0 / hd)
    ang = positions.astype(F32)[..., None] * inv
    cos, sin = jnp.cos(ang), jnp.sin(ang)
    reps = LANES // hd
    return (jnp.tile(jnp.concatenate([cos, cos], axis=-1), (1, 1, reps)),
            jnp.tile(jnp.concatenate([-sin, sin], axis=-1), (1, 1, reps)))


def _prep(proj, cols, tables, p, n_sel, ts=ROW_TILE):
    bn, sn, _ = proj.shape
    hd, dd = HEAD_DIM, DIFF_DIM
    scale = hd ** -0.5
    c64, s64, c32, s32 = tables

    def cspec(name):
        off, width = cols[name]
        assert off % width == 0
        return pl.BlockSpec((1, ts, width), lambda b, i, blk=off // width: (b, i, blk))

    def tile_gain(g, reps, mult=1.0):
        return (jnp.tile(g.astype(F32), reps) * mult).reshape(1, -1)

    ones = jnp.ones((hd,), F32)
    gak = jnp.stack([jnp.concatenate([p["nsa_kn"][j].astype(F32), ones]) for j in range(3)])
    gains = [tile_gain(p["nsa_qn"], NSA_HEADS, scale), gak,
             tile_gain(p["dif_qn"], 2 * DIFF_HEADS, dd ** -0.5), tile_gain(p["dif_kn"], 2 * DIFF_HEADS),
             tile_gain(p["swa_qn"], SWA_HEADS, scale), tile_gain(p["swa_kn"], SWA_KV_HEADS)]
    lane = np.arange(LANES)
    bd64 = jnp.asarray((lane[:, None] // hd == lane[None, :] // hd).astype(np.float32), BF16)
    bd32 = jnp.asarray((lane[:, None] // dd == lane[None, :] // dd).astype(np.float32), BF16)
    names = ("a_q", "b_q", "b_k", "b_v", "c_q", "c_k", "c_v", "d_q", "a_kcvc", "a_ksvs", "a_kwvw", "d_k", "d_v")
    tab = pl.BlockSpec((1, ts, LANES), lambda b, i: (b, i, 0))
    full = lambda a: pl.BlockSpec(a.shape, lambda b, i: (0,) * a.ndim)

    def hm(nh, w):
        return (jax.ShapeDtypeStruct((bn, nh, sn, w), BF16), pl.BlockSpec((1, nh, ts, w), lambda b, i: (b, 0, i, 0)))

    def tm_(w):
        return (jax.ShapeDtypeStruct((bn, sn, w), BF16), pl.BlockSpec((1, ts, w), lambda b, i: (b, i, 0)))

    outs = [hm(NSA_HEADS, hd), tm_(hd), tm_(hd), tm_(n_sel + hd), tm_(hd), hm(1, hd), hm(1, hd),
            hm(SB_HEADS, hd), hm(SB_HEADS, hd), hm(SB_HEADS, hd),
            hm(2 * DIFF_HEADS, dd), hm(2 * DIFF_HEADS, dd), hm(DIFF_HEADS, 2 * dd),
            hm(SWA_HEADS, hd), hm(SWA_KV_HEADS, hd), hm(SWA_KV_HEADS, hd)]
    consts = gains + [bd64, bd32]
    res = pl.pallas_call(
        functools.partial(_prep_kernel, ts=ts, sb_scale=-scale * LOG2E),
        grid=(bn, sn // ts),
        in_specs=[cspec(n) for n in names] + [tab] * 4 + [full(a) for a in consts],
        out_specs=[o[1] for o in outs],
        out_shape=[o[0] for o in outs],
        compiler_params=_cparams("parallel", "parallel"),
        name="mixer_prep",
    )(*([proj] * len(names)), c64, s64, c32, s32, *consts)
    keys = ("a_q", "kc", "vc", "kx", "vs", "kw", "vw", "b_q", "b_k", "b_v", "c_q", "c_k", "c_v", "d_q", "d_k", "d_v")
    return dict(zip(keys, res))


def _layer_columns(d):
    cols = {}
    off = 0
    for name, width in (("br_g", N_BRANCH * d), ("ga", 3 * NSA_HEADS * HEAD_DIM), ("a_q", 256),
                        ("b_q", 256), ("b_k", 256), ("b_v", 256), ("c_q", 256), ("c_k", 256), ("c_v", 256),
                        ("d_q", 256), ("a_kcvc", 128), ("a_ksvs", 128), ("a_kwvw", 128), ("d_k", 128), ("d_v", 128)):
        cols[name] = (off, width)
        off += width
    return cols, off


def _reorder_w_in(w_in, d, n_pad):
    ref_splits = (256, 128, 128, 128, 12, 256, 256, 256, 256, 256, 256, 256, 128, 128, N_BRANCH * d)
    names = ("a_q", "a_kcvc", "a_ksvs", "a_kwvw", "a_g", "b_q", "b_k", "b_v",
             "c_q", "c_k", "c_v", "d_q", "d_k", "d_v", "br_g")
    starts = np.cumsum((0,) + ref_splits)
    src = {n: (int(starts[i]), ref_splits[i]) for i, n in enumerate(names)}
    cols, total = _layer_columns(d)
    idx = np.zeros((n_pad,), np.int32)
    keep = np.zeros((n_pad,), np.float32)
    for name, (off, width) in cols.items():
        if name == "ga":
            g0 = src["a_g"][0]
            for j in range(3):
                for h in range(NSA_HEADS):
                    base = off + j * NSA_HEADS * HEAD_DIM + h * HEAD_DIM
                    idx[base:base + HEAD_DIM] = g0 + h * 3 + j
        else:
            idx[off:off + width] = src[name][0] + np.arange(width)
        keep[off:off + width] = 1.0
    w = w_in[:, idx] * keep[None, :]
    return w.astype(BF16)


def _mixer_layer(x, positions, tables, mod, p, lam_init):
    bn, sn, d = x.shape
    sh1, sc1, g1 = mod[0], mod[1], mod[2]
    cols, total = _layer_columns(d)
    n_pad = -(-total // PROJ_TILE[1]) * PROJ_TILE[1]
    w_in = _reorder_w_in(p["w_in"], d, n_pad)
    proj = _in_projection(x, p["norm1"], sc1, sh1, w_in)
    n_cmp = (sn - NSA_CMP_LEN) // NSA_CMP_STRIDE + 1
    n_sel = sn // NSA_SEL_BLOCK
    n_top = min(NSA_N_SEL, n_sel)
    t = _prep(proj, cols, tables, p, n_sel)
    k_cmp = _compress(t["kc"], p["nsa_pe_k"], p["nsa_w1_k"], p["nsa_w2_k"]).astype(BF16)
    v_cmp = _compress(t["vc"], p["nsa_pe_v"], p["nsa_w1_v"], p["nsa_w2_v"]).astype(BF16)
    cmp_start = np.arange(sn // NSA_CMP_STRIDE) * NSA_CMP_STRIDE
    sel_start = np.arange(n_sel) * NSA_SEL_BLOCK
    overlap = ((cmp_start[:, None] <= (sel_start + NSA_SEL_BLOCK - 1)[None, :]) &
               ((cmp_start + NSA_CMP_LEN - 1)[:, None] >= sel_start[None, :]) &
               (np.arange(sn // NSA_CMP_STRIDE) < n_cmp)[:, None]).astype(np.float32)
    o_c, bias = _nsa_cmp(t["a_q"], k_cmp, v_cmp, jnp.asarray(overlap, BF16), n_cmp, n_top)
    o_s = _nsa_sel(t["a_q"], t["kx"], t["vs"], bias)
    o_w = _banded_attention(t["a_q"], t["kw"], t["vw"], NSA_WINDOW)
    o_b = _stick_breaking(t["b_q"], t["b_k"], t["b_v"])
    lam = (jnp.exp(jnp.sum(p["dif_lq1"] * p["dif_lk1"])) - jnp.exp(jnp.sum(p["dif_lq2"] * p["dif_lk2"]))
           + lam_init)
    o_cd = _diff_attention(t["c_q"], t["c_k"], t["c_v"], p["dif_subln"], lam, jnp.asarray(1.0 - lam_init, F32))
    o_d = _banded_attention(t["d_q"], t["d_k"], t["d_v"], SWA_WINDOW, sinks=p["swa_sinks"])
    return _merge(proj, o_c, o_s, o_w, o_b, o_cd, o_d, p["w_up"].astype(BF16), p["w_out"].astype(BF16),
                  x, g1, cols["ga"][0])


def kernel(x, c, positions, w_ada, b_ada, norm1, norm2, w_in, nsa_qn, nsa_kn, nsa_pe_k, nsa_w1_k, nsa_w2_k,
           nsa_pe_v, nsa_w1_v, nsa_w2_v, dif_qn, dif_kn, dif_lq1, dif_lk1, dif_lq2, dif_lk2, dif_subln,
           swa_qn, swa_kn, swa_sinks, w_up, w_out, w_router, b_router, w_gu, b_gu, w_dn, b_dn):
    bn, sn, d = x.shape
    depth = w_ada.shape[0]
    c_pad = jnp.zeros((8, d), F32).at[:bn].set(c)
    tables = _rope_tables(positions, HEAD_DIM) + _rope_tables(positions, DIFF_DIM)
    for l in range(depth):
        lam_init = 0.8 - 0.6 * math.exp(-0.3 * l)
        mod = _linear(c_pad, w_ada[l], b_ada[l], tn=512, precision=lax.Precision.HIGHEST)[:bn]
        mod = mod.reshape(bn, 6, 1, d).transpose(1, 0, 2, 3)
        p = dict(norm1=norm1[l], w_in=w_in[l], nsa_qn=nsa_qn[l], nsa_kn=nsa_kn[l], nsa_pe_k=nsa_pe_k[l],
                 nsa_w1_k=nsa_w1_k[l], nsa_w2_k=nsa_w2_k[l], nsa_pe_v=nsa_pe_v[l], nsa_w1_v=nsa_w1_v[l],
                 nsa_w2_v=nsa_w2_v[l], dif_qn=dif_qn[l], dif_kn=dif_kn[l], dif_lq1=dif_lq1[l],
                 dif_lk1=dif_lk1[l], dif_lq2=dif_lq2[l], dif_lk2=dif_lk2[l], dif_subln=dif_subln[l],
                 swa_qn=swa_qn[l], swa_kn=swa_kn[l], swa_sinks=swa_sinks[l], w_up=w_up[l], w_out=w_out[l])
        x = _mixer_layer(x, positions, tables, mod, p, lam_init)
        x = _moe(x, norm2[l], mod[4], mod[3], mod[5], w_router[l], b_router[l], l, w_gu, b_gu, w_dn, b_dn)
    return x
```
